```python
import math
import jax
import jax.numpy as jnp
from jax import lax
import numpy as np

D_MODEL = 1024
BATCH = 4
SEQ = 8192
DEPTH = 2

GRID_W = 64
CTX_LEN = 256
Q_BLOCK = 128
ROPE_THETA = 10000.0
ROPE_DIM = 64
LN_EPS = 1e-6

DIFF_HEADS = 4
DIFF_HD = 64
DIFF_VD = 2 * DIFF_HD
W_DIFF = DIFF_HEADS * DIFF_VD
S5_GROUP = 16
S5_GROUPS = 32
S5_STATE = 64
W_S5 = S5_GROUP * S5_GROUPS
ML_HEADS = 4
ML_HD = 128
W_ML = ML_HEADS * ML_HD
ML_CHUNK = 64
ML_CONV = 3
GQA_HEADS = 8
GQA_KV = 2
GQA_HD = 64
GQA_GROUP = GQA_HEADS // GQA_KV
W_GQA = GQA_HEADS * GQA_HD

N_BRANCH = 4
W_BRANCH = 512

PEER_HEADS = 8
PEER_NKEYS = 128
PEER_EXPERTS = PEER_NKEYS * PEER_NKEYS
PEER_TOPK = 16
PEER_DQ = 256
PEER_BLOCK = 128

ALPHA = (2 * DEPTH) ** 0.25
BETA = (8 * DEPTH) ** -0.25

IN_SPLITS = (W_DIFF, W_DIFF, W_DIFF,
             W_S5,
             W_ML, W_ML, W_ML, W_ML,
             4 * ML_HEADS,
             W_GQA, GQA_KV * GQA_HD, GQA_KV * GQA_HD,
             N_BRANCH * D_MODEL)
N_IN = sum(IN_SPLITS)
IN_OFFSETS = tuple(sum(IN_SPLITS[:i + 1]) for i in range(len(IN_SPLITS) - 1))

kernel_name = 'hybrid_diffusion_backbone'


def layer_norm(x, g=None, b=None):
    xf = x.astype(jnp.float32)
    xc = xf - jnp.mean(xf, axis=-1, keepdims=True)
    y = xc * lax.rsqrt(jnp.mean(xc * xc, axis=-1, keepdims=True) + LN_EPS)
    if g is not None:
        y = y * g.astype(jnp.float32) + b.astype(jnp.float32)
    return y.astype(x.dtype)


def rms_norm(x, g):
    xf = x.astype(jnp.float32)
    y = xf * lax.rsqrt(jnp.mean(xf * xf, axis=-1, keepdims=True) + LN_EPS)
    return (y * g.astype(jnp.float32)).astype(x.dtype)


def modulate(x, shift, scale):
    return layer_norm(x) * (1 + scale) + shift


def rope_angles(length):
    rows = length // GRID_W
    row = jnp.repeat(jnp.arange(rows, dtype=jnp.float32), GRID_W)
    col = jnp.tile(jnp.arange(GRID_W, dtype=jnp.float32), rows)
    nf = ROPE_DIM // 4
    inv = ROPE_THETA ** (-jnp.arange(nf, dtype=jnp.float32) / nf)
    return row[:, None] * inv, col[:, None] * inv


def _rotate(x, ang):
    x1, x2 = jnp.split(x, 2, axis=-1)
    cos, sin = jnp.cos(ang), jnp.sin(ang)
    return jnp.concatenate([x1 * cos - x2 * sin, x2 * cos + x1 * sin], axis=-1)


def rope_2d(x, ang_row, ang_col):
    shape = (x.shape[1],) + (1,) * (x.ndim - 3) + (ang_row.shape[-1],)
    xf = x.astype(jnp.float32)
    half = ROPE_DIM // 2
    out = jnp.concatenate([_rotate(xf[..., :half], ang_row.reshape(shape)),
                           _rotate(xf[..., half:], ang_col.reshape(shape))], axis=-1)
    return out.astype(x.dtype)


def sweep_query_blocks(fn, q):
    B, L = q.shape[:2]
    nb = L // Q_BLOCK
    qb = jnp.moveaxis(q.reshape((B, nb, Q_BLOCK) + q.shape[2:]), 1, 0)
    out = jnp.moveaxis(lax.map(fn, qb), 0, 1)
    return out.reshape((B, L) + out.shape[3:])


def diff_attend(q, k, v, lam):
    scale = DIFF_HD ** -0.5

    def block(qb):
        s = jnp.einsum('bqhcd,bshcd->bhcqs', qb, k, preferred_element_type=jnp.float32) * scale
        p = jax.nn.softmax(s, axis=-1)
        a = (p[:, :, 0] - lam * p[:, :, 1]).astype(v.dtype)
        return jnp.einsum('bhqs,bshe->bqhe', a, v)
    return sweep_query_blocks(block, q)


def gqa_attend(q, k, v):
    scale = GQA_HD ** -0.5

    def block(qb):
        s = jnp.einsum('bqkgd,bskd->bkgqs', qb, k, preferred_element_type=jnp.float32) * scale
        p = jax.nn.softmax(s, axis=-1).astype(v.dtype)
        return jnp.einsum('bkgqs,bskd->bqkgd', p, v)
    return sweep_query_blocks(block, q)


def diff_mixer(q_c, k_c, v_c, q_l, k_l, v_l, lam_vec, norm_g, lam_init, ang_row, ang_col, with_ctx):
    def heads_qk(t):
        return t.reshape(t.shape[:2] + (DIFF_HEADS, 2, DIFF_HD))

    def heads_v(t):
        return t.reshape(t.shape[:2] + (DIFF_HEADS, DIFF_VD))

    lv = lam_vec.astype(jnp.float32)
    lam = jnp.exp(jnp.sum(lv[0] * lv[1])) - jnp.exp(jnp.sum(lv[2] * lv[3])) + lam_init
    kc, vc = heads_qk(k_c), heads_v(v_c)
    ql = rope_2d(heads_qk(q_l), ang_row, ang_col)
    kl = rope_2d(heads_qk(k_l), ang_row, ang_col)
    k_all = jnp.concatenate([kl, kc], axis=1)
    v_all = jnp.concatenate([heads_v(v_l), vc], axis=1)

    def finish(o):
        return (rms_norm(o, norm_g) * (1 - lam_init)).reshape(o.shape[:2] + (W_DIFF,))
    y_l = finish(diff_attend(ql, k_all, v_all, lam))
    y_c = finish(diff_attend(heads_qk(q_c), kc, vc, lam)) if with_ctx else None
    return y_c, y_l


def gqa_mixer(q_c, k_c, v_c, q_l, k_l, v_l, qn_g, kn_g, ang_row, ang_col, with_ctx):
    def hq(t):
        return rms_norm(t.reshape(t.shape[:2] + (GQA_KV, GQA_GROUP, GQA_HD)), qn_g)

    def hk(t):
        return rms_norm(t.reshape(t.shape[:2] + (GQA_KV, GQA_HD)), kn_g)

    def hv(t):
        return t.reshape(t.shape[:2] + (GQA_KV, GQA_HD))

    kc, vc = hk(k_c), hv(v_c)
    ql = rope_2d(hq(q_l), ang_row, ang_col)
    kl = rope_2d(hk(k_l), ang_row, ang_col)
    k_all = jnp.concatenate([kl, kc], axis=1)
    v_all = jnp.concatenate([hv(v_l), vc], axis=1)
    y_l = gqa_attend(ql, k_all, v_all)
    y_l = y_l.reshape(y_l.shape[:2] + (W_GQA,))
    y_c = None
    if with_ctx:
        y_c = gqa_attend(hq(q_c), kc, vc)
        y_c = y_c.reshape(y_c.shape[:2] + (W_GQA,))
    return y_c, y_l


def s5_discretise(a_re, a_im, log_dt, b_re, b_im):
    dt = jnp.exp(log_dt.astype(jnp.float32))[:, None]
    ar, ai = a_re.astype(jnp.float32), a_im.astype(jnp.float32)
    mag = jnp.exp(ar * dt)
    lr, li = mag * jnp.cos(ai * dt), mag * jnp.sin(ai * dt)
    den = ar * ar + ai * ai
    cr = ((lr - 1) * ar + li * ai) / den
    ci = (li * ar - (lr - 1) * ai) / den
    br, bi = b_re.astype(jnp.float32), b_im.astype(jnp.float32)
    return lr, li, cr[..., None] * br - ci[..., None] * bi, cr[..., None] * bi + ci[..., None] * br


def _complex_affine_combine(e1, e2):
    a1r, a1i, b1r, b1i = e1
    a2r, a2i, b2r, b2i = e2
    return (a1r * a2r - a1i * a2i, a1r * a2i + a1i * a2r,
            a2r * b1r - a2i * b1i + b2r, a2r * b1i + a2i * b1r + b2i)


def s5_scan(u, lr, li, br, bi, h0, reverse):
    L = u.shape[1]
    bu_r = jnp.einsum('blgc,gnc->blgn', u, br)
    bu_i = jnp.einsum('blgc,gnc->blgn', u, bi)
    if h0 is not None:
        idx = L - 1 if reverse else 0
        bu_r = bu_r.at[:, idx].add(lr * h0[0] - li * h0[1])
        bu_i = bu_i.at[:, idx].add(lr * h0[1] + li * h0[0])
    ar = jnp.broadcast_to(lr, (1, L) + lr.shape)
    ai = jnp.broadcast_to(li, (1, L) + li.shape)
    _, _, hr, hi = lax.associative_scan(_complex_affine_combine, (ar, ai, bu_r, bu_i), reverse=reverse, axis=1)
    return hr, hi


def s5_readout(hr, hi, c_re, c_im):
    return jnp.einsum('blgn,gcn->blgc', hr, c_re) - jnp.einsum('blgn,gcn->blgc', hi, c_im)


def s5_mixer(u_c, u_l, a_re, a_im, log_dt, b_re, b_im, c_re, c_im, d_skip, w_glu, b_glu, with_ctx):
    def groups(t):
        return t.astype(jnp.float32).reshape(t.shape[:2] + (S5_GROUPS, S5_GROUP))
    uc, ul = groups(u_c), groups(u_l)
    dg = d_skip.astype(jnp.float32).reshape(S5_GROUPS, S5_GROUP)
    y_l = ul * dg
    y_c = uc * dg if with_ctx else None
    for dirn, reverse in ((0, False), (1, True)):
        lr, li, br, bi = s5_discretise(a_re[dirn], a_im[dirn], log_dt[dirn], b_re[dirn], b_im[dirn])
        hcr, hci = s5_scan(uc, lr, li, br, bi, None, reverse)
        end = 0 if reverse else -1
        hlr, hli = s5_scan(ul, lr, li, br, bi, (hcr[:, end], hci[:, end]), reverse)
        y_l = y_l + s5_readout(hlr, hli, c_re[dirn], c_im[dirn])
        if with_ctx:
            y_c = y_c + s5_readout(hcr, hci, c_re[dirn], c_im[dirn])

    def glu(y, ref):
        z = jax.nn.gelu(y.reshape(y.shape[:2] + (W_S5,)), approximate=False).astype(ref.dtype)
        a, g = jnp.split(z @ w_glu + b_glu, 2, axis=-1)
        return a * jax.nn.sigmoid(g)
    return (glu(y_c, u_c) if with_ctx else None), glu(y_l, u_l)


def centred_conv(x, w, b):
    K = w.shape[0]
    pad = K // 2
    L = x.shape[1]
    xp = jnp.pad(x, ((0, 0), (pad, pad), (0, 0)))
    out = b
    for j in range(K):
        out = out + xp[:, j:j + L] * w[j]
    return out


def mlstm_chunked(q, k, v, log_i, log_f, state):
    B, H, L, d = q.shape
    nc = L // ML_CHUNK

    def chunks(t):
        return jnp.moveaxis(t.reshape((B, H, nc, ML_CHUNK) + t.shape[3:]), 2, 0)
    mask = jnp.tril(jnp.ones((ML_CHUNK, ML_CHUNK), dtype=bool))

    def step(carry, xs):
        C, n, m = carry
        qc, kc, vc, ic, fc = xs
        b = jnp.cumsum(fc, axis=-1)
        logw = jnp.where(mask, b[..., :, None] - b[..., None, :] + ic[..., None, :], -jnp.inf)
        m_inter = b + m[..., None]
        m_t = jnp.maximum(m_inter, jnp.max(logw, axis=-1))
        s = jnp.einsum('bhtd,bhsd->bhts', qc, kc) * jnp.exp(logw - m_t[..., None])
        inter = jnp.exp(m_inter - m_t)
        num = jnp.einsum('bhts,bhsd->bhtd', s, vc) + inter[..., None] * jnp.einsum('bhvk,bhtk->bhtv', C, qc)
        den = jnp.sum(s, axis=-1) + inter * jnp.einsum('bhk,bhtk->bht', n, qc)
        h = num / jnp.maximum(jnp.abs(den), jnp.exp(-m_t))[..., None]
        b_end = b[..., -1]
        g = b_end[..., None] - b + ic
        m_new = jnp.maximum(b_end + m, jnp.max(g, axis=-1))
        decay = jnp.exp(b_end + m - m_new)
        wk = jnp.exp(g - m_new[..., None])
        C_new = decay[..., None, None] * C + jnp.einsum('bhtv,bhtk->bhvk', vc * wk[..., None], kc)
        n_new = decay[..., None] * n + jnp.einsum('bht,bhtk->bhk', wk, kc)
        return (C_new, n_new, m_new), h

    final, hs = lax.scan(step, state, (chunks(q), chunks(k), chunks(v), chunks(log_i), chunks(log_f)))
    return jnp.moveaxis(hs, 0, 2).reshape(B, H, L, d), final


def mlstm_mixer(parts_c, parts_l, conv_w, conv_b, gate_b, norm_g, with_ctx):
    def prep(q, k, v, gates):
        B, L = q.shape[:2]
        qk = jax.nn.silu(centred_conv(jnp.concatenate([q, k], axis=-1), conv_w, conv_b))
        q, k = jnp.split(qk, 2, axis=-1)

        def heads(t):
            return t.reshape(B, L, ML_HEADS, ML_HD).transpose(0, 2, 1, 3).astype(jnp.float32)
        g = (gates + gate_b).astype(jnp.float32).reshape(B, L, 4, ML_HEADS).transpose(2, 0, 3, 1)
        return heads(q), heads(k) * ML_HD ** -0.5, heads(v), g

    qc, kc, vc, gc = prep(parts_c[0], parts_c[1], parts_c[2], parts_c[4])
    ql, kl, vl, gl = prep(parts_l[0], parts_l[1], parts_l[2], parts_l[4])
    B = qc.shape[0]
    st0 = (jnp.zeros((B, ML_HEADS, ML_HD, ML_HD), jnp.float32),
           jnp.zeros((B, ML_HEADS, ML_HD), jnp.float32),
           jnp.zeros((B, ML_HEADS), jnp.float32))
    logsig = jax.nn.log_sigmoid

    def flip(t):
        return jnp.flip(t, axis=2)
    hc_f, st_f = mlstm_chunked(qc, kc, vc, gc[0], logsig(gc[1]), st0)
    hl_f, _ = mlstm_chunked(ql, kl, vl, gl[0], logsig(gl[1]), st_f)
    hc_b, st_b = mlstm_chunked(flip(qc), flip(kc), flip(vc), flip(gc[2]), flip(logsig(gc[3])), st0)
    hl_b, _ = mlstm_chunked(flip(ql), flip(kl), flip(vl), flip(gl[2]), flip(logsig(gl[3])), st_b)

    def finish(h, o):
        B_, L = o.shape[:2]
        hn = rms_norm(h.transpose(0, 2, 1, 3), norm_g.reshape(ML_HEADS, ML_HD))
        og = jax.nn.sigmoid(o.astype(jnp.float32)).reshape(B_, L, ML_HEADS, ML_HD)
        return (hn * og).reshape(B_, L, W_ML).astype(o.dtype)
    y_l = finish(hl_f + flip(hl_b), parts_l[3])
    y_c = finish(hc_f + flip(hc_b), parts_c[3]) if with_ctx else None
    return y_c, y_l


def merge_branches(ys, gate_pre, b_gate, w_branch, w_out):
    gates = jnp.split(jax.nn.sigmoid((gate_pre + b_gate).astype(jnp.float32)).astype(gate_pre.dtype), N_BRANCH, axis=-1)
    merged = gates[0] * (ys[0] @ w_branch[0])
    for k in range(1, N_BRANCH):
        merged = merged + gates[k] * (ys[k] @ w_branch[k])
    return merged @ w_out


def peer_ffn(x, wq, subkeys, emb_u, emb_v):
    B, L, D = x.shape
    T = B * L
    t = x.reshape(T, D)
    q = (t @ wq).reshape(T, PEER_HEADS, 2, PEER_DQ // 2)
    s = jnp.einsum('thpd,pkd->thpk', q, subkeys, preferred_element_type=jnp.float32)
    s1, i1 = lax.top_k(s[:, :, 0], PEER_TOPK)
    s2, i2 = lax.top_k(s[:, :, 1], PEER_TOPK)
    cand = (s1[..., :, None] + s2[..., None, :]).reshape(T, PEER_HEADS, PEER_TOPK * PEER_TOPK)
    cidx = (i1[..., :, None] * PEER_NKEYS + i2[..., None, :]).reshape(T, PEER_HEADS, PEER_TOPK * PEER_TOPK)
    top, pos = lax.top_k(cand, PEER_TOPK)
    idx = jnp.take_along_axis(cidx, pos, axis=-1)
    gate = jax.nn.softmax(top, axis=-1)
    nb = T // PEER_BLOCK

    def block(args):
        tb, ib, gb = args
        act = jax.nn.gelu(jnp.einsum('td,thkd->thk', tb, emb_u[ib], preferred_element_type=jnp.float32), approximate=False)
        w = (gb * act).astype(emb_v.dtype)
        return jnp.einsum('thk,thkd->td', w, emb_v[ib])
    out = lax.map(block, (t.reshape(nb, PEER_BLOCK, D),
                          idx.reshape(nb, PEER_BLOCK, PEER_HEADS, PEER_TOPK),
                          gate.reshape(nb, PEER_BLOCK, PEER_HEADS, PEER_TOPK)))
    return out.reshape(B, L, D)


def setup_inputs(seed: int = 0) -> dict:
    key = jax.random.key(seed)
    ks = iter(jax.random.split(key, 48))
    f32 = jnp.float32

    def nrm(shape, scale):
        return jax.random.normal(next(ks), shape, f32) * scale

    D = D_MODEL
    gate_base = jnp.tile(jnp.concatenate([jnp.zeros((ML_HEADS,), f32), jnp.linspace(3.0, 6.0, ML_HEADS, dtype=f32)]), 2)
    return {
        'x': nrm((BATCH, SEQ, D), 1.0),
        'c': nrm((BATCH, D), 1.0),
        'ctx': nrm((BATCH, CTX_LEN, D), 1.0),
        'c_ctx': nrm((D,), 1.0),
        'ada_w': nrm((DEPTH, D, 6 * D), 0.5 * D ** -0.5),
        'ada_b': nrm((DEPTH, 6 * D), 0.02),
        'w_in': nrm((DEPTH, D, N_IN), D ** -0.5),
        'b_gate': nrm((DEPTH, N_BRANCH * D), 0.02),
        'diff_lam': nrm((DEPTH, 4, DIFF_HD), 0.1),
        'diff_norm_g': 1.0 + nrm((DEPTH, DIFF_VD), 0.02),
        'gqa_qnorm_g': 1.0 + nrm((DEPTH, GQA_HD), 0.02),
        'gqa_knorm_g': 1.0 + nrm((DEPTH, GQA_HD), 0.02),
        's5_a_re': -0.5 + nrm((DEPTH, 2, S5_GROUPS, S5_STATE), 0.01),
        's5_a_im': jnp.pi * jnp.arange(S5_STATE, dtype=f32) + nrm((DEPTH, 2, S5_GROUPS, S5_STATE), 0.01),
        's5_log_dt': jax.random.uniform(next(ks), (DEPTH, 2, S5_GROUPS), f32, math.log(1e-3), math.log(1e-1)),
        's5_b_re': nrm((DEPTH, 2, S5_GROUPS, S5_STATE, S5_GROUP), (2 * S5_GROUP) ** -0.5),
        's5_b_im': nrm((DEPTH, 2, S5_GROUPS, S5_STATE, S5_GROUP), (2 * S5_GROUP) ** -0.5),
        's5_c_re': nrm((DEPTH, 2, S5_GROUPS, S5_GROUP, S5_STATE), (2 * S5_STATE) ** -0.5),
        's5_c_im': nrm((DEPTH, 2, S5_GROUPS, S5_GROUP, S5_STATE), (2 * S5_STATE) ** -0.5),
        's5_d': nrm((DEPTH, W_S5), 0.5),
        's5_w_glu': nrm((DEPTH, W_S5, 2 * W_S5), W_S5 ** -0.5),
        's5_b_glu': nrm((DEPTH, 2 * W_S5), 0.02),
        'ml_conv_w': nrm((DEPTH, ML_CONV, 2 * W_ML), ML_CONV ** -0.5),
        'ml_conv_b': nrm((DEPTH, 2 * W_ML), 0.02),
        'ml_gate_b': gate_base[None, :] + nrm((DEPTH, 4 * ML_HEADS), 0.1),
        'ml_norm_g': 1.0 + nrm((DEPTH, W_ML), 0.02),
        'w_branch': nrm((DEPTH, N_BRANCH, W_BRANCH, D), BETA * W_BRANCH ** -0.5),
        'w_out': nrm((DEPTH, D, D), BETA * D ** -0.5),
        'ln_mix_g': 1.0 + nrm((DEPTH, D), 0.02),
        'ln_mix_b': nrm((DEPTH, D), 0.02),
        'ln_ffn_g': 1.0 + nrm((DEPTH, D), 0.02),
        'ln_ffn_b': nrm((DEPTH, D), 0.02),
        'peer_wq': nrm((DEPTH, D, PEER_HEADS * PEER_DQ), D ** -0.5),
        'peer_subkeys': nrm((DEPTH, 2, PEER_NKEYS, PEER_DQ // 2), (PEER_DQ // 2) ** -0.5),
        'peer_u': nrm((DEPTH, PEER_EXPERTS, D), D ** -0.5),
        'peer_v': nrm((DEPTH, PEER_EXPERTS, D), BETA * PEER_HEADS ** -0.5),
    }


def reference(x, c, ctx, c_ctx, ada_w, ada_b, w_in, b_gate, diff_lam, diff_norm_g, gqa_qnorm_g, gqa_knorm_g,
              s5_a_re, s5_a_im, s5_log_dt, s5_b_re, s5_b_im, s5_c_re, s5_c_im, s5_d, s5_w_glu, s5_b_glu,
              ml_conv_w, ml_conv_b, ml_gate_b, ml_norm_g, w_branch, w_out, ln_mix_g, ln_mix_b, ln_ffn_g, ln_ffn_b,
              peer_wq, peer_subkeys, peer_u, peer_v):
    ang_row, ang_col = rope_angles(x.shape[1])
    cond_lat = jax.nn.silu(c)
    cond_ctx = jax.nn.silu(c_ctx)
    h_lat, h_ctx = x, ctx
    for l in range(DEPTH):
        with_ctx = l < DEPTH - 1
        lam_init = 0.8 - 0.6 * math.exp(-0.3 * l)
        mod_lat = (cond_lat @ ada_w[l] + ada_b[l])[:, None, :]
        mod_ctx = (cond_ctx @ ada_w[l] + ada_b[l])[None, None, :]
        sh1_l, sc1_l, g1_l, sh2_l, sc2_l, g2_l = jnp.split(mod_lat, 6, axis=-1)
        sh1_c, sc1_c, g1_c, sh2_c, sc2_c, g2_c = jnp.split(mod_ctx, 6, axis=-1)

        pl = jnp.split(modulate(h_lat, sh1_l, sc1_l) @ w_in[l], IN_OFFSETS, axis=-1)
        pc = jnp.split(modulate(h_ctx, sh1_c, sc1_c) @ w_in[l], IN_OFFSETS, axis=-1)

        yd_c, yd_l = diff_mixer(pc[0], pc[1], pc[2], pl[0], pl[1], pl[2], diff_lam[l], diff_norm_g[l],
                                lam_init, ang_row, ang_col, with_ctx)
        ys_c, ys_l = s5_mixer(pc[3], pl[3], s5_a_re[l], s5_a_im[l], s5_log_dt[l], s5_b_re[l], s5_b_im[l],
                              s5_c_re[l], s5_c_im[l], s5_d[l], s5_w_glu[l], s5_b_glu[l], with_ctx)
        ym_c, ym_l = mlstm_mixer(pc[4:9], pl[4:9], ml_conv_w[l], ml_conv_b[l], ml_gate_b[l], ml_norm_g[l], with_ctx)
        yg_c, yg_l = gqa_mixer(pc[9], pc[10], pc[11], pl[9], pl[10], pl[11], gqa_qnorm_g[l], gqa_knorm_g[l],
                               ang_row, ang_col, with_ctx)

        mix_l = merge_branches((yd_l, ys_l, ym_l, yg_l), pl[12], b_gate[l], w_branch[l], w_out[l])
        h_lat = layer_norm(ALPHA * h_lat + g1_l * mix_l, ln_mix_g[l], ln_mix_b[l])
        f_l = peer_ffn(modulate(h_lat, sh2_l, sc2_l), peer_wq[l], peer_subkeys[l], peer_u[l], peer_v[l])
        h_lat = layer_norm(ALPHA * h_lat + g2_l * f_l, ln_ffn_g[l], ln_ffn_b[l])

        if with_ctx:
            mix_c = merge_branches((yd_c, ys_c, ym_c, yg_c), pc[12], b_gate[l], w_branch[l], w_out[l])
            h_ctx = layer_norm(ALPHA * h_ctx + g1_c * mix_c, ln_mix_g[l], ln_mix_b[l])
            f_c = peer_ffn(modulate(h_ctx, sh2_c, sc2_c), peer_wq[l], peer_subkeys[l], peer_u[l], peer_v[l])
            h_ctx = layer_norm(ALPHA * h_ctx + g2_c * f_c, ln_ffn_g[l], ln_ffn_b[l])
    return h_lat
```

```python
import functools
import math

import numpy as np
import jax
import jax.numpy as jnp
from jax import lax
from jax.experimental import pallas as pl
from jax.experimental.pallas import tpu as pltpu

F32 = jnp.float32
BF16 = jnp.bfloat16
I32 = jnp.int32

D_MODEL = 1024
DEPTH = 2
GRID_W = 64
ROPE_THETA = 10000.0
LN_EPS = 1e-6
HD = 64
DIFF_HEADS = 4
GQA_HEADS = 8
GQA_KV = 2
S5_GROUP = 16
S5_GROUPS = 32
S5_STATE = 64
ML_HEADS = 4
ML_HD = 128
N_BRANCH = 4
W_BRANCH = 512
PEER_HEADS = 8
PEER_NKEYS = 128
PEER_TOPK = 16
PEER_DQ = 256
ALPHA = (2 * DEPTH) ** 0.25

LANES = 128
TM = 256
TQ = 256
TK = 256
S5_TC = 128
ML_T = 256
PEER_CA = 8
VMEM_LIMIT = 56 * 1024 * 1024

IN_SPLITS = (512, 512, 512, 512, 512, 512, 512, 512, 16, 512, 128, 128, N_BRANCH * D_MODEL)
IN_OFFS = tuple(int(v) for v in np.cumsum((0,) + IN_SPLITS))


def _cparams(sem):
    return pltpu.CompilerParams(dimension_semantics=sem, vmem_limit_bytes=VMEM_LIMIT)


def _const_spec(shape):
    nd = len(shape)
    return pl.BlockSpec(shape, lambda *_: (0,) * nd, pipeline_mode=pl.Buffered(1))


def _ln(x):
    xc = x - jnp.mean(x, axis=-1, keepdims=True)
    return xc * lax.rsqrt(jnp.mean(xc * xc, axis=-1, keepdims=True) + LN_EPS)


def _sigmoid(x):
    return 1.0 / (1.0 + jnp.exp(-x))


def _gelu(x):
    return 0.5 * x * (1.0 + lax.erf(x * (2.0 ** -0.5)))


def _dot(a, b):
    return jnp.dot(a, b, preferred_element_type=F32)


def _dot_nt(a, b):
    return lax.dot_general(a, b, (((1,), (1,)), ((), ())), preferred_element_type=F32)


def _dot_tn(a, b):
    return lax.dot_general(a, b, (((0,), (0,)), ((), ())), preferred_element_type=F32)


def _ada_kernel(c_ref, w_ref, b_ref, o_ref):
    c = c_ref[...]
    o_ref[...] = _dot((c * _sigmoid(c)).astype(BF16), w_ref[...]) + b_ref[...]


def ada_modulation(cond, w, b):
    R, D = cond.shape
    N = w.shape[1]
    tn = 1536
    return pl.pallas_call(
        _ada_kernel, out_shape=jax.ShapeDtypeStruct((R, N), F32), grid=(N // tn,),
        in_specs=[pl.BlockSpec((R, D), lambda j: (0, 0)), pl.BlockSpec((D, tn), lambda j: (0, j)),
                  pl.BlockSpec((1, tn), lambda j: (0, j))],
        out_specs=pl.BlockSpec((R, tn), lambda j: (0, j)),
        compiler_params=_cparams(("arbitrary",)), name="ada")(cond, w, b)


def _proj_attn_kernel(x_ref, mod_ref, w_ref, tab_ref, ones_ref,
                      dq_ref, dk_ref, dv_ref, gq_ref, gk_ref, gv_ref):
    mod = mod_ref[0, 0]
    xm = (_ln(x_ref[0]) * (1.0 + mod[1:2, :]) + mod[0:1, :]).astype(BF16)

    def mm(lo, n):
        return _dot(xm, w_ref[:, lo:lo + n])

    def rope_store(ref, lo, n, ci, norm):
        t, tp = mm(lo, n), mm(lo + n, n)
        c, s = tab_ref[ci], tab_ref[ci + 1]
        for j in range(n // LANES):
            sl = slice(j * LANES, (j + 1) * LANES)
            tb = t[:, sl]
            y = tb * c + tp[:, sl] * s
            if norm:
                ss = _dot((tb * tb).astype(BF16), ones_ref[...])
                y = y * lax.rsqrt(ss * (1.0 / HD) + LN_EPS)
            ref[0, :, sl] = y.astype(ref.dtype)

    rope_store(dq_ref, 0, 512, 0, False)
    rope_store(dk_ref, 1024, 512, 2, False)
    dv_ref[0] = mm(2048, 512).astype(dv_ref.dtype)
    rope_store(gq_ref, 2560, 512, 4, True)
    rope_store(gk_ref, 3584, 256, 6, True)
    gv_ref[0] = mm(4096, 256).astype(gv_ref.dtype)


def proj_attn(xa, mod, w_a, tab, ones_bd, nct):
    B, S, D = xa.shape
    tok = lambda n: pl.BlockSpec((1, TM, n), lambda b, i: (b, i, 0))
    outs = [jax.ShapeDtypeStruct((B, S, n), BF16) for n in (512, 512, 512, 512, 256, 256)]
    return pl.pallas_call(
        _proj_attn_kernel, out_shape=outs, grid=(B, S // TM),
        in_specs=[tok(D),
                  pl.BlockSpec((1, 1, 6, D), lambda b, i: (b, jnp.where(i >= nct, 1, 0), 0, 0)),
                  _const_spec(w_a.shape),
                  pl.BlockSpec((8, TM, LANES), lambda b, i: (0, i, 0)),
                  _const_spec(ones_bd.shape)],
        out_specs=[tok(n) for n in (512, 512, 512, 512, 256, 256)],
        compiler_params=_cparams(("parallel", "parallel")), name="proj_attn")(xa, mod, w_a, tab, ones_bd)


def _proj_seq_kernel(x_ref, mod_ref, w_ref, gb_ref, u_ref, qk_ref, v_ref, o_ref, g_ref):
    mod = mod_ref[0, 0]
    xm = (_ln(x_ref[0]) * (1.0 + mod[1:2, :]) + mod[0:1, :]).astype(BF16)
    u_ref[0] = _dot(xm, w_ref[:, 0:512])
    qk_ref[0] = _dot(xm, w_ref[:, 512:1536])
    v_ref[0] = _dot(xm, w_ref[:, 1536:2048])
    o_ref[0] = _dot(xm, w_ref[:, 2048:2560])
    g_ref[0] = _dot(xm, w_ref[:, 2560:2688]) + gb_ref[...]


def proj_seq(xa, mod, w_b, gate_b, nct):
    B, S, D = xa.shape
    tok = lambda n: pl.BlockSpec((1, TM, n), lambda b, i: (b, i, 0))
    widths = (512, 1024, 512, 512, LANES)
    return pl.pallas_call(
        _proj_seq_kernel, out_shape=[jax.ShapeDtypeStruct((B, S, n), F32) for n in widths], grid=(B, S // TM),
        in_specs=[tok(D),
                  pl.BlockSpec((1, 1, 6, D), lambda b, i: (b, jnp.where(i >= nct, 1, 0), 0, 0)),
                  _const_spec(w_b.shape), _const_spec(gate_b.shape)],
        out_specs=[tok(n) for n in widths],
        compiler_params=_cparams(("parallel", "parallel")), name="proj_seq")(xa, mod, w_b, gate_b)


def _flash(qq, k_ref, v_ref, n_chunks):
    R = qq.shape[0]

    def body(c, carry):
        m, l, acc = carry
        off = pl.multiple_of(c * TK, TK)
        s = _dot_nt(qq, k_ref[0, pl.ds(off, TK), :])
        m_new = jnp.maximum(m, jnp.max(s, axis=1, keepdims=True))
        alpha = jnp.exp(m - m_new)
        p = jnp.exp(s - m_new)
        l = alpha * l + jnp.sum(p, axis=1, keepdims=True)
        acc = alpha * acc + _dot(p.astype(BF16), v_ref[0, pl.ds(off, TK), :])
        return m_new, l, acc

    init = (jnp.full((R, 1), -jnp.inf, F32), jnp.zeros((R, 1), F32), jnp.zeros((R, LANES), F32))
    _, l, acc = lax.fori_loop(0, n_chunks, body, init)
    return acc / l


def _diff_attn_kernel(q_ref, k_ref, v_ref, lam_ref, g_ref, o_ref, *, nq_ctx, nk_ctx, nk_all, lam_init):
    i = pl.program_id(2)
    q = q_ref[0]
    lane = lax.broadcasted_iota(I32, q.shape, 1)
    zero = jnp.zeros_like(q)
    qq = jnp.concatenate([jnp.where(lane < HD, q, zero), jnp.where(lane >= HD, q, zero)], axis=0)
    o = _flash(qq, k_ref, v_ref, jnp.where(i < nq_ctx, nk_ctx, nk_all))
    lv = lam_ref[...]
    lam = (jnp.exp(jnp.sum(lv[0:1] * lv[1:2], axis=1, keepdims=True))
           - jnp.exp(jnp.sum(lv[2:3] * lv[3:4], axis=1, keepdims=True)) + lam_init)
    d = o[:TQ] - lam * o[TQ:]
    y = d * lax.rsqrt(jnp.mean(d * d, axis=-1, keepdims=True) + LN_EPS)
    o_ref[0] = (y * g_ref[...] * (1.0 - lam_init)).astype(o_ref.dtype)


def diff_attention(q, k, v, lam_vec, norm_g, nctx, lam_init):
    B, S, _ = q.shape
    kern = functools.partial(_diff_attn_kernel, nq_ctx=nctx // TQ, nk_ctx=nctx // TK, nk_all=S // TK,
                             lam_init=lam_init)
    return pl.pallas_call(
        kern, out_shape=jax.ShapeDtypeStruct((B, S, 512), BF16), grid=(B, DIFF_HEADS, S // TQ),
        in_specs=[pl.BlockSpec((1, TQ, LANES), lambda b, h, i: (b, i, h)),
                  pl.BlockSpec((1, S, LANES), lambda b, h, i: (b, 0, h)),
                  pl.BlockSpec((1, S, LANES), lambda b, h, i: (b, 0, h)),
                  pl.BlockSpec((4, HD), lambda b, h, i: (0, 0)),
                  pl.BlockSpec((1, LANES), lambda b, h, i: (0, 0))],
        out_specs=pl.BlockSpec((1, TQ, LANES), lambda b, h, i: (b, i, h)),
        compiler_params=_cparams(("parallel", "parallel", "parallel")), name="diff_attn")(q, k, v, lam_vec, norm_g)


def _gqa_attn_kernel(q_ref, k_ref, v_ref, o_ref, *, nq_ctx, nk_ctx, nk_all):
    i = pl.program_id(2)
    q = q_ref[0]
    lane = lax.broadcasted_iota(I32, (TQ, LANES), 1)
    zero = jnp.zeros((TQ, LANES), q.dtype)
    parts = []
    for j in range(2):
        blk = q[:, j * LANES:(j + 1) * LANES]
        parts += [jnp.where(lane < HD, blk, zero), jnp.where(lane >= HD, blk, zero)]
    o = _flash(jnp.concatenate(parts, axis=0), k_ref, v_ref, jnp.where(i < nq_ctx, nk_ctx, nk_all))
    for j in range(2):
        pair = jnp.where(lane < HD, o[(2 * j) * TQ:(2 * j + 1) * TQ], o[(2 * j + 1) * TQ:(2 * j + 2) * TQ])
        o_ref[0, :, j * LANES:(j + 1) * LANES] = pair.astype(o_ref.dtype)


def gqa_attention(q, k, v, nctx):
    B, S, _ = q.shape
    kern = functools.partial(_gqa_attn_kernel, nq_ctx=nctx // TQ, nk_ctx=nctx // TK, nk_all=S // TK)
    return pl.pallas_call(
        kern, out_shape=jax.ShapeDtypeStruct((B, S, 512), BF16), grid=(B, GQA_KV, S // TQ),
        in_specs=[pl.BlockSpec((1, TQ, 2 * LANES), lambda b, g, i: (b, i, g)),
                  pl.BlockSpec((1, S, LANES), lambda b, g, i: (b, 0, g)),
                  pl.BlockSpec((1, S, LANES), lambda b, g, i: (b, 0, g))],
        out_specs=pl.BlockSpec((1, TQ, 2 * LANES), lambda b, g, i: (b, i, g)),
        compiler_params=_cparams(("parallel", "parallel", "parallel")), name="gqa_attn")(q, k, v)


def _s5_kernel(uf_ref, ub_ref, bblk_ref, cblk_ref, lam_ref, yf_ref, yb_ref, buf_f, buf_b, st_ref, *, nb):
    H = S5_GROUPS * S5_STATE // 2

    @pl.when(pl.program_id(0) == 0)
    def _():
        st_ref[...] = jnp.zeros_like(st_ref)

    for d, (u_ref, buf) in enumerate(((uf_ref, buf_f), (ub_ref, buf_b))):
        for b in range(nb):
            for hh in range(2):
                uh = u_ref[b, :, hh * 256:(hh + 1) * 256].astype(BF16)
                buf[:, 2 * b + hh, :] = _dot(uh, bblk_ref[d, hh])

    lfr, lfi, lbr, lbi = lam_ref[0, 0], lam_ref[0, 1], lam_ref[1, 0], lam_ref[1, 1]

    def step(t, carry):
        fr, fi, br, bi = carry
        x = buf_f[t]
        nfr = lfr * fr - lfi * fi + x[:, :H]
        nfi = lfr * fi + lfi * fr + x[:, H:]
        buf_f[t] = jnp.concatenate([nfr, nfi], axis=1)
        tb = S5_TC - 1 - t
        z = buf_b[tb]
        nbr = lbr * br - lbi * bi + z[:, :H]
        nbi = lbr * bi + lbi * br + z[:, H:]
        buf_b[tb] = jnp.concatenate([nbr, nbi], axis=1)
        return nfr, nfi, nbr, nbi

    fin = lax.fori_loop(0, S5_TC, step, (st_ref[0], st_ref[1], st_ref[2], st_ref[3]))
    for j in range(4):
        st_ref[j] = fin[j]

    for d, (y_ref, buf) in enumerate(((yf_ref, buf_f), (yb_ref, buf_b))):
        for b in range(nb):
            for hh in range(2):
                h = buf[:, 2 * b + hh, :].astype(BF16)
                y_ref[b, :, hh * 256:(hh + 1) * 256] = _dot(h, cblk_ref[d, hh])


def s5_scan(u, bblk, cblk, lam, nctx):
    B, S, W = u.shape
    assert 2 * B == 8, "the scan packs (sample, half) pairs into the eight sublanes of a vreg"
    nch, nc0 = S // S5_TC, nctx // S5_TC

    def bwd(i):
        return jnp.where(i < nc0, nc0 - 1 - i, (nch - 1) - (i - nc0))
    H2 = S5_GROUPS * S5_STATE
    blk = lambda f: pl.BlockSpec((B, S5_TC, W), f)
    return pl.pallas_call(
        functools.partial(_s5_kernel, nb=B),
        out_shape=[jax.ShapeDtypeStruct((B, S, W), F32)] * 2, grid=(nch,),
        in_specs=[blk(lambda i: (0, i, 0)), blk(lambda i: (0, bwd(i), 0)),
                  _const_spec(bblk.shape), _const_spec(cblk.shape), _const_spec(lam.shape)],
        out_specs=[blk(lambda i: (0, i, 0)), blk(lambda i: (0, bwd(i), 0))],
        scratch_shapes=[pltpu.VMEM((S5_TC, 8, H2), F32), pltpu.VMEM((S5_TC, 8, H2), F32),
                        pltpu.VMEM((4, 8, H2 // 2), F32)],
        compiler_params=_cparams(("arbitrary",)), name="s5_scan")(u, u, bblk, cblk, lam)


def _ml_prep_kernel(x_ref, prev_ref, next_ref, w_ref, b_ref, q_ref, k_ref, *, seg_starts, seg_ends):
    i = pl.program_id(1)
    x = x_ref[0]
    row = lax.broadcasted_iota(I32, x.shape, 0)
    first = functools.reduce(jnp.logical_or, [i == s for s in seg_starts])
    last = functools.reduce(jnp.logical_or, [i == s for s in seg_ends])
    pr = jnp.where(first, 0.0, prev_ref[0, 7:8, :])
    nx = jnp.where(last, 0.0, next_ref[0, 0:1, :])
    xp = jnp.where(row == 0, pr, pltpu.roll(x, 1, 0))
    xn = jnp.where(row == TM - 1, nx, pltpu.roll(x, TM - 1, 0))
    w = w_ref[...]
    y = b_ref[...] + xp * w[0:1, :] + x * w[1:2, :] + xn * w[2:3, :]
    y = y * _sigmoid(y)
    q_ref[0] = y[:, :512].astype(q_ref.dtype)
    k_ref[0] = (y[:, 512:] * (ML_HD ** -0.5)).astype(k_ref.dtype)


def ml_prep(qk, conv_w, conv_b, nctx):
    B, S, W = qk.shape
    nt, nct, r8 = S // TM, nctx // TM, TM // 8
    kern = functools.partial(_ml_prep_kernel, seg_starts=(0, nct), seg_ends=(nct - 1, nt - 1))
    return pl.pallas_call(
        kern, out_shape=[jax.ShapeDtypeStruct((B, S, 512), BF16)] * 2, grid=(B, nt),
        in_specs=[pl.BlockSpec((1, TM, W), lambda b, i: (b, i, 0)),
                  pl.BlockSpec((1, 8, W), lambda b, i: (b, jnp.maximum(i * r8 - 1, 0), 0)),
                  pl.BlockSpec((1, 8, W), lambda b, i: (b, jnp.minimum((i + 1) * r8, S // 8 - 1), 0)),
                  pl.BlockSpec((3, W), lambda b, i: (0, 0)), pl.BlockSpec((1, W), lambda b, i: (0, 0))],
        out_specs=[pl.BlockSpec((1, TM, 512), lambda b, i: (b, i, 0))] * 2,
        compiler_params=_cparams(("parallel", "parallel")), name="ml_prep")(qk, qk, qk, conv_w, conv_b)


def _log_sigmoid(x):
    return jnp.minimum(x, 0.0) - jnp.log1p(jnp.exp(-jnp.abs(x)))


def _mlstm_kernel(q_ref, k_ref, v_ref, g_ref, h_ref, c_sc, n_sc, m_sc, *, reverse):
    T = ML_T
    gi, gf = (8, 12) if reverse else (0, 4)

    @pl.when(pl.program_id(1) == 0)
    def _():
        c_sc[...] = jnp.zeros_like(c_sc)
        n_sc[...] = jnp.zeros_like(n_sc)
        m_sc[...] = jnp.zeros_like(m_sc)

    g = g_ref[0]
    gt = g.T
    r = lax.broadcasted_iota(I32, (T, T), 0)
    c = lax.broadcasted_iota(I32, (T, T), 1)
    mask = (c >= r) if reverse else (c <= r)
    tri = jnp.where(mask, 1.0, 0.0)
    hp = lax.Precision.HIGHEST
    bcol_all = jnp.dot(tri, _log_sigmoid(g), preferred_element_type=F32, precision=hp)
    brow_all = lax.dot_general(_log_sigmoid(gt[0:16]), tri, (((1,), (1,)), ((), ())),
                               preferred_element_type=F32, precision=hp)
    end = 0 if reverse else T - 1
    for hh in range(ML_HEADS):
        sl = slice(hh * ML_HD, (hh + 1) * ML_HD)
        q, k, v = q_ref[0, :, sl], k_ref[0, :, sl], v_ref[0, :, sl]
        bcol, brow = bcol_all[:, gf + hh:gf + hh + 1], brow_all[gf + hh:gf + hh + 1, :]
        icol, irow = g[:, gi + hh:gi + hh + 1], gt[gi + hh:gi + hh + 1, :]
        m_old = m_sc[hh][:, 0:1]
        n_old = n_sc[hh]
        c_old = c_sc[hh]
        logw = jnp.where(mask, bcol - brow + irow, -jnp.inf)
        m_inter = bcol + m_old
        m_t = jnp.maximum(m_inter, jnp.max(logw, axis=1, keepdims=True))
        s = _dot_nt(q, k) * jnp.exp(logw - m_t)
        inter = jnp.exp(m_inter - m_t)
        num = _dot(s.astype(BF16), v.astype(BF16)) + inter * _dot_nt(q, c_old.astype(BF16))
        den = jnp.sum(s, axis=1, keepdims=True) + inter * jnp.sum(q.astype(F32) * n_old, axis=1, keepdims=True)
        h_ref[0, :, sl] = num / jnp.maximum(jnp.abs(den), jnp.exp(-m_t))
        b_end = bcol[end:end + 1, :]
        g_row, g_col = b_end - brow + irow, b_end - bcol + icol
        m_new = jnp.maximum(b_end + m_old, jnp.max(g_row, axis=1, keepdims=True))
        decay = jnp.exp(b_end + m_old - m_new)
        wk = jnp.exp(g_col - m_new)
        c_sc[hh] = decay * c_old + _dot_tn((v * wk).astype(BF16), k)
        n_sc[hh] = decay * n_old + jnp.sum(k.astype(F32) * wk, axis=0, keepdims=True)
        m_sc[hh] = jnp.broadcast_to(m_new, (1, LANES))


def mlstm_scan(q, k, v, g, nctx, reverse):
    B, S, W = q.shape
    nch, nc0 = S // ML_T, nctx // ML_T

    def order(i):
        return jnp.where(i < nc0, nc0 - 1 - i, (nch - 1) - (i - nc0)) if reverse else i
    blk = lambda n: pl.BlockSpec((1, ML_T, n), lambda b, i: (b, order(i), 0))
    return pl.pallas_call(
        functools.partial(_mlstm_kernel, reverse=reverse),
        out_shape=jax.ShapeDtypeStruct((B, S, W), F32), grid=(B, nch),
        in_specs=[blk(W), blk(W), blk(W), blk(LANES)], out_specs=blk(W),
        scratch_shapes=[pltpu.VMEM((ML_HEADS, ML_HD, ML_HD), F32), pltpu.VMEM((ML_HEADS, 1, ML_HD), F32),
                        pltpu.VMEM((ML_HEADS, 1, LANES), F32)],
        compiler_params=_cparams(("parallel", "arbitrary")), name="mlstm_bwd" if reverse else "mlstm_fwd")(q, k, v, g)


def _merge_kernel(x_ref, mod_ref, yd_ref, yg_ref, u_ref, sf_ref, sb_ref, hf_ref, hb_ref, mo_ref,
                  s5d_ref, wglu_ref, bglu_ref, mlg_ref, wgate_ref, bgate_ref, wbr_ref, wout_ref,
                  lng_ref, lnb_ref, ones_ref, o_ref):
    x = x_ref[0]
    mod = mod_ref[0, 0]
    xm = (_ln(x) * (1.0 + mod[1:2, :]) + mod[0:1, :]).astype(BF16)
    ys = u_ref[0] * s5d_ref[...] + sf_ref[0] + sb_ref[0]
    z = _dot(_gelu(ys).astype(BF16), wglu_ref[...]) + bglu_ref[...]
    ys = z[:, :512] * _sigmoid(z[:, 512:])
    hm = hf_ref[0] + hb_ref[0]
    ss = _dot((hm * hm).astype(BF16), ones_ref[...])
    ym = hm * lax.rsqrt(ss * (1.0 / ML_HD) + LN_EPS) * mlg_ref[...] * _sigmoid(mo_ref[0])
    branches = (yd_ref[0], ys.astype(BF16), ym.astype(BF16), yg_ref[0])
    merged = None
    for j, yb in enumerate(branches):
        gate = _sigmoid(_dot(xm, wgate_ref[:, j * D_MODEL:(j + 1) * D_MODEL]) + bgate_ref[:, j * D_MODEL:(j + 1) * D_MODEL])
        term = gate * _dot(yb, wbr_ref[j])
        merged = term if merged is None else merged + term
    mix = _dot(merged.astype(BF16), wout_ref[...])
    o_ref[0] = _ln(ALPHA * x + mod[2:3, :] * mix) * lng_ref[...] + lnb_ref[...]


def merge(xa, mod, yd, yg, u, sf, sb, hf, hb, mo, s5d, wglu, bglu, mlg, wgate, bgate, wbr, wout, lng, lnb, ones_ml, nct):
    B, S, D = xa.shape
    tok = lambda n: pl.BlockSpec((1, TM, n), lambda b, i: (b, i, 0))
    consts = (s5d, wglu, bglu, mlg, wgate, bgate, wbr, wout, lng, lnb, ones_ml)
    return pl.pallas_call(
        _merge_kernel, out_shape=jax.ShapeDtypeStruct((B, S, D), F32), grid=(B, S // TM),
        in_specs=[tok(D), pl.BlockSpec((1, 1, 6, D), lambda b, i: (b, jnp.where(i >= nct, 1, 0), 0, 0))]
        + [tok(512)] * 8 + [_const_spec(a.shape) for a in consts],
        out_specs=tok(D),
        compiler_params=_cparams(("parallel", "parallel")), name="merge")(xa, mod, yd, yg, u, sf, sb, hf, hb, mo, *consts)


def _top16(s, val_ref, idx_ref, payload=None):
    R = s.shape[0]
    ri = lax.broadcasted_iota(I32, s.shape, 0).astype(F32)

    def body(kk, s):
        m = jnp.max(s, axis=0, keepdims=True)
        ix = jnp.min(jnp.where(s == m, ri, float(R)), axis=0, keepdims=True)
        hit = ri == ix
        val_ref[pl.ds(kk, 1), :] = m
        idx_ref[pl.ds(kk, 1), :] = ix if payload is None else jnp.max(jnp.where(hit, payload, -1.0), axis=0, keepdims=True)
        return jnp.where(hit, -jnp.inf, s)

    lax.fori_loop(0, PEER_TOPK, body, s)


def _peer_route_kernel(x_ref, mod_ref, wq_ref, sk_ref, t_ref, ids_ref, gate_ref,
                       xm_sc, v1_sc, i1_sc, v2_sc, i2_sc, cand_sc, cidx_sc, top_sc, tid_sc):
    h = pl.program_id(2)

    @pl.when(h == 0)
    def _():
        mod = mod_ref[0, 0]
        xm = (_ln(x_ref[0]) * (1.0 + mod[4:5, :]) + mod[3:4, :]).astype(BF16)
        xm_sc[...] = xm
        t_ref[0] = xm

    q = _dot(xm_sc[...], wq_ref[...]).astype(BF16)
    half = PEER_DQ // 2
    _top16(_dot_nt(sk_ref[0], q[:, :half]), v1_sc, i1_sc)
    _top16(_dot_nt(sk_ref[1], q[:, half:]), v2_sc, i2_sc)
    v2, i2 = v2_sc[...], i2_sc[...]
    for p in range(PEER_TOPK):
        cand_sc[p * PEER_TOPK:(p + 1) * PEER_TOPK, :] = v1_sc[p:p + 1, :] + v2
        cidx_sc[p * PEER_TOPK:(p + 1) * PEER_TOPK, :] = i1_sc[p:p + 1, :] * float(PEER_NKEYS) + i2
    _top16(cand_sc[...], top_sc, tid_sc, payload=cidx_sc[...])
    top = top_sc[...]
    e = jnp.exp(top - top[0:1, :])
    gate_ref[...] = e / jnp.sum(e, axis=0, keepdims=True)
    ids_ref[...] = tid_sc[...].astype(I32)


def peer_route(h1, mod, wq, subkeys, nct):
    B, S, D = h1.shape
    nt = S // TM
    f = lambda n: pltpu.VMEM((n, TM), F32)
    return pl.pallas_call(
        _peer_route_kernel,
        out_shape=[jax.ShapeDtypeStruct((B, S, D), BF16),
                   jax.ShapeDtypeStruct((PEER_HEADS * PEER_TOPK, B * S), I32),
                   jax.ShapeDtypeStruct((PEER_HEADS * PEER_TOPK, B * S), F32)],
        grid=(B, nt, PEER_HEADS),
        in_specs=[pl.BlockSpec((1, TM, D), lambda b, i, h: (b, i, 0)),
                  pl.BlockSpec((1, 1, 6, D), lambda b, i, h: (b, jnp.where(i >= nct, 1, 0), 0, 0)),
                  pl.BlockSpec((D, PEER_DQ), lambda b, i, h: (0, h)),
                  pl.BlockSpec((2, PEER_NKEYS, PEER_DQ // 2), lambda b, i, h: (0, 0, 0))],
        out_specs=[pl.BlockSpec((1, TM, D), lambda b, i, h: (b, i, 0)),
                   pl.BlockSpec((PEER_TOPK, TM), lambda b, i, h: (h, b * nt + i)),
                   pl.BlockSpec((PEER_TOPK, TM), lambda b, i, h: (h, b * nt + i))],
        scratch_shapes=[pltpu.VMEM((TM, D), BF16), f(16), f(16), f(16), f(16), f(256), f(256), f(16), f(16)],
        compiler_params=_cparams(("parallel", "parallel", "arbitrary")), name="peer_route")(h1, mod, wq, subkeys)


def _peer_expert_kernel(t_ref, ids_ref, gate_ref, u_ref, v_ref, x_ref, mod_ref, lng_ref, lnb_ref, o_ref,
                        a_sc, acc_sc, *, nchunk):
    j = pl.program_id(1)
    NK = PEER_NKEYS

    @pl.when(j < nchunk)
    def _():
        a = _dot_nt(t_ref[...], u_ref[...])
        for al in range(PEER_CA):
            a_sc[pl.ds(j * PEER_CA + al, TM, stride=NK), :] = a[:, al * NK:(al + 1) * NK]

    @pl.when(j == nchunk - 1)
    def _():
        io = lax.broadcasted_iota(I32, (NK, NK), 0)

        def body(t, carry):
            ids = ids_ref[pl.ds(t, 1), :]
            r1 = io == (ids >> 7)
            r2 = io == (ids & (NK - 1))
            rows = pl.ds(pl.multiple_of(t * NK, NK), NK)
            at = a_sc[rows, :].astype(BF16)
            picked = _dot(at, jnp.where(r2, 1.0, 0.0).astype(BF16))
            act = jnp.sum(jnp.where(r1, picked, 0.0), axis=0, keepdims=True)
            w = gate_ref[pl.ds(t, 1), :] * _gelu(act)
            a_sc[rows, :] = _dot_nt(jnp.where(r1, 1.0, 0.0).astype(BF16), jnp.where(r2, w, 0.0).astype(BF16))
            return carry

        lax.fori_loop(0, TM, body, 0)
        acc_sc[...] = jnp.zeros_like(acc_sc)

    @pl.when(j >= nchunk)
    def _():
        jj = j - nchunk
        w = jnp.concatenate([a_sc[pl.ds(jj * PEER_CA + al, TM, stride=NK), :].astype(BF16)
                             for al in range(PEER_CA)], axis=1)
        acc_sc[...] += _dot(w, v_ref[...])

    @pl.when(j == 2 * nchunk - 1)
    def _():
        mod = mod_ref[0, 0]
        o_ref[...] = _ln(ALPHA * x_ref[...] + mod[5:6, :] * acc_sc[...]) * lng_ref[...] + lnb_ref[...]


def peer_experts(t, ids, gates, emb_u, emb_v, h1, mod, lng, lnb, nt_per_sample, nct):
    T, D = h1.shape
    E = emb_u.shape[0]
    ce = PEER_CA * PEER_NKEYS
    nchunk = E // ce
    tok = lambda n: pl.BlockSpec((TM, n), lambda i, j: (i, 0))
    return pl.pallas_call(
        functools.partial(_peer_expert_kernel, nchunk=nchunk),
        out_shape=jax.ShapeDtypeStruct((T, D), F32), grid=(T // TM, 2 * nchunk),
        in_specs=[tok(D), tok(LANES), tok(LANES),
                  pl.BlockSpec((ce, D), lambda i, j: (jnp.minimum(j, nchunk - 1), 0)),
                  pl.BlockSpec((ce, D), lambda i, j: (jnp.maximum(j - nchunk, 0), 0)),
                  tok(D),
                  pl.BlockSpec((1, 1, 6, D), lambda i, j: (i // nt_per_sample, jnp.where(i % nt_per_sample >= nct, 1, 0), 0, 0)),
                  pl.BlockSpec((1, D), lambda i, j: (0, 0)), pl.BlockSpec((1, D), lambda i, j: (0, 0))],
        out_specs=tok(D),
        scratch_shapes=[pltpu.VMEM((TM * PEER_NKEYS, PEER_NKEYS), F32), pltpu.VMEM((TM, D), F32)],
        compiler_params=_cparams(("parallel", "arbitrary")), name="peer_experts")(t, ids, gates, emb_u, emb_v, h1, mod, lng, lnb)


_ROPE_IDX = np.arange(HD)
_ROPE_PERM = np.where(_ROPE_IDX % 32 < 16, _ROPE_IDX + 16, _ROPE_IDX - 16)
_ROPE_SIGN = np.where(_ROPE_IDX % 32 < 16, -1.0, 1.0).astype(np.float32)


def _rope_partner(w):
    n = w.shape[1] // HD
    perm = np.concatenate([h * HD + _ROPE_PERM for h in range(n)])
    return w[:, perm] * jnp.asarray(np.tile(_ROPE_SIGN, n))


def _rope_tables(L, nctx, gq, gk):
    rows = L // GRID_W
    row = jnp.repeat(jnp.arange(rows, dtype=F32), GRID_W)
    col = jnp.tile(jnp.arange(GRID_W, dtype=F32), rows)
    nf = HD // 4
    inv = ROPE_THETA ** (-jnp.arange(nf, dtype=F32) / nf)
    ar, ac = row[:, None] * inv, col[:, None] * inv
    cos = jnp.concatenate([jnp.cos(ar), jnp.cos(ar), jnp.cos(ac), jnp.cos(ac)], axis=1)
    sin = jnp.concatenate([jnp.sin(ar), jnp.sin(ar), jnp.sin(ac), jnp.sin(ac)], axis=1)
    cos = jnp.concatenate([jnp.ones((nctx, HD), F32), cos], axis=0)
    sin = jnp.concatenate([jnp.zeros((nctx, HD), F32), sin], axis=0)
    scale = HD ** -0.5
    one = jnp.ones((HD,), F32)
    tabs = []
    for g, sc in ((one, scale), (one, 1.0), (gq, scale), (gk, 1.0)):
        tabs += [cos * (g * sc), sin * (g[_ROPE_PERM] * sc)]
    return jnp.tile(jnp.stack(tabs), (1, 1, 2))


def _s5_tables(a_re, a_im, log_dt, b_re, b_im, c_re, c_im, nb):
    dt = jnp.exp(log_dt)[..., None]
    mag = jnp.exp(a_re * dt)
    lr, li = mag * jnp.cos(a_im * dt), mag * jnp.sin(a_im * dt)
    den = a_re * a_re + a_im * a_im
    cr = ((lr - 1) * a_re + li * a_im) / den
    ci = (li * a_re - (lr - 1) * a_im) / den
    br = cr[..., None] * b_re - ci[..., None] * b_im
    bi = cr[..., None] * b_im + ci[..., None] * b_re
    G2, N, C = S5_GROUPS // 2, S5_STATE, S5_GROUP
    eye = jnp.eye(G2, dtype=F32)

    def blockdiag_in(m):
        m = m.reshape(2, 2, G2, N, C)
        return jnp.einsum('dhgnc,gk->dhgckn', m, eye).reshape(2, 2, G2 * C, G2 * N)

    def blockdiag_out(m):
        m = m.reshape(2, 2, G2, C, N)
        return jnp.einsum('dhgcn,gk->dhgnkc', m, eye).reshape(2, 2, G2 * N, G2 * C)

    bblk = jnp.concatenate([blockdiag_in(br), blockdiag_in(bi)], axis=-1).astype(BF16)
    cblk = jnp.concatenate([blockdiag_out(c_re), blockdiag_out(-c_im)], axis=-2).astype(BF16)
    lam = jnp.stack([lr, li], axis=1).reshape(2, 2, 2, G2 * N)
    lam = jnp.tile(lam[:, :, None], (1, 1, nb, 1, 1)).reshape(2, 2, 2 * nb, G2 * N)
    return bblk, cblk, lam


def _blockdiag_ones(group, n=LANES):
    i = np.arange(n) // group
    return jnp.asarray((i[:, None] == i[None, :]).astype(np.float32), dtype=BF16)


def kernel(x, c, ctx, c_ctx, ada_w, ada_b, w_in, b_gate, diff_lam, diff_norm_g, gqa_qnorm_g, gqa_knorm_g,
           s5_a_re, s5_a_im, s5_log_dt, s5_b_re, s5_b_im, s5_c_re, s5_c_im, s5_d, s5_w_glu, s5_b_glu,
           ml_conv_w, ml_conv_b, ml_gate_b, ml_norm_g, w_branch, w_out, ln_mix_g, ln_mix_b, ln_ffn_g, ln_ffn_b,
           peer_wq, peer_subkeys, peer_u, peer_v):
    B, L, D = x.shape
    nctx = ctx.shape[1]
    S = nctx + L
    assert D == D_MODEL and nctx % TM == 0 and L % TM == 0 and L % GRID_W == 0
    nt, nct = S // TM, nctx // TM
    depth = ada_w.shape[0]

    h = jnp.concatenate([ctx, x], axis=1)
    R = -(-(B + 1) // 8) * 8
    cond = jnp.zeros((R, D), F32).at[:B].set(c).at[B].set(c_ctx)
    ones64, ones128 = _blockdiag_ones(HD), _blockdiag_ones(ML_HD, ML_HEADS * ML_HD)
    o = IN_OFFS

    for l in range(depth):
        lam_init = 0.8 - 0.6 * math.exp(-0.3 * l)
        m = ada_modulation(cond, ada_w[l].astype(BF16), ada_b[l][None, :])
        mod = jnp.stack([jnp.broadcast_to(m[B], (B, 6 * D)), m[:B]], axis=1).reshape(B, 2, 6, D)

        w = w_in[l]
        seg = lambda i: w[:, o[i]:o[i + 1]]
        dup = lambda t: jnp.concatenate([t[:, :HD], t[:, :HD], t[:, HD:], t[:, HD:]], axis=1)
        w_a = jnp.concatenate([seg(0), _rope_partner(seg(0)), seg(1), _rope_partner(seg(1)), seg(2),
                               seg(9), _rope_partner(seg(9)), dup(seg(10)), dup(_rope_partner(seg(10))),
                               dup(seg(11))], axis=1).astype(BF16)
        w_b = jnp.concatenate([seg(3), seg(4), seg(5), seg(6), seg(7), seg(8),
                               jnp.zeros((D, LANES - 16), F32)], axis=1).astype(BF16)
        gate_b = jnp.concatenate([ml_gate_b[l], jnp.zeros((LANES - 16,), F32)])[None, :]
        tab = _rope_tables(L, nctx, gqa_qnorm_g[l], gqa_knorm_g[l])

        dq, dk, dv, gq, gk, gv = proj_attn(h, mod, w_a, tab, ones64, nct)
        u, mqk, mv, mo, mg = proj_seq(h, mod, w_b, gate_b, nct)

        yd = diff_attention(dq, dk, dv, diff_lam[l], diff_norm_g[l][None, :], nctx, lam_init)
        yg = gqa_attention(gq, gk, gv, nctx)

        bblk, cblk, lam = _s5_tables(s5_a_re[l], s5_a_im[l], s5_log_dt[l], s5_b_re[l], s5_b_im[l],
                                     s5_c_re[l], s5_c_im[l], B)
        sf, sb = s5_scan(u, bblk, cblk, lam, nctx)

        mq, mk = ml_prep(mqk, ml_conv_w[l], ml_conv_b[l][None, :], nctx)
        hf = mlstm_scan(mq, mk, mv, mg, nctx, False)
        hb = mlstm_scan(mq, mk, mv, mg, nctx, True)

        h1 = merge(h, mod, yd, yg, u, sf, sb, hf, hb, mo,
                   s5_d[l][None, :], s5_w_glu[l].astype(BF16), s5_b_glu[l][None, :], ml_norm_g[l][None, :],
                   seg(12).astype(BF16), b_gate[l][None, :], w_branch[l].astype(BF16), w_out[l].astype(BF16),
                   ln_mix_g[l][None, :], ln_mix_b[l][None, :], ones128, nct)

        t, ids, gates = peer_route(h1, mod, peer_wq[l].astype(BF16), peer_subkeys[l].astype(BF16), nct)
        h = peer_experts(t.reshape(B * S, D), ids.T, gates.T, peer_u[l].astype(BF16), peer_v[l].astype(BF16),
                         h1.reshape(B * S, D), mod, ln_ffn_g[l][None, :], ln_ffn_b[l][None, :], nt, nct).reshape(B, S, D)
    return h[:, nctx:]
```

```python
import functools
import math

import numpy as np
import jax
import jax.numpy as jnp
from jax import lax
from jax.experimental import pallas as pl
from jax.experimental.pallas import tpu as pltpu

F32 = jnp.float32
BF16 = jnp.bfloat16
I32 = jnp.int32

D_MODEL = 1024
DEPTH = 2
GRID_W = 64
ROPE_THETA = 10000.0
LN_EPS = 1e-6
HD = 64
DIFF_HEADS = 4
GQA_HEADS = 8
GQA_KV = 2
S5_GROUP = 16
S5_GROUPS = 32
S5_STATE = 64
ML_HEADS = 4
ML_HD = 128
N_BRANCH = 4
W_BRANCH = 512
PEER_HEADS = 8
PEER_NKEYS = 128
PEER_TOPK = 16
PEER_DQ = 256
ALPHA = (2 * DEPTH) ** 0.25

LANES = 128
TM = 256
TQ = 256
TK = 1024
S5_TC = 128
ML_T = 256
PEER_CA = 8
VMEM_LIMIT = 56 * 1024 * 1024

IN_SPLITS = (512, 512, 512, 512, 512, 512, 512, 512, 16, 512, 128, 128, N_BRANCH * D_MODEL)
IN_OFFS = tuple(int(v) for v in np.cumsum((0,) + IN_SPLITS))


def _cparams(sem):
    return pltpu.CompilerParams(dimension_semantics=sem, vmem_limit_bytes=VMEM_LIMIT)


def _const_spec(shape):
    nd = len(shape)
    return pl.BlockSpec(shape, lambda *_: (0,) * nd, pipeline_mode=pl.Buffered(1))


def _ln(x):
    xc = x - jnp.mean(x, axis=-1, keepdims=True)
    return xc * lax.rsqrt(jnp.mean(xc * xc, axis=-1, keepdims=True) + LN_EPS)


def _sigmoid(x):
    return 1.0 / (1.0 + jnp.exp(-x))


def _gelu(x):
    return 0.5 * x * (1.0 + lax.erf(x * (2.0 ** -0.5)))


def _dot(a, b):
    return jnp.dot(a, b, preferred_element_type=F32)


def _dot_nt(a, b):
    return lax.dot_general(a, b, (((1,), (1,)), ((), ())), preferred_element_type=F32)


def _dot_tn(a, b):
    return lax.dot_general(a, b, (((0,), (0,)), ((), ())), preferred_element_type=F32)


def _ada_kernel(c_ref, w_ref, b_ref, o_ref):
    c = c_ref[...]
    o_ref[...] = _dot((c * _sigmoid(c)).astype(BF16), w_ref[...]) + b_ref[...]


def ada_modulation(cond, w, b):
    R, D = cond.shape
    N = w.shape[1]
    tn = 1536
    return pl.pallas_call(
        _ada_kernel, out_shape=jax.ShapeDtypeStruct((R, N), F32), grid=(N // tn,),
        in_specs=[pl.BlockSpec((R, D), lambda j: (0, 0)), pl.BlockSpec((D, tn), lambda j: (0, j)),
                  pl.BlockSpec((1, tn), lambda j: (0, j))],
        out_specs=pl.BlockSpec((R, tn), lambda j: (0, j)),
        compiler_params=_cparams(("arbitrary",)), name="ada")(cond, w, b)


def _proj_attn_kernel(x_ref, mod_ref, w_ref, tab_ref, ones_ref,
                      dq_ref, dk_ref, dv_ref, gq_ref, gk_ref, gv_ref):
    mod = mod_ref[0, 0]
    xm = (_ln(x_ref[0]) * (1.0 + mod[1:2, :]) + mod[0:1, :]).astype(BF16)

    def mm(lo, n):
        return _dot(xm, w_ref[:, lo:lo + n])

    def rope_store(ref, lo, n, ci, norm):
        t, tp = mm(lo, n), mm(lo + n, n)
        c, s = tab_ref[ci], tab_ref[ci + 1]
        for j in range(n // LANES):
            sl = slice(j * LANES, (j + 1) * LANES)
            tb = t[:, sl]
            y = tb * c + tp[:, sl] * s
            if norm:
                ss = _dot((tb * tb).astype(BF16), ones_ref[...])
                y = y * lax.rsqrt(ss * (1.0 / HD) + LN_EPS)
            ref[0, :, sl] = y.astype(ref.dtype)

    rope_store(dq_ref, 0, 512, 0, False)
    rope_store(dk_ref, 1024, 512, 2, False)
    dv_ref[0] = mm(2048, 512).astype(dv_ref.dtype)
    rope_store(gq_ref, 2560, 512, 4, True)
    rope_store(gk_ref, 3584, 256, 6, True)
    gv_ref[0] = mm(4096, 256).astype(gv_ref.dtype)


def proj_attn(xa, mod, w_a, tab, ones_bd, nct):
    B, S, D = xa.shape
    tok = lambda n: pl.BlockSpec((1, TM, n), lambda b, i: (b, i, 0))
    outs = [jax.ShapeDtypeStruct((B, S, n), BF16) for n in (512, 512, 512, 512, 256, 256)]
    return pl.pallas_call(
        _proj_attn_kernel, out_shape=outs, grid=(B, S // TM),
        in_specs=[tok(D),
                  pl.BlockSpec((1, 1, 6, D), lambda b, i: (b, jnp.where(i >= nct, 1, 0), 0, 0)),
                  _const_spec(w_a.shape),
                  pl.BlockSpec((8, TM, LANES), lambda b, i: (0, i, 0)),
                  _const_spec(ones_bd.shape)],
        out_specs=[tok(n) for n in (512, 512, 512, 512, 256, 256)],
        compiler_params=_cparams(("parallel", "parallel")), name="proj_attn")(xa, mod, w_a, tab, ones_bd)


def _proj_seq_kernel(x_ref, mod_ref, w_ref, gb_ref, u_ref, qk_ref, v_ref, o_ref, g_ref):
    mod = mod_ref[0, 0]
    xm = (_ln(x_ref[0]) * (1.0 + mod[1:2, :]) + mod[0:1, :]).astype(BF16)
    u_ref[...] = _dot(xm, w_ref[:, 0:512])
    qk_ref[0] = _dot(xm, w_ref[:, 512:1536])
    v_ref[0] = _dot(xm, w_ref[:, 1536:2048])
    o_ref[0] = _dot(xm, w_ref[:, 2048:2560])
    g_ref[0] = _dot(xm, w_ref[:, 2560:2688]) + gb_ref[...]


def proj_seq(xa, mod, w_b, gate_b, nct):
    B, S, D = xa.shape
    tok = lambda n: pl.BlockSpec((1, TM, n), lambda b, i: (b, i, 0))
    widths = (1024, 512, 512, LANES)
    return pl.pallas_call(
        _proj_seq_kernel,
        out_shape=[jax.ShapeDtypeStruct((S, B * 512), F32)] + [jax.ShapeDtypeStruct((B, S, n), F32) for n in widths],
        grid=(B, S // TM),
        in_specs=[tok(D),
                  pl.BlockSpec((1, 1, 6, D), lambda b, i: (b, jnp.where(i >= nct, 1, 0), 0, 0)),
                  _const_spec(w_b.shape), _const_spec(gate_b.shape)],
        out_specs=[pl.BlockSpec((TM, 512), lambda b, i: (i, b))] + [tok(n) for n in widths],
        compiler_params=_cparams(("parallel", "parallel")), name="proj_seq")(xa, mod, w_b, gate_b)


def _flash(qq, k_ref, v_ref, nctx, n_lat, tk):
    R = qq.shape[0]

    def chunk(carry, rows):
        m, l, acc = carry
        s = _dot_nt(qq, k_ref[0, rows, :])
        m_new = jnp.maximum(m, jnp.max(s, axis=1, keepdims=True))
        alpha = jnp.exp(m - m_new)
        p = jnp.exp(s - m_new)
        l = alpha * l + jnp.sum(p, axis=1, keepdims=True)
        acc = alpha * acc + _dot(p.astype(BF16), v_ref[0, rows, :])
        return m_new, l, acc

    init = (jnp.full((R, 1), -jnp.inf, F32), jnp.zeros((R, 1), F32), jnp.zeros((R, LANES), F32))
    carry = chunk(init, pl.ds(0, nctx))
    _, l, acc = lax.fori_loop(
        0, n_lat, lambda c, carry: chunk(carry, pl.ds(pl.multiple_of(nctx + c * tk, LANES), tk)), carry)
    return acc / l


def _diff_attn_kernel(q_ref, k_ref, v_ref, lam_ref, g_ref, o_ref, *, nctx, n_lat, tk, lam_init):
    i = pl.program_id(2)
    q = q_ref[0]
    lane = lax.broadcasted_iota(I32, q.shape, 1)
    zero = jnp.zeros_like(q)
    qq = jnp.concatenate([jnp.where(lane < HD, q, zero), jnp.where(lane >= HD, q, zero)], axis=0)
    o = _flash(qq, k_ref, v_ref, nctx, jnp.where(i < nctx // TQ, 0, n_lat), tk)
    lv = lam_ref[...]
    lam = (jnp.exp(jnp.sum(lv[0:1] * lv[1:2], axis=1, keepdims=True))
           - jnp.exp(jnp.sum(lv[2:3] * lv[3:4], axis=1, keepdims=True)) + lam_init)
    d = o[:TQ] - lam * o[TQ:]
    y = d * lax.rsqrt(jnp.mean(d * d, axis=-1, keepdims=True) + LN_EPS)
    o_ref[0] = (y * g_ref[...] * (1.0 - lam_init)).astype(o_ref.dtype)


def diff_attention(q, k, v, lam_vec, norm_g, nctx, lam_init):
    B, S, _ = q.shape
    tk = math.gcd(S - nctx, TK)
    kern = functools.partial(_diff_attn_kernel, nctx=nctx, n_lat=(S - nctx) // tk, tk=tk, lam_init=lam_init)
    return pl.pallas_call(
        kern, out_shape=jax.ShapeDtypeStruct((B, S, 512), BF16), grid=(B, DIFF_HEADS, S // TQ),
        in_specs=[pl.BlockSpec((1, TQ, LANES), lambda b, h, i: (b, i, h)),
                  pl.BlockSpec((1, S, LANES), lambda b, h, i: (b, 0, h)),
                  pl.BlockSpec((1, S, LANES), lambda b, h, i: (b, 0, h)),
                  pl.BlockSpec((4, HD), lambda b, h, i: (0, 0)),
                  pl.BlockSpec((1, LANES), lambda b, h, i: (0, 0))],
        out_specs=pl.BlockSpec((1, TQ, LANES), lambda b, h, i: (b, i, h)),
        compiler_params=_cparams(("parallel", "parallel", "parallel")), name="diff_attn")(q, k, v, lam_vec, norm_g)


def _gqa_attn_kernel(q_ref, k_ref, v_ref, o_ref, *, nctx, n_lat, tk):
    i = pl.program_id(2)
    q = q_ref[0]
    lane = lax.broadcasted_iota(I32, (TQ, LANES), 1)
    zero = jnp.zeros((TQ, LANES), q.dtype)
    parts = []
    for j in range(2):
        blk = q[:, j * LANES:(j + 1) * LANES]
        parts += [jnp.where(lane < HD, blk, zero), jnp.where(lane >= HD, blk, zero)]
    o = _flash(jnp.concatenate(parts, axis=0), k_ref, v_ref, nctx, jnp.where(i < nctx // TQ, 0, n_lat), tk)
    for j in range(2):
        pair = jnp.where(lane < HD, o[(2 * j) * TQ:(2 * j + 1) * TQ], o[(2 * j + 1) * TQ:(2 * j + 2) * TQ])
        o_ref[0, :, j * LANES:(j + 1) * LANES] = pair.astype(o_ref.dtype)


def gqa_attention(q, k, v, nctx):
    B, S, _ = q.shape
    tk = math.gcd(S - nctx, TK)
    kern = functools.partial(_gqa_attn_kernel, nctx=nctx, n_lat=(S - nctx) // tk, tk=tk)
    return pl.pallas_call(
        kern, out_shape=jax.ShapeDtypeStruct((B, S, 512), BF16), grid=(B, GQA_KV, S // TQ),
        in_specs=[pl.BlockSpec((1, TQ, 2 * LANES), lambda b, g, i: (b, i, g)),
                  pl.BlockSpec((1, S, LANES), lambda b, g, i: (b, 0, g)),
                  pl.BlockSpec((1, S, LANES), lambda b, g, i: (b, 0, g))],
        out_specs=pl.BlockSpec((1, TQ, 2 * LANES), lambda b, g, i: (b, i, g)),
        compiler_params=_cparams(("parallel", "parallel", "parallel")), name="gqa_attn")(q, k, v)


def _s5_kernel(uf_ref, ub_ref, bblk_ref, cblk_ref, lam_ref, yf_ref, yb_ref, buf_f, buf_b, st_ref):
    H = S5_GROUPS * S5_STATE // 2
    R = buf_f.shape[0]
    RB = 256

    @pl.when(pl.program_id(0) == 0)
    def _():
        st_ref[...] = jnp.zeros_like(st_ref)

    half0 = (lax.broadcasted_iota(I32, (RB, 1), 0) & 1) == 0

    for d, (u_ref, buf) in enumerate(((uf_ref, buf_f), (ub_ref, buf_b))):
        for rb in range(R // RB):
            rows = slice(rb * RB, (rb + 1) * RB)
            uh = u_ref[rows, :].astype(BF16)
            for cols in (slice(0, H), slice(H, 2 * H)):
                buf[rows, cols] = jnp.where(half0, _dot(uh, bblk_ref[d, 0, :, cols]), _dot(uh, bblk_ref[d, 1, :, cols]))

    lfr, lfi, lbr, lbi = lam_ref[0, 0], lam_ref[0, 1], lam_ref[1, 0], lam_ref[1, 1]

    def step(t, carry):
        fr, fi, br, bi = carry
        rf = pl.ds(pl.multiple_of(t * 8, 8), 8)
        x = buf_f[rf, :]
        nfr = lfr * fr - lfi * fi + x[:, :H]
        nfi = lfr * fi + lfi * fr + x[:, H:]
        buf_f[rf, :] = jnp.concatenate([nfr, nfi], axis=1)
        rb = pl.ds(pl.multiple_of((S5_TC - 1 - t) * 8, 8), 8)
        z = buf_b[rb, :]
        nbr = lbr * br - lbi * bi + z[:, :H]
        nbi = lbr * bi + lbi * br + z[:, H:]
        buf_b[rb, :] = jnp.concatenate([nbr, nbi], axis=1)
        return nfr, nfi, nbr, nbi

    fin = lax.fori_loop(0, S5_TC, step, (st_ref[0], st_ref[1], st_ref[2], st_ref[3]))
    for j in range(4):
        st_ref[j] = fin[j]

    for d, (y_ref, buf) in enumerate(((yf_ref, buf_f), (yb_ref, buf_b))):
        for rb in range(R // RB):
            rows = slice(rb * RB, (rb + 1) * RB)
            h = buf[rows, :].astype(BF16)
            y_ref[rows, :] = jnp.where(half0, _dot(h, cblk_ref[d, 0]), _dot(h, cblk_ref[d, 1]))


def s5_scan(u, bblk, cblk, lam, S, nctx):
    assert u.shape[0] == S * 8, "the scan packs (sample, half) pairs into the eight sublanes of a vreg"
    nch, nc0 = S // S5_TC, nctx // S5_TC

    def bwd(i):
        return jnp.where(i < nc0, nc0 - 1 - i, (nch - 1) - (i - nc0))
    H2 = S5_GROUPS * S5_STATE
    R, W = S5_TC * 8, u.shape[1]
    blk = lambda f: pl.BlockSpec((R, W), f)
    return pl.pallas_call(
        _s5_kernel, out_shape=[jax.ShapeDtypeStruct(u.shape, F32)] * 2, grid=(nch,),
        in_specs=[blk(lambda i: (i, 0)), blk(lambda i: (bwd(i), 0)),
                  _const_spec(bblk.shape), _const_spec(cblk.shape), _const_spec(lam.shape)],
        out_specs=[blk(lambda i: (i, 0)), blk(lambda i: (bwd(i), 0))],
        scratch_shapes=[pltpu.VMEM((R, H2), F32), pltpu.VMEM((R, H2), F32), pltpu.VMEM((4, 8, H2 // 2), F32)],
        compiler_params=_cparams(("arbitrary",)), name="s5_scan")(u, u, bblk, cblk, lam)


def _ml_prep_kernel(x_ref, prev_ref, next_ref, w_ref, b_ref, q_ref, k_ref, *, seg_starts, seg_ends):
    i = pl.program_id(1)
    x = x_ref[0]
    row = lax.broadcasted_iota(I32, x.shape, 0)
    first = functools.reduce(jnp.logical_or, [i == s for s in seg_starts])
    last = functools.reduce(jnp.logical_or, [i == s for s in seg_ends])
    pr = jnp.where(first, 0.0, prev_ref[0, 7:8, :])
    nx = jnp.where(last, 0.0, next_ref[0, 0:1, :])
    xp = jnp.where(row == 0, pr, pltpu.roll(x, 1, 0))
    xn = jnp.where(row == TM - 1, nx, pltpu.roll(x, TM - 1, 0))
    w = w_ref[...]
    y = b_ref[...] + xp * w[0:1, :] + x * w[1:2, :] + xn * w[2:3, :]
    y = y * _sigmoid(y)
    q_ref[0] = y[:, :512].astype(q_ref.dtype)
    k_ref[0] = (y[:, 512:] * (ML_HD ** -0.5)).astype(k_ref.dtype)


def ml_prep(qk, conv_w, conv_b, nctx):
    B, S, W = qk.shape
    nt, nct, r8 = S // TM, nctx // TM, TM // 8
    kern = functools.partial(_ml_prep_kernel, seg_starts=(0, nct), seg_ends=(nct - 1, nt - 1))
    return pl.pallas_call(
        kern, out_shape=[jax.ShapeDtypeStruct((B, S, 512), BF16)] * 2, grid=(B, nt),
        in_specs=[pl.BlockSpec((1, TM, W), lambda b, i: (b, i, 0)),
                  pl.BlockSpec((1, 8, W), lambda b, i: (b, jnp.maximum(i * r8 - 1, 0), 0)),
                  pl.BlockSpec((1, 8, W), lambda b, i: (b, jnp.minimum((i + 1) * r8, S // 8 - 1), 0)),
                  pl.BlockSpec((3, W), lambda b, i: (0, 0)), pl.BlockSpec((1, W), lambda b, i: (0, 0))],
        out_specs=[pl.BlockSpec((1, TM, 512), lambda b, i: (b, i, 0))] * 2,
        compiler_params=_cparams(("parallel", "parallel")), name="ml_prep")(qk, qk, qk, conv_w, conv_b)


def _log_sigmoid(x):
    return jnp.minimum(x, 0.0) - jnp.log1p(jnp.exp(-jnp.abs(x)))


def _mlstm_kernel(q_ref, k_ref, v_ref, g_ref, h_ref, c_sc, n_sc, m_sc, *, reverse):
    T = ML_T
    gi, gf = (8, 12) if reverse else (0, 4)

    @pl.when(pl.program_id(1) == 0)
    def _():
        c_sc[...] = jnp.zeros_like(c_sc)
        n_sc[...] = jnp.zeros_like(n_sc)
        m_sc[...] = jnp.zeros_like(m_sc)

    g = g_ref[0]
    gt = g.T
    r = lax.broadcasted_iota(I32, (T, T), 0)
    c = lax.broadcasted_iota(I32, (T, T), 1)
    mask = (c >= r) if reverse else (c <= r)
    tri = jnp.where(mask, 1.0, 0.0)
    hp = lax.Precision.HIGHEST
    bcol_all = jnp.dot(tri, _log_sigmoid(g), preferred_element_type=F32, precision=hp)
    brow_all = lax.dot_general(_log_sigmoid(gt[0:16]), tri, (((1,), (1,)), ((), ())),
                               preferred_element_type=F32, precision=hp)
    end = 0 if reverse else T - 1
    for hh in range(ML_HEADS):
        sl = slice(hh * ML_HD, (hh + 1) * ML_HD)
        q, k, v = q_ref[0, :, sl], k_ref[0, :, sl], v_ref[0, :, sl]
        bcol, brow = bcol_all[:, gf + hh:gf + hh + 1], brow_all[gf + hh:gf + hh + 1, :]
        icol, irow = g[:, gi + hh:gi + hh + 1], gt[gi + hh:gi + hh + 1, :]
        m_old = m_sc[hh][:, 0:1]
        n_old = n_sc[hh]
        c_old = c_sc[hh]
        logw = jnp.where(mask, bcol - brow + irow, -jnp.inf)
        m_inter = bcol + m_old
        m_t = jnp.maximum(m_inter, jnp.max(logw, axis=1, keepdims=True))
        s = _dot_nt(q, k) * jnp.exp(logw - m_t)
        inter = jnp.exp(m_inter - m_t)
        num = _dot(s.astype(BF16), v.astype(BF16)) + inter * _dot_nt(q, c_old.astype(BF16))
        den = jnp.sum(s, axis=1, keepdims=True) + inter * jnp.sum(q.astype(F32) * n_old, axis=1, keepdims=True)
        h_ref[0, :, sl] = num / jnp.maximum(jnp.abs(den), jnp.exp(-m_t))
        b_end = bcol[end:end + 1, :]
        g_row, g_col = b_end - brow + irow, b_end - bcol + icol
        m_new = jnp.maximum(b_end + m_old, jnp.max(g_row, axis=1, keepdims=True))
        decay = jnp.exp(b_end + m_old - m_new)
        wk = jnp.exp(g_col - m_new)
        c_sc[hh] = decay * c_old + _dot_tn((v * wk).astype(BF16), k)
        n_sc[hh] = decay * n_old + jnp.sum(k.astype(F32) * wk, axis=0, keepdims=True)
        m_sc[hh] = jnp.broadcast_to(m_new, (1, LANES))


def mlstm_scan(q, k, v, g, nctx, reverse):
    B, S, W = q.shape
    nch, nc0 = S // ML_T, nctx // ML_T

    def order(i):
        return jnp.where(i < nc0, nc0 - 1 - i, (nch - 1) - (i - nc0)) if reverse else i
    blk = lambda n: pl.BlockSpec((1, ML_T, n), lambda b, i: (b, order(i), 0))
    return pl.pallas_call(
        functools.partial(_mlstm_kernel, reverse=reverse),
        out_shape=jax.ShapeDtypeStruct((B, S, W), F32), grid=(B, nch),
        in_specs=[blk(W), blk(W), blk(W), blk(LANES)], out_specs=blk(W),
        scratch_shapes=[pltpu.VMEM((ML_HEADS, ML_HD, ML_HD), F32), pltpu.VMEM((ML_HEADS, 1, ML_HD), F32),
                        pltpu.VMEM((ML_HEADS, 1, LANES), F32)],
        compiler_params=_cparams(("parallel", "arbitrary")), name="mlstm_bwd" if reverse else "mlstm_fwd")(q, k, v, g)


def _merge_kernel(x_ref, mod_ref, yd_ref, yg_ref, u_ref, sf_ref, sb_ref, hf_ref, hb_ref, mo_ref,
                  s5d_ref, wglu_ref, bglu_ref, mlg_ref, wgate_ref, bgate_ref, wbr_ref, wout_ref,
                  lng_ref, lnb_ref, ones_ref, o_ref):
    x = x_ref[0]
    mod = mod_ref[0, 0]
    xm = (_ln(x) * (1.0 + mod[1:2, :]) + mod[0:1, :]).astype(BF16)
    ys = u_ref[...] * s5d_ref[...] + sf_ref[...] + sb_ref[...]
    z = _dot(_gelu(ys).astype(BF16), wglu_ref[...]) + bglu_ref[...]
    ys = z[:, :512] * _sigmoid(z[:, 512:])
    hm = hf_ref[0] + hb_ref[0]
    ss = _dot((hm * hm).astype(BF16), ones_ref[...])
    ym = hm * lax.rsqrt(ss * (1.0 / ML_HD) + LN_EPS) * mlg_ref[...] * _sigmoid(mo_ref[0])
    branches = (yd_ref[0], ys.astype(BF16), ym.astype(BF16), yg_ref[0])
    merged = None
    for j, yb in enumerate(branches):
        gate = _sigmoid(_dot(xm, wgate_ref[:, j * D_MODEL:(j + 1) * D_MODEL]) + bgate_ref[:, j * D_MODEL:(j + 1) * D_MODEL])
        term = gate * _dot(yb, wbr_ref[j])
        merged = term if merged is None else merged + term
    mix = _dot(merged.astype(BF16), wout_ref[...])
    o_ref[0] = _ln(ALPHA * x + mod[2:3, :] * mix) * lng_ref[...] + lnb_ref[...]


def merge(xa, mod, yd, yg, u, sf, sb, hf, hb, mo, s5d, wglu, bglu, mlg, wgate, bgate, wbr, wout, lng, lnb, ones_ml, nct):
    B, S, D = xa.shape
    tok = lambda n: pl.BlockSpec((1, TM, n), lambda b, i: (b, i, 0))
    consts = (s5d, wglu, bglu, mlg, wgate, bgate, wbr, wout, lng, lnb, ones_ml)
    return pl.pallas_call(
        _merge_kernel, out_shape=jax.ShapeDtypeStruct((B, S, D), F32), grid=(B, S // TM),
        in_specs=[tok(D), pl.BlockSpec((1, 1, 6, D), lambda b, i: (b, jnp.where(i >= nct, 1, 0), 0, 0))]
        + [tok(512)] * 2 + [pl.BlockSpec((TM, 512), lambda b, i: (i, b))] * 3 + [tok(512)] * 3
        + [_const_spec(a.shape) for a in consts],
        out_specs=tok(D),
        compiler_params=_cparams(("parallel", "parallel")), name="merge")(xa, mod, yd, yg, u, sf, sb, hf, hb, mo, *consts)


def _top16(s, val_ref, idx_ref, payload=None):
    R = s.shape[0]
    ri = lax.broadcasted_iota(I32, s.shape, 0).astype(F32)

    def body(kk, s):
        m = jnp.max(s, axis=0, keepdims=True)
        ix = jnp.min(jnp.where(s == m, ri, float(R)), axis=0, keepdims=True)
        hit = ri == ix
        val_ref[pl.ds(kk, 1), :] = m
        idx_ref[pl.ds(kk, 1), :] = ix if payload is None else jnp.max(jnp.where(hit, payload, -1.0), axis=0, keepdims=True)
        return jnp.where(hit, -jnp.inf, s)

    lax.fori_loop(0, PEER_TOPK, body, s)


_PEER_CAND_ROWS = ((0, 16),) + tuple((16 + 8 * (p - 1), 8) for p in range(1, 8))
_PEER_NCAND = _PEER_CAND_ROWS[-1][0] + 16


def _peer_route_kernel(x_ref, mod_ref, wq_ref, sk_ref, t_ref, ids_ref, gate_ref,
                       xm_sc, v1_sc, i1_sc, v2_sc, i2_sc, cand_sc, cidx_sc, top_sc, tid_sc):
    h = pl.program_id(2)

    @pl.when(h == 0)
    def _():
        mod = mod_ref[0, 0]
        xm = (_ln(x_ref[0]) * (1.0 + mod[4:5, :]) + mod[3:4, :]).astype(BF16)
        xm_sc[...] = xm
        t_ref[0] = xm

    q = _dot(xm_sc[...], wq_ref[...]).astype(BF16)
    half = PEER_DQ // 2
    _top16(_dot_nt(sk_ref[0], q[:, :half]), v1_sc, i1_sc)
    _top16(_dot_nt(sk_ref[1], q[:, half:]), v2_sc, i2_sc)
    nk = float(PEER_NKEYS)
    for p, (lo, n) in enumerate(_PEER_CAND_ROWS):
        cand_sc[lo:lo + n, :] = v1_sc[p:p + 1, :] + v2_sc[0:n, :]
        cidx_sc[lo:lo + n, :] = i1_sc[p:p + 1, :] * nk + i2_sc[0:n, :]
    lo = _PEER_CAND_ROWS[-1][0] + _PEER_CAND_ROWS[-1][1]
    cand_sc[lo:lo + 8, :] = v1_sc[8:16, :] + v2_sc[0:1, :]
    cidx_sc[lo:lo + 8, :] = i1_sc[8:16, :] * nk + i2_sc[0:1, :]
    _top16(cand_sc[...], top_sc, tid_sc, payload=cidx_sc[...])
    top = top_sc[...]
    e = jnp.exp(top - top[0:1, :])
    gate_ref[...] = e / jnp.sum(e, axis=0, keepdims=True)
    ids_ref[...] = tid_sc[...].astype(I32)


def peer_route(h1, mod, wq, subkeys, nct):
    B, S, D = h1.shape
    nt = S // TM
    f = lambda n: pltpu.VMEM((n, TM), F32)
    return pl.pallas_call(
        _peer_route_kernel,
        out_shape=[jax.ShapeDtypeStruct((B, S, D), BF16),
                   jax.ShapeDtypeStruct((PEER_HEADS * PEER_TOPK, B * S), I32),
                   jax.ShapeDtypeStruct((PEER_HEADS * PEER_TOPK, B * S), F32)],
        grid=(B, nt, PEER_HEADS),
        in_specs=[pl.BlockSpec((1, TM, D), lambda b, i, h: (b, i, 0)),
                  pl.BlockSpec((1, 1, 6, D), lambda b, i, h: (b, jnp.where(i >= nct, 1, 0), 0, 0)),
                  pl.BlockSpec((D, PEER_DQ), lambda b, i, h: (0, h)),
                  pl.BlockSpec((2, PEER_NKEYS, PEER_DQ // 2), lambda b, i, h: (0, 0, 0))],
        out_specs=[pl.BlockSpec((1, TM, D), lambda b, i, h: (b, i, 0)),
                   pl.BlockSpec((PEER_TOPK, TM), lambda b, i, h: (h, b * nt + i)),
                   pl.BlockSpec((PEER_TOPK, TM), lambda b, i, h: (h, b * nt + i))],
        scratch_shapes=[pltpu.VMEM((TM, D), BF16), f(16), f(16), f(16), f(16), f(_PEER_NCAND), f(_PEER_NCAND),
                        f(16), f(16)],
        compiler_params=_cparams(("parallel", "parallel", "arbitrary")), name="peer_route")(h1, mod, wq, subkeys)


def _peer_expert_kernel(t_ref, ids_ref, gate_ref, u_ref, v_ref, x_ref, mod_ref, lng_ref, lnb_ref, o_ref,
                        a_sc, w_sc, wgt_sc, acc_sc, *, nchunk):
    j = pl.program_id(1)
    NK = PEER_NKEYS

    @pl.when(j < nchunk)
    def _():
        a = _dot(t_ref[...], u_ref[...])
        a3 = jnp.stack([a[:, al * NK:(al + 1) * NK] for al in range(PEER_CA)], axis=0)
        a_sc[:, pl.ds(pl.multiple_of(j * PEER_CA, PEER_CA), PEER_CA), :] = pltpu.einshape("atb->tab", a3)

    @pl.when(j == nchunk - 1)
    def _():
        io_b = lax.broadcasted_iota(I32, (2 * NK, 2 * NK), 0)
        io_a = lax.broadcasted_iota(I32, (NK, 2 * NK), 0)
        second = jnp.where(lax.broadcasted_iota(I32, (1, 2 * NK), 1) >= NK, NK, 0)

        def masks(two):
            idp = ids_ref[two, :]
            ids = jnp.concatenate([idp[0:1], idp[1:2]], axis=1)
            onehot = jnp.where(io_b == (ids & (NK - 1)) + second, 1.0, 0.0).astype(BF16)
            return onehot, io_a == (ids >> 7)

        def pick(p, carry):
            two = pl.ds(pl.multiple_of(p * 2, 2), 2)
            onehot, r1 = masks(two)
            a2 = a_sc[two]
            at = jnp.concatenate([a2[0], a2[1]], axis=1).astype(BF16)
            picked = _dot(at, onehot)
            act = jnp.sum(jnp.where(r1, picked, 0.0), axis=0, keepdims=True)
            gp = gate_ref[two, :]
            w = jnp.concatenate([gp[0:1], gp[1:2]], axis=1) * _gelu(act)
            wgt_sc[two, :] = jnp.concatenate([w[:, :NK], w[:, NK:]], axis=0)
            return carry

        def scatter(p, carry):
            two = pl.ds(pl.multiple_of(p * 2, 2), 2)
            onehot, r1 = masks(two)
            wp = wgt_sc[two, :]
            w = jnp.concatenate([wp[0:1], wp[1:2]], axis=1)
            wt = _dot_nt(jnp.where(r1, w, 0.0).astype(BF16), onehot)
            w_sc[two] = jnp.stack([wt[:, :NK], wt[:, NK:]], axis=0)
            return carry

        lax.fori_loop(0, TM // 2, pick, 0, unroll=16)
        lax.fori_loop(0, TM // 2, scatter, 0, unroll=16)
        acc_sc[...] = jnp.zeros_like(acc_sc)

    @pl.when(j >= nchunk)
    def _():
        jj = j - nchunk
        w3 = pltpu.einshape("tab->atb", w_sc[:, pl.ds(pl.multiple_of(jj * PEER_CA, PEER_CA), PEER_CA), :])
        w = jnp.concatenate([w3[al] for al in range(PEER_CA)], axis=1).astype(BF16)
        acc_sc[...] += _dot(w, v_ref[...])

    @pl.when(j == 2 * nchunk - 1)
    def _():
        mod = mod_ref[0, 0]
        o_ref[...] = _ln(ALPHA * x_ref[...] + mod[5:6, :] * acc_sc[...]) * lng_ref[...] + lnb_ref[...]


def peer_experts(t, ids, gates, emb_ut, emb_v, h1, mod, lng, lnb, nt_per_sample, nct):
    T, D = h1.shape
    E = emb_v.shape[0]
    ce = PEER_CA * PEER_NKEYS
    nchunk = E // ce
    tok = lambda n: pl.BlockSpec((TM, n), lambda i, j: (i, 0))
    return pl.pallas_call(
        functools.partial(_peer_expert_kernel, nchunk=nchunk),
        out_shape=jax.ShapeDtypeStruct((T, D), F32), grid=(T // TM, 2 * nchunk),
        in_specs=[tok(D), tok(LANES), tok(LANES),
                  pl.BlockSpec((D, ce), lambda i, j: (0, jnp.minimum(j, nchunk - 1))),
                  pl.BlockSpec((ce, D), lambda i, j: (jnp.maximum(j - nchunk, 0), 0)),
                  tok(D),
                  pl.BlockSpec((1, 1, 6, D), lambda i, j: (i // nt_per_sample, jnp.where(i % nt_per_sample >= nct, 1, 0), 0, 0)),
                  pl.BlockSpec((1, D), lambda i, j: (0, 0)), pl.BlockSpec((1, D), lambda i, j: (0, 0))],
        out_specs=tok(D),
        scratch_shapes=[pltpu.VMEM((TM, PEER_NKEYS, PEER_NKEYS), F32), pltpu.VMEM((TM, PEER_NKEYS, PEER_NKEYS), F32),
                        pltpu.VMEM((TM, LANES), F32), pltpu.VMEM((TM, D), F32)],
        compiler_params=_cparams(("parallel", "arbitrary")), name="peer_experts")(t, ids, gates, emb_ut, emb_v, h1, mod, lng, lnb)


_ROPE_IDX = np.arange(HD)
_ROPE_PERM = np.where(_ROPE_IDX % 32 < 16, _ROPE_IDX + 16, _ROPE_IDX - 16)
_ROPE_SIGN = np.where(_ROPE_IDX % 32 < 16, -1.0, 1.0).astype(np.float32)


def _rope_partner(w):
    n = w.shape[1] // HD
    perm = np.concatenate([h * HD + _ROPE_PERM for h in range(n)])
    return w[:, perm] * jnp.asarray(np.tile(_ROPE_SIGN, n))


def _rope_tables(L, nctx, gq, gk):
    rows = L // GRID_W
    row = jnp.repeat(jnp.arange(rows, dtype=F32), GRID_W)
    col = jnp.tile(jnp.arange(GRID_W, dtype=F32), rows)
    nf = HD // 4
    inv = ROPE_THETA ** (-jnp.arange(nf, dtype=F32) / nf)
    ar, ac = row[:, None] * inv, col[:, None] * inv
    cos = jnp.concatenate([jnp.cos(ar), jnp.cos(ar), jnp.cos(ac), jnp.cos(ac)], axis=1)
    sin = jnp.concatenate([jnp.sin(ar), jnp.sin(ar), jnp.sin(ac), jnp.sin(ac)], axis=1)
    cos = jnp.concatenate([jnp.ones((nctx, HD), F32), cos], axis=0)
    sin = jnp.concatenate([jnp.zeros((nctx, HD), F32), sin], axis=0)
    scale = HD ** -0.5
    one = jnp.ones((HD,), F32)
    tabs = []
    for g, sc in ((one, scale), (one, 1.0), (gq, scale), (gk, 1.0)):
        tabs += [cos * (g * sc), sin * (g[_ROPE_PERM] * sc)]
    return jnp.tile(jnp.stack(tabs), (1, 1, 2))


def _s5_tables(a_re, a_im, log_dt, b_re, b_im, c_re, c_im, nb):
    dt = jnp.exp(log_dt)[..., None]
    mag = jnp.exp(a_re * dt)
    lr, li = mag * jnp.cos(a_im * dt), mag * jnp.sin(a_im * dt)
    den = a_re * a_re + a_im * a_im
    cr = ((lr - 1) * a_re + li * a_im) / den
    ci = (li * a_re - (lr - 1) * a_im) / den
    br = cr[..., None] * b_re - ci[..., None] * b_im
    bi = cr[..., None] * b_im + ci[..., None] * b_re
    G2, N, C = S5_GROUPS // 2, S5_STATE, S5_GROUP
    eye = jnp.eye(G2, dtype=F32)

    def blockdiag_in(m):
        m = m.reshape(2, 2, G2, N, C)
        return jnp.einsum('dhgnc,gk->dhgckn', m, eye).reshape(2, 2, G2 * C, G2 * N)

    def blockdiag_out(m):
        m = m.reshape(2, 2, G2, C, N)
        return jnp.einsum('dhgcn,gk->dhgnkc', m, eye).reshape(2, 2, G2 * N, G2 * C)

    bblk = jnp.concatenate([blockdiag_in(br), blockdiag_in(bi)], axis=-1).astype(BF16)
    cblk = jnp.concatenate([blockdiag_out(c_re), blockdiag_out(-c_im)], axis=-2).astype(BF16)
    lam = jnp.stack([lr, li], axis=1).reshape(2, 2, 2, G2 * N)
    lam = jnp.tile(lam[:, :, None], (1, 1, nb, 1, 1)).reshape(2, 2, 2 * nb, G2 * N)
    return bblk, cblk, lam


def _blockdiag_ones(group, n=LANES):
    i = np.arange(n) // group
    return jnp.asarray((i[:, None] == i[None, :]).astype(np.float32), dtype=BF16)


def kernel(x, c, ctx, c_ctx, ada_w, ada_b, w_in, b_gate, diff_lam, diff_norm_g, gqa_qnorm_g, gqa_knorm_g,
           s5_a_re, s5_a_im, s5_log_dt, s5_b_re, s5_b_im, s5_c_re, s5_c_im, s5_d, s5_w_glu, s5_b_glu,
           ml_conv_w, ml_conv_b, ml_gate_b, ml_norm_g, w_branch, w_out, ln_mix_g, ln_mix_b, ln_ffn_g, ln_ffn_b,
           peer_wq, peer_subkeys, peer_u, peer_v):
    B, L, D = x.shape
    nctx = ctx.shape[1]
    S = nctx + L
    assert D == D_MODEL and nctx % TM == 0 and L % TM == 0 and L % GRID_W == 0
    nt, nct = S // TM, nctx // TM
    depth = ada_w.shape[0]

    h = jnp.concatenate([ctx, x], axis=1)
    R = -(-(B + 1) // 8) * 8
    cond = jnp.zeros((R, D), F32).at[:B].set(c).at[B].set(c_ctx)
    ones64, ones128 = _blockdiag_ones(HD), _blockdiag_ones(ML_HD, ML_HEADS * ML_HD)
    o = IN_OFFS

    for l in range(depth):
        lam_init = 0.8 - 0.6 * math.exp(-0.3 * l)
        m = ada_modulation(cond, ada_w[l].astype(BF16), ada_b[l][None, :])
        mod = jnp.stack([jnp.broadcast_to(m[B], (B, 6 * D)), m[:B]], axis=1).reshape(B, 2, 6, D)

        w = w_in[l]
        seg = lambda i: w[:, o[i]:o[i + 1]]
        dup = lambda t: jnp.concatenate([t[:, :HD], t[:, :HD], t[:, HD:], t[:, HD:]], axis=1)
        w_a = jnp.concatenate([seg(0), _rope_partner(seg(0)), seg(1), _rope_partner(seg(1)), seg(2),
                               seg(9), _rope_partner(seg(9)), dup(seg(10)), dup(_rope_partner(seg(10))),
                               dup(seg(11))], axis=1).astype(BF16)
        w_b = jnp.concatenate([seg(3), seg(4), seg(5), seg(6), seg(7), seg(8),
                               jnp.zeros((D, LANES - 16), F32)], axis=1).astype(BF16)
        gate_b = jnp.concatenate([ml_gate_b[l], jnp.zeros((LANES - 16,), F32)])[None, :]
        tab = _rope_tables(L, nctx, gqa_qnorm_g[l], gqa_knorm_g[l])

        dq, dk, dv, gq, gk, gv = proj_attn(h, mod, w_a, tab, ones64, nct)
        u, mqk, mv, mo, mg = proj_seq(h, mod, w_b, gate_b, nct)

        yd = diff_attention(dq, dk, dv, diff_lam[l], diff_norm_g[l][None, :], nctx, lam_init)
        yg = gqa_attention(gq, gk, gv, nctx)

        bblk, cblk, lam = _s5_tables(s5_a_re[l], s5_a_im[l], s5_log_dt[l], s5_b_re[l], s5_b_im[l],
                                     s5_c_re[l], s5_c_im[l], B)
        sf, sb = (y.reshape(u.shape) for y in s5_scan(u.reshape(S * 2 * B, 256), bblk, cblk, lam, S, nctx))

        mq, mk = ml_prep(mqk, ml_conv_w[l], ml_conv_b[l][None, :], nctx)
        hf = mlstm_scan(mq, mk, mv, mg, nctx, False)
        hb = mlstm_scan(mq, mk, mv, mg, nctx, True)

        h1 = merge(h, mod, yd, yg, u, sf, sb, hf, hb, mo,
                   s5_d[l][None, :], s5_w_glu[l].astype(BF16), s5_b_glu[l][None, :], ml_norm_g[l][None, :],
                   seg(12).astype(BF16), b_gate[l][None, :], w_branch[l].astype(BF16), w_out[l].astype(BF16),
                   ln_mix_g[l][None, :], ln_mix_b[l][None, :], ones128, nct)

        t, ids, gates = peer_route(h1, mod, peer_wq[l].astype(BF16), peer_subkeys[l].astype(BF16), nct)
        h = peer_experts(t.reshape(B * S, D), ids.T, gates.T, peer_u[l].astype(BF16).T, peer_v[l].astype(BF16),
                         h1.reshape(B * S, D), mod, ln_ffn_g[l][None, :], ln_ffn_b[l][None, :], nt, nct).reshape(B, S, D)
    return h[:, nctx:]
```

```python
import functools
import math

import numpy as np
import jax
import jax.numpy as jnp
from jax import lax
from jax.experimental import pallas as pl
from jax.experimental.pallas import tpu as pltpu

F32 = jnp.float32
BF16 = jnp.bfloat16
I32 = jnp.int32

D_MODEL = 1024
DEPTH = 2
GRID_W = 64
ROPE_THETA = 10000.0
LN_EPS = 1e-6
HD = 64
DIFF_HEADS = 4
GQA_HEADS = 8
GQA_KV = 2
S5_GROUP = 16
S5_GROUPS = 32
S5_STATE = 64
ML_HEADS = 4
ML_HD = 128
N_BRANCH = 4
W_BRANCH = 512
PEER_HEADS = 8
PEER_NKEYS = 128
PEER_TOPK = 16
PEER_DQ = 256
ALPHA = (2 * DEPTH) ** 0.25

LANES = 128
TM = 256
TM_E = 2 * TM
TQ = 256
TK = 2048
S5_TC = 128
ML_T = 256
PEER_CA = 8
VMEM_LIMIT = 56 * 1024 * 1024

IN_SPLITS = (512, 512, 512, 512, 512, 512, 512, 512, 16, 512, 128, 128, N_BRANCH * D_MODEL)
IN_OFFS = tuple(int(v) for v in np.cumsum((0,) + IN_SPLITS))


def _cparams(sem):
    return pltpu.CompilerParams(dimension_semantics=sem, vmem_limit_bytes=VMEM_LIMIT)


def _const_spec(shape):
    nd = len(shape)
    return pl.BlockSpec(shape, lambda *_: (0,) * nd, pipeline_mode=pl.Buffered(1))


def _ln(x):
    xc = x - jnp.mean(x, axis=-1, keepdims=True)
    return xc * lax.rsqrt(jnp.mean(xc * xc, axis=-1, keepdims=True) + LN_EPS)


def _sigmoid(x):
    return 1.0 / (1.0 + jnp.exp(-x))


def _gelu(x):
    return 0.5 * x * (1.0 + lax.erf(x * (2.0 ** -0.5)))


def _dot(a, b):
    return jnp.dot(a, b, preferred_element_type=F32)


def _dot_nt(a, b):
    return lax.dot_general(a, b, (((1,), (1,)), ((), ())), preferred_element_type=F32)


def _dot_tn(a, b):
    return lax.dot_general(a, b, (((0,), (0,)), ((), ())), preferred_element_type=F32)


def _ada_kernel(c_ref, w_ref, b_ref, o_ref):
    c = c_ref[...]
    o_ref[...] = _dot((c * _sigmoid(c)).astype(BF16), w_ref[...]) + b_ref[...]


def ada_modulation(cond, w, b):
    R, D = cond.shape
    N = w.shape[1]
    tn = 1536
    return pl.pallas_call(
        _ada_kernel, out_shape=jax.ShapeDtypeStruct((R, N), F32), grid=(N // tn,),
        in_specs=[pl.BlockSpec((R, D), lambda j: (0, 0)), pl.BlockSpec((D, tn), lambda j: (0, j)),
                  pl.BlockSpec((1, tn), lambda j: (0, j))],
        out_specs=pl.BlockSpec((R, tn), lambda j: (0, j)),
        compiler_params=_cparams(("arbitrary",)), name="ada")(cond, w, b)


def _proj_attn_kernel(x_ref, mod_ref, w_ref, tab_ref, ones_ref,
                      dq_ref, dk_ref, dv_ref, gq_ref, gk_ref, gv_ref):
    mod = mod_ref[0, 0]
    xm = (_ln(x_ref[0]) * (1.0 + mod[1:2, :]) + mod[0:1, :]).astype(BF16)

    def mm(lo, n):
        return _dot(xm, w_ref[:, lo:lo + n])

    def rope_store(ref, lo, n, ci, norm):
        t, tp = mm(lo, n), mm(lo + n, n)
        c, s = tab_ref[ci], tab_ref[ci + 1]
        for j in range(n // LANES):
            sl = slice(j * LANES, (j + 1) * LANES)
            tb = t[:, sl]
            y = tb * c + tp[:, sl] * s
            if norm:
                ss = _dot((tb * tb).astype(BF16), ones_ref[...])
                y = y * lax.rsqrt(ss * (1.0 / HD) + LN_EPS)
            ref[0, :, sl] = y.astype(ref.dtype)

    rope_store(dq_ref, 0, 512, 0, False)
    rope_store(dk_ref, 1024, 512, 2, False)
    dv_ref[0] = mm(2048, 512).astype(dv_ref.dtype)
    rope_store(gq_ref, 2560, 512, 4, True)
    rope_store(gk_ref, 3584, 256, 6, True)
    gv_ref[0] = mm(4096, 256).astype(gv_ref.dtype)


def proj_attn(xa, mod, w_a, tab, ones_bd, nct):
    B, S, D = xa.shape
    tok = lambda n: pl.BlockSpec((1, TM, n), lambda b, i: (b, i, 0))
    outs = [jax.ShapeDtypeStruct((B, S, n), BF16) for n in (512, 512, 512, 512, 256, 256)]
    return pl.pallas_call(
        _proj_attn_kernel, out_shape=outs, grid=(B, S // TM),
        in_specs=[tok(D),
                  pl.BlockSpec((1, 1, 6, D), lambda b, i: (b, jnp.where(i >= nct, 1, 0), 0, 0)),
                  _const_spec(w_a.shape),
                  pl.BlockSpec((8, TM, LANES), lambda b, i: (0, i, 0)),
                  _const_spec(ones_bd.shape)],
        out_specs=[tok(n) for n in (512, 512, 512, 512, 256, 256)],
        compiler_params=_cparams(("parallel", "parallel")), name="proj_attn")(xa, mod, w_a, tab, ones_bd)


def _proj_seq_kernel(x_ref, mod_ref, w_ref, gb_ref, u_ref, qk_ref, v_ref, o_ref, g_ref):
    mod = mod_ref[0, 0]
    xm = (_ln(x_ref[0]) * (1.0 + mod[1:2, :]) + mod[0:1, :]).astype(BF16)
    u_ref[...] = _dot(xm, w_ref[:, 0:512])
    qk_ref[0] = _dot(xm, w_ref[:, 512:1536])
    v_ref[0] = _dot(xm, w_ref[:, 1536:2048])
    o_ref[0] = _dot(xm, w_ref[:, 2048:2560])
    g_ref[0] = _dot(xm, w_ref[:, 2560:2688]) + gb_ref[...]


def proj_seq(xa, mod, w_b, gate_b, nct):
    B, S, D = xa.shape
    tok = lambda n: pl.BlockSpec((1, TM, n), lambda b, i: (b, i, 0))
    widths = (1024, 512, 512, LANES)
    return pl.pallas_call(
        _proj_seq_kernel,
        out_shape=[jax.ShapeDtypeStruct((S, B * 512), F32)] + [jax.ShapeDtypeStruct((B, S, n), F32) for n in widths],
        grid=(B, S // TM),
        in_specs=[tok(D),
                  pl.BlockSpec((1, 1, 6, D), lambda b, i: (b, jnp.where(i >= nct, 1, 0), 0, 0)),
                  _const_spec(w_b.shape), _const_spec(gate_b.shape)],
        out_specs=[pl.BlockSpec((TM, 512), lambda b, i: (i, b))] + [tok(n) for n in widths],
        compiler_params=_cparams(("parallel", "parallel")), name="proj_seq")(xa, mod, w_b, gate_b)


def _flash(qq, k_ref, v_ref, nctx, n_lat, tk):
    R = qq.shape[0]

    def chunk(carry, rows):
        m, l, acc = carry
        s = _dot_nt(qq, k_ref[0, rows, :])
        m_new = jnp.maximum(m, jnp.max(s, axis=1, keepdims=True))
        alpha = jnp.exp(m - m_new)
        p = jnp.exp(s - m_new)
        l = alpha * l + jnp.sum(p, axis=1, keepdims=True)
        acc = alpha * acc + _dot(p.astype(BF16), v_ref[0, rows, :])
        return m_new, l, acc

    init = (jnp.full((R, 1), -jnp.inf, F32), jnp.zeros((R, 1), F32), jnp.zeros((R, LANES), F32))
    carry = chunk(init, pl.ds(0, nctx))
    _, l, acc = lax.fori_loop(
        0, n_lat, lambda c, carry: chunk(carry, pl.ds(pl.multiple_of(nctx + c * tk, LANES), tk)), carry)
    return acc / l


def _diff_attn_kernel(q_ref, k_ref, v_ref, lam_ref, g_ref, o_ref, *, nctx, n_lat, tk, lam_init):
    i = pl.program_id(2)
    q = q_ref[0]
    lane = lax.broadcasted_iota(I32, q.shape, 1)
    zero = jnp.zeros_like(q)
    qq = jnp.concatenate([jnp.where(lane < HD, q, zero), jnp.where(lane >= HD, q, zero)], axis=0)
    o = _flash(qq, k_ref, v_ref, nctx, jnp.where(i < nctx // TQ, 0, n_lat), tk)
    lv = lam_ref[...]
    lam = (jnp.exp(jnp.sum(lv[0:1] * lv[1:2], axis=1, keepdims=True))
           - jnp.exp(jnp.sum(lv[2:3] * lv[3:4], axis=1, keepdims=True)) + lam_init)
    d = o[:TQ] - lam * o[TQ:]
    y = d * lax.rsqrt(jnp.mean(d * d, axis=-1, keepdims=True) + LN_EPS)
    o_ref[0] = (y * g_ref[...] * (1.0 - lam_init)).astype(o_ref.dtype)


def diff_attention(q, k, v, lam_vec, norm_g, nctx, lam_init):
    B, S, _ = q.shape
    tk = math.gcd(S - nctx, TK)
    kern = functools.partial(_diff_attn_kernel, nctx=nctx, n_lat=(S - nctx) // tk, tk=tk, lam_init=lam_init)
    return pl.pallas_call(
        kern, out_shape=jax.ShapeDtypeStruct((B, S, 512), BF16), grid=(B, DIFF_HEADS, S // TQ),
        in_specs=[pl.BlockSpec((1, TQ, LANES), lambda b, h, i: (b, i, h)),
                  pl.BlockSpec((1, S, LANES), lambda b, h, i: (b, 0, h)),
                  pl.BlockSpec((1, S, LANES), lambda b, h, i: (b, 0, h)),
                  pl.BlockSpec((4, HD), lambda b, h, i: (0, 0)),
                  pl.BlockSpec((1, LANES), lambda b, h, i: (0, 0))],
        out_specs=pl.BlockSpec((1, TQ, LANES), lambda b, h, i: (b, i, h)),
        compiler_params=_cparams(("parallel", "parallel", "parallel")), name="diff_attn")(q, k, v, lam_vec, norm_g)


def _gqa_attn_kernel(q_ref, k_ref, v_ref, o_ref, *, nctx, n_lat, tk):
    i = pl.program_id(2)
    q = q_ref[0]
    lane = lax.broadcasted_iota(I32, (TQ, LANES), 1)
    zero = jnp.zeros((TQ, LANES), q.dtype)
    parts = []
    for j in range(2):
        blk = q[:, j * LANES:(j + 1) * LANES]
        parts += [jnp.where(lane < HD, blk, zero), jnp.where(lane >= HD, blk, zero)]
    o = _flash(jnp.concatenate(parts, axis=0), k_ref, v_ref, nctx, jnp.where(i < nctx // TQ, 0, n_lat), tk)
    for j in range(2):
        pair = jnp.where(lane < HD, o[(2 * j) * TQ:(2 * j + 1) * TQ], o[(2 * j + 1) * TQ:(2 * j + 2) * TQ])
        o_ref[0, :, j * LANES:(j + 1) * LANES] = pair.astype(o_ref.dtype)


def gqa_attention(q, k, v, nctx):
    B, S, _ = q.shape
    tk = math.gcd(S - nctx, TK)
    kern = functools.partial(_gqa_attn_kernel, nctx=nctx, n_lat=(S - nctx) // tk, tk=tk)
    return pl.pallas_call(
        kern, out_shape=jax.ShapeDtypeStruct((B, S, 512), BF16), grid=(B, GQA_KV, S // TQ),
        in_specs=[pl.BlockSpec((1, TQ, 2 * LANES), lambda b, g, i: (b, i, g)),
                  pl.BlockSpec((1, S, LANES), lambda b, g, i: (b, 0, g)),
                  pl.BlockSpec((1, S, LANES), lambda b, g, i: (b, 0, g))],
        out_specs=pl.BlockSpec((1, TQ, 2 * LANES), lambda b, g, i: (b, i, g)),
        compiler_params=_cparams(("parallel", "parallel", "parallel")), name="gqa_attn")(q, k, v)


def _s5_kernel(uf_ref, ub_ref, bblk_ref, cblk_ref, lam_ref, yf_ref, yb_ref, buf_f, buf_b, st_ref):
    H = S5_GROUPS * S5_STATE // 2
    R = buf_f.shape[0]
    RB = 256

    @pl.when(pl.program_id(0) == 0)
    def _():
        st_ref[...] = jnp.zeros_like(st_ref)

    half0 = (lax.broadcasted_iota(I32, (RB, 1), 0) & 1) == 0

    for d, (u_ref, buf) in enumerate(((uf_ref, buf_f), (ub_ref, buf_b))):
        for rb in range(R // RB):
            rows = slice(rb * RB, (rb + 1) * RB)
            uh = u_ref[rows, :].astype(BF16)
            for cols in (slice(0, H), slice(H, 2 * H)):
                buf[rows, cols] = jnp.where(half0, _dot(uh, bblk_ref[d, 0, :, cols]), _dot(uh, bblk_ref[d, 1, :, cols]))

    lfr, lfi, lbr, lbi = lam_ref[0, 0], lam_ref[0, 1], lam_ref[1, 0], lam_ref[1, 1]

    def step(t, carry):
        fr, fi, br, bi = carry
        rf = pl.ds(pl.multiple_of(t * 8, 8), 8)
        x = buf_f[rf, :]
        nfr = lfr * fr - lfi * fi + x[:, :H]
        nfi = lfr * fi + lfi * fr + x[:, H:]
        buf_f[rf, :] = jnp.concatenate([nfr, nfi], axis=1)
        rb = pl.ds(pl.multiple_of((S5_TC - 1 - t) * 8, 8), 8)
        z = buf_b[rb, :]
        nbr = lbr * br - lbi * bi + z[:, :H]
        nbi = lbr * bi + lbi * br + z[:, H:]
        buf_b[rb, :] = jnp.concatenate([nbr, nbi], axis=1)
        return nfr, nfi, nbr, nbi

    fin = lax.fori_loop(0, S5_TC, step, (st_ref[0], st_ref[1], st_ref[2], st_ref[3]))
    for j in range(4):
        st_ref[j] = fin[j]

    for d, (y_ref, buf) in enumerate(((yf_ref, buf_f), (yb_ref, buf_b))):
        for rb in range(R // RB):
            rows = slice(rb * RB, (rb + 1) * RB)
            h = buf[rows, :].astype(BF16)
            y_ref[rows, :] = jnp.where(half0, _dot(h, cblk_ref[d, 0]), _dot(h, cblk_ref[d, 1]))


def s5_scan(u, bblk, cblk, lam, S, nctx):
    assert u.shape[0] == S * 8, "the scan packs (sample, half) pairs into the eight sublanes of a vreg"
    nch, nc0 = S // S5_TC, nctx // S5_TC

    def bwd(i):
        return jnp.where(i < nc0, nc0 - 1 - i, (nch - 1) - (i - nc0))
    H2 = S5_GROUPS * S5_STATE
    R, W = S5_TC * 8, u.shape[1]
    blk = lambda f: pl.BlockSpec((R, W), f)
    return pl.pallas_call(
        _s5_kernel, out_shape=[jax.ShapeDtypeStruct(u.shape, F32)] * 2, grid=(nch,),
        in_specs=[blk(lambda i: (i, 0)), blk(lambda i: (bwd(i), 0)),
                  _const_spec(bblk.shape), _const_spec(cblk.shape), _const_spec(lam.shape)],
        out_specs=[blk(lambda i: (i, 0)), blk(lambda i: (bwd(i), 0))],
        scratch_shapes=[pltpu.VMEM((R, H2), F32), pltpu.VMEM((R, H2), F32), pltpu.VMEM((4, 8, H2 // 2), F32)],
        compiler_params=_cparams(("arbitrary",)), name="s5_scan")(u, u, bblk, cblk, lam)


def _ml_prep_kernel(x_ref, prev_ref, next_ref, w_ref, b_ref, q_ref, k_ref, *, seg_starts, seg_ends):
    i = pl.program_id(1)
    x = x_ref[0]
    row = lax.broadcasted_iota(I32, x.shape, 0)
    first = functools.reduce(jnp.logical_or, [i == s for s in seg_starts])
    last = functools.reduce(jnp.logical_or, [i == s for s in seg_ends])
    pr = jnp.where(first, 0.0, prev_ref[0, 7:8, :])
    nx = jnp.where(last, 0.0, next_ref[0, 0:1, :])
    xp = jnp.where(row == 0, pr, pltpu.roll(x, 1, 0))
    xn = jnp.where(row == TM - 1, nx, pltpu.roll(x, TM - 1, 0))
    w = w_ref[...]
    y = b_ref[...] + xp * w[0:1, :] + x * w[1:2, :] + xn * w[2:3, :]
    y = y * _sigmoid(y)
    q_ref[0] = y[:, :512].astype(q_ref.dtype)
    k_ref[0] = (y[:, 512:] * (ML_HD ** -0.5)).astype(k_ref.dtype)


def ml_prep(qk, conv_w, conv_b, nctx):
    B, S, W = qk.shape
    nt, nct, r8 = S // TM, nctx // TM, TM // 8
    kern = functools.partial(_ml_prep_kernel, seg_starts=(0, nct), seg_ends=(nct - 1, nt - 1))
    return pl.pallas_call(
        kern, out_shape=[jax.ShapeDtypeStruct((B, S, 512), BF16)] * 2, grid=(B, nt),
        in_specs=[pl.BlockSpec((1, TM, W), lambda b, i: (b, i, 0)),
                  pl.BlockSpec((1, 8, W), lambda b, i: (b, jnp.maximum(i * r8 - 1, 0), 0)),
                  pl.BlockSpec((1, 8, W), lambda b, i: (b, jnp.minimum((i + 1) * r8, S // 8 - 1), 0)),
                  pl.BlockSpec((3, W), lambda b, i: (0, 0)), pl.BlockSpec((1, W), lambda b, i: (0, 0))],
        out_specs=[pl.BlockSpec((1, TM, 512), lambda b, i: (b, i, 0))] * 2,
        compiler_params=_cparams(("parallel", "parallel")), name="ml_prep")(qk, qk, qk, conv_w, conv_b)


def _log_sigmoid(x):
    return jnp.minimum(x, 0.0) - jnp.log1p(jnp.exp(-jnp.abs(x)))


def _mlstm_kernel(q_ref, k_ref, v_ref, g_ref, h_ref, c_sc, n_sc, m_sc, *, reverse):
    T = ML_T
    gi, gf = (8, 12) if reverse else (0, 4)

    @pl.when(pl.program_id(1) == 0)
    def _():
        c_sc[...] = jnp.zeros_like(c_sc)
        n_sc[...] = jnp.zeros_like(n_sc)
        m_sc[...] = jnp.zeros_like(m_sc)

    g = g_ref[0]
    gt = g.T
    r = lax.broadcasted_iota(I32, (T, T), 0)
    c = lax.broadcasted_iota(I32, (T, T), 1)
    mask = (c >= r) if reverse else (c <= r)
    tri = jnp.where(mask, 1.0, 0.0)
    hp = lax.Precision.HIGHEST
    bcol_all = jnp.dot(tri, _log_sigmoid(g), preferred_element_type=F32, precision=hp)
    brow_all = lax.dot_general(_log_sigmoid(gt[0:16]), tri, (((1,), (1,)), ((), ())),
                               preferred_element_type=F32, precision=hp)
    end = 0 if reverse else T - 1
    for hh in range(ML_HEADS):
        sl = slice(hh * ML_HD, (hh + 1) * ML_HD)
        q, k, v = q_ref[0, :, sl], k_ref[0, :, sl], v_ref[0, :, sl]
        bcol, brow = bcol_all[:, gf + hh:gf + hh + 1], brow_all[gf + hh:gf + hh + 1, :]
        icol, irow = g[:, gi + hh:gi + hh + 1], gt[gi + hh:gi + hh + 1, :]
        m_old = m_sc[hh][:, 0:1]
        n_old = n_sc[hh]
        c_old = c_sc[hh]
        logw = jnp.where(mask, bcol - brow + irow, -jnp.inf)
        m_inter = bcol + m_old
        m_t = jnp.maximum(m_inter, jnp.max(logw, axis=1, keepdims=True))
        s = _dot_nt(q, k) * jnp.exp(logw - m_t)
        inter = jnp.exp(m_inter - m_t)
        num = _dot(s.astype(BF16), v.astype(BF16)) + inter * _dot_nt(q, c_old.astype(BF16))
        den = jnp.sum(s, axis=1, keepdims=True) + inter * jnp.sum(q.astype(F32) * n_old, axis=1, keepdims=True)
        h_ref[0, :, sl] = num / jnp.maximum(jnp.abs(den), jnp.exp(-m_t))
        b_end = bcol[end:end + 1, :]
        g_row, g_col = b_end - brow + irow, b_end - bcol + icol
        m_new = jnp.maximum(b_end + m_old, jnp.max(g_row, axis=1, keepdims=True))
        decay = jnp.exp(b_end + m_old - m_new)
        wk = jnp.exp(g_col - m_new)
        c_sc[hh] = decay * c_old + _dot_tn((v * wk).astype(BF16), k)
        n_sc[hh] = decay * n_old + jnp.sum(k.astype(F32) * wk, axis=0, keepdims=True)
        m_sc[hh] = jnp.broadcast_to(m_new, (1, LANES))


def mlstm_scan(q, k, v, g, nctx, reverse):
    B, S, W = q.shape
    nch, nc0 = S // ML_T, nctx // ML_T

    def order(i):
        return jnp.where(i < nc0, nc0 - 1 - i, (nch - 1) - (i - nc0)) if reverse else i
    blk = lambda n: pl.BlockSpec((1, ML_T, n), lambda b, i: (b, order(i), 0))
    return pl.pallas_call(
        functools.partial(_mlstm_kernel, reverse=reverse),
        out_shape=jax.ShapeDtypeStruct((B, S, W), F32), grid=(B, nch),
        in_specs=[blk(W), blk(W), blk(W), blk(LANES)], out_specs=blk(W),
        scratch_shapes=[pltpu.VMEM((ML_HEADS, ML_HD, ML_HD), F32), pltpu.VMEM((ML_HEADS, 1, ML_HD), F32),
                        pltpu.VMEM((ML_HEADS, 1, LANES), F32)],
        compiler_params=_cparams(("parallel", "arbitrary")), name="mlstm_bwd" if reverse else "mlstm_fwd")(q, k, v, g)


def _merge_kernel(x_ref, mod_ref, yd_ref, yg_ref, u_ref, sf_ref, sb_ref, hf_ref, hb_ref, mo_ref,
                  s5d_ref, wglu_ref, bglu_ref, mlg_ref, wgate_ref, bgate_ref, wbr_ref, wout_ref,
                  lng_ref, lnb_ref, ones_ref, o_ref):
    x = x_ref[0]
    mod = mod_ref[0, 0]
    xm = (_ln(x) * (1.0 + mod[1:2, :]) + mod[0:1, :]).astype(BF16)
    ys = u_ref[...] * s5d_ref[...] + sf_ref[...] + sb_ref[...]
    z = _dot(_gelu(ys).astype(BF16), wglu_ref[...]) + bglu_ref[...]
    ys = z[:, :512] * _sigmoid(z[:, 512:])
    hm = hf_ref[0] + hb_ref[0]
    ss = _dot((hm * hm).astype(BF16), ones_ref[...])
    ym = hm * lax.rsqrt(ss * (1.0 / ML_HD) + LN_EPS) * mlg_ref[...] * _sigmoid(mo_ref[0])
    branches = (yd_ref[0], ys.astype(BF16), ym.astype(BF16), yg_ref[0])
    merged = None
    for j, yb in enumerate(branches):
        gate = _sigmoid(_dot(xm, wgate_ref[:, j * D_MODEL:(j + 1) * D_MODEL]) + bgate_ref[:, j * D_MODEL:(j + 1) * D_MODEL])
        term = gate * _dot(yb, wbr_ref[j])
        merged = term if merged is None else merged + term
    mix = _dot(merged.astype(BF16), wout_ref[...])
    o_ref[0] = _ln(ALPHA * x + mod[2:3, :] * mix) * lng_ref[...] + lnb_ref[...]


def merge(xa, mod, yd, yg, u, sf, sb, hf, hb, mo, s5d, wglu, bglu, mlg, wgate, bgate, wbr, wout, lng, lnb, ones_ml, nct):
    B, S, D = xa.shape
    tok = lambda n: pl.BlockSpec((1, TM, n), lambda b, i: (b, i, 0))
    consts = (s5d, wglu, bglu, mlg, wgate, bgate, wbr, wout, lng, lnb, ones_ml)
    return pl.pallas_call(
        _merge_kernel, out_shape=jax.ShapeDtypeStruct((B, S, D), F32), grid=(B, S // TM),
        in_specs=[tok(D), pl.BlockSpec((1, 1, 6, D), lambda b, i: (b, jnp.where(i >= nct, 1, 0), 0, 0))]
        + [tok(512)] * 2 + [pl.BlockSpec((TM, 512), lambda b, i: (i, b))] * 3 + [tok(512)] * 3
        + [_const_spec(a.shape) for a in consts],
        out_specs=tok(D),
        compiler_params=_cparams(("parallel", "parallel")), name="merge")(xa, mod, yd, yg, u, sf, sb, hf, hb, mo, *consts)


def _top16(*problems):
    def one(kk, s, val_ref, idx_ref, payload):
        R = s.shape[0]
        ri = lax.broadcasted_iota(I32, s.shape, 0).astype(F32)
        m = jnp.max(s, axis=0, keepdims=True)
        ix = jnp.min(jnp.where(s == m, ri, float(R)), axis=0, keepdims=True)
        hit = ri == ix
        val_ref[pl.ds(kk, 1), :] = m
        idx_ref[pl.ds(kk, 1), :] = ix if payload is None else jnp.max(jnp.where(hit, payload, -1.0), axis=0, keepdims=True)
        return jnp.where(hit, -jnp.inf, s)

    def body(kk, ss):
        return tuple(one(kk, s, *prob[1:]) for s, prob in zip(ss, problems))

    lax.fori_loop(0, PEER_TOPK, body, tuple(prob[0] for prob in problems))


_PEER_CAND_ROWS = ((0, 16),) + tuple((16 + 8 * (p - 1), 8) for p in range(1, 8))
_PEER_NCAND = _PEER_CAND_ROWS[-1][0] + 16


def _peer_route_kernel(x_ref, mod_ref, wq_ref, sk_ref, t_ref, ids_ref, gate_ref,
                       xm_sc, v1_sc, i1_sc, v2_sc, i2_sc, cand_sc, cidx_sc, top_sc, tid_sc):
    h = pl.program_id(2)

    @pl.when(h == 0)
    def _():
        mod = mod_ref[0, 0]
        xm = (_ln(x_ref[0]) * (1.0 + mod[4:5, :]) + mod[3:4, :]).astype(BF16)
        xm_sc[...] = xm
        t_ref[0] = xm

    q = _dot(xm_sc[...], wq_ref[...]).astype(BF16)
    half = PEER_DQ // 2
    _top16((_dot_nt(sk_ref[0], q[:, :half]), v1_sc, i1_sc, None))
    _top16((_dot_nt(sk_ref[1], q[:, half:]), v2_sc, i2_sc, None))
    nk = float(PEER_NKEYS)
    for p, (lo, n) in enumerate(_PEER_CAND_ROWS):
        cand_sc[lo:lo + n, :] = v1_sc[p:p + 1, :] + v2_sc[0:n, :]
        cidx_sc[lo:lo + n, :] = i1_sc[p:p + 1, :] * nk + i2_sc[0:n, :]
    lo = _PEER_CAND_ROWS[-1][0] + _PEER_CAND_ROWS[-1][1]
    cand_sc[lo:lo + 8, :] = v1_sc[8:16, :] + v2_sc[0:1, :]
    cidx_sc[lo:lo + 8, :] = i1_sc[8:16, :] * nk + i2_sc[0:1, :]
    _top16((cand_sc[...], top_sc, tid_sc, cidx_sc[...]))
    top = top_sc[...]
    e = jnp.exp(top - top[0:1, :])
    gate_ref[...] = e / jnp.sum(e, axis=0, keepdims=True)
    ids_ref[...] = tid_sc[...].astype(I32)


def peer_route(h1, mod, wq, subkeys, nct):
    B, S, D = h1.shape
    nt = S // TM
    f = lambda n: pltpu.VMEM((n, TM), F32)
    return pl.pallas_call(
        _peer_route_kernel,
        out_shape=[jax.ShapeDtypeStruct((B, S, D), BF16),
                   jax.ShapeDtypeStruct((PEER_HEADS * PEER_TOPK, B * S), I32),
                   jax.ShapeDtypeStruct((PEER_HEADS * PEER_TOPK, B * S), F32)],
        grid=(B, nt, PEER_HEADS),
        in_specs=[pl.BlockSpec((1, TM, D), lambda b, i, h: (b, i, 0)),
                  pl.BlockSpec((1, 1, 6, D), lambda b, i, h: (b, jnp.where(i >= nct, 1, 0), 0, 0)),
                  pl.BlockSpec((D, PEER_DQ), lambda b, i, h: (0, h)),
                  pl.BlockSpec((2, PEER_NKEYS, PEER_DQ // 2), lambda b, i, h: (0, 0, 0))],
        out_specs=[pl.BlockSpec((1, TM, D), lambda b, i, h: (b, i, 0)),
                   pl.BlockSpec((PEER_TOPK, TM), lambda b, i, h: (h, b * nt + i)),
                   pl.BlockSpec((PEER_TOPK, TM), lambda b, i, h: (h, b * nt + i))],
        scratch_shapes=[pltpu.VMEM((TM, D), BF16), f(16), f(16), f(16), f(16), f(_PEER_NCAND), f(_PEER_NCAND),
                        f(16), f(16)],
        compiler_params=_cparams(("parallel", "parallel", "arbitrary")), name="peer_route")(h1, mod, wq, subkeys)


def _peer_expert_kernel(t_ref, ids_ref, gate_ref, u_ref, v_ref, x_ref, mod0_ref, mod1_ref, lng_ref, lnb_ref, o_ref,
                        a_sc, wgt_sc, acc_sc, *, nchunk):
    j = pl.program_id(1)
    NK = PEER_NKEYS
    w_sc = a_sc

    @pl.when(j < nchunk)
    def _():
        a = _dot(t_ref[...], u_ref[...])
        a3 = jnp.stack([a[:, al * NK:(al + 1) * NK] for al in range(PEER_CA)], axis=0)
        a_sc[:, pl.ds(pl.multiple_of(j * PEER_CA, PEER_CA), PEER_CA), :] = pltpu.einshape("atb->tab", a3)

    @pl.when(j == nchunk - 1)
    def _():
        io_b = lax.broadcasted_iota(I32, (2 * NK, 2 * NK), 0)
        io_a = lax.broadcasted_iota(I32, (NK, 2 * NK), 0)
        second = jnp.where(lax.broadcasted_iota(I32, (1, 2 * NK), 1) >= NK, NK, 0)

        def masks(two):
            idp = ids_ref[two, :]
            ids = jnp.concatenate([idp[0:1], idp[1:2]], axis=1)
            onehot = jnp.where(io_b == (ids & (NK - 1)) + second, 1.0, 0.0).astype(BF16)
            return onehot, io_a == (ids >> 7)

        def pick(p, carry):
            two = pl.ds(pl.multiple_of(p * 2, 2), 2)
            onehot, r1 = masks(two)
            a2 = a_sc[two]
            at = jnp.concatenate([a2[0], a2[1]], axis=1).astype(BF16)
            picked = _dot(at, onehot)
            act = jnp.sum(jnp.where(r1, picked, 0.0), axis=0, keepdims=True)
            gp = gate_ref[two, :]
            w = jnp.concatenate([gp[0:1], gp[1:2]], axis=1) * _gelu(act)
            wgt_sc[two, :] = jnp.concatenate([w[:, :NK], w[:, NK:]], axis=0)
            return carry

        def scatter(p, carry):
            two = pl.ds(pl.multiple_of(p * 2, 2), 2)
            onehot, r1 = masks(two)
            wp = wgt_sc[two, :]
            w = jnp.concatenate([wp[0:1], wp[1:2]], axis=1)
            wt = _dot_nt(jnp.where(r1, w, 0.0).astype(BF16), onehot)
            w_sc[two] = jnp.stack([wt[:, :NK], wt[:, NK:]], axis=0)
            return carry

        lax.fori_loop(0, TM_E // 2, pick, 0, unroll=16)
        lax.fori_loop(0, TM_E // 2, scatter, 0, unroll=16)
        acc_sc[...] = jnp.zeros_like(acc_sc)

    @pl.when(j >= nchunk)
    def _():
        jj = j - nchunk
        w3 = pltpu.einshape("tab->atb", w_sc[:, pl.ds(pl.multiple_of(jj * PEER_CA, PEER_CA), PEER_CA), :])
        w = jnp.concatenate([w3[al] for al in range(PEER_CA)], axis=1).astype(BF16)
        acc_sc[...] += _dot(w, v_ref[...])

    @pl.when(j == 2 * nchunk - 1)
    def _():
        for half, mod_ref in enumerate((mod0_ref, mod1_ref)):
            rows = slice(half * TM, (half + 1) * TM)
            y = ALPHA * x_ref[rows, :] + mod_ref[0, 0][5:6, :] * acc_sc[rows, :]
            o_ref[rows, :] = _ln(y) * lng_ref[...] + lnb_ref[...]


def peer_experts(t, ids, gates, emb_ut, emb_v, h1, mod, lng, lnb, nt_per_sample, nct):
    T, D = h1.shape
    E = emb_v.shape[0]
    ce = PEER_CA * PEER_NKEYS
    nchunk = E // ce
    assert T % TM_E == 0 and TM_E == 2 * TM
    tok = lambda n, **kw: pl.BlockSpec((TM_E, n), lambda i, j: (i, 0), **kw)

    def mod_spec(half):
        def index(i, j):
            r = 2 * i + half
            return (r // nt_per_sample, jnp.where(r % nt_per_sample >= nct, 1, 0), 0, 0)
        return pl.BlockSpec((1, 1, 6, D), index)
    once = dict(pipeline_mode=pl.Buffered(1))
    return pl.pallas_call(
        functools.partial(_peer_expert_kernel, nchunk=nchunk),
        out_shape=jax.ShapeDtypeStruct((T, D), F32), grid=(T // TM_E, 2 * nchunk),
        in_specs=[tok(D, **once), tok(LANES), tok(LANES),
                  pl.BlockSpec((D, ce), lambda i, j: (0, jnp.minimum(j, nchunk - 1))),
                  pl.BlockSpec((ce, D), lambda i, j: (jnp.maximum(j - nchunk, 0), 0)),
                  tok(D, **once), mod_spec(0), mod_spec(1),
                  pl.BlockSpec((1, D), lambda i, j: (0, 0)), pl.BlockSpec((1, D), lambda i, j: (0, 0))],
        out_specs=tok(D),
        scratch_shapes=[pltpu.VMEM((TM_E, PEER_NKEYS, PEER_NKEYS), F32), pltpu.VMEM((TM_E, LANES), F32),
                        pltpu.VMEM((TM_E, D), F32)],
        compiler_params=_cparams(("parallel", "arbitrary")), name="peer_experts")(t, ids, gates, emb_ut, emb_v, h1, mod, mod, lng, lnb)


_ROPE_IDX = np.arange(HD)
_ROPE_PERM = np.where(_ROPE_IDX % 32 < 16, _ROPE_IDX + 16, _ROPE_IDX - 16)
_ROPE_SIGN = np.where(_ROPE_IDX % 32 < 16, -1.0, 1.0).astype(np.float32)


def _rope_partner(w):
    n = w.shape[1] // HD
    perm = np.concatenate([h * HD + _ROPE_PERM for h in range(n)])
    return w[:, perm] * jnp.asarray(np.tile(_ROPE_SIGN, n))


def _rope_tables(L, nctx, gq, gk):
    rows = L // GRID_W
    row = jnp.repeat(jnp.arange(rows, dtype=F32), GRID_W)
    col = jnp.tile(jnp.arange(GRID_W, dtype=F32), rows)
    nf = HD // 4
    inv = ROPE_THETA ** (-jnp.arange(nf, dtype=F32) / nf)
    ar, ac = row[:, None] * inv, col[:, None] * inv
    cos = jnp.concatenate([jnp.cos(ar), jnp.cos(ar), jnp.cos(ac), jnp.cos(ac)], axis=1)
    sin = jnp.concatenate([jnp.sin(ar), jnp.sin(ar), jnp.sin(ac), jnp.sin(ac)], axis=1)
    cos = jnp.concatenate([jnp.ones((nctx, HD), F32), cos], axis=0)
    sin = jnp.concatenate([jnp.zeros((nctx, HD), F32), sin], axis=0)
    scale = HD ** -0.5
    one = jnp.ones((HD,), F32)
    tabs = []
    for g, sc in ((one, scale), (one, 1.0), (gq, scale), (gk, 1.0)):
        tabs += [cos * (g * sc), sin * (g[_ROPE_PERM] * sc)]
    return jnp.tile(jnp.stack(tabs), (1, 1, 2))


def _s5_tables(a_re, a_im, log_dt, b_re, b_im, c_re, c_im, nb):
    dt = jnp.exp(log_dt)[..., None]
    mag = jnp.exp(a_re * dt)
    lr, li = mag * jnp.cos(a_im * dt), mag * jnp.sin(a_im * dt)
    den = a_re * a_re + a_im * a_im
    cr = ((lr - 1) * a_re + li * a_im) / den
    ci = (li * a_re - (lr - 1) * a_im) / den
    br = cr[..., None] * b_re - ci[..., None] * b_im
    bi = cr[..., None] * b_im + ci[..., None] * b_re
    G2, N, C = S5_GROUPS // 2, S5_STATE, S5_GROUP
    eye = jnp.eye(G2, dtype=F32)

    def blockdiag_in(m):
        m = m.reshape(2, 2, G2, N, C)
        return jnp.einsum('dhgnc,gk->dhgckn', m, eye).reshape(2, 2, G2 * C, G2 * N)

    def blockdiag_out(m):
        m = m.reshape(2, 2, G2, C, N)
        return jnp.einsum('dhgcn,gk->dhgnkc', m, eye).reshape(2, 2, G2 * N, G2 * C)

    bblk = jnp.concatenate([blockdiag_in(br), blockdiag_in(bi)], axis=-1).astype(BF16)
    cblk = jnp.concatenate([blockdiag_out(c_re), blockdiag_out(-c_im)], axis=-2).astype(BF16)
    lam = jnp.stack([lr, li], axis=1).reshape(2, 2, 2, G2 * N)
    lam = jnp.tile(lam[:, :, None], (1, 1, nb, 1, 1)).reshape(2, 2, 2 * nb, G2 * N)
    return bblk, cblk, lam


def _blockdiag_ones(group, n=LANES):
    i = np.arange(n) // group
    return jnp.asarray((i[:, None] == i[None, :]).astype(np.float32), dtype=BF16)


def kernel(x, c, ctx, c_ctx, ada_w, ada_b, w_in, b_gate, diff_lam, diff_norm_g, gqa_qnorm_g, gqa_knorm_g,
           s5_a_re, s5_a_im, s5_log_dt, s5_b_re, s5_b_im, s5_c_re, s5_c_im, s5_d, s5_w_glu, s5_b_glu,
           ml_conv_w, ml_conv_b, ml_gate_b, ml_norm_g, w_branch, w_out, ln_mix_g, ln_mix_b, ln_ffn_g, ln_ffn_b,
           peer_wq, peer_subkeys, peer_u, peer_v):
    B, L, D = x.shape
    nctx = ctx.shape[1]
    S = nctx + L
    assert D == D_MODEL and nctx % TM == 0 and L % TM == 0 and L % GRID_W == 0
    nt, nct = S // TM, nctx // TM
    depth = ada_w.shape[0]

    h = jnp.concatenate([ctx, x], axis=1)
    R = -(-(B + 1) // 8) * 8
    cond = jnp.zeros((R, D), F32).at[:B].set(c).at[B].set(c_ctx)
    ones64, ones128 = _blockdiag_ones(HD), _blockdiag_ones(ML_HD, ML_HEADS * ML_HD)
    o = IN_OFFS

    for l in range(depth):
        lam_init = 0.8 - 0.6 * math.exp(-0.3 * l)
        m = ada_modulation(cond, ada_w[l].astype(BF16), ada_b[l][None, :])
        mod = jnp.stack([jnp.broadcast_to(m[B], (B, 6 * D)), m[:B]], axis=1).reshape(B, 2, 6, D)

        w = w_in[l]
        seg = lambda i: w[:, o[i]:o[i + 1]]
        dup = lambda t: jnp.concatenate([t[:, :HD], t[:, :HD], t[:, HD:], t[:, HD:]], axis=1)
        w_a = jnp.concatenate([seg(0), _rope_partner(seg(0)), seg(1), _rope_partner(seg(1)), seg(2),
                               seg(9), _rope_partner(seg(9)), dup(seg(10)), dup(_rope_partner(seg(10))),
                               dup(seg(11))], axis=1).astype(BF16)
        w_b = jnp.concatenate([seg(3), seg(4), seg(5), seg(6), seg(7), seg(8),
                               jnp.zeros((D, LANES - 16), F32)], axis=1).astype(BF16)
        gate_b = jnp.concatenate([ml_gate_b[l], jnp.zeros((LANES - 16,), F32)])[None, :]
        tab = _rope_tables(L, nctx, gqa_qnorm_g[l], gqa_knorm_g[l])

        dq, dk, dv, gq, gk, gv = proj_attn(h, mod, w_a, tab, ones64, nct)
        u, mqk, mv, mo, mg = proj_seq(h, mod, w_b, gate_b, nct)

        yd = diff_attention(dq, dk, dv, diff_lam[l], diff_norm_g[l][None, :], nctx, lam_init)
        yg = gqa_attention(gq, gk, gv, nctx)

        bblk, cblk, lam = _s5_tables(s5_a_re[l], s5_a_im[l], s5_log_dt[l], s5_b_re[l], s5_b_im[l],
                                     s5_c_re[l], s5_c_im[l], B)
        sf, sb = (y.reshape(u.shape) for y in s5_scan(u.reshape(S * 2 * B, 256), bblk, cblk, lam, S, nctx))

        mq, mk = ml_prep(mqk, ml_conv_w[l], ml_conv_b[l][None, :], nctx)
        hf = mlstm_scan(mq, mk, mv, mg, nctx, False)
        hb = mlstm_scan(mq, mk, mv, mg, nctx, True)

        h1 = merge(h, mod, yd, yg, u, sf, sb, hf, hb, mo,
                   s5_d[l][None, :], s5_w_glu[l].astype(BF16), s5_b_glu[l][None, :], ml_norm_g[l][None, :],
                   seg(12).astype(BF16), b_gate[l][None, :], w_branch[l].astype(BF16), w_out[l].astype(BF16),
                   ln_mix_g[l][None, :], ln_mix_b[l][None, :], ones128, nct)

        t, ids, gates = peer_route(h1, mod, peer_wq[l].astype(BF16), peer_subkeys[l].astype(BF16), nct)
        h = peer_experts(t.reshape(B * S, D), ids.T, gates.T, peer_u[l].astype(BF16).T, peer_v[l].astype(BF16),
                         h1.reshape(B * S, D), mod, ln_ffn_g[l][None, :], ln_ffn_b[l][None, :], nt, nct).reshape(B, S, D)
    return h[:, nctx:]
```

```python
import functools
import math

import numpy as np
import jax
import jax.numpy as jnp
from jax import lax
from jax.experimental import pallas as pl
from jax.experimental.pallas import tpu as pltpu

F32 = jnp.float32
BF16 = jnp.bfloat16
I32 = jnp.int32

D_MODEL = 1024
DEPTH = 2
GRID_W = 64
ROPE_THETA = 10000.0
LN_EPS = 1e-6
HD = 64
DIFF_HEADS = 4
GQA_HEADS = 8
GQA_KV = 2
S5_GROUP = 16
S5_GROUPS = 32
S5_STATE = 64
ML_HEADS = 4
ML_HD = 128
N_BRANCH = 4
W_BRANCH = 512
PEER_HEADS = 8
PEER_NKEYS = 128
PEER_TOPK = 16
PEER_DQ = 256
ALPHA = (2 * DEPTH) ** 0.25

LANES = 128
TM = 256
TM_E = 2 * TM
TQ = 256
TK = 2048
S5_TC = 128
ML_T = 256
PEER_CA = 16
VMEM_LIMIT = 56 * 1024 * 1024

IN_SPLITS = (512, 512, 512, 512, 512, 512, 512, 512, 16, 512, 128, 128, N_BRANCH * D_MODEL)
IN_OFFS = tuple(int(v) for v in np.cumsum((0,) + IN_SPLITS))


def _cparams(sem):
    return pltpu.CompilerParams(dimension_semantics=sem, vmem_limit_bytes=VMEM_LIMIT)


def _const_spec(shape):
    nd = len(shape)
    return pl.BlockSpec(shape, lambda *_: (0,) * nd, pipeline_mode=pl.Buffered(1))


def _ln(x):
    xc = x - jnp.mean(x, axis=-1, keepdims=True)
    return xc * lax.rsqrt(jnp.mean(xc * xc, axis=-1, keepdims=True) + LN_EPS)


def _sigmoid(x):
    return 1.0 / (1.0 + jnp.exp(-x))


def _gelu(x):
    return 0.5 * x * (1.0 + lax.erf(x * (2.0 ** -0.5)))


def _dot(a, b):
    return jnp.dot(a, b, preferred_element_type=F32)


def _dot_nt(a, b):
    return lax.dot_general(a, b, (((1,), (1,)), ((), ())), preferred_element_type=F32)


def _dot_tn(a, b):
    return lax.dot_general(a, b, (((0,), (0,)), ((), ())), preferred_element_type=F32)


def _ada_kernel(c_ref, w_ref, b_ref, o_ref):
    c = c_ref[...]
    o_ref[...] = _dot((c * _sigmoid(c)).astype(BF16), w_ref[...]) + b_ref[...]


def ada_modulation(cond, w, b):
    R, D = cond.shape
    N = w.shape[1]
    tn = 1536
    return pl.pallas_call(
        _ada_kernel, out_shape=jax.ShapeDtypeStruct((R, N), F32), grid=(N // tn,),
        in_specs=[pl.BlockSpec((R, D), lambda j: (0, 0)), pl.BlockSpec((D, tn), lambda j: (0, j)),
                  pl.BlockSpec((1, tn), lambda j: (0, j))],
        out_specs=pl.BlockSpec((R, tn), lambda j: (0, j)),
        compiler_params=_cparams(("arbitrary",)), name="ada")(cond, w, b)


def _proj_attn_kernel(x_ref, mod_ref, w_ref, tab_ref, ones_ref,
                      dq_ref, dk_ref, dv_ref, gq_ref, gk_ref, gv_ref):
    mod = mod_ref[0, 0]
    xm = (_ln(x_ref[0]) * (1.0 + mod[1:2, :]) + mod[0:1, :]).astype(BF16)

    def mm(lo, n):
        return _dot(xm, w_ref[:, lo:lo + n])

    def rope_store(ref, lo, n, ci, norm):
        t, tp = mm(lo, n), mm(lo + n, n)
        c, s = tab_ref[ci], tab_ref[ci + 1]
        for j in range(n // LANES):
            sl = slice(j * LANES, (j + 1) * LANES)
            tb = t[:, sl]
            y = tb * c + tp[:, sl] * s
            if norm:
                ss = _dot((tb * tb).astype(BF16), ones_ref[...])
                y = y * lax.rsqrt(ss * (1.0 / HD) + LN_EPS)
            ref[0, :, sl] = y.astype(ref.dtype)

    rope_store(dq_ref, 0, 512, 0, False)
    rope_store(dk_ref, 1024, 512, 2, False)
    dv_ref[0] = mm(2048, 512).astype(dv_ref.dtype)
    rope_store(gq_ref, 2560, 512, 4, True)
    rope_store(gk_ref, 3584, 256, 6, True)
    gv_ref[0] = mm(4096, 256).astype(gv_ref.dtype)


def proj_attn(xa, mod, w_a, tab, ones_bd, nct):
    B, S, D = xa.shape
    tok = lambda n: pl.BlockSpec((1, TM, n), lambda b, i: (b, i, 0))
    outs = [jax.ShapeDtypeStruct((B, S, n), BF16) for n in (512, 512, 512, 512, 256, 256)]
    return pl.pallas_call(
        _proj_attn_kernel, out_shape=outs, grid=(B, S // TM),
        in_specs=[tok(D),
                  pl.BlockSpec((1, 1, 6, D), lambda b, i: (b, jnp.where(i >= nct, 1, 0), 0, 0)),
                  _const_spec(w_a.shape),
                  pl.BlockSpec((8, TM, LANES), lambda b, i: (0, i, 0)),
                  _const_spec(ones_bd.shape)],
        out_specs=[tok(n) for n in (512, 512, 512, 512, 256, 256)],
        compiler_params=_cparams(("parallel", "parallel")), name="proj_attn")(xa, mod, w_a, tab, ones_bd)


def _proj_seq_kernel(x_ref, mod_ref, w_ref, gb_ref, u_ref, qk_ref, v_ref, o_ref, g_ref):
    mod = mod_ref[0, 0]
    xm = (_ln(x_ref[0]) * (1.0 + mod[1:2, :]) + mod[0:1, :]).astype(BF16)
    u_ref[...] = _dot(xm, w_ref[:, 0:512])
    qk_ref[0] = _dot(xm, w_ref[:, 512:1536])
    v_ref[0] = _dot(xm, w_ref[:, 1536:2048])
    o_ref[0] = _dot(xm, w_ref[:, 2048:2560])
    g_ref[0] = _dot(xm, w_ref[:, 2560:2688]) + gb_ref[...]


def proj_seq(xa, mod, w_b, gate_b, nct):
    B, S, D = xa.shape
    tok = lambda n: pl.BlockSpec((1, TM, n), lambda b, i: (b, i, 0))
    widths = (1024, 512, 512, LANES)
    return pl.pallas_call(
        _proj_seq_kernel,
        out_shape=[jax.ShapeDtypeStruct((S, B * 512), F32)] + [jax.ShapeDtypeStruct((B, S, n), F32) for n in widths],
        grid=(B, S // TM),
        in_specs=[tok(D),
                  pl.BlockSpec((1, 1, 6, D), lambda b, i: (b, jnp.where(i >= nct, 1, 0), 0, 0)),
                  _const_spec(w_b.shape), _const_spec(gate_b.shape)],
        out_specs=[pl.BlockSpec((TM, 512), lambda b, i: (i, b))] + [tok(n) for n in widths],
        compiler_params=_cparams(("parallel", "parallel")), name="proj_seq")(xa, mod, w_b, gate_b)


def _flash(qq, k_ref, v_ref, nctx, n_lat, tk):
    R = qq.shape[0]

    def chunk(carry, rows):
        m, l, acc = carry
        s = _dot_nt(qq, k_ref[0, rows, :])
        m_new = jnp.maximum(m, jnp.max(s, axis=1, keepdims=True))
        alpha = jnp.exp(m - m_new)
        p = jnp.exp(s - m_new)
        l = alpha * l + jnp.sum(p, axis=1, keepdims=True)
        acc = alpha * acc + _dot(p.astype(BF16), v_ref[0, rows, :])
        return m_new, l, acc

    carry = (jnp.full((R, 1), -jnp.inf, F32), jnp.zeros((R, 1), F32), jnp.zeros((R, LANES), F32))
    carry = chunk(carry, pl.ds(0, nctx))
    for c in range(n_lat):
        carry = chunk(carry, pl.ds(nctx + c * tk, tk))
    _, l, acc = carry
    return acc / l


def _ctx_or_all(i, nctx, n_lat, attend):
    @pl.when(i < nctx // TQ)
    def _():
        attend(0)

    @pl.when(i >= nctx // TQ)
    def _():
        attend(n_lat)


def _diff_attn_kernel(q_ref, k_ref, v_ref, lam_ref, g_ref, o_ref, *, nctx, n_lat, tk, lam_init):
    i = pl.program_id(2)
    q = q_ref[0]
    lane = lax.broadcasted_iota(I32, q.shape, 1)
    zero = jnp.zeros_like(q)
    qq = jnp.concatenate([jnp.where(lane < HD, q, zero), jnp.where(lane >= HD, q, zero)], axis=0)
    lv = lam_ref[...]
    lam = (jnp.exp(jnp.sum(lv[0:1] * lv[1:2], axis=1, keepdims=True))
           - jnp.exp(jnp.sum(lv[2:3] * lv[3:4], axis=1, keepdims=True)) + lam_init)

    def attend(n):
        o = _flash(qq, k_ref, v_ref, nctx, n, tk)
        d = o[:TQ] - lam * o[TQ:]
        y = d * lax.rsqrt(jnp.mean(d * d, axis=-1, keepdims=True) + LN_EPS)
        o_ref[0] = (y * g_ref[...] * (1.0 - lam_init)).astype(o_ref.dtype)

    _ctx_or_all(i, nctx, n_lat, attend)


def diff_attention(q, k, v, lam_vec, norm_g, nctx, lam_init):
    B, S, _ = q.shape
    tk = math.gcd(S - nctx, TK)
    kern = functools.partial(_diff_attn_kernel, nctx=nctx, n_lat=(S - nctx) // tk, tk=tk, lam_init=lam_init)
    return pl.pallas_call(
        kern, out_shape=jax.ShapeDtypeStruct((B, S, 512), BF16), grid=(B, DIFF_HEADS, S // TQ),
        in_specs=[pl.BlockSpec((1, TQ, LANES), lambda b, h, i: (b, i, h)),
                  pl.BlockSpec((1, S, LANES), lambda b, h, i: (b, 0, h)),
                  pl.BlockSpec((1, S, LANES), lambda b, h, i: (b, 0, h)),
                  pl.BlockSpec((4, HD), lambda b, h, i: (0, 0)),
                  pl.BlockSpec((1, LANES), lambda b, h, i: (0, 0))],
        out_specs=pl.BlockSpec((1, TQ, LANES), lambda b, h, i: (b, i, h)),
        compiler_params=_cparams(("parallel", "parallel", "parallel")), name="diff_attn")(q, k, v, lam_vec, norm_g)


def _gqa_attn_kernel(q_ref, k_ref, v_ref, o_ref, *, nctx, n_lat, tk):
    i = pl.program_id(2)
    q = q_ref[0]
    lane = lax.broadcasted_iota(I32, (TQ, LANES), 1)
    zero = jnp.zeros((TQ, LANES), q.dtype)
    parts = []
    for j in range(2):
        blk = q[:, j * LANES:(j + 1) * LANES]
        parts += [jnp.where(lane < HD, blk, zero), jnp.where(lane >= HD, blk, zero)]
    qq = jnp.concatenate(parts, axis=0)

    def attend(n):
        o = _flash(qq, k_ref, v_ref, nctx, n, tk)
        for j in range(2):
            pair = jnp.where(lane < HD, o[(2 * j) * TQ:(2 * j + 1) * TQ], o[(2 * j + 1) * TQ:(2 * j + 2) * TQ])
            o_ref[0, :, j * LANES:(j + 1) * LANES] = pair.astype(o_ref.dtype)

    _ctx_or_all(i, nctx, n_lat, attend)


def gqa_attention(q, k, v, nctx):
    B, S, _ = q.shape
    tk = math.gcd(S - nctx, TK)
    kern = functools.partial(_gqa_attn_kernel, nctx=nctx, n_lat=(S - nctx) // tk, tk=tk)
    return pl.pallas_call(
        kern, out_shape=jax.ShapeDtypeStruct((B, S, 512), BF16), grid=(B, GQA_KV, S // TQ),
        in_specs=[pl.BlockSpec((1, TQ, 2 * LANES), lambda b, g, i: (b, i, g)),
                  pl.BlockSpec((1, S, LANES), lambda b, g, i: (b, 0, g)),
                  pl.BlockSpec((1, S, LANES), lambda b, g, i: (b, 0, g))],
        out_specs=pl.BlockSpec((1, TQ, 2 * LANES), lambda b, g, i: (b, i, g)),
        compiler_params=_cparams(("parallel", "parallel", "parallel")), name="gqa_attn")(q, k, v)


def _s5_kernel(uf_ref, ub_ref, bblk_ref, cblk_ref, lam_ref, yf_ref, yb_ref, buf_f, buf_b, st_ref):
    H = S5_GROUPS * S5_STATE // 2
    R = buf_f.shape[0]
    RB = 256

    @pl.when(pl.program_id(0) == 0)
    def _():
        st_ref[...] = jnp.zeros_like(st_ref)

    half0 = (lax.broadcasted_iota(I32, (RB, 1), 0) & 1) == 0

    for d, (u_ref, buf) in enumerate(((uf_ref, buf_f), (ub_ref, buf_b))):
        for rb in range(R // RB):
            rows = slice(rb * RB, (rb + 1) * RB)
            uh = u_ref[rows, :].astype(BF16)
            for cols in (slice(0, H), slice(H, 2 * H)):
                buf[rows, cols] = jnp.where(half0, _dot(uh, bblk_ref[d, 0, :, cols]), _dot(uh, bblk_ref[d, 1, :, cols]))

    lfr, lfi, lbr, lbi = lam_ref[0, 0], lam_ref[0, 1], lam_ref[1, 0], lam_ref[1, 1]

    def step(t, carry):
        fr, fi, br, bi = carry
        rf = pl.ds(pl.multiple_of(t * 8, 8), 8)
        x = buf_f[rf, :]
        nfr = lfr * fr - lfi * fi + x[:, :H]
        nfi = lfr * fi + lfi * fr + x[:, H:]
        buf_f[rf, :] = jnp.concatenate([nfr, nfi], axis=1)
        rb = pl.ds(pl.multiple_of((S5_TC - 1 - t) * 8, 8), 8)
        z = buf_b[rb, :]
        nbr = lbr * br - lbi * bi + z[:, :H]
        nbi = lbr * bi + lbi * br + z[:, H:]
        buf_b[rb, :] = jnp.concatenate([nbr, nbi], axis=1)
        return nfr, nfi, nbr, nbi

    fin = lax.fori_loop(0, S5_TC, step, (st_ref[0], st_ref[1], st_ref[2], st_ref[3]))
    for j in range(4):
        st_ref[j] = fin[j]

    for d, (y_ref, buf) in enumerate(((yf_ref, buf_f), (yb_ref, buf_b))):
        for rb in range(R // RB):
            rows = slice(rb * RB, (rb + 1) * RB)
            h = buf[rows, :].astype(BF16)
            y_ref[rows, :] = jnp.where(half0, _dot(h, cblk_ref[d, 0]), _dot(h, cblk_ref[d, 1]))


def s5_scan(u, bblk, cblk, lam, S, nctx):
    assert u.shape[0] == S * 8, "the scan packs (sample, half) pairs into the eight sublanes of a vreg"
    nch, nc0 = S // S5_TC, nctx // S5_TC

    def bwd(i):
        return jnp.where(i < nc0, nc0 - 1 - i, (nch - 1) - (i - nc0))
    H2 = S5_GROUPS * S5_STATE
    R, W = S5_TC * 8, u.shape[1]
    blk = lambda f: pl.BlockSpec((R, W), f)
    return pl.pallas_call(
        _s5_kernel, out_shape=[jax.ShapeDtypeStruct(u.shape, F32)] * 2, grid=(nch,),
        in_specs=[blk(lambda i: (i, 0)), blk(lambda i: (bwd(i), 0)),
                  _const_spec(bblk.shape), _const_spec(cblk.shape), _const_spec(lam.shape)],
        out_specs=[blk(lambda i: (i, 0)), blk(lambda i: (bwd(i), 0))],
        scratch_shapes=[pltpu.VMEM((R, H2), F32), pltpu.VMEM((R, H2), F32), pltpu.VMEM((4, 8, H2 // 2), F32)],
        compiler_params=_cparams(("arbitrary",)), name="s5_scan")(u, u, bblk, cblk, lam)


def _ml_prep_kernel(x_ref, prev_ref, next_ref, w_ref, b_ref, q_ref, k_ref, *, seg_starts, seg_ends):
    i = pl.program_id(1)
    x = x_ref[0]
    row = lax.broadcasted_iota(I32, x.shape, 0)
    first = functools.reduce(jnp.logical_or, [i == s for s in seg_starts])
    last = functools.reduce(jnp.logical_or, [i == s for s in seg_ends])
    pr = jnp.where(first, 0.0, prev_ref[0, 7:8, :])
    nx = jnp.where(last, 0.0, next_ref[0, 0:1, :])
    xp = jnp.where(row == 0, pr, pltpu.roll(x, 1, 0))
    xn = jnp.where(row == TM - 1, nx, pltpu.roll(x, TM - 1, 0))
    w = w_ref[...]
    y = b_ref[...] + xp * w[0:1, :] + x * w[1:2, :] + xn * w[2:3, :]
    y = y * _sigmoid(y)
    q_ref[0] = y[:, :512].astype(q_ref.dtype)
    k_ref[0] = (y[:, 512:] * (ML_HD ** -0.5)).astype(k_ref.dtype)


def ml_prep(qk, conv_w, conv_b, nctx):
    B, S, W = qk.shape
    nt, nct, r8 = S // TM, nctx // TM, TM // 8
    kern = functools.partial(_ml_prep_kernel, seg_starts=(0, nct), seg_ends=(nct - 1, nt - 1))
    return pl.pallas_call(
        kern, out_shape=[jax.ShapeDtypeStruct((B, S, 512), BF16)] * 2, grid=(B, nt),
        in_specs=[pl.BlockSpec((1, TM, W), lambda b, i: (b, i, 0)),
                  pl.BlockSpec((1, 8, W), lambda b, i: (b, jnp.maximum(i * r8 - 1, 0), 0)),
                  pl.BlockSpec((1, 8, W), lambda b, i: (b, jnp.minimum((i + 1) * r8, S // 8 - 1), 0)),
                  pl.BlockSpec((3, W), lambda b, i: (0, 0)), pl.BlockSpec((1, W), lambda b, i: (0, 0))],
        out_specs=[pl.BlockSpec((1, TM, 512), lambda b, i: (b, i, 0))] * 2,
        compiler_params=_cparams(("parallel", "parallel")), name="ml_prep")(qk, qk, qk, conv_w, conv_b)


def _log_sigmoid(x):
    return jnp.minimum(x, 0.0) - jnp.log1p(jnp.exp(-jnp.abs(x)))


def _mlstm_kernel(q_ref, k_ref, v_ref, g_ref, h_ref, c_sc, n_sc, m_sc, *, reverse):
    T = ML_T
    gi, gf = (8, 12) if reverse else (0, 4)

    @pl.when(pl.program_id(1) == 0)
    def _():
        c_sc[...] = jnp.zeros_like(c_sc)
        n_sc[...] = jnp.zeros_like(n_sc)
        m_sc[...] = jnp.zeros_like(m_sc)

    g = g_ref[0]
    gt = g.T
    r = lax.broadcasted_iota(I32, (T, T), 0)
    c = lax.broadcasted_iota(I32, (T, T), 1)
    mask = (c >= r) if reverse else (c <= r)
    tri = jnp.where(mask, 1.0, 0.0)
    hp = lax.Precision.HIGHEST
    bcol_all = jnp.dot(tri, _log_sigmoid(g), preferred_element_type=F32, precision=hp)
    brow_all = lax.dot_general(_log_sigmoid(gt[0:16]), tri, (((1,), (1,)), ((), ())),
                               preferred_element_type=F32, precision=hp)
    end = 0 if reverse else T - 1
    for hh in range(ML_HEADS):
        sl = slice(hh * ML_HD, (hh + 1) * ML_HD)
        q, k, v = q_ref[0, :, sl], k_ref[0, :, sl], v_ref[0, :, sl]
        bcol, brow = bcol_all[:, gf + hh:gf + hh + 1], brow_all[gf + hh:gf + hh + 1, :]
        icol, irow = g[:, gi + hh:gi + hh + 1], gt[gi + hh:gi + hh + 1, :]
        m_old = m_sc[hh][:, 0:1]
        n_old = n_sc[hh]
        c_old = c_sc[hh]
        logw = jnp.where(mask, bcol - brow + irow, -jnp.inf)
        m_inter = bcol + m_old
        m_t = jnp.maximum(m_inter, jnp.max(logw, axis=1, keepdims=True))
        s = _dot_nt(q, k) * jnp.exp(logw - m_t)
        inter = jnp.exp(m_inter - m_t)
        num = _dot(s.astype(BF16), v.astype(BF16)) + inter * _dot_nt(q, c_old.astype(BF16))
        den = jnp.sum(s, axis=1, keepdims=True) + inter * jnp.sum(q.astype(F32) * n_old, axis=1, keepdims=True)
        h_ref[0, :, sl] = num / jnp.maximum(jnp.abs(den), jnp.exp(-m_t))
        b_end = bcol[end:end + 1, :]
        g_row, g_col = b_end - brow + irow, b_end - bcol + icol
        m_new = jnp.maximum(b_end + m_old, jnp.max(g_row, axis=1, keepdims=True))
        decay = jnp.exp(b_end + m_old - m_new)
        wk = jnp.exp(g_col - m_new)
        c_sc[hh] = decay * c_old + _dot_tn((v * wk).astype(BF16), k)
        n_sc[hh] = decay * n_old + jnp.sum(k.astype(F32) * wk, axis=0, keepdims=True)
        m_sc[hh] = jnp.broadcast_to(m_new, (1, LANES))


def mlstm_scan(q, k, v, g, nctx, reverse):
    B, S, W = q.shape
    nch, nc0 = S // ML_T, nctx // ML_T

    def order(i):
        return jnp.where(i < nc0, nc0 - 1 - i, (nch - 1) - (i - nc0)) if reverse else i
    blk = lambda n: pl.BlockSpec((1, ML_T, n), lambda b, i: (b, order(i), 0))
    return pl.pallas_call(
        functools.partial(_mlstm_kernel, reverse=reverse),
        out_shape=jax.ShapeDtypeStruct((B, S, W), F32), grid=(B, nch),
        in_specs=[blk(W), blk(W), blk(W), blk(LANES)], out_specs=blk(W),
        scratch_shapes=[pltpu.VMEM((ML_HEADS, ML_HD, ML_HD), F32), pltpu.VMEM((ML_HEADS, 1, ML_HD), F32),
                        pltpu.VMEM((ML_HEADS, 1, LANES), F32)],
        compiler_params=_cparams(("parallel", "arbitrary")), name="mlstm_bwd" if reverse else "mlstm_fwd")(q, k, v, g)


def _merge_kernel(x_ref, mod_ref, yd_ref, yg_ref, u_ref, sf_ref, sb_ref, hf_ref, hb_ref, mo_ref,
                  s5d_ref, wglu_ref, bglu_ref, mlg_ref, wgate_ref, bgate_ref, wbr_ref, wout_ref,
                  lng_ref, lnb_ref, ones_ref, o_ref):
    x = x_ref[0]
    mod = mod_ref[0, 0]
    xm = (_ln(x) * (1.0 + mod[1:2, :]) + mod[0:1, :]).astype(BF16)
    ys = u_ref[...] * s5d_ref[...] + sf_ref[...] + sb_ref[...]
    z = _dot(_gelu(ys).astype(BF16), wglu_ref[...]) + bglu_ref[...]
    ys = z[:, :512] * _sigmoid(z[:, 512:])
    hm = hf_ref[0] + hb_ref[0]
    ss = _dot((hm * hm).astype(BF16), ones_ref[...])
    ym = hm * lax.rsqrt(ss * (1.0 / ML_HD) + LN_EPS) * mlg_ref[...] * _sigmoid(mo_ref[0])
    branches = (yd_ref[0], ys.astype(BF16), ym.astype(BF16), yg_ref[0])
    merged = None
    for j, yb in enumerate(branches):
        gate = _sigmoid(_dot(xm, wgate_ref[:, j * D_MODEL:(j + 1) * D_MODEL]) + bgate_ref[:, j * D_MODEL:(j + 1) * D_MODEL])
        term = gate * _dot(yb, wbr_ref[j])
        merged = term if merged is None else merged + term
    mix = _dot(merged.astype(BF16), wout_ref[...])
    o_ref[0] = _ln(ALPHA * x + mod[2:3, :] * mix) * lng_ref[...] + lnb_ref[...]


def merge(xa, mod, yd, yg, u, sf, sb, hf, hb, mo, s5d, wglu, bglu, mlg, wgate, bgate, wbr, wout, lng, lnb, ones_ml, nct):
    B, S, D = xa.shape
    tok = lambda n: pl.BlockSpec((1, TM, n), lambda b, i: (b, i, 0))
    consts = (s5d, wglu, bglu, mlg, wgate, bgate, wbr, wout, lng, lnb, ones_ml)
    return pl.pallas_call(
        _merge_kernel, out_shape=jax.ShapeDtypeStruct((B, S, D), F32), grid=(B, S // TM),
        in_specs=[tok(D), pl.BlockSpec((1, 1, 6, D), lambda b, i: (b, jnp.where(i >= nct, 1, 0), 0, 0))]
        + [tok(512)] * 2 + [pl.BlockSpec((TM, 512), lambda b, i: (i, b))] * 3 + [tok(512)] * 3
        + [_const_spec(a.shape) for a in consts],
        out_specs=tok(D),
        compiler_params=_cparams(("parallel", "parallel")), name="merge")(xa, mod, yd, yg, u, sf, sb, hf, hb, mo, *consts)


def _top16(*problems):
    def one(kk, s, val_ref, idx_ref, payload):
        R = s.shape[0]
        ri = lax.broadcasted_iota(I32, s.shape, 0).astype(F32)
        m = jnp.max(s, axis=0, keepdims=True)
        ix = jnp.min(jnp.where(s == m, ri, float(R)), axis=0, keepdims=True)
        hit = ri == ix
        val_ref[pl.ds(kk, 1), :] = m
        idx_ref[pl.ds(kk, 1), :] = ix if payload is None else jnp.max(jnp.where(hit, payload, -1.0), axis=0, keepdims=True)
        return jnp.where(hit, -jnp.inf, s)

    def body(kk, ss):
        return tuple(one(kk, s, *prob[1:]) for s, prob in zip(ss, problems))

    lax.fori_loop(0, PEER_TOPK, body, tuple(prob[0] for prob in problems))


_PEER_CAND_ROWS = ((0, 16),) + tuple((16 + 8 * (p - 1), 8) for p in range(1, 8))
_PEER_NCAND = _PEER_CAND_ROWS[-1][0] + 16


def _peer_route_kernel(x_ref, mod_ref, wq_ref, sk_ref, t_ref, ids_ref, gate_ref,
                       xm_sc, v1_sc, i1_sc, v2_sc, i2_sc, cand_sc, cidx_sc, top_sc, tid_sc):
    h = pl.program_id(2)

    @pl.when(h == 0)
    def _():
        mod = mod_ref[0, 0]
        xm = (_ln(x_ref[0]) * (1.0 + mod[4:5, :]) + mod[3:4, :]).astype(BF16)
        xm_sc[...] = xm
        t_ref[0] = xm

    q = _dot(xm_sc[...], wq_ref[...]).astype(BF16)
    half = PEER_DQ // 2
    _top16((_dot_nt(sk_ref[0], q[:, :half]), v1_sc, i1_sc, None))
    _top16((_dot_nt(sk_ref[1], q[:, half:]), v2_sc, i2_sc, None))
    nk = float(PEER_NKEYS)
    for p, (lo, n) in enumerate(_PEER_CAND_ROWS):
        cand_sc[lo:lo + n, :] = v1_sc[p:p + 1, :] + v2_sc[0:n, :]
        cidx_sc[lo:lo + n, :] = i1_sc[p:p + 1, :] * nk + i2_sc[0:n, :]
    lo = _PEER_CAND_ROWS[-1][0] + _PEER_CAND_ROWS[-1][1]
    cand_sc[lo:lo + 8, :] = v1_sc[8:16, :] + v2_sc[0:1, :]
    cidx_sc[lo:lo + 8, :] = i1_sc[8:16, :] * nk + i2_sc[0:1, :]
    _top16((cand_sc[...], top_sc, tid_sc, cidx_sc[...]))
    top = top_sc[...]
    e = jnp.exp(top - top[0:1, :])
    gate_ref[...] = e / jnp.sum(e, axis=0, keepdims=True)
    ids_ref[...] = tid_sc[...].astype(I32)


def peer_route(h1, mod, wq, subkeys, nct):
    B, S, D = h1.shape
    nt = S // TM
    f = lambda n: pltpu.VMEM((n, TM), F32)
    return pl.pallas_call(
        _peer_route_kernel,
        out_shape=[jax.ShapeDtypeStruct((B, S, D), BF16),
                   jax.ShapeDtypeStruct((PEER_HEADS * PEER_TOPK, B * S), I32),
                   jax.ShapeDtypeStruct((PEER_HEADS * PEER_TOPK, B * S), F32)],
        grid=(B, nt, PEER_HEADS),
        in_specs=[pl.BlockSpec((1, TM, D), lambda b, i, h: (b, i, 0)),
                  pl.BlockSpec((1, 1, 6, D), lambda b, i, h: (b, jnp.where(i >= nct, 1, 0), 0, 0)),
                  pl.BlockSpec((D, PEER_DQ), lambda b, i, h: (0, h)),
                  pl.BlockSpec((2, PEER_NKEYS, PEER_DQ // 2), lambda b, i, h: (0, 0, 0))],
        out_specs=[pl.BlockSpec((1, TM, D), lambda b, i, h: (b, i, 0)),
                   pl.BlockSpec((PEER_TOPK, TM), lambda b, i, h: (h, b * nt + i)),
                   pl.BlockSpec((PEER_TOPK, TM), lambda b, i, h: (h, b * nt + i))],
        scratch_shapes=[pltpu.VMEM((TM, D), BF16), f(16), f(16), f(16), f(16), f(_PEER_NCAND), f(_PEER_NCAND),
                        f(16), f(16)],
        compiler_params=_cparams(("parallel", "parallel", "arbitrary")), name="peer_route")(h1, mod, wq, subkeys)


def _peer_expert_kernel(t_ref, ids_ref, gate_ref, u_ref, v_ref, x_ref, mod0_ref, mod1_ref, lng_ref, lnb_ref, o_ref,
                        a_sc, raw_sc, wraw_sc, wgt_sc, *, nchunk):
    j = pl.program_id(1)
    NK, CA = PEER_NKEYS, PEER_CA
    w_sc = a_sc

    def dense(slot):
        raw_sc[slot] = _dot(t_ref[...], u_ref[...]).astype(BF16)

    def relayout_in(c, slot):
        a = raw_sc[slot]
        a3 = jnp.stack([a[:, al * NK:(al + 1) * NK] for al in range(CA)], axis=0)
        a_sc[:, pl.ds(pl.multiple_of(c * CA, CA), CA), :] = pltpu.einshape("atb->tab", a3)

    def relayout_out(c, slot):
        w3 = pltpu.einshape("tab->atb", w_sc[:, pl.ds(pl.multiple_of(c * CA, CA), CA), :])
        wraw_sc[slot] = jnp.concatenate([w3[al] for al in range(CA)], axis=1)

    def weighted_sum(slot):
        o_ref[...] += _dot(wraw_sc[slot], v_ref[...])

    @pl.when(j == 0)
    def _():
        dense(0)

    for slot in (0, 1):
        @pl.when(jnp.logical_and(jnp.logical_and(j > 0, j < nchunk), j % 2 == slot))
        def _():
            relayout_in(j - 1, 1 - slot)
            dense(slot)

    @pl.when(j == nchunk - 1)
    def _():
        relayout_in(j, (nchunk - 1) % 2)
        io_b = lax.broadcasted_iota(I32, (2 * NK, 2 * NK), 0)
        io_a = lax.broadcasted_iota(I32, (NK, 2 * NK), 0)
        second = jnp.where(lax.broadcasted_iota(I32, (1, 2 * NK), 1) >= NK, NK, 0)

        def masks(two):
            idp = ids_ref[two, :]
            ids = jnp.concatenate([idp[0:1], idp[1:2]], axis=1)
            onehot = jnp.where(io_b == (ids & (NK - 1)) + second, 1.0, 0.0).astype(BF16)
            return onehot, io_a == (ids >> 7)

        def pick(p, carry):
            two = pl.ds(pl.multiple_of(p * 2, 2), 2)
            onehot, r1 = masks(two)
            a2 = a_sc[two]
            at = jnp.concatenate([a2[0], a2[1]], axis=1)
            picked = _dot(at, onehot)
            act = jnp.sum(jnp.where(r1, picked, 0.0), axis=0, keepdims=True)
            gp = gate_ref[two, :]
            w = jnp.concatenate([gp[0:1], gp[1:2]], axis=1) * _gelu(act)
            wgt_sc[two, :] = jnp.concatenate([w[:, :NK], w[:, NK:]], axis=0)
            return carry

        def scatter(p, carry):
            two = pl.ds(pl.multiple_of(p * 2, 2), 2)
            onehot, r1 = masks(two)
            wp = wgt_sc[two, :]
            w = jnp.concatenate([wp[0:1], wp[1:2]], axis=1)
            wt = _dot_nt(jnp.where(r1, w, 0.0).astype(BF16), onehot)
            w_sc[two] = jnp.stack([wt[:, :NK], wt[:, NK:]], axis=0).astype(BF16)
            return carry

        lax.fori_loop(0, TM_E // 2, pick, 0, unroll=16)
        lax.fori_loop(0, TM_E // 2, scatter, 0, unroll=16)
        o_ref[...] = jnp.zeros_like(o_ref)
        relayout_out(0, 0)

    for slot in (0, 1):
        @pl.when(jnp.logical_and(jnp.logical_and(j >= nchunk, j < 2 * nchunk - 1), (j - nchunk) % 2 == slot))
        def _():
            weighted_sum(slot)
            relayout_out(j - nchunk + 1, 1 - slot)

    @pl.when(j == 2 * nchunk - 1)
    def _():
        weighted_sum((nchunk - 1) % 2)
        for half, mod_ref in enumerate((mod0_ref, mod1_ref)):
            rows = slice(half * TM, (half + 1) * TM)
            y = ALPHA * x_ref[rows, :] + mod_ref[0, 0][5:6, :] * o_ref[rows, :]
            o_ref[rows, :] = _ln(y) * lng_ref[...] + lnb_ref[...]


def peer_experts(t, ids, gates, emb_ut, emb_v, h1, mod, lng, lnb, nt_per_sample, nct):
    T, D = h1.shape
    E = emb_v.shape[0]
    ce = PEER_CA * PEER_NKEYS
    nchunk = E // ce
    assert T % TM_E == 0 and TM_E == 2 * TM
    tok = lambda n, **kw: pl.BlockSpec((TM_E, n), lambda i, j: (i, 0), **kw)

    def mod_spec(half):
        def index(i, j):
            r = 2 * i + half
            return (r // nt_per_sample, jnp.where(r % nt_per_sample >= nct, 1, 0), 0, 0)
        return pl.BlockSpec((1, 1, 6, D), index)
    once = dict(pipeline_mode=pl.Buffered(1))
    return pl.pallas_call(
        functools.partial(_peer_expert_kernel, nchunk=nchunk),
        out_shape=jax.ShapeDtypeStruct((T, D), F32), grid=(T // TM_E, 2 * nchunk),
        in_specs=[tok(D, **once), tok(LANES, **once), tok(LANES, **once),
                  pl.BlockSpec((D, ce), lambda i, j: (0, jnp.minimum(j, nchunk - 1))),
                  pl.BlockSpec((ce, D), lambda i, j: (jnp.maximum(j - nchunk, 0), 0)),
                  tok(D, **once), mod_spec(0), mod_spec(1),
                  pl.BlockSpec((1, D), lambda i, j: (0, 0)), pl.BlockSpec((1, D), lambda i, j: (0, 0))],
        out_specs=tok(D),
        scratch_shapes=[pltpu.VMEM((TM_E, PEER_NKEYS, PEER_NKEYS), BF16), pltpu.VMEM((2, TM_E, ce), BF16),
                        pltpu.VMEM((2, TM_E, ce), BF16), pltpu.VMEM((TM_E, LANES), F32)],
        compiler_params=_cparams(("parallel", "arbitrary")), name="peer_experts")(t, ids, gates, emb_ut, emb_v, h1, mod, mod, lng, lnb)


_ROPE_IDX = np.arange(HD)
_ROPE_PERM = np.where(_ROPE_IDX % 32 < 16, _ROPE_IDX + 16, _ROPE_IDX - 16)
_ROPE_SIGN = np.where(_ROPE_IDX % 32 < 16, -1.0, 1.0).astype(np.float32)


def _rope_partner(w):
    n = w.shape[1] // HD
    perm = np.concatenate([h * HD + _ROPE_PERM for h in range(n)])
    return w[:, perm] * jnp.asarray(np.tile(_ROPE_SIGN, n))


def _rope_tables(L, nctx, gq, gk):
    rows = L // GRID_W
    row = jnp.repeat(jnp.arange(rows, dtype=F32), GRID_W)
    col = jnp.tile(jnp.arange(GRID_W, dtype=F32), rows)
    nf = HD // 4
    inv = ROPE_THETA ** (-jnp.arange(nf, dtype=F32) / nf)
    ar, ac = row[:, None] * inv, col[:, None] * inv
    cos = jnp.concatenate([jnp.cos(ar), jnp.cos(ar), jnp.cos(ac), jnp.cos(ac)], axis=1)
    sin = jnp.concatenate([jnp.sin(ar), jnp.sin(ar), jnp.sin(ac), jnp.sin(ac)], axis=1)
    cos = jnp.concatenate([jnp.ones((nctx, HD), F32), cos], axis=0)
    sin = jnp.concatenate([jnp.zeros((nctx, HD), F32), sin], axis=0)
    scale = HD ** -0.5
    one = jnp.ones((HD,), F32)
    tabs = []
    for g, sc in ((one, scale), (one, 1.0), (gq, scale), (gk, 1.0)):
        tabs += [cos * (g * sc), sin * (g[_ROPE_PERM] * sc)]
    return jnp.tile(jnp.stack(tabs), (1, 1, 2))


def _s5_tables(a_re, a_im, log_dt, b_re, b_im, c_re, c_im, nb):
    dt = jnp.exp(log_dt)[..., None]
    mag = jnp.exp(a_re * dt)
    lr, li = mag * jnp.cos(a_im * dt), mag * jnp.sin(a_im * dt)
    den = a_re * a_re + a_im * a_im
    cr = ((lr - 1) * a_re + li * a_im) / den
    ci = (li * a_re - (lr - 1) * a_im) / den
    br = cr[..., None] * b_re - ci[..., None] * b_im
    bi = cr[..., None] * b_im + ci[..., None] * b_re
    G2, N, C = S5_GROUPS // 2, S5_STATE, S5_GROUP
    eye = jnp.eye(G2, dtype=F32)

    def blockdiag_in(m):
        m = m.reshape(2, 2, G2, N, C)
        return jnp.einsum('dhgnc,gk->dhgckn', m, eye).reshape(2, 2, G2 * C, G2 * N)

    def blockdiag_out(m):
        m = m.reshape(2, 2, G2, C, N)
        return jnp.einsum('dhgcn,gk->dhgnkc', m, eye).reshape(2, 2, G2 * N, G2 * C)

    bblk = jnp.concatenate([blockdiag_in(br), blockdiag_in(bi)], axis=-1).astype(BF16)
    cblk = jnp.concatenate([blockdiag_out(c_re), blockdiag_out(-c_im)], axis=-2).astype(BF16)
    lam = jnp.stack([lr, li], axis=1).reshape(2, 2, 2, G2 * N)
    lam = jnp.tile(lam[:, :, None], (1, 1, nb, 1, 1)).reshape(2, 2, 2 * nb, G2 * N)
    return bblk, cblk, lam


def _blockdiag_ones(group, n=LANES):
    i = np.arange(n) // group
    return jnp.asarray((i[:, None] == i[None, :]).astype(np.float32), dtype=BF16)


def kernel(x, c, ctx, c_ctx, ada_w, ada_b, w_in, b_gate, diff_lam, diff_norm_g, gqa_qnorm_g, gqa_knorm_g,
           s5_a_re, s5_a_im, s5_log_dt, s5_b_re, s5_b_im, s5_c_re, s5_c_im, s5_d, s5_w_glu, s5_b_glu,
           ml_conv_w, ml_conv_b, ml_gate_b, ml_norm_g, w_branch, w_out, ln_mix_g, ln_mix_b, ln_ffn_g, ln_ffn_b,
           peer_wq, peer_subkeys, peer_u, peer_v):
    B, L, D = x.shape
    nctx = ctx.shape[1]
    S = nctx + L
    assert D == D_MODEL and nctx % TM == 0 and L % TM == 0 and L % GRID_W == 0
    nt, nct = S // TM, nctx // TM
    depth = ada_w.shape[0]

    h = jnp.concatenate([ctx, x], axis=1)
    R = -(-(B + 1) // 8) * 8
    cond = jnp.zeros((R, D), F32).at[:B].set(c).at[B].set(c_ctx)
    ones64, ones128 = _blockdiag_ones(HD), _blockdiag_ones(ML_HD, ML_HEADS * ML_HD)
    o = IN_OFFS

    for l in range(depth):
        lam_init = 0.8 - 0.6 * math.exp(-0.3 * l)
        m = ada_modulation(cond, ada_w[l].astype(BF16), ada_b[l][None, :])
        mod = jnp.stack([jnp.broadcast_to(m[B], (B, 6 * D)), m[:B]], axis=1).reshape(B, 2, 6, D)

        w = w_in[l]
        seg = lambda i: w[:, o[i]:o[i + 1]]
        dup = lambda t: jnp.concatenate([t[:, :HD], t[:, :HD], t[:, HD:], t[:, HD:]], axis=1)
        w_a = jnp.concatenate([seg(0), _rope_partner(seg(0)), seg(1), _rope_partner(seg(1)), seg(2),
                               seg(9), _rope_partner(seg(9)), dup(seg(10)), dup(_rope_partner(seg(10))),
                               dup(seg(11))], axis=1).astype(BF16)
        w_b = jnp.concatenate([seg(3), seg(4), seg(5), seg(6), seg(7), seg(8),
                               jnp.zeros((D, LANES - 16), F32)], axis=1).astype(BF16)
        gate_b = jnp.concatenate([ml_gate_b[l], jnp.zeros((LANES - 16,), F32)])[None, :]
        tab = _rope_tables(L, nctx, gqa_qnorm_g[l], gqa_knorm_g[l])

        dq, dk, dv, gq, gk, gv = proj_attn(h, mod, w_a, tab, ones64, nct)
        u, mqk, mv, mo, mg = proj_seq(h, mod, w_b, gate_b, nct)

        yd = diff_attention(dq, dk, dv, diff_lam[l], diff_norm_g[l][None, :], nctx, lam_init)
        yg = gqa_attention(gq, gk, gv, nctx)

        bblk, cblk, lam = _s5_tables(s5_a_re[l], s5_a_im[l], s5_log_dt[l], s5_b_re[l], s5_b_im[l],
                                     s5_c_re[l], s5_c_im[l], B)
        sf, sb = (y.reshape(u.shape) for y in s5_scan(u.reshape(S * 2 * B, 256), bblk, cblk, lam, S, nctx))

        mq, mk = ml_prep(mqk, ml_conv_w[l], ml_conv_b[l][None, :], nctx)
        hf = mlstm_scan(mq, mk, mv, mg, nctx, False)
        hb = mlstm_scan(mq, mk, mv, mg, nctx, True)

        h1 = merge(h, mod, yd, yg, u, sf, sb, hf, hb, mo,
                   s5_d[l][None, :], s5_w_glu[l].astype(BF16), s5_b_glu[l][None, :], ml_norm_g[l][None, :],
                   seg(12).astype(BF16), b_gate[l][None, :], w_branch[l].astype(BF16), w_out[l].astype(BF16),
                   ln_mix_g[l][None, :], ln_mix_b[l][None, :], ones128, nct)

        t, ids, gates = peer_route(h1, mod, peer_wq[l].astype(BF16), peer_subkeys[l].astype(BF16), nct)
        h = peer_experts(t.reshape(B * S, D), ids.T, gates.T, peer_u[l].astype(BF16).T, peer_v[l].astype(BF16),
                         h1.reshape(B * S, D), mod, ln_ffn_g[l][None, :], ln_ffn_b[l][None, :], nt, nct).reshape(B, S, D)
    return h[:, nctx:]
```

```python
import functools
import math

import numpy as np
import jax
import jax.numpy as jnp
from jax import lax
from jax.experimental import pallas as pl
from jax.experimental.pallas import tpu as pltpu

F32 = jnp.float32
BF16 = jnp.bfloat16
I32 = jnp.int32

D_MODEL = 1024
DEPTH = 2
GRID_W = 64
ROPE_THETA = 10000.0
LN_EPS = 1e-6
HD = 64
DIFF_HEADS = 4
GQA_HEADS = 8
GQA_KV = 2
S5_GROUP = 16
S5_GROUPS = 32
S5_STATE = 64
ML_HEADS = 4
ML_HD = 128
N_BRANCH = 4
W_BRANCH = 512
PEER_HEADS = 8
PEER_NKEYS = 128
PEER_TOPK = 16
PEER_DQ = 256
ALPHA = (2 * DEPTH) ** 0.25

LANES = 128
TM = 256
TM_E = 2 * TM
TQ = 256
TK = 2048
S5_TC = 128
ML_T = 256
ML_LOCKSTEP = 4
PEER_CA = 16
VMEM_LIMIT = 56 * 1024 * 1024

IN_SPLITS = (512, 512, 512, 512, 512, 512, 512, 512, 16, 512, 128, 128, N_BRANCH * D_MODEL)
IN_OFFS = tuple(int(v) for v in np.cumsum((0,) + IN_SPLITS))


def _cparams(sem):
    return pltpu.CompilerParams(dimension_semantics=sem, vmem_limit_bytes=VMEM_LIMIT)


def _const_spec(shape):
    nd = len(shape)
    return pl.BlockSpec(shape, lambda *_: (0,) * nd, pipeline_mode=pl.Buffered(1))


def _ln(x):
    xc = x - jnp.mean(x, axis=-1, keepdims=True)
    return xc * lax.rsqrt(jnp.mean(xc * xc, axis=-1, keepdims=True) + LN_EPS)


def _sigmoid(x):
    return 1.0 / (1.0 + jnp.exp(-x))


def _gelu(x):
    return 0.5 * x * (1.0 + lax.erf(x * (2.0 ** -0.5)))


def _dot(a, b):
    return jnp.dot(a, b, preferred_element_type=F32)


def _dot_nt(a, b):
    return lax.dot_general(a, b, (((1,), (1,)), ((), ())), preferred_element_type=F32)


def _dot_tn(a, b):
    return lax.dot_general(a, b, (((0,), (0,)), ((), ())), preferred_element_type=F32)


def _ada_kernel(c_ref, w_ref, b_ref, o_ref):
    c = c_ref[...]
    o_ref[...] = _dot((c * _sigmoid(c)).astype(BF16), w_ref[...]) + b_ref[...]


def ada_modulation(cond, w, b):
    R, D = cond.shape
    N = w.shape[1]
    tn = 1536
    return pl.pallas_call(
        _ada_kernel, out_shape=jax.ShapeDtypeStruct((R, N), F32), grid=(N // tn,),
        in_specs=[pl.BlockSpec((R, D), lambda j: (0, 0)), pl.BlockSpec((D, tn), lambda j: (0, j)),
                  pl.BlockSpec((1, tn), lambda j: (0, j))],
        out_specs=pl.BlockSpec((R, tn), lambda j: (0, j)),
        compiler_params=_cparams(("arbitrary",)), name="ada")(cond, w, b)


def _proj_attn_kernel(x_ref, mod_ref, w_ref, tab_ref, ones_ref,
                      dq_ref, dk_ref, dv_ref, gq_ref, gk_ref, gv_ref):
    mod = mod_ref[0, 0]
    xm = (_ln(x_ref[0]) * (1.0 + mod[1:2, :]) + mod[0:1, :]).astype(BF16)

    def mm(lo, n):
        return _dot(xm, w_ref[:, lo:lo + n])

    def rope_store(ref, lo, n, ci, norm):
        t, tp = mm(lo, n), mm(lo + n, n)
        c, s = tab_ref[ci], tab_ref[ci + 1]
        for j in range(n // LANES):
            sl = slice(j * LANES, (j + 1) * LANES)
            tb = t[:, sl]
            y = tb * c + tp[:, sl] * s
            if norm:
                ss = _dot((tb * tb).astype(BF16), ones_ref[...])
                y = y * lax.rsqrt(ss * (1.0 / HD) + LN_EPS)
            ref[0, :, sl] = y.astype(ref.dtype)

    rope_store(dq_ref, 0, 512, 0, False)
    rope_store(dk_ref, 1024, 512, 2, False)
    dv_ref[0] = mm(2048, 512).astype(dv_ref.dtype)
    rope_store(gq_ref, 2560, 512, 4, True)
    rope_store(gk_ref, 3584, 256, 6, True)
    gv_ref[0] = mm(4096, 256).astype(gv_ref.dtype)


def proj_attn(xa, mod, w_a, tab, ones_bd, nct):
    B, S, D = xa.shape
    tok = lambda n: pl.BlockSpec((1, TM, n), lambda b, i: (b, i, 0))
    outs = [jax.ShapeDtypeStruct((B, S, n), BF16) for n in (512, 512, 512, 512, 256, 256)]
    return pl.pallas_call(
        _proj_attn_kernel, out_shape=outs, grid=(B, S // TM),
        in_specs=[tok(D),
                  pl.BlockSpec((1, 1, 6, D), lambda b, i: (b, jnp.where(i >= nct, 1, 0), 0, 0)),
                  _const_spec(w_a.shape),
                  pl.BlockSpec((8, TM, LANES), lambda b, i: (0, i, 0)),
                  _const_spec(ones_bd.shape)],
        out_specs=[tok(n) for n in (512, 512, 512, 512, 256, 256)],
        compiler_params=_cparams(("parallel", "parallel")), name="proj_attn")(xa, mod, w_a, tab, ones_bd)


def _proj_seq_kernel(x_ref, mod_ref, w_ref, gb_ref, u_ref, qk_ref, v_ref, o_ref, g_ref):
    mod = mod_ref[0, 0]
    xm = (_ln(x_ref[0]) * (1.0 + mod[1:2, :]) + mod[0:1, :]).astype(BF16)
    u_ref[...] = _dot(xm, w_ref[:, 0:512])
    qk_ref[0] = _dot(xm, w_ref[:, 512:1536])
    v_ref[0] = _dot(xm, w_ref[:, 1536:2048])
    o_ref[0] = _dot(xm, w_ref[:, 2048:2560])
    g_ref[0] = _dot(xm, w_ref[:, 2560:2688]) + gb_ref[...]


def proj_seq(xa, mod, w_b, gate_b, nct):
    B, S, D = xa.shape
    tok = lambda n: pl.BlockSpec((1, TM, n), lambda b, i: (b, i, 0))
    widths = (1024, 512, 512, LANES)
    return pl.pallas_call(
        _proj_seq_kernel,
        out_shape=[jax.ShapeDtypeStruct((S, B * 512), F32)] + [jax.ShapeDtypeStruct((B, S, n), F32) for n in widths],
        grid=(B, S // TM),
        in_specs=[tok(D),
                  pl.BlockSpec((1, 1, 6, D), lambda b, i: (b, jnp.where(i >= nct, 1, 0), 0, 0)),
                  _const_spec(w_b.shape), _const_spec(gate_b.shape)],
        out_specs=[pl.BlockSpec((TM, 512), lambda b, i: (i, b))] + [tok(n) for n in widths],
        compiler_params=_cparams(("parallel", "parallel")), name="proj_seq")(xa, mod, w_b, gate_b)


def _flash(qq, k_ref, v_ref, nctx, n_lat, tk):
    R = qq.shape[0]

    def chunk(carry, rows):
        m, l, acc = carry
        s = _dot_nt(qq, k_ref[0, rows, :])
        m_new = jnp.maximum(m, jnp.max(s, axis=1, keepdims=True))
        alpha = jnp.exp(m - m_new)
        p = jnp.exp(s - m_new)
        l = alpha * l + jnp.sum(p, axis=1, keepdims=True)
        acc = alpha * acc + _dot(p.astype(BF16), v_ref[0, rows, :])
        return m_new, l, acc

    carry = (jnp.full((R, 1), -jnp.inf, F32), jnp.zeros((R, 1), F32), jnp.zeros((R, LANES), F32))
    carry = chunk(carry, pl.ds(0, nctx))
    for c in range(n_lat):
        carry = chunk(carry, pl.ds(nctx + c * tk, tk))
    _, l, acc = carry
    return acc / l


def _ctx_or_all(i, nctx, n_lat, attend):
    @pl.when(i < nctx // TQ)
    def _():
        attend(0)

    @pl.when(i >= nctx // TQ)
    def _():
        attend(n_lat)


def _diff_attn_kernel(q_ref, k_ref, v_ref, lam_ref, g_ref, o_ref, *, nctx, n_lat, tk, lam_init):
    i = pl.program_id(2)
    q = q_ref[0]
    lane = lax.broadcasted_iota(I32, q.shape, 1)
    zero = jnp.zeros_like(q)
    qq = jnp.concatenate([jnp.where(lane < HD, q, zero), jnp.where(lane >= HD, q, zero)], axis=0)
    lv = lam_ref[...]
    lam = (jnp.exp(jnp.sum(lv[0:1] * lv[1:2], axis=1, keepdims=True))
           - jnp.exp(jnp.sum(lv[2:3] * lv[3:4], axis=1, keepdims=True)) + lam_init)

    def attend(n):
        o = _flash(qq, k_ref, v_ref, nctx, n, tk)
        d = o[:TQ] - lam * o[TQ:]
        y = d * lax.rsqrt(jnp.mean(d * d, axis=-1, keepdims=True) + LN_EPS)
        o_ref[0] = (y * g_ref[...] * (1.0 - lam_init)).astype(o_ref.dtype)

    _ctx_or_all(i, nctx, n_lat, attend)


def diff_attention(q, k, v, lam_vec, norm_g, nctx, lam_init):
    B, S, _ = q.shape
    tk = math.gcd(S - nctx, TK)
    kern = functools.partial(_diff_attn_kernel, nctx=nctx, n_lat=(S - nctx) // tk, tk=tk, lam_init=lam_init)
    return pl.pallas_call(
        kern, out_shape=jax.ShapeDtypeStruct((B, S, 512), BF16), grid=(B, DIFF_HEADS, S // TQ),
        in_specs=[pl.BlockSpec((1, TQ, LANES), lambda b, h, i: (b, i, h)),
                  pl.BlockSpec((1, S, LANES), lambda b, h, i: (b, 0, h)),
                  pl.BlockSpec((1, S, LANES), lambda b, h, i: (b, 0, h)),
                  pl.BlockSpec((4, HD), lambda b, h, i: (0, 0)),
                  pl.BlockSpec((1, LANES), lambda b, h, i: (0, 0))],
        out_specs=pl.BlockSpec((1, TQ, LANES), lambda b, h, i: (b, i, h)),
        compiler_params=_cparams(("parallel", "parallel", "parallel")), name="diff_attn")(q, k, v, lam_vec, norm_g)


def _gqa_attn_kernel(q_ref, k_ref, v_ref, o_ref, *, nctx, n_lat, tk):
    i = pl.program_id(2)
    q = q_ref[0]
    lane = lax.broadcasted_iota(I32, (TQ, LANES), 1)
    zero = jnp.zeros((TQ, LANES), q.dtype)
    parts = []
    for j in range(2):
        blk = q[:, j * LANES:(j + 1) * LANES]
        parts += [jnp.where(lane < HD, blk, zero), jnp.where(lane >= HD, blk, zero)]
    qq = jnp.concatenate(parts, axis=0)

    def attend(n):
        o = _flash(qq, k_ref, v_ref, nctx, n, tk)
        for j in range(2):
            pair = jnp.where(lane < HD, o[(2 * j) * TQ:(2 * j + 1) * TQ], o[(2 * j + 1) * TQ:(2 * j + 2) * TQ])
            o_ref[0, :, j * LANES:(j + 1) * LANES] = pair.astype(o_ref.dtype)

    _ctx_or_all(i, nctx, n_lat, attend)


def gqa_attention(q, k, v, nctx):
    B, S, _ = q.shape
    tk = math.gcd(S - nctx, TK)
    kern = functools.partial(_gqa_attn_kernel, nctx=nctx, n_lat=(S - nctx) // tk, tk=tk)
    return pl.pallas_call(
        kern, out_shape=jax.ShapeDtypeStruct((B, S, 512), BF16), grid=(B, GQA_KV, S // TQ),
        in_specs=[pl.BlockSpec((1, TQ, 2 * LANES), lambda b, g, i: (b, i, g)),
                  pl.BlockSpec((1, S, LANES), lambda b, g, i: (b, 0, g)),
                  pl.BlockSpec((1, S, LANES), lambda b, g, i: (b, 0, g))],
        out_specs=pl.BlockSpec((1, TQ, 2 * LANES), lambda b, g, i: (b, i, g)),
        compiler_params=_cparams(("parallel", "parallel", "parallel")), name="gqa_attn")(q, k, v)


def _s5_kernel(uf_ref, ub_ref, bblk_ref, cblk_ref, lam_ref, yf_ref, yb_ref, buf_f, buf_b, st_ref):
    H = S5_GROUPS * S5_STATE // 2
    R = buf_f.shape[0]
    RB = 256

    @pl.when(pl.program_id(0) == 0)
    def _():
        st_ref[...] = jnp.zeros_like(st_ref)

    half0 = (lax.broadcasted_iota(I32, (RB, 1), 0) & 1) == 0

    for d, (u_ref, buf) in enumerate(((uf_ref, buf_f), (ub_ref, buf_b))):
        for rb in range(R // RB):
            rows = slice(rb * RB, (rb + 1) * RB)
            uh = u_ref[rows, :].astype(BF16)
            for cols in (slice(0, H), slice(H, 2 * H)):
                buf[rows, cols] = jnp.where(half0, _dot(uh, bblk_ref[d, 0, :, cols]), _dot(uh, bblk_ref[d, 1, :, cols]))

    lfr, lfi, lbr, lbi = lam_ref[0, 0], lam_ref[0, 1], lam_ref[1, 0], lam_ref[1, 1]

    def step(t, carry):
        fr, fi, br, bi = carry
        rf = pl.ds(pl.multiple_of(t * 8, 8), 8)
        x = buf_f[rf, :]
        nfr = lfr * fr - lfi * fi + x[:, :H]
        nfi = lfr * fi + lfi * fr + x[:, H:]
        buf_f[rf, :] = jnp.concatenate([nfr, nfi], axis=1)
        rb = pl.ds(pl.multiple_of((S5_TC - 1 - t) * 8, 8), 8)
        z = buf_b[rb, :]
        nbr = lbr * br - lbi * bi + z[:, :H]
        nbi = lbr * bi + lbi * br + z[:, H:]
        buf_b[rb, :] = jnp.concatenate([nbr, nbi], axis=1)
        return nfr, nfi, nbr, nbi

    fin = lax.fori_loop(0, S5_TC, step, (st_ref[0], st_ref[1], st_ref[2], st_ref[3]))
    for j in range(4):
        st_ref[j] = fin[j]

    for d, (y_ref, buf) in enumerate(((yf_ref, buf_f), (yb_ref, buf_b))):
        for rb in range(R // RB):
            rows = slice(rb * RB, (rb + 1) * RB)
            h = buf[rows, :].astype(BF16)
            y_ref[rows, :] = jnp.where(half0, _dot(h, cblk_ref[d, 0]), _dot(h, cblk_ref[d, 1]))


def s5_scan(u, bblk, cblk, lam, S, nctx):
    assert u.shape[0] == S * 8, "the scan packs (sample, half) pairs into the eight sublanes of a vreg"
    nch, nc0 = S // S5_TC, nctx // S5_TC

    def bwd(i):
        return jnp.where(i < nc0, nc0 - 1 - i, (nch - 1) - (i - nc0))
    H2 = S5_GROUPS * S5_STATE
    R, W = S5_TC * 8, u.shape[1]
    blk = lambda f: pl.BlockSpec((R, W), f)
    return pl.pallas_call(
        _s5_kernel, out_shape=[jax.ShapeDtypeStruct(u.shape, F32)] * 2, grid=(nch,),
        in_specs=[blk(lambda i: (i, 0)), blk(lambda i: (bwd(i), 0)),
                  _const_spec(bblk.shape), _const_spec(cblk.shape), _const_spec(lam.shape)],
        out_specs=[blk(lambda i: (i, 0)), blk(lambda i: (bwd(i), 0))],
        scratch_shapes=[pltpu.VMEM((R, H2), F32), pltpu.VMEM((R, H2), F32), pltpu.VMEM((4, 8, H2 // 2), F32)],
        compiler_params=_cparams(("arbitrary",)), name="s5_scan")(u, u, bblk, cblk, lam)


def _ml_prep_kernel(x_ref, prev_ref, next_ref, w_ref, b_ref, q_ref, k_ref, *, seg_starts, seg_ends):
    i = pl.program_id(1)
    x = x_ref[0]
    row = lax.broadcasted_iota(I32, x.shape, 0)
    first = functools.reduce(jnp.logical_or, [i == s for s in seg_starts])
    last = functools.reduce(jnp.logical_or, [i == s for s in seg_ends])
    pr = jnp.where(first, 0.0, prev_ref[0, 7:8, :])
    nx = jnp.where(last, 0.0, next_ref[0, 0:1, :])
    xp = jnp.where(row == 0, pr, pltpu.roll(x, 1, 0))
    xn = jnp.where(row == TM - 1, nx, pltpu.roll(x, TM - 1, 0))
    w = w_ref[...]
    y = b_ref[...] + xp * w[0:1, :] + x * w[1:2, :] + xn * w[2:3, :]
    y = y * _sigmoid(y)
    q_ref[0] = y[:, :512].astype(q_ref.dtype)
    k_ref[0] = (y[:, 512:] * (ML_HD ** -0.5)).astype(k_ref.dtype)


def ml_prep(qk, conv_w, conv_b, nctx):
    B, S, W = qk.shape
    nt, nct, r8 = S // TM, nctx // TM, TM // 8
    kern = functools.partial(_ml_prep_kernel, seg_starts=(0, nct), seg_ends=(nct - 1, nt - 1))
    return pl.pallas_call(
        kern, out_shape=[jax.ShapeDtypeStruct((B, S, 512), BF16)] * 2, grid=(B, nt),
        in_specs=[pl.BlockSpec((1, TM, W), lambda b, i: (b, i, 0)),
                  pl.BlockSpec((1, 8, W), lambda b, i: (b, jnp.maximum(i * r8 - 1, 0), 0)),
                  pl.BlockSpec((1, 8, W), lambda b, i: (b, jnp.minimum((i + 1) * r8, S // 8 - 1), 0)),
                  pl.BlockSpec((3, W), lambda b, i: (0, 0)), pl.BlockSpec((1, W), lambda b, i: (0, 0))],
        out_specs=[pl.BlockSpec((1, TM, 512), lambda b, i: (b, i, 0))] * 2,
        compiler_params=_cparams(("parallel", "parallel")), name="ml_prep")(qk, qk, qk, conv_w, conv_b)


def _log_sigmoid(x):
    return jnp.minimum(x, 0.0) - jnp.log1p(jnp.exp(-jnp.abs(x)))


def _mlstm_kernel(q_ref, k_ref, v_ref, g_ref, h_ref, c_sc, n_sc, m_sc, *, reverse):
    T = ML_T
    gi, gf = (8, 12) if reverse else (0, 4)

    @pl.when(pl.program_id(0) == 0)
    def _():
        c_sc[...] = jnp.zeros_like(c_sc)
        n_sc[...] = jnp.zeros_like(n_sc)
        m_sc[...] = jnp.zeros_like(m_sc)

    r = lax.broadcasted_iota(I32, (T, T), 0)
    c = lax.broadcasted_iota(I32, (T, T), 1)
    mask = (c >= r) if reverse else (c <= r)
    tri = jnp.where(mask, 1.0, 0.0)
    hp = lax.Precision.HIGHEST
    end = 0 if reverse else T - 1
    def gates(b):
        g = g_ref[b]
        gt = g.T
        bcol_all = jnp.dot(tri, _log_sigmoid(g), preferred_element_type=F32, precision=hp)
        brow_all = lax.dot_general(_log_sigmoid(gt[0:16]), tri, (((1,), (1,)), ((), ())),
                                   preferred_element_type=F32, precision=hp)
        return g, gt, bcol_all, brow_all

    def chain(b, hh, g, gt, bcol_all, brow_all):
        st = b * ML_HEADS + hh
        sl = slice(hh * ML_HD, (hh + 1) * ML_HD)
        q, k, v = q_ref[b, :, sl], k_ref[b, :, sl], v_ref[b, :, sl]
        bcol, brow = bcol_all[:, gf + hh:gf + hh + 1], brow_all[gf + hh:gf + hh + 1, :]
        icol, irow = g[:, gi + hh:gi + hh + 1], gt[gi + hh:gi + hh + 1, :]
        m_old = m_sc[st][:, 0:1]
        n_old = n_sc[st]
        c_old = c_sc[st]
        logw = jnp.where(mask, bcol - brow + irow, -jnp.inf)
        m_inter = bcol + m_old
        m_t = jnp.maximum(m_inter, jnp.max(logw, axis=1, keepdims=True))
        yield
        s = _dot_nt(q, k) * jnp.exp(logw - m_t)
        inter = jnp.exp(m_inter - m_t)
        yield
        num = _dot(s.astype(BF16), v.astype(BF16)) + inter * _dot_nt(q, c_old.astype(BF16))
        den = jnp.sum(s, axis=1, keepdims=True) + inter * jnp.sum(q.astype(F32) * n_old, axis=1, keepdims=True)
        h_ref[b, :, sl] = num / jnp.maximum(jnp.abs(den), jnp.exp(-m_t))
        yield
        b_end = bcol[end:end + 1, :]
        g_row, g_col = b_end - brow + irow, b_end - bcol + icol
        m_new = jnp.maximum(b_end + m_old, jnp.max(g_row, axis=1, keepdims=True))
        decay = jnp.exp(b_end + m_old - m_new)
        wk = jnp.exp(g_col - m_new)
        c_sc[st] = decay * c_old + _dot_tn((v * wk).astype(BF16), k)
        n_sc[st] = decay * n_old + jnp.sum(k.astype(F32) * wk, axis=0, keepdims=True)
        m_sc[st] = jnp.broadcast_to(m_new, (1, LANES))

    nb = q_ref.shape[0]
    for b0 in range(0, nb, ML_LOCKSTEP):
        chains = []
        for b in range(b0, min(b0 + ML_LOCKSTEP, nb)):
            shared = gates(b)
            chains += [chain(b, hh, *shared) for hh in range(ML_HEADS)]
        for _ in range(4):
            for ch in chains:
                next(ch, None)


def mlstm_scan(q, k, v, g, nctx, reverse):
    B, S, W = q.shape
    nch, nc0 = S // ML_T, nctx // ML_T

    def order(i):
        return jnp.where(i < nc0, nc0 - 1 - i, (nch - 1) - (i - nc0)) if reverse else i
    blk = lambda n: pl.BlockSpec((B, ML_T, n), lambda i: (0, order(i), 0))
    nst = B * ML_HEADS
    return pl.pallas_call(
        functools.partial(_mlstm_kernel, reverse=reverse),
        out_shape=jax.ShapeDtypeStruct((B, S, W), F32), grid=(nch,),
        in_specs=[blk(W), blk(W), blk(W), blk(LANES)], out_specs=blk(W),
        scratch_shapes=[pltpu.VMEM((nst, ML_HD, ML_HD), F32), pltpu.VMEM((nst, 1, ML_HD), F32),
                        pltpu.VMEM((nst, 1, LANES), F32)],
        compiler_params=_cparams(("arbitrary",)), name="mlstm_bwd" if reverse else "mlstm_fwd")(q, k, v, g)


def _merge_kernel(x_ref, mod_ref, yd_ref, yg_ref, u_ref, sf_ref, sb_ref, hf_ref, hb_ref, mo_ref,
                  s5d_ref, wglu_ref, bglu_ref, mlg_ref, wgate_ref, bgate_ref, wbr_ref, wout_ref,
                  lng_ref, lnb_ref, ones_ref, o_ref):
    x = x_ref[0]
    mod = mod_ref[0, 0]
    xm = (_ln(x) * (1.0 + mod[1:2, :]) + mod[0:1, :]).astype(BF16)
    ys = u_ref[...] * s5d_ref[...] + sf_ref[...] + sb_ref[...]
    z = _dot(_gelu(ys).astype(BF16), wglu_ref[...]) + bglu_ref[...]
    ys = z[:, :512] * _sigmoid(z[:, 512:])
    hm = hf_ref[0] + hb_ref[0]
    ss = _dot((hm * hm).astype(BF16), ones_ref[...])
    ym = hm * lax.rsqrt(ss * (1.0 / ML_HD) + LN_EPS) * mlg_ref[...] * _sigmoid(mo_ref[0])
    branches = (yd_ref[0], ys.astype(BF16), ym.astype(BF16), yg_ref[0])
    merged = None
    for j, yb in enumerate(branches):
        gate = _sigmoid(_dot(xm, wgate_ref[:, j * D_MODEL:(j + 1) * D_MODEL]) + bgate_ref[:, j * D_MODEL:(j + 1) * D_MODEL])
        term = gate * _dot(yb, wbr_ref[j])
        merged = term if merged is None else merged + term
    mix = _dot(merged.astype(BF16), wout_ref[...])
    o_ref[0] = _ln(ALPHA * x + mod[2:3, :] * mix) * lng_ref[...] + lnb_ref[...]


def merge(xa, mod, yd, yg, u, sf, sb, hf, hb, mo, s5d, wglu, bglu, mlg, wgate, bgate, wbr, wout, lng, lnb, ones_ml, nct):
    B, S, D = xa.shape
    tok = lambda n: pl.BlockSpec((1, TM, n), lambda b, i: (b, i, 0))
    consts = (s5d, wglu, bglu, mlg, wgate, bgate, wbr, wout, lng, lnb, ones_ml)
    return pl.pallas_call(
        _merge_kernel, out_shape=jax.ShapeDtypeStruct((B, S, D), F32), grid=(B, S // TM),
        in_specs=[tok(D), pl.BlockSpec((1, 1, 6, D), lambda b, i: (b, jnp.where(i >= nct, 1, 0), 0, 0))]
        + [tok(512)] * 2 + [pl.BlockSpec((TM, 512), lambda b, i: (i, b))] * 3 + [tok(512)] * 3
        + [_const_spec(a.shape) for a in consts],
        out_specs=tok(D),
        compiler_params=_cparams(("parallel", "parallel")), name="merge")(xa, mod, yd, yg, u, sf, sb, hf, hb, mo, *consts)


def _top16(*problems):
    def one(kk, s, val_ref, idx_ref, payload):
        R = s.shape[0]
        ri = lax.broadcasted_iota(I32, s.shape, 0).astype(F32)
        m = jnp.max(s, axis=0, keepdims=True)
        ix = jnp.min(jnp.where(s == m, ri, float(R)), axis=0, keepdims=True)
        hit = ri == ix
        val_ref[pl.ds(kk, 1), :] = m
        idx_ref[pl.ds(kk, 1), :] = ix if payload is None else jnp.max(jnp.where(hit, payload, -1.0), axis=0, keepdims=True)
        return jnp.where(hit, -jnp.inf, s)

    def body(kk, ss):
        return tuple(one(kk, s, *prob[1:]) for s, prob in zip(ss, problems))

    lax.fori_loop(0, PEER_TOPK, body, tuple(prob[0] for prob in problems))


_PEER_CAND_ROWS = ((0, 16),) + tuple((16 + 8 * (p - 1), 8) for p in range(1, 8))
_PEER_NCAND = _PEER_CAND_ROWS[-1][0] + 16


def _peer_route_kernel(x_ref, mod_ref, wq_ref, sk_ref, t_ref, ids_ref, gate_ref,
                       xm_sc, v1_sc, i1_sc, v2_sc, i2_sc, cand_sc, cidx_sc, top_sc, tid_sc):
    h = pl.program_id(2)

    @pl.when(h == 0)
    def _():
        mod = mod_ref[0, 0]
        xm = (_ln(x_ref[0]) * (1.0 + mod[4:5, :]) + mod[3:4, :]).astype(BF16)
        xm_sc[...] = xm
        t_ref[0] = xm

    q = _dot(xm_sc[...], wq_ref[...]).astype(BF16)
    half = PEER_DQ // 2
    _top16((_dot_nt(sk_ref[0], q[:, :half]), v1_sc, i1_sc, None))
    _top16((_dot_nt(sk_ref[1], q[:, half:]), v2_sc, i2_sc, None))
    nk = float(PEER_NKEYS)
    for p, (lo, n) in enumerate(_PEER_CAND_ROWS):
        cand_sc[lo:lo + n, :] = v1_sc[p:p + 1, :] + v2_sc[0:n, :]
        cidx_sc[lo:lo + n, :] = i1_sc[p:p + 1, :] * nk + i2_sc[0:n, :]
    lo = _PEER_CAND_ROWS[-1][0] + _PEER_CAND_ROWS[-1][1]
    cand_sc[lo:lo + 8, :] = v1_sc[8:16, :] + v2_sc[0:1, :]
    cidx_sc[lo:lo + 8, :] = i1_sc[8:16, :] * nk + i2_sc[0:1, :]
    _top16((cand_sc[...], top_sc, tid_sc, cidx_sc[...]))
    top = top_sc[...]
    e = jnp.exp(top - top[0:1, :])
    gate_ref[...] = e / jnp.sum(e, axis=0, keepdims=True)
    ids_ref[...] = tid_sc[...].astype(I32)


def peer_route(h1, mod, wq, subkeys, nct):
    B, S, D = h1.shape
    nt = S // TM
    f = lambda n: pltpu.VMEM((n, TM), F32)
    return pl.pallas_call(
        _peer_route_kernel,
        out_shape=[jax.ShapeDtypeStruct((B, S, D), BF16),
                   jax.ShapeDtypeStruct((PEER_HEADS * PEER_TOPK, B * S), I32),
                   jax.ShapeDtypeStruct((PEER_HEADS * PEER_TOPK, B * S), F32)],
        grid=(B, nt, PEER_HEADS),
        in_specs=[pl.BlockSpec((1, TM, D), lambda b, i, h: (b, i, 0)),
                  pl.BlockSpec((1, 1, 6, D), lambda b, i, h: (b, jnp.where(i >= nct, 1, 0), 0, 0)),
                  pl.BlockSpec((D, PEER_DQ), lambda b, i, h: (0, h)),
                  pl.BlockSpec((2, PEER_NKEYS, PEER_DQ // 2), lambda b, i, h: (0, 0, 0))],
        out_specs=[pl.BlockSpec((1, TM, D), lambda b, i, h: (b, i, 0)),
                   pl.BlockSpec((PEER_TOPK, TM), lambda b, i, h: (h, b * nt + i)),
                   pl.BlockSpec((PEER_TOPK, TM), lambda b, i, h: (h, b * nt + i))],
        scratch_shapes=[pltpu.VMEM((TM, D), BF16), f(16), f(16), f(16), f(16), f(_PEER_NCAND), f(_PEER_NCAND),
                        f(16), f(16)],
        compiler_params=_cparams(("parallel", "parallel", "arbitrary")), name="peer_route")(h1, mod, wq, subkeys)


def _peer_expert_kernel(t_ref, ids_ref, gate_ref, u_ref, v_ref, x_ref, mod0_ref, mod1_ref, lng_ref, lnb_ref, o_ref,
                        a_sc, raw_sc, wraw_sc, wgt_sc, *, nchunk):
    j = pl.program_id(1)
    NK, CA = PEER_NKEYS, PEER_CA
    w_sc = a_sc

    def dense(slot):
        raw_sc[slot] = _dot(t_ref[...], u_ref[...]).astype(BF16)

    def relayout_in(c, slot):
        a = raw_sc[slot]
        a3 = jnp.stack([a[:, al * NK:(al + 1) * NK] for al in range(CA)], axis=0)
        a_sc[:, pl.ds(pl.multiple_of(c * CA, CA), CA), :] = pltpu.einshape("atb->tab", a3)

    def relayout_out(c, slot):
        w3 = pltpu.einshape("tab->atb", w_sc[:, pl.ds(pl.multiple_of(c * CA, CA), CA), :])
        wraw_sc[slot] = jnp.concatenate([w3[al] for al in range(CA)], axis=1)

    def weighted_sum(slot):
        o_ref[...] += _dot(wraw_sc[slot], v_ref[...])

    @pl.when(j == 0)
    def _():
        dense(0)

    for slot in (0, 1):
        @pl.when(jnp.logical_and(jnp.logical_and(j > 0, j < nchunk), j % 2 == slot))
        def _():
            relayout_in(j - 1, 1 - slot)
            dense(slot)

    @pl.when(j == nchunk - 1)
    def _():
        relayout_in(j, (nchunk - 1) % 2)
        io_k = lax.broadcasted_iota(I32, (NK, NK), 0)
        io_a = lax.broadcasted_iota(I32, (NK, 2 * NK), 0)

        def masks(two):
            idp = ids_ref[two, :]
            ids = jnp.concatenate([idp[0:1], idp[1:2]], axis=1)
            blocks = [jnp.where(io_k == (idp[tok:tok + 1] & (NK - 1)), 1.0, 0.0).astype(BF16) for tok in range(2)]
            zero = jnp.zeros((NK, NK), BF16)
            onehot = jnp.concatenate([jnp.concatenate([blocks[0], zero], axis=1),
                                      jnp.concatenate([zero, blocks[1]], axis=1)], axis=0)
            return onehot, io_a == (ids >> 7)

        def pick(p, carry):
            two = pl.ds(pl.multiple_of(p * 2, 2), 2)
            onehot, r1 = masks(two)
            a2 = a_sc[two]
            at = jnp.concatenate([a2[0], a2[1]], axis=1)
            picked = _dot(at, onehot)
            act = jnp.sum(jnp.where(r1, picked, 0.0), axis=0, keepdims=True)
            gp = gate_ref[two, :]
            w = jnp.concatenate([gp[0:1], gp[1:2]], axis=1) * _gelu(act)
            wgt_sc[two, :] = jnp.concatenate([w[:, :NK], w[:, NK:]], axis=0)
            return carry

        def scatter(p, carry):
            two = pl.ds(pl.multiple_of(p * 2, 2), 2)
            onehot, r1 = masks(two)
            wp = wgt_sc[two, :]
            w = jnp.concatenate([wp[0:1], wp[1:2]], axis=1)
            wt = _dot_nt(jnp.where(r1, w, 0.0).astype(BF16), onehot)
            w_sc[two] = jnp.stack([wt[:, :NK], wt[:, NK:]], axis=0).astype(BF16)
            return carry

        lax.fori_loop(0, TM_E // 2, pick, 0, unroll=16)
        lax.fori_loop(0, TM_E // 2, scatter, 0, unroll=16)
        o_ref[...] = jnp.zeros_like(o_ref)
        relayout_out(0, 0)

    for slot in (0, 1):
        @pl.when(jnp.logical_and(jnp.logical_and(j >= nchunk, j < 2 * nchunk - 1), (j - nchunk) % 2 == slot))
        def _():
            weighted_sum(slot)
            relayout_out(j - nchunk + 1, 1 - slot)

    @pl.when(j == 2 * nchunk - 1)
    def _():
        weighted_sum((nchunk - 1) % 2)
        for half, mod_ref in enumerate((mod0_ref, mod1_ref)):
            rows = slice(half * TM, (half + 1) * TM)
            y = ALPHA * x_ref[rows, :] + mod_ref[0, 0][5:6, :] * o_ref[rows, :]
            o_ref[rows, :] = _ln(y) * lng_ref[...] + lnb_ref[...]


def peer_experts(t, ids, gates, emb_ut, emb_v, h1, mod, lng, lnb, nt_per_sample, nct):
    T, D = h1.shape
    E = emb_v.shape[0]
    ce = PEER_CA * PEER_NKEYS
    nchunk = E // ce
    assert T % TM_E == 0 and TM_E == 2 * TM
    tok = lambda n, **kw: pl.BlockSpec((TM_E, n), lambda i, j: (i, 0), **kw)

    def mod_spec(half):
        def index(i, j):
            r = 2 * i + half
            return (r // nt_per_sample, jnp.where(r % nt_per_sample >= nct, 1, 0), 0, 0)
        return pl.BlockSpec((1, 1, 6, D), index)
    once = dict(pipeline_mode=pl.Buffered(1))
    return pl.pallas_call(
        functools.partial(_peer_expert_kernel, nchunk=nchunk),
        out_shape=jax.ShapeDtypeStruct((T, D), F32), grid=(T // TM_E, 2 * nchunk),
        in_specs=[tok(D, **once), tok(LANES, **once), tok(LANES, **once),
                  pl.BlockSpec((D, ce), lambda i, j: (0, jnp.minimum(j, nchunk - 1))),
                  pl.BlockSpec((ce, D), lambda i, j: (jnp.maximum(j - nchunk, 0), 0)),
                  tok(D, **once), mod_spec(0), mod_spec(1),
                  pl.BlockSpec((1, D), lambda i, j: (0, 0)), pl.BlockSpec((1, D), lambda i, j: (0, 0))],
        out_specs=tok(D),
        scratch_shapes=[pltpu.VMEM((TM_E, PEER_NKEYS, PEER_NKEYS), BF16), pltpu.VMEM((2, TM_E, ce), BF16),
                        pltpu.VMEM((2, TM_E, ce), BF16), pltpu.VMEM((TM_E, LANES), F32)],
        compiler_params=_cparams(("parallel", "arbitrary")), name="peer_experts")(t, ids, gates, emb_ut, emb_v, h1, mod, mod, lng, lnb)


_ROPE_IDX = np.arange(HD)
_ROPE_PERM = np.where(_ROPE_IDX % 32 < 16, _ROPE_IDX + 16, _ROPE_IDX - 16)
_ROPE_SIGN = np.where(_ROPE_IDX % 32 < 16, -1.0, 1.0).astype(np.float32)


def _rope_partner(w):
    n = w.shape[1] // HD
    perm = np.concatenate([h * HD + _ROPE_PERM for h in range(n)])
    return w[:, perm] * jnp.asarray(np.tile(_ROPE_SIGN, n))


def _rope_tables(L, nctx, gq, gk):
    rows = L // GRID_W
    row = jnp.repeat(jnp.arange(rows, dtype=F32), GRID_W)
    col = jnp.tile(jnp.arange(GRID_W, dtype=F32), rows)
    nf = HD // 4
    inv = ROPE_THETA ** (-jnp.arange(nf, dtype=F32) / nf)
    ar, ac = row[:, None] * inv, col[:, None] * inv
    cos = jnp.concatenate([jnp.cos(ar), jnp.cos(ar), jnp.cos(ac), jnp.cos(ac)], axis=1)
    sin = jnp.concatenate([jnp.sin(ar), jnp.sin(ar), jnp.sin(ac), jnp.sin(ac)], axis=1)
    cos = jnp.concatenate([jnp.ones((nctx, HD), F32), cos], axis=0)
    sin = jnp.concatenate([jnp.zeros((nctx, HD), F32), sin], axis=0)
    scale = HD ** -0.5
    one = jnp.ones((HD,), F32)
    tabs = []
    for g, sc in ((one, scale), (one, 1.0), (gq, scale), (gk, 1.0)):
        tabs += [cos * (g * sc), sin * (g[_ROPE_PERM] * sc)]
    return jnp.tile(jnp.stack(tabs), (1, 1, 2))


def _s5_tables(a_re, a_im, log_dt, b_re, b_im, c_re, c_im, nb):
    dt = jnp.exp(log_dt)[..., None]
    mag = jnp.exp(a_re * dt)
    lr, li = mag * jnp.cos(a_im * dt), mag * jnp.sin(a_im * dt)
    den = a_re * a_re + a_im * a_im
    cr = ((lr - 1) * a_re + li * a_im) / den
    ci = (li * a_re - (lr - 1) * a_im) / den
    br = cr[..., None] * b_re - ci[..., None] * b_im
    bi = cr[..., None] * b_im + ci[..., None] * b_re
    G2, N, C = S5_GROUPS // 2, S5_STATE, S5_GROUP
    eye = jnp.eye(G2, dtype=F32)

    def blockdiag_in(m):
        m = m.reshape(2, 2, G2, N, C)
        return jnp.einsum('dhgnc,gk->dhgckn', m, eye).reshape(2, 2, G2 * C, G2 * N)

    def blockdiag_out(m):
        m = m.reshape(2, 2, G2, C, N)
        return jnp.einsum('dhgcn,gk->dhgnkc', m, eye).reshape(2, 2, G2 * N, G2 * C)

    bblk = jnp.concatenate([blockdiag_in(br), blockdiag_in(bi)], axis=-1).astype(BF16)
    cblk = jnp.concatenate([blockdiag_out(c_re), blockdiag_out(-c_im)], axis=-2).astype(BF16)
    lam = jnp.stack([lr, li], axis=1).reshape(2, 2, 2, G2 * N)
    lam = jnp.tile(lam[:, :, None], (1, 1, nb, 1, 1)).reshape(2, 2, 2 * nb, G2 * N)
    return bblk, cblk, lam


def _blockdiag_ones(group, n=LANES):
    i = np.arange(n) // group
    return jnp.asarray((i[:, None] == i[None, :]).astype(np.float32), dtype=BF16)


def kernel(x, c, ctx, c_ctx, ada_w, ada_b, w_in, b_gate, diff_lam, diff_norm_g, gqa_qnorm_g, gqa_knorm_g,
           s5_a_re, s5_a_im, s5_log_dt, s5_b_re, s5_b_im, s5_c_re, s5_c_im, s5_d, s5_w_glu, s5_b_glu,
           ml_conv_w, ml_conv_b, ml_gate_b, ml_norm_g, w_branch, w_out, ln_mix_g, ln_mix_b, ln_ffn_g, ln_ffn_b,
           peer_wq, peer_subkeys, peer_u, peer_v):
    B, L, D = x.shape
    nctx = ctx.shape[1]
    S = nctx + L
    assert D == D_MODEL and nctx % TM == 0 and L % TM == 0 and L % GRID_W == 0
    nt, nct = S // TM, nctx // TM
    depth = ada_w.shape[0]

    h = jnp.concatenate([ctx, x], axis=1)
    R = -(-(B + 1) // 8) * 8
    cond = jnp.zeros((R, D), F32).at[:B].set(c).at[B].set(c_ctx)
    ones64, ones128 = _blockdiag_ones(HD), _blockdiag_ones(ML_HD, ML_HEADS * ML_HD)
    o = IN_OFFS

    for l in range(depth):
        lam_init = 0.8 - 0.6 * math.exp(-0.3 * l)
        m = ada_modulation(cond, ada_w[l].astype(BF16), ada_b[l][None, :])
        mod = jnp.stack([jnp.broadcast_to(m[B], (B, 6 * D)), m[:B]], axis=1).reshape(B, 2, 6, D)

        w = w_in[l]
        seg = lambda i: w[:, o[i]:o[i + 1]]
        dup = lambda t: jnp.concatenate([t[:, :HD], t[:, :HD], t[:, HD:], t[:, HD:]], axis=1)
        w_a = jnp.concatenate([seg(0), _rope_partner(seg(0)), seg(1), _rope_partner(seg(1)), seg(2),
                               seg(9), _rope_partner(seg(9)), dup(seg(10)), dup(_rope_partner(seg(10))),
                               dup(seg(11))], axis=1).astype(BF16)
        w_b = jnp.concatenate([seg(3), seg(4), seg(5), seg(6), seg(7), seg(8),
                               jnp.zeros((D, LANES - 16), F32)], axis=1).astype(BF16)
        gate_b = jnp.concatenate([ml_gate_b[l], jnp.zeros((LANES - 16,), F32)])[None, :]
        tab = _rope_tables(L, nctx, gqa_qnorm_g[l], gqa_knorm_g[l])

        dq, dk, dv, gq, gk, gv = proj_attn(h, mod, w_a, tab, ones64, nct)
        u, mqk, mv, mo, mg = proj_seq(h, mod, w_b, gate_b, nct)

        yd = diff_attention(dq, dk, dv, diff_lam[l], diff_norm_g[l][None, :], nctx, lam_init)
        yg = gqa_attention(gq, gk, gv, nctx)

        bblk, cblk, lam = _s5_tables(s5_a_re[l], s5_a_im[l], s5_log_dt[l], s5_b_re[l], s5_b_im[l],
                                     s5_c_re[l], s5_c_im[l], B)
        sf, sb = (y.reshape(u.shape) for y in s5_scan(u.reshape(S * 2 * B, 256), bblk, cblk, lam, S, nctx))

        mq, mk = ml_prep(mqk, ml_conv_w[l], ml_conv_b[l][None, :], nctx)
        hf = mlstm_scan(mq, mk, mv, mg, nctx, False)
        hb = mlstm_scan(mq, mk, mv, mg, nctx, True)

        h1 = merge(h, mod, yd, yg, u, sf, sb, hf, hb, mo,
                   s5_d[l][None, :], s5_w_glu[l].astype(BF16), s5_b_glu[l][None, :], ml_norm_g[l][None, :],
                   seg(12).astype(BF16), b_gate[l][None, :], w_branch[l].astype(BF16), w_out[l].astype(BF16),
                   ln_mix_g[l][None, :], ln_mix_b[l][None, :], ones128, nct)

        t, ids, gates = peer_route(h1, mod, peer_wq[l].astype(BF16), peer_subkeys[l].astype(BF16), nct)
        h = peer_experts(t.reshape(B * S, D), ids.T, gates.T, peer_u[l].astype(BF16).T, peer_v[l].astype(BF16),
                         h1.reshape(B * S, D), mod, ln_ffn_g[l][None, :], ln_ffn_b[l][None, :], nt, nct).reshape(B, S, D)
    return h[:, nctx:]
```

```python
import functools
import math

import numpy as np
import jax
import jax.numpy as jnp
from jax import lax
from jax.experimental import pallas as pl
from jax.experimental.pallas import tpu as pltpu

F32 = jnp.float32
BF16 = jnp.bfloat16
I32 = jnp.int32

D_MODEL = 1024
DEPTH = 2
GRID_W = 64
ROPE_THETA = 10000.0
LN_EPS = 1e-6
HD = 64
DIFF_HEADS = 4
GQA_HEADS = 8
GQA_KV = 2
S5_GROUP = 16
S5_GROUPS = 32
S5_STATE = 64
ML_HEADS = 4
ML_HD = 128
N_BRANCH = 4
W_BRANCH = 512
PEER_HEADS = 8
PEER_NKEYS = 128
PEER_TOPK = 16
PEER_DQ = 256
ALPHA = (2 * DEPTH) ** 0.25

LANES = 128
TM = 256
TM_E = 2 * TM
TQ = 256
TK = 2048
S5_TC = 128
ML_T = 256
ML_LOCKSTEP = 4
PEER_CA = 16
VMEM_LIMIT = 56 * 1024 * 1024

IN_SPLITS = (512, 512, 512, 512, 512, 512, 512, 512, 16, 512, 128, 128, N_BRANCH * D_MODEL)
IN_OFFS = tuple(int(v) for v in np.cumsum((0,) + IN_SPLITS))


def _cparams(sem):
    return pltpu.CompilerParams(dimension_semantics=sem, vmem_limit_bytes=VMEM_LIMIT)


def _const_spec(shape):
    nd = len(shape)
    return pl.BlockSpec(shape, lambda *_: (0,) * nd, pipeline_mode=pl.Buffered(1))


def _ln(x):
    xc = x - jnp.mean(x, axis=-1, keepdims=True)
    return xc * lax.rsqrt(jnp.mean(xc * xc, axis=-1, keepdims=True) + LN_EPS)


def _sigmoid(x):
    return 1.0 / (1.0 + jnp.exp(-x))


def _gelu(x):
    return 0.5 * x * (1.0 + lax.erf(x * (2.0 ** -0.5)))


def _dot(a, b):
    return jnp.dot(a, b, preferred_element_type=F32)


def _dot_nt(a, b):
    return lax.dot_general(a, b, (((1,), (1,)), ((), ())), preferred_element_type=F32)


def _dot_tn(a, b):
    return lax.dot_general(a, b, (((0,), (0,)), ((), ())), preferred_element_type=F32)


def _ada_kernel(c_ref, w_ref, b_ref, o_ref):
    c = c_ref[...]
    o_ref[...] = _dot((c * _sigmoid(c)).astype(BF16), w_ref[...]) + b_ref[...]


def ada_modulation(cond, w, b):
    R, D = cond.shape
    N = w.shape[1]
    tn = 1536
    return pl.pallas_call(
        _ada_kernel, out_shape=jax.ShapeDtypeStruct((R, N), F32), grid=(N // tn,),
        in_specs=[pl.BlockSpec((R, D), lambda j: (0, 0)), pl.BlockSpec((D, tn), lambda j: (0, j)),
                  pl.BlockSpec((1, tn), lambda j: (0, j))],
        out_specs=pl.BlockSpec((R, tn), lambda j: (0, j)),
        compiler_params=_cparams(("arbitrary",)), name="ada")(cond, w, b)


def _proj_attn_kernel(x_ref, mod_ref, w_ref, tab_ref, ones_ref,
                      dq_ref, dk_ref, dv_ref, gq_ref, gk_ref, gv_ref):
    mod = mod_ref[0, 0]
    xm = (_ln(x_ref[0]) * (1.0 + mod[1:2, :]) + mod[0:1, :]).astype(BF16)

    def mm(lo, n):
        return _dot(xm, w_ref[:, lo:lo + n])

    def rope_store(ref, lo, n, ci, norm):
        t, tp = mm(lo, n), mm(lo + n, n)
        c, s = tab_ref[ci], tab_ref[ci + 1]
        for j in range(n // LANES):
            sl = slice(j * LANES, (j + 1) * LANES)
            tb = t[:, sl]
            y = tb * c + tp[:, sl] * s
            if norm:
                ss = _dot((tb * tb).astype(BF16), ones_ref[...])
                y = y * lax.rsqrt(ss * (1.0 / HD) + LN_EPS)
            ref[0, :, sl] = y.astype(ref.dtype)

    def value_store(ref, lo, n, width):
        lane = lax.broadcasted_iota(I32, (1, n), 1)
        ref[0] = (mm(lo, n) + jnp.where(lane % (2 * width) >= width, 1.0, 0.0)).astype(ref.dtype)

    rope_store(dq_ref, 0, 512, 0, False)
    rope_store(dk_ref, 1024, 512, 2, False)
    dv_ref[0] = mm(2048, 512).astype(dv_ref.dtype)
    rope_store(gq_ref, 2560, 512, 4, True)
    rope_store(gk_ref, 3584, 256, 6, True)
    value_store(gv_ref, 4096, 256, HD)


def proj_attn(xa, mod, w_a, tab, ones_bd, nct):
    B, S, D = xa.shape
    tok = lambda n: pl.BlockSpec((1, TM, n), lambda b, i: (b, i, 0))
    widths = (512, 512, 512, 512, 256, 256)
    outs = [jax.ShapeDtypeStruct((B, S, n), BF16) for n in widths]
    return pl.pallas_call(
        _proj_attn_kernel, out_shape=outs, grid=(B, S // TM),
        in_specs=[tok(D),
                  pl.BlockSpec((1, 1, 6, D), lambda b, i: (b, jnp.where(i >= nct, 1, 0), 0, 0)),
                  _const_spec(w_a.shape),
                  pl.BlockSpec((8, TM, LANES), lambda b, i: (0, i, 0)),
                  _const_spec(ones_bd.shape)],
        out_specs=[tok(n) for n in widths],
        compiler_params=_cparams(("parallel", "parallel")), name="proj_attn")(xa, mod, w_a, tab, ones_bd)


def _proj_seq_kernel(x_ref, mod_ref, w_ref, gb_ref, u_ref, qk_ref, v_ref, o_ref, g_ref):
    mod = mod_ref[0, 0]
    xm = (_ln(x_ref[0]) * (1.0 + mod[1:2, :]) + mod[0:1, :]).astype(BF16)
    u_ref[...] = _dot(xm, w_ref[:, 0:512])
    qk_ref[0] = _dot(xm, w_ref[:, 512:1536])
    v_ref[0] = _dot(xm, w_ref[:, 1536:2048])
    o_ref[0] = _dot(xm, w_ref[:, 2048:2560])
    g_ref[0] = _dot(xm, w_ref[:, 2560:2688]) + gb_ref[...]


def proj_seq(xa, mod, w_b, gate_b, nct):
    B, S, D = xa.shape
    tok = lambda n: pl.BlockSpec((1, TM, n), lambda b, i: (b, i, 0))
    widths = (1024, 512, 512, LANES)
    return pl.pallas_call(
        _proj_seq_kernel,
        out_shape=[jax.ShapeDtypeStruct((S, B * 512), F32)] + [jax.ShapeDtypeStruct((B, S, n), F32) for n in widths],
        grid=(B, S // TM),
        in_specs=[tok(D),
                  pl.BlockSpec((1, 1, 6, D), lambda b, i: (b, jnp.where(i >= nct, 1, 0), 0, 0)),
                  _const_spec(w_b.shape), _const_spec(gate_b.shape)],
        out_specs=[pl.BlockSpec((TM, 512), lambda b, i: (i, b))] + [tok(n) for n in widths],
        compiler_params=_cparams(("parallel", "parallel")), name="proj_seq")(xa, mod, w_b, gate_b)


def _flash(qq, k_ref, v_ref, nctx, n_lat, tk, dv):
    R, W = qq.shape[0], v_ref.shape[2]
    mxu_sums = dv < W

    def chunk(carry, rows):
        m, l, acc = carry
        s = _dot_nt(qq, k_ref[0, rows, :])
        m_new = jnp.maximum(m, jnp.max(s, axis=1, keepdims=True))
        alpha = jnp.exp(m - m_new)
        if mxu_sums:
            p = jnp.exp((s - m_new).astype(BF16))
        else:
            p = jnp.exp(s - m_new)
            l = alpha * l + jnp.sum(p, axis=1, keepdims=True)
        return m_new, l, alpha * acc + _dot(p.astype(BF16), v_ref[0, rows, :])

    carry = (jnp.full((R, 1), -jnp.inf, F32), jnp.zeros((R, 1), F32), jnp.zeros((R, W), F32))
    carry = chunk(carry, pl.ds(0, nctx))
    for c in range(n_lat):
        carry = chunk(carry, pl.ds(nctx + c * tk, tk))
    _, l, acc = carry
    return acc[:, :dv] / (acc[:, dv:dv + 1] if mxu_sums else l)


def _ctx_or_all(i, nctx, n_lat, attend):
    @pl.when(i < nctx // TQ)
    def _():
        attend(0)

    @pl.when(i >= nctx // TQ)
    def _():
        attend(n_lat)


def _diff_attn_kernel(q_ref, k_ref, v_ref, lam_ref, g_ref, o_ref, *, nctx, n_lat, tk, lam_init):
    i = pl.program_id(2)
    q = q_ref[0]
    lane = lax.broadcasted_iota(I32, q.shape, 1)
    zero = jnp.zeros_like(q)
    qq = jnp.concatenate([jnp.where(lane < HD, q, zero), jnp.where(lane >= HD, q, zero)], axis=0)
    lv = lam_ref[...]
    lam = (jnp.exp(jnp.sum(lv[0:1] * lv[1:2], axis=1, keepdims=True))
           - jnp.exp(jnp.sum(lv[2:3] * lv[3:4], axis=1, keepdims=True)) + lam_init)

    def attend(n):
        o = _flash(qq, k_ref, v_ref, nctx, n, tk, 2 * HD)
        d = o[:TQ] - lam * o[TQ:]
        y = d * lax.rsqrt(jnp.mean(d * d, axis=-1, keepdims=True) + LN_EPS)
        o_ref[0] = (y * g_ref[...] * (1.0 - lam_init)).astype(o_ref.dtype)

    _ctx_or_all(i, nctx, n_lat, attend)


def diff_attention(q, k, v, lam_vec, norm_g, nctx, lam_init):
    B, S, _ = q.shape
    tk = math.gcd(S - nctx, TK)
    kern = functools.partial(_diff_attn_kernel, nctx=nctx, n_lat=(S - nctx) // tk, tk=tk, lam_init=lam_init)
    return pl.pallas_call(
        kern, out_shape=jax.ShapeDtypeStruct((B, S, 512), BF16), grid=(B, DIFF_HEADS, S // TQ),
        in_specs=[pl.BlockSpec((1, TQ, LANES), lambda b, h, i: (b, i, h)),
                  pl.BlockSpec((1, S, LANES), lambda b, h, i: (b, 0, h)),
                  pl.BlockSpec((1, S, LANES), lambda b, h, i: (b, 0, h)),
                  pl.BlockSpec((4, HD), lambda b, h, i: (0, 0)),
                  pl.BlockSpec((1, LANES), lambda b, h, i: (0, 0))],
        out_specs=pl.BlockSpec((1, TQ, LANES), lambda b, h, i: (b, i, h)),
        compiler_params=_cparams(("parallel", "parallel", "parallel")), name="diff_attn")(q, k, v, lam_vec, norm_g)


def _gqa_attn_kernel(q_ref, k_ref, v_ref, o_ref, *, nctx, n_lat, tk):
    i = pl.program_id(2)
    q = q_ref[0]
    lane = lax.broadcasted_iota(I32, (TQ, LANES), 1)
    zero = jnp.zeros((TQ, LANES), q.dtype)
    parts = []
    for j in range(2):
        blk = q[:, j * LANES:(j + 1) * LANES]
        parts += [jnp.where(lane < HD, blk, zero), jnp.where(lane >= HD, blk, zero)]
    qq = jnp.concatenate(parts, axis=0)

    def attend(n):
        o = _flash(qq, k_ref, v_ref, nctx, n, tk, HD)
        o_ref[0] = jnp.concatenate([o[h * TQ:(h + 1) * TQ] for h in range(4)], axis=1).astype(o_ref.dtype)

    _ctx_or_all(i, nctx, n_lat, attend)


def gqa_attention(q, k, v, nctx):
    B, S, _ = q.shape
    tk = math.gcd(S - nctx, TK)
    kern = functools.partial(_gqa_attn_kernel, nctx=nctx, n_lat=(S - nctx) // tk, tk=tk)
    return pl.pallas_call(
        kern, out_shape=jax.ShapeDtypeStruct((B, S, 512), BF16), grid=(B, GQA_KV, S // TQ),
        in_specs=[pl.BlockSpec((1, TQ, 2 * LANES), lambda b, g, i: (b, i, g)),
                  pl.BlockSpec((1, S, LANES), lambda b, g, i: (b, 0, g)),
                  pl.BlockSpec((1, S, LANES), lambda b, g, i: (b, 0, g))],
        out_specs=pl.BlockSpec((1, TQ, 2 * LANES), lambda b, g, i: (b, i, g)),
        compiler_params=_cparams(("parallel", "parallel", "parallel")), name="gqa_attn")(q, k, v)


def _s5_kernel(uf_ref, ub_ref, bblk_ref, cblk_ref, lam_ref, yf_ref, yb_ref, buf_f, buf_b, st_ref):
    H = S5_GROUPS * S5_STATE // 2
    R = buf_f.shape[0]
    RB = 256

    @pl.when(pl.program_id(0) == 0)
    def _():
        st_ref[...] = jnp.zeros_like(st_ref)

    half0 = (lax.broadcasted_iota(I32, (RB, 1), 0) & 1) == 0

    for d, (u_ref, buf) in enumerate(((uf_ref, buf_f), (ub_ref, buf_b))):
        for rb in range(R // RB):
            rows = slice(rb * RB, (rb + 1) * RB)
            uh = u_ref[rows, :].astype(BF16)
            for cols in (slice(0, H), slice(H, 2 * H)):
                buf[rows, cols] = jnp.where(half0, _dot(uh, bblk_ref[d, 0, :, cols]), _dot(uh, bblk_ref[d, 1, :, cols]))

    lfr, lfi, lbr, lbi = lam_ref[0, 0], lam_ref[0, 1], lam_ref[1, 0], lam_ref[1, 1]

    def step(t, carry):
        fr, fi, br, bi = carry
        rf = pl.ds(pl.multiple_of(t * 8, 8), 8)
        x = buf_f[rf, :]
        nfr = lfr * fr - lfi * fi + x[:, :H]
        nfi = lfr * fi + lfi * fr + x[:, H:]
        buf_f[rf, :] = jnp.concatenate([nfr, nfi], axis=1)
        rb = pl.ds(pl.multiple_of((S5_TC - 1 - t) * 8, 8), 8)
        z = buf_b[rb, :]
        nbr = lbr * br - lbi * bi + z[:, :H]
        nbi = lbr * bi + lbi * br + z[:, H:]
        buf_b[rb, :] = jnp.concatenate([nbr, nbi], axis=1)
        return nfr, nfi, nbr, nbi

    fin = lax.fori_loop(0, S5_TC, step, (st_ref[0], st_ref[1], st_ref[2], st_ref[3]))
    for j in range(4):
        st_ref[j] = fin[j]

    for d, (y_ref, buf) in enumerate(((yf_ref, buf_f), (yb_ref, buf_b))):
        for rb in range(R // RB):
            rows = slice(rb * RB, (rb + 1) * RB)
            h = buf[rows, :].astype(BF16)
            y_ref[rows, :] = jnp.where(half0, _dot(h, cblk_ref[d, 0]), _dot(h, cblk_ref[d, 1]))


def s5_scan(u, bblk, cblk, lam, S, nctx):
    assert u.shape[0] == S * 8, "the scan packs (sample, half) pairs into the eight sublanes of a vreg"
    nch, nc0 = S // S5_TC, nctx // S5_TC

    def bwd(i):
        return jnp.where(i < nc0, nc0 - 1 - i, (nch - 1) - (i - nc0))
    H2 = S5_GROUPS * S5_STATE
    R, W = S5_TC * 8, u.shape[1]
    blk = lambda f: pl.BlockSpec((R, W), f)
    return pl.pallas_call(
        _s5_kernel, out_shape=[jax.ShapeDtypeStruct(u.shape, F32)] * 2, grid=(nch,),
        in_specs=[blk(lambda i: (i, 0)), blk(lambda i: (bwd(i), 0)),
                  _const_spec(bblk.shape), _const_spec(cblk.shape), _const_spec(lam.shape)],
        out_specs=[blk(lambda i: (i, 0)), blk(lambda i: (bwd(i), 0))],
        scratch_shapes=[pltpu.VMEM((R, H2), F32), pltpu.VMEM((R, H2), F32), pltpu.VMEM((4, 8, H2 // 2), F32)],
        compiler_params=_cparams(("arbitrary",)), name="s5_scan")(u, u, bblk, cblk, lam)


def _ml_prep_kernel(x_ref, prev_ref, next_ref, w_ref, b_ref, q_ref, k_ref, *, seg_starts, seg_ends):
    i = pl.program_id(1)
    x = x_ref[0]
    row = lax.broadcasted_iota(I32, x.shape, 0)
    first = functools.reduce(jnp.logical_or, [i == s for s in seg_starts])
    last = functools.reduce(jnp.logical_or, [i == s for s in seg_ends])
    pr = jnp.where(first, 0.0, prev_ref[0, 7:8, :])
    nx = jnp.where(last, 0.0, next_ref[0, 0:1, :])
    xp = jnp.where(row == 0, pr, pltpu.roll(x, 1, 0))
    xn = jnp.where(row == TM - 1, nx, pltpu.roll(x, TM - 1, 0))
    w = w_ref[...]
    y = b_ref[...] + xp * w[0:1, :] + x * w[1:2, :] + xn * w[2:3, :]
    y = y * _sigmoid(y)
    q_ref[0] = y[:, :512].astype(q_ref.dtype)
    k_ref[0] = (y[:, 512:] * (ML_HD ** -0.5)).astype(k_ref.dtype)


def ml_prep(qk, conv_w, conv_b, nctx):
    B, S, W = qk.shape
    nt, nct, r8 = S // TM, nctx // TM, TM // 8
    kern = functools.partial(_ml_prep_kernel, seg_starts=(0, nct), seg_ends=(nct - 1, nt - 1))
    return pl.pallas_call(
        kern, out_shape=[jax.ShapeDtypeStruct((B, S, 512), BF16)] * 2, grid=(B, nt),
        in_specs=[pl.BlockSpec((1, TM, W), lambda b, i: (b, i, 0)),
                  pl.BlockSpec((1, 8, W), lambda b, i: (b, jnp.maximum(i * r8 - 1, 0), 0)),
                  pl.BlockSpec((1, 8, W), lambda b, i: (b, jnp.minimum((i + 1) * r8, S // 8 - 1), 0)),
                  pl.BlockSpec((3, W), lambda b, i: (0, 0)), pl.BlockSpec((1, W), lambda b, i: (0, 0))],
        out_specs=[pl.BlockSpec((1, TM, 512), lambda b, i: (b, i, 0))] * 2,
        compiler_params=_cparams(("parallel", "parallel")), name="ml_prep")(qk, qk, qk, conv_w, conv_b)


def _log_sigmoid(x):
    return jnp.minimum(x, 0.0) - jnp.log1p(jnp.exp(-jnp.abs(x)))


def _mlstm_kernel(q_ref, k_ref, v_ref, g_ref, h_ref, c_sc, n_sc, m_sc, *, reverse):
    T = ML_T
    gi, gf = (8, 12) if reverse else (0, 4)

    @pl.when(pl.program_id(0) == 0)
    def _():
        c_sc[...] = jnp.zeros_like(c_sc)
        n_sc[...] = jnp.zeros_like(n_sc)
        m_sc[...] = jnp.zeros_like(m_sc)

    r = lax.broadcasted_iota(I32, (T, T), 0)
    c = lax.broadcasted_iota(I32, (T, T), 1)
    mask = (c >= r) if reverse else (c <= r)
    tri = jnp.where(mask, 1.0, 0.0)
    hp = lax.Precision.HIGHEST
    end = 0 if reverse else T - 1
    def gates(b):
        g = g_ref[b]
        gt = g.T
        bcol_all = jnp.dot(tri, _log_sigmoid(g), preferred_element_type=F32, precision=hp)
        brow_all = lax.dot_general(_log_sigmoid(gt[0:16]), tri, (((1,), (1,)), ((), ())),
                                   preferred_element_type=F32, precision=hp)
        return g, gt, bcol_all, brow_all

    def chain(b, hh, g, gt, bcol_all, brow_all):
        st = b * ML_HEADS + hh
        sl = slice(hh * ML_HD, (hh + 1) * ML_HD)
        q, k, v = q_ref[b, :, sl], k_ref[b, :, sl], v_ref[b, :, sl]
        bcol, brow = bcol_all[:, gf + hh:gf + hh + 1], brow_all[gf + hh:gf + hh + 1, :]
        icol, irow = g[:, gi + hh:gi + hh + 1], gt[gi + hh:gi + hh + 1, :]
        m_old = m_sc[st][:, 0:1]
        n_old = n_sc[st]
        c_old = c_sc[st]
        logw = jnp.where(mask, bcol - brow + irow, -jnp.inf)
        m_inter = bcol + m_old
        m_t = jnp.maximum(m_inter, jnp.max(logw, axis=1, keepdims=True))
        yield
        s = _dot_nt(q, k) * jnp.exp(logw - m_t)
        inter = jnp.exp(m_inter - m_t)
        yield
        num = _dot(s.astype(BF16), v.astype(BF16)) + inter * _dot_nt(q, c_old.astype(BF16))
        den = jnp.sum(s, axis=1, keepdims=True) + inter * jnp.sum(q.astype(F32) * n_old, axis=1, keepdims=True)
        h_ref[b, :, sl] = num / jnp.maximum(jnp.abs(den), jnp.exp(-m_t))
        yield
        b_end = bcol[end:end + 1, :]
        g_row, g_col = b_end - brow + irow, b_end - bcol + icol
        m_new = jnp.maximum(b_end + m_old, jnp.max(g_row, axis=1, keepdims=True))
        decay = jnp.exp(b_end + m_old - m_new)
        wk = jnp.exp(g_col - m_new)
        c_sc[st] = decay * c_old + _dot_tn((v * wk).astype(BF16), k)
        n_sc[st] = decay * n_old + jnp.sum(k.astype(F32) * wk, axis=0, keepdims=True)
        m_sc[st] = jnp.broadcast_to(m_new, (1, LANES))

    nb = q_ref.shape[0]
    for b0 in range(0, nb, ML_LOCKSTEP):
        chains = []
        for b in range(b0, min(b0 + ML_LOCKSTEP, nb)):
            shared = gates(b)
            chains += [chain(b, hh, *shared) for hh in range(ML_HEADS)]
        for _ in range(4):
            for ch in chains:
                next(ch, None)


def mlstm_scan(q, k, v, g, nctx, reverse):
    B, S, W = q.shape
    nch, nc0 = S // ML_T, nctx // ML_T

    def order(i):
        return jnp.where(i < nc0, nc0 - 1 - i, (nch - 1) - (i - nc0)) if reverse else i
    blk = lambda n: pl.BlockSpec((B, ML_T, n), lambda i: (0, order(i), 0))
    nst = B * ML_HEADS
    return pl.pallas_call(
        functools.partial(_mlstm_kernel, reverse=reverse),
        out_shape=jax.ShapeDtypeStruct((B, S, W), F32), grid=(nch,),
        in_specs=[blk(W), blk(W), blk(W), blk(LANES)], out_specs=blk(W),
        scratch_shapes=[pltpu.VMEM((nst, ML_HD, ML_HD), F32), pltpu.VMEM((nst, 1, ML_HD), F32),
                        pltpu.VMEM((nst, 1, LANES), F32)],
        compiler_params=_cparams(("arbitrary",)), name="mlstm_bwd" if reverse else "mlstm_fwd")(q, k, v, g)


def _merge_kernel(x_ref, mod_ref, yd_ref, yg_ref, u_ref, sf_ref, sb_ref, hf_ref, hb_ref, mo_ref,
                  s5d_ref, wglu_ref, bglu_ref, mlg_ref, wgate_ref, bgate_ref, wbr_ref, wout_ref,
                  lng_ref, lnb_ref, ones_ref, o_ref):
    x = x_ref[0]
    mod = mod_ref[0, 0]
    xm = (_ln(x) * (1.0 + mod[1:2, :]) + mod[0:1, :]).astype(BF16)
    ys = u_ref[...] * s5d_ref[...] + sf_ref[...] + sb_ref[...]
    z = _dot(_gelu(ys).astype(BF16), wglu_ref[...]) + bglu_ref[...]
    ys = z[:, :512] * _sigmoid(z[:, 512:])
    hm = hf_ref[0] + hb_ref[0]
    ss = _dot((hm * hm).astype(BF16), ones_ref[...])
    ym = hm * lax.rsqrt(ss * (1.0 / ML_HD) + LN_EPS) * mlg_ref[...] * _sigmoid(mo_ref[0])
    branches = (yd_ref[0], ys.astype(BF16), ym.astype(BF16), yg_ref[0])
    merged = None
    for j, yb in enumerate(branches):
        gate = _sigmoid(_dot(xm, wgate_ref[:, j * D_MODEL:(j + 1) * D_MODEL]) + bgate_ref[:, j * D_MODEL:(j + 1) * D_MODEL])
        term = gate * _dot(yb, wbr_ref[j])
        merged = term if merged is None else merged + term
    mix = _dot(merged.astype(BF16), wout_ref[...])
    o_ref[0] = _ln(ALPHA * x + mod[2:3, :] * mix) * lng_ref[...] + lnb_ref[...]


def merge(xa, mod, yd, yg, u, sf, sb, hf, hb, mo, s5d, wglu, bglu, mlg, wgate, bgate, wbr, wout, lng, lnb, ones_ml, nct):
    B, S, D = xa.shape
    tok = lambda n: pl.BlockSpec((1, TM, n), lambda b, i: (b, i, 0))
    consts = (s5d, wglu, bglu, mlg, wgate, bgate, wbr, wout, lng, lnb, ones_ml)
    return pl.pallas_call(
        _merge_kernel, out_shape=jax.ShapeDtypeStruct((B, S, D), F32), grid=(B, S // TM),
        in_specs=[tok(D), pl.BlockSpec((1, 1, 6, D), lambda b, i: (b, jnp.where(i >= nct, 1, 0), 0, 0))]
        + [tok(512)] * 2 + [pl.BlockSpec((TM, 512), lambda b, i: (i, b))] * 3 + [tok(512)] * 3
        + [_const_spec(a.shape) for a in consts],
        out_specs=tok(D),
        compiler_params=_cparams(("parallel", "parallel")), name="merge")(xa, mod, yd, yg, u, sf, sb, hf, hb, mo, *consts)


def _top16(*problems):
    def one(kk, s, val_ref, idx_ref, payload):
        R = s.shape[0]
        ri = lax.broadcasted_iota(I32, s.shape, 0).astype(F32)
        m = jnp.max(s, axis=0, keepdims=True)
        ix = jnp.min(jnp.where(s == m, ri, float(R)), axis=0, keepdims=True)
        hit = ri == ix
        val_ref[pl.ds(kk, 1), :] = m
        idx_ref[pl.ds(kk, 1), :] = ix if payload is None else jnp.max(jnp.where(hit, payload, -1.0), axis=0, keepdims=True)
        return jnp.where(hit, -jnp.inf, s)

    def body(kk, ss):
        return tuple(one(kk, s, *prob[1:]) for s, prob in zip(ss, problems))

    lax.fori_loop(0, PEER_TOPK, body, tuple(prob[0] for prob in problems))


_PEER_CAND_ROWS = ((0, 16),) + tuple((16 + 8 * (p - 1), 8) for p in range(1, 8))
_PEER_NCAND = _PEER_CAND_ROWS[-1][0] + 16


def _peer_route_kernel(x_ref, mod_ref, wq_ref, sk_ref, t_ref, ids_ref, gate_ref,
                       xm_sc, v1_sc, i1_sc, v2_sc, i2_sc, cand_sc, cidx_sc, top_sc, tid_sc):
    h = pl.program_id(2)

    @pl.when(h == 0)
    def _():
        mod = mod_ref[0, 0]
        xm = (_ln(x_ref[0]) * (1.0 + mod[4:5, :]) + mod[3:4, :]).astype(BF16)
        xm_sc[...] = xm
        t_ref[0] = xm

    q = _dot(xm_sc[...], wq_ref[...]).astype(BF16)
    half = PEER_DQ // 2
    _top16((_dot_nt(sk_ref[0], q[:, :half]), v1_sc, i1_sc, None))
    _top16((_dot_nt(sk_ref[1], q[:, half:]), v2_sc, i2_sc, None))
    nk = float(PEER_NKEYS)
    for p, (lo, n) in enumerate(_PEER_CAND_ROWS):
        cand_sc[lo:lo + n, :] = v1_sc[p:p + 1, :] + v2_sc[0:n, :]
        cidx_sc[lo:lo + n, :] = i1_sc[p:p + 1, :] * nk + i2_sc[0:n, :]
    lo = _PEER_CAND_ROWS[-1][0] + _PEER_CAND_ROWS[-1][1]
    cand_sc[lo:lo + 8, :] = v1_sc[8:16, :] + v2_sc[0:1, :]
    cidx_sc[lo:lo + 8, :] = i1_sc[8:16, :] * nk + i2_sc[0:1, :]
    _top16((cand_sc[...], top_sc, tid_sc, cidx_sc[...]))
    top = top_sc[...]
    e = jnp.exp(top - top[0:1, :])
    gate_ref[...] = e / jnp.sum(e, axis=0, keepdims=True)
    ids_ref[...] = tid_sc[...].astype(I32)


def peer_route(h1, mod, wq, subkeys, nct):
    B, S, D = h1.shape
    nt = S // TM
    f = lambda n: pltpu.VMEM((n, TM), F32)
    return pl.pallas_call(
        _peer_route_kernel,
        out_shape=[jax.ShapeDtypeStruct((B, S, D), BF16),
                   jax.ShapeDtypeStruct((PEER_HEADS * PEER_TOPK, B * S), I32),
                   jax.ShapeDtypeStruct((PEER_HEADS * PEER_TOPK, B * S), F32)],
        grid=(B, nt, PEER_HEADS),
        in_specs=[pl.BlockSpec((1, TM, D), lambda b, i, h: (b, i, 0)),
                  pl.BlockSpec((1, 1, 6, D), lambda b, i, h: (b, jnp.where(i >= nct, 1, 0), 0, 0)),
                  pl.BlockSpec((D, PEER_DQ), lambda b, i, h: (0, h)),
                  pl.BlockSpec((2, PEER_NKEYS, PEER_DQ // 2), lambda b, i, h: (0, 0, 0))],
        out_specs=[pl.BlockSpec((1, TM, D), lambda b, i, h: (b, i, 0)),
                   pl.BlockSpec((PEER_TOPK, TM), lambda b, i, h: (h, b * nt + i)),
                   pl.BlockSpec((PEER_TOPK, TM), lambda b, i, h: (h, b * nt + i))],
        scratch_shapes=[pltpu.VMEM((TM, D), BF16), f(16), f(16), f(16), f(16), f(_PEER_NCAND), f(_PEER_NCAND),
                        f(16), f(16)],
        compiler_params=_cparams(("parallel", "parallel", "arbitrary")), name="peer_route")(h1, mod, wq, subkeys)


def _peer_expert_kernel(t_ref, ids_ref, gate_ref, u_ref, v_ref, x_ref, mod0_ref, mod1_ref, lng_ref, lnb_ref, o_ref,
                        a_sc, raw_sc, wraw_sc, wgt_sc, *, nchunk):
    j = pl.program_id(1)
    NK, CA = PEER_NKEYS, PEER_CA
    w_sc = a_sc

    def dense(slot):
        raw_sc[slot] = _dot(t_ref[...], u_ref[0]).astype(BF16)

    def relayout_in(c, slot):
        a = raw_sc[slot]
        a3 = jnp.stack([a[:, al * NK:(al + 1) * NK] for al in range(CA)], axis=0)
        a_sc[:, pl.ds(pl.multiple_of(c * CA, CA), CA), :] = pltpu.einshape("atb->tab", a3)

    def relayout_out(c, slot):
        w3 = pltpu.einshape("tab->atb", w_sc[:, pl.ds(pl.multiple_of(c * CA, CA), CA), :])
        wraw_sc[slot] = jnp.concatenate([w3[al] for al in range(CA)], axis=1)

    def weighted_sum(slot):
        o_ref[...] += _dot(wraw_sc[slot], v_ref[...])

    @pl.when(j == 0)
    def _():
        dense(0)

    for slot in (0, 1):
        @pl.when(jnp.logical_and(jnp.logical_and(j > 0, j < nchunk), j % 2 == slot))
        def _():
            relayout_in(j - 1, 1 - slot)
            dense(slot)

    @pl.when(j == nchunk - 1)
    def _():
        relayout_in(j, (nchunk - 1) % 2)
        io_k = lax.broadcasted_iota(I32, (NK, NK), 0)
        io_a = lax.broadcasted_iota(I32, (NK, 2 * NK), 0)

        def masks(two):
            idp = ids_ref[two, :]
            ids = jnp.concatenate([idp[0:1], idp[1:2]], axis=1)
            blocks = [jnp.where(io_k == (idp[tok:tok + 1] & (NK - 1)), 1.0, 0.0).astype(BF16) for tok in range(2)]
            zero = jnp.zeros((NK, NK), BF16)
            onehot = jnp.concatenate([jnp.concatenate([blocks[0], zero], axis=1),
                                      jnp.concatenate([zero, blocks[1]], axis=1)], axis=0)
            return onehot, io_a == (ids >> 7)

        def pick(p, carry):
            two = pl.ds(pl.multiple_of(p * 2, 2), 2)
            onehot, r1 = masks(two)
            a2 = a_sc[two]
            at = jnp.concatenate([a2[0], a2[1]], axis=1)
            picked = _dot(at, onehot)
            act = jnp.sum(jnp.where(r1, picked, 0.0), axis=0, keepdims=True)
            gp = gate_ref[two, :]
            w = jnp.concatenate([gp[0:1], gp[1:2]], axis=1) * _gelu(act)
            wgt_sc[two, :] = jnp.concatenate([w[:, :NK], w[:, NK:]], axis=0)
            return carry

        def scatter(p, carry):
            two = pl.ds(pl.multiple_of(p * 2, 2), 2)
            onehot, r1 = masks(two)
            wp = wgt_sc[two, :]
            w = jnp.concatenate([wp[0:1], wp[1:2]], axis=1)
            wt = _dot_nt(jnp.where(r1, w, 0.0).astype(BF16), onehot)
            w_sc[two] = jnp.stack([wt[:, :NK], wt[:, NK:]], axis=0).astype(BF16)
            return carry

        lax.fori_loop(0, TM_E // 2, pick, 0, unroll=16)
        lax.fori_loop(0, TM_E // 2, scatter, 0, unroll=16)
        o_ref[...] = jnp.zeros_like(o_ref)
        relayout_out(0, 0)

    for slot in (0, 1):
        @pl.when(jnp.logical_and(jnp.logical_and(j >= nchunk, j < 2 * nchunk - 1), (j - nchunk) % 2 == slot))
        def _():
            weighted_sum(slot)
            relayout_out(j - nchunk + 1, 1 - slot)

    @pl.when(j == 2 * nchunk - 1)
    def _():
        weighted_sum((nchunk - 1) % 2)
        for half, mod_ref in enumerate((mod0_ref, mod1_ref)):
            rows = slice(half * TM, (half + 1) * TM)
            y = ALPHA * x_ref[rows, :] + mod_ref[0, 0][5:6, :] * o_ref[rows, :]
            o_ref[rows, :] = _ln(y) * lng_ref[...] + lnb_ref[...]


def peer_experts(t, ids, gates, emb_u, emb_v, h1, mod, lng, lnb, nt_per_sample, nct):
    T, D = h1.shape
    E = emb_v.shape[0]
    ce = PEER_CA * PEER_NKEYS
    nchunk = E // ce
    emb_ut = emb_u.reshape(nchunk, ce, D).transpose(0, 2, 1)
    assert T % TM_E == 0 and TM_E == 2 * TM
    tok = lambda n, **kw: pl.BlockSpec((TM_E, n), lambda i, j: (i, 0), **kw)

    def mod_spec(half):
        def index(i, j):
            r = 2 * i + half
            return (r // nt_per_sample, jnp.where(r % nt_per_sample >= nct, 1, 0), 0, 0)
        return pl.BlockSpec((1, 1, 6, D), index)
    once = dict(pipeline_mode=pl.Buffered(1))
    return pl.pallas_call(
        functools.partial(_peer_expert_kernel, nchunk=nchunk),
        out_shape=jax.ShapeDtypeStruct((T, D), F32), grid=(T // TM_E, 2 * nchunk),
        in_specs=[tok(D, **once), tok(LANES, **once), tok(LANES, **once),
                  pl.BlockSpec((1, D, ce), lambda i, j: (jnp.minimum(j, nchunk - 1), 0, 0)),
                  pl.BlockSpec((ce, D), lambda i, j: (jnp.maximum(j - nchunk, 0), 0)),
                  tok(D, **once), mod_spec(0), mod_spec(1),
                  pl.BlockSpec((1, D), lambda i, j: (0, 0)), pl.BlockSpec((1, D), lambda i, j: (0, 0))],
        out_specs=tok(D),
        scratch_shapes=[pltpu.VMEM((TM_E, PEER_NKEYS, PEER_NKEYS), BF16), pltpu.VMEM((2, TM_E, ce), BF16),
                        pltpu.VMEM((2, TM_E, ce), BF16), pltpu.VMEM((TM_E, LANES), F32)],
        compiler_params=_cparams(("parallel", "arbitrary")), name="peer_experts")(t, ids, gates, emb_ut, emb_v, h1, mod, mod, lng, lnb)


_ROPE_IDX = np.arange(HD)
_ROPE_PERM = np.where(_ROPE_IDX % 32 < 16, _ROPE_IDX + 16, _ROPE_IDX - 16)
_ROPE_SIGN = np.where(_ROPE_IDX % 32 < 16, -1.0, 1.0).astype(np.float32)


def _rope_partner(w):
    n = w.shape[1] // HD
    perm = np.concatenate([h * HD + _ROPE_PERM for h in range(n)])
    return w[:, perm] * jnp.asarray(np.tile(_ROPE_SIGN, n))


def _rope_tables(L, nctx, gq, gk):
    rows = L // GRID_W
    row = jnp.repeat(jnp.arange(rows, dtype=F32), GRID_W)
    col = jnp.tile(jnp.arange(GRID_W, dtype=F32), rows)
    nf = HD // 4
    inv = ROPE_THETA ** (-jnp.arange(nf, dtype=F32) / nf)
    ar, ac = row[:, None] * inv, col[:, None] * inv
    cos = jnp.concatenate([jnp.cos(ar), jnp.cos(ar), jnp.cos(ac), jnp.cos(ac)], axis=1)
    sin = jnp.concatenate([jnp.sin(ar), jnp.sin(ar), jnp.sin(ac), jnp.sin(ac)], axis=1)
    cos = jnp.concatenate([jnp.ones((nctx, HD), F32), cos], axis=0)
    sin = jnp.concatenate([jnp.zeros((nctx, HD), F32), sin], axis=0)
    scale = HD ** -0.5
    one = jnp.ones((HD,), F32)
    tabs = []
    for g, sc in ((one, scale), (one, 1.0), (gq, scale), (gk, 1.0)):
        tabs += [cos * (g * sc), sin * (g[_ROPE_PERM] * sc)]
    return jnp.tile(jnp.stack(tabs), (1, 1, 2))


def _s5_tables(a_re, a_im, log_dt, b_re, b_im, c_re, c_im, nb):
    dt = jnp.exp(log_dt)[..., None]
    mag = jnp.exp(a_re * dt)
    lr, li = mag * jnp.cos(a_im * dt), mag * jnp.sin(a_im * dt)
    den = a_re * a_re + a_im * a_im
    cr = ((lr - 1) * a_re + li * a_im) / den
    ci = (li * a_re - (lr - 1) * a_im) / den
    br = cr[..., None] * b_re - ci[..., None] * b_im
    bi = cr[..., None] * b_im + ci[..., None] * b_re
    G2, N, C = S5_GROUPS // 2, S5_STATE, S5_GROUP
    eye = jnp.eye(G2, dtype=F32)

    def blockdiag_in(m):
        m = m.reshape(2, 2, G2, N, C)
        return jnp.einsum('dhgnc,gk->dhgckn', m, eye).reshape(2, 2, G2 * C, G2 * N)

    def blockdiag_out(m):
        m = m.reshape(2, 2, G2, C, N)
        return jnp.einsum('dhgcn,gk->dhgnkc', m, eye).reshape(2, 2, G2 * N, G2 * C)

    bblk = jnp.concatenate([blockdiag_in(br), blockdiag_in(bi)], axis=-1).astype(BF16)
    cblk = jnp.concatenate([blockdiag_out(c_re), blockdiag_out(-c_im)], axis=-2).astype(BF16)
    lam = jnp.stack([lr, li], axis=1).reshape(2, 2, 2, G2 * N)
    lam = jnp.tile(lam[:, :, None], (1, 1, nb, 1, 1)).reshape(2, 2, 2 * nb, G2 * N)
    return bblk, cblk, lam


def _blockdiag_ones(group, n=LANES):
    i = np.arange(n) // group
    return jnp.asarray((i[:, None] == i[None, :]).astype(np.float32), dtype=BF16)


def kernel(x, c, ctx, c_ctx, ada_w, ada_b, w_in, b_gate, diff_lam, diff_norm_g, gqa_qnorm_g, gqa_knorm_g,
           s5_a_re, s5_a_im, s5_log_dt, s5_b_re, s5_b_im, s5_c_re, s5_c_im, s5_d, s5_w_glu, s5_b_glu,
           ml_conv_w, ml_conv_b, ml_gate_b, ml_norm_g, w_branch, w_out, ln_mix_g, ln_mix_b, ln_ffn_g, ln_ffn_b,
           peer_wq, peer_subkeys, peer_u, peer_v):
    B, L, D = x.shape
    nctx = ctx.shape[1]
    S = nctx + L
    assert D == D_MODEL and nctx % TM == 0 and L % TM == 0 and L % GRID_W == 0
    nt, nct = S // TM, nctx // TM
    depth = ada_w.shape[0]

    h = jnp.concatenate([ctx, x], axis=1)
    R = -(-(B + 1) // 8) * 8
    cond = jnp.zeros((R, D), F32).at[:B].set(c).at[B].set(c_ctx)
    ones64, ones128 = _blockdiag_ones(HD), _blockdiag_ones(ML_HD, ML_HEADS * ML_HD)
    o = IN_OFFS

    for l in range(depth):
        lam_init = 0.8 - 0.6 * math.exp(-0.3 * l)
        m = ada_modulation(cond, ada_w[l].astype(BF16), ada_b[l][None, :])
        mod = jnp.stack([jnp.broadcast_to(m[B], (B, 6 * D)), m[:B]], axis=1).reshape(B, 2, 6, D)

        w = w_in[l]
        seg = lambda i: w[:, o[i]:o[i + 1]]
        dup = lambda t: jnp.concatenate([t[:, :HD], t[:, :HD], t[:, HD:], t[:, HD:]], axis=1)
        def with_ones_slots(t, width):
            heads = t.reshape(D, -1, width)
            return jnp.concatenate([heads, jnp.zeros_like(heads)], axis=2).reshape(D, -1)
        w_a = jnp.concatenate([seg(0), _rope_partner(seg(0)), seg(1), _rope_partner(seg(1)), seg(2),
                               seg(9), _rope_partner(seg(9)), dup(seg(10)), dup(_rope_partner(seg(10))),
                               with_ones_slots(seg(11), HD)], axis=1).astype(BF16)
        w_b = jnp.concatenate([seg(3), seg(4), seg(5), seg(6), seg(7), seg(8),
                               jnp.zeros((D, LANES - 16), F32)], axis=1).astype(BF16)
        gate_b = jnp.concatenate([ml_gate_b[l], jnp.zeros((LANES - 16,), F32)])[None, :]
        tab = _rope_tables(L, nctx, gqa_qnorm_g[l], gqa_knorm_g[l])

        dq, dk, dv, gq, gk, gv = proj_attn(h, mod, w_a, tab, ones64, nct)
        u, mqk, mv, mo, mg = proj_seq(h, mod, w_b, gate_b, nct)

        yd = diff_attention(dq, dk, dv, diff_lam[l], diff_norm_g[l][None, :], nctx, lam_init)
        yg = gqa_attention(gq, gk, gv, nctx)

        bblk, cblk, lam = _s5_tables(s5_a_re[l], s5_a_im[l], s5_log_dt[l], s5_b_re[l], s5_b_im[l],
                                     s5_c_re[l], s5_c_im[l], B)
        sf, sb = (y.reshape(u.shape) for y in s5_scan(u.reshape(S * 2 * B, 256), bblk, cblk, lam, S, nctx))

        mq, mk = ml_prep(mqk, ml_conv_w[l], ml_conv_b[l][None, :], nctx)
        hf = mlstm_scan(mq, mk, mv, mg, nctx, False)
        hb = mlstm_scan(mq, mk, mv, mg, nctx, True)

        h1 = merge(h, mod, yd, yg, u, sf, sb, hf, hb, mo,
                   s5_d[l][None, :], s5_w_glu[l].astype(BF16), s5_b_glu[l][None, :], ml_norm_g[l][None, :],
                   seg(12).astype(BF16), b_gate[l][None, :], w_branch[l].astype(BF16), w_out[l].astype(BF16),
                   ln_mix_g[l][None, :], ln_mix_b[l][None, :], ones128, nct)

        t, ids, gates = peer_route(h1, mod, peer_wq[l].astype(BF16), peer_subkeys[l].astype(BF16), nct)
        h = peer_experts(t.reshape(B * S, D), ids.T, gates.T, peer_u[l].astype(BF16), peer_v[l].astype(BF16),
                         h1.reshape(B * S, D), mod, ln_ffn_g[l][None, :], ln_ffn_b[l][None, :], nt, nct).reshape(B, S, D)
    return h[:, nctx:]
```

```python
import functools
import math

import numpy as np
import jax
import jax.numpy as jnp
from jax import lax
from jax.experimental import pallas as pl
from jax.experimental.pallas import tpu as pltpu

F32 = jnp.float32
BF16 = jnp.bfloat16
I32 = jnp.int32

D_MODEL = 1024
DEPTH = 2
GRID_W = 64
ROPE_THETA = 10000.0
LN_EPS = 1e-6
HD = 64
DIFF_HEADS = 4
GQA_HEADS = 8
GQA_KV = 2
S5_GROUP = 16
S5_GROUPS = 32
S5_STATE = 64
ML_HEADS = 4
ML_HD = 128
N_BRANCH = 4
W_BRANCH = 512
PEER_HEADS = 8
PEER_NKEYS = 128
PEER_TOPK = 16
PEER_DQ = 256
ALPHA = (2 * DEPTH) ** 0.25

LANES = 128
TM = 256
TM_E = 2 * TM
TQ = 256
TK = 2048
S5_TC = 128
ML_T = 256
ML_LOCKSTEP = 4
PEER_CA = 16
VMEM_LIMIT = 56 * 1024 * 1024

IN_SPLITS = (512, 512, 512, 512, 512, 512, 512, 512, 16, 512, 128, 128, N_BRANCH * D_MODEL)
IN_OFFS = tuple(int(v) for v in np.cumsum((0,) + IN_SPLITS))


def _cparams(sem):
    return pltpu.CompilerParams(dimension_semantics=sem, vmem_limit_bytes=VMEM_LIMIT)


def _const_spec(shape):
    nd = len(shape)
    return pl.BlockSpec(shape, lambda *_: (0,) * nd, pipeline_mode=pl.Buffered(1))


def _ln(x):
    xc = x - jnp.mean(x, axis=-1, keepdims=True)
    return xc * lax.rsqrt(jnp.mean(xc * xc, axis=-1, keepdims=True) + LN_EPS)


def _sigmoid(x):
    return 1.0 / (1.0 + jnp.exp(-x))


def _gelu(x):
    return 0.5 * x * (1.0 + lax.erf(x * (2.0 ** -0.5)))


def _dot(a, b):
    return jnp.dot(a, b, preferred_element_type=F32)


def _dot_nt(a, b):
    return lax.dot_general(a, b, (((1,), (1,)), ((), ())), preferred_element_type=F32)


def _dot_tn(a, b):
    return lax.dot_general(a, b, (((0,), (0,)), ((), ())), preferred_element_type=F32)


def _ada_kernel(c_ref, w_ref, b_ref, o_ref):
    c = c_ref[...]
    o_ref[...] = _dot((c * _sigmoid(c)).astype(BF16), w_ref[...]) + b_ref[...]


def ada_modulation(cond, w, b):
    R, D = cond.shape
    N = w.shape[1]
    tn = 1536
    return pl.pallas_call(
        _ada_kernel, out_shape=jax.ShapeDtypeStruct((R, N), F32), grid=(N // tn,),
        in_specs=[pl.BlockSpec((R, D), lambda j: (0, 0)), pl.BlockSpec((D, tn), lambda j: (0, j)),
                  pl.BlockSpec((1, tn), lambda j: (0, j))],
        out_specs=pl.BlockSpec((R, tn), lambda j: (0, j)),
        compiler_params=_cparams(("arbitrary",)), name="ada")(cond, w, b)


def _proj_attn_kernel(x_ref, mod_ref, w_ref, tab_ref, ones_ref,
                      dq_ref, dk_ref, dv_ref, gq_ref, gk_ref, gv_ref):
    mod = mod_ref[0, 0]
    xm = (_ln(x_ref[0]) * (1.0 + mod[1:2, :]) + mod[0:1, :]).astype(BF16)

    def mm(lo, n):
        return _dot(xm, w_ref[:, lo:lo + n])

    def rope_store(ref, lo, n, ci, norm):
        t, tp = mm(lo, n), mm(lo + n, n)
        c, s = tab_ref[ci], tab_ref[ci + 1]
        for j in range(n // LANES):
            sl = slice(j * LANES, (j + 1) * LANES)
            tb = t[:, sl]
            y = tb * c + tp[:, sl] * s
            if norm:
                ss = _dot((tb * tb).astype(BF16), ones_ref[...])
                y = y * lax.rsqrt(ss * (1.0 / HD) + LN_EPS)
            ref[0, :, sl] = y.astype(ref.dtype)

    def value_store(ref, lo, n, width):
        lane = lax.broadcasted_iota(I32, (1, n), 1)
        ref[0] = (mm(lo, n) + jnp.where(lane % (2 * width) >= width, 1.0, 0.0)).astype(ref.dtype)

    rope_store(dq_ref, 0, 512, 0, False)
    rope_store(dk_ref, 1024, 512, 2, False)
    dv_ref[0] = mm(2048, 512).astype(dv_ref.dtype)
    rope_store(gq_ref, 2560, 512, 4, True)
    rope_store(gk_ref, 3584, 256, 6, True)
    value_store(gv_ref, 4096, 256, HD)


def proj_attn(xa, mod, w_a, tab, ones_bd, nct):
    B, S, D = xa.shape
    tok = lambda n: pl.BlockSpec((1, TM, n), lambda b, i: (b, i, 0))
    widths = (512, 512, 512, 512, 256, 256)
    outs = [jax.ShapeDtypeStruct((B, S, n), BF16) for n in widths]
    return pl.pallas_call(
        _proj_attn_kernel, out_shape=outs, grid=(B, S // TM),
        in_specs=[tok(D),
                  pl.BlockSpec((1, 1, 6, D), lambda b, i: (b, jnp.where(i >= nct, 1, 0), 0, 0)),
                  _const_spec(w_a.shape),
                  pl.BlockSpec((8, TM, LANES), lambda b, i: (0, i, 0)),
                  _const_spec(ones_bd.shape)],
        out_specs=[tok(n) for n in widths],
        compiler_params=_cparams(("parallel", "parallel")), name="proj_attn")(xa, mod, w_a, tab, ones_bd)


def _proj_seq_kernel(x_ref, mod_ref, w_ref, gb_ref, u_ref, qk_ref, v_ref, o_ref, g_ref):
    mod = mod_ref[0, 0]
    xm = (_ln(x_ref[0]) * (1.0 + mod[1:2, :]) + mod[0:1, :]).astype(BF16)
    u_ref[...] = _dot(xm, w_ref[:, 0:512])
    qk_ref[0] = _dot(xm, w_ref[:, 512:1536])
    v_ref[0] = _dot(xm, w_ref[:, 1536:2048])
    o_ref[0] = _dot(xm, w_ref[:, 2048:2560])
    g_ref[0] = _dot(xm, w_ref[:, 2560:2688]) + gb_ref[...]


def proj_seq(xa, mod, w_b, gate_b, nct):
    B, S, D = xa.shape
    tok = lambda n: pl.BlockSpec((1, TM, n), lambda b, i: (b, i, 0))
    widths = (1024, 512, 512, LANES)
    return pl.pallas_call(
        _proj_seq_kernel,
        out_shape=[jax.ShapeDtypeStruct((S, B * 512), F32)] + [jax.ShapeDtypeStruct((B, S, n), F32) for n in widths],
        grid=(B, S // TM),
        in_specs=[tok(D),
                  pl.BlockSpec((1, 1, 6, D), lambda b, i: (b, jnp.where(i >= nct, 1, 0), 0, 0)),
                  _const_spec(w_b.shape), _const_spec(gate_b.shape)],
        out_specs=[pl.BlockSpec((TM, 512), lambda b, i: (i, b))] + [tok(n) for n in widths],
        compiler_params=_cparams(("parallel", "parallel")), name="proj_seq")(xa, mod, w_b, gate_b)


def _flash(qq, k_ref, v_ref, nctx, n_lat, tk, dv):
    R, W = qq.shape[0], v_ref.shape[2]
    mxu_sums = dv < W

    def chunk(carry, rows):
        m, l, acc = carry
        s = _dot_nt(qq, k_ref[0, rows, :])
        m_new = jnp.maximum(m, jnp.max(s, axis=1, keepdims=True))
        alpha = jnp.exp(m - m_new)
        if mxu_sums:
            p = jnp.exp((s - m_new).astype(BF16))
        else:
            p = jnp.exp(s - m_new)
            l = alpha * l + jnp.sum(p, axis=1, keepdims=True)
        return m_new, l, alpha * acc + _dot(p.astype(BF16), v_ref[0, rows, :])

    carry = (jnp.full((R, 1), -jnp.inf, F32), jnp.zeros((R, 1), F32), jnp.zeros((R, W), F32))
    carry = chunk(carry, pl.ds(0, nctx))
    for c in range(n_lat):
        carry = chunk(carry, pl.ds(nctx + c * tk, tk))
    _, l, acc = carry
    return acc[:, :dv] / (acc[:, dv:dv + 1] if mxu_sums else l)


def _ctx_or_all(i, nctx, n_lat, attend):
    @pl.when(i < nctx // TQ)
    def _():
        attend(0)

    @pl.when(i >= nctx // TQ)
    def _():
        attend(n_lat)


def _diff_attn_kernel(q_ref, k_ref, v_ref, lam_ref, g_ref, o_ref, *, nctx, n_lat, tk, lam_init):
    i = pl.program_id(2)
    q = q_ref[0]
    lane = lax.broadcasted_iota(I32, q.shape, 1)
    zero = jnp.zeros_like(q)
    qq = jnp.concatenate([jnp.where(lane < HD, q, zero), jnp.where(lane >= HD, q, zero)], axis=0)
    lv = lam_ref[...]
    lam = (jnp.exp(jnp.sum(lv[0:1] * lv[1:2], axis=1, keepdims=True))
           - jnp.exp(jnp.sum(lv[2:3] * lv[3:4], axis=1, keepdims=True)) + lam_init)

    def attend(n):
        o = _flash(qq, k_ref, v_ref, nctx, n, tk, 2 * HD)
        d = o[:TQ] - lam * o[TQ:]
        y = d * lax.rsqrt(jnp.mean(d * d, axis=-1, keepdims=True) + LN_EPS)
        o_ref[0] = (y * g_ref[...] * (1.0 - lam_init)).astype(o_ref.dtype)

    _ctx_or_all(i, nctx, n_lat, attend)


def diff_attention(q, k, v, lam_vec, norm_g, nctx, lam_init):
    B, S, _ = q.shape
    tk = math.gcd(S - nctx, TK)
    kern = functools.partial(_diff_attn_kernel, nctx=nctx, n_lat=(S - nctx) // tk, tk=tk, lam_init=lam_init)
    return pl.pallas_call(
        kern, out_shape=jax.ShapeDtypeStruct((B, S, 512), BF16), grid=(B, DIFF_HEADS, S // TQ),
        in_specs=[pl.BlockSpec((1, TQ, LANES), lambda b, h, i: (b, i, h)),
                  pl.BlockSpec((1, S, LANES), lambda b, h, i: (b, 0, h)),
                  pl.BlockSpec((1, S, LANES), lambda b, h, i: (b, 0, h)),
                  pl.BlockSpec((4, HD), lambda b, h, i: (0, 0)),
                  pl.BlockSpec((1, LANES), lambda b, h, i: (0, 0))],
        out_specs=pl.BlockSpec((1, TQ, LANES), lambda b, h, i: (b, i, h)),
        compiler_params=_cparams(("parallel", "parallel", "parallel")), name="diff_attn")(q, k, v, lam_vec, norm_g)


def _gqa_attn_kernel(q_ref, k_ref, v_ref, o_ref, *, nctx, n_lat, tk):
    i = pl.program_id(2)
    q = q_ref[0]
    lane = lax.broadcasted_iota(I32, (TQ, LANES), 1)
    zero = jnp.zeros((TQ, LANES), q.dtype)
    parts = []
    for j in range(2):
        blk = q[:, j * LANES:(j + 1) * LANES]
        parts += [jnp.where(lane < HD, blk, zero), jnp.where(lane >= HD, blk, zero)]
    qq = jnp.concatenate(parts, axis=0)

    def attend(n):
        o = _flash(qq, k_ref, v_ref, nctx, n, tk, HD)
        o_ref[0] = jnp.concatenate([o[h * TQ:(h + 1) * TQ] for h in range(4)], axis=1).astype(o_ref.dtype)

    _ctx_or_all(i, nctx, n_lat, attend)


def gqa_attention(q, k, v, nctx):
    B, S, _ = q.shape
    tk = math.gcd(S - nctx, TK)
    kern = functools.partial(_gqa_attn_kernel, nctx=nctx, n_lat=(S - nctx) // tk, tk=tk)
    return pl.pallas_call(
        kern, out_shape=jax.ShapeDtypeStruct((B, S, 512), BF16), grid=(B, GQA_KV, S // TQ),
        in_specs=[pl.BlockSpec((1, TQ, 2 * LANES), lambda b, g, i: (b, i, g)),
                  pl.BlockSpec((1, S, LANES), lambda b, g, i: (b, 0, g)),
                  pl.BlockSpec((1, S, LANES), lambda b, g, i: (b, 0, g))],
        out_specs=pl.BlockSpec((1, TQ, 2 * LANES), lambda b, g, i: (b, i, g)),
        compiler_params=_cparams(("parallel", "parallel", "parallel")), name="gqa_attn")(q, k, v)


def _s5_kernel(uf_ref, ub_ref, bblk_ref, cblk_ref, lam_ref, yf_ref, yb_ref, buf_f, buf_b, st_ref):
    H = S5_GROUPS * S5_STATE // 2
    R = buf_f.shape[0]
    RB = 256

    @pl.when(pl.program_id(0) == 0)
    def _():
        st_ref[...] = jnp.zeros_like(st_ref)

    half0 = (lax.broadcasted_iota(I32, (RB, 1), 0) & 1) == 0

    for d, (u_ref, buf) in enumerate(((uf_ref, buf_f), (ub_ref, buf_b))):
        for rb in range(R // RB):
            rows = slice(rb * RB, (rb + 1) * RB)
            uh = u_ref[rows, :].astype(BF16)
            for cols in (slice(0, H), slice(H, 2 * H)):
                buf[rows, cols] = jnp.where(half0, _dot(uh, bblk_ref[d, 0, :, cols]), _dot(uh, bblk_ref[d, 1, :, cols]))

    lfr, lfi, lbr, lbi = lam_ref[0, 0], lam_ref[0, 1], lam_ref[1, 0], lam_ref[1, 1]

    def step(t, carry):
        fr, fi, br, bi = carry
        rf = pl.ds(pl.multiple_of(t * 8, 8), 8)
        x = buf_f[rf, :]
        nfr = lfr * fr - lfi * fi + x[:, :H]
        nfi = lfr * fi + lfi * fr + x[:, H:]
        buf_f[rf, :] = jnp.concatenate([nfr, nfi], axis=1)
        rb = pl.ds(pl.multiple_of((S5_TC - 1 - t) * 8, 8), 8)
        z = buf_b[rb, :]
        nbr = lbr * br - lbi * bi + z[:, :H]
        nbi = lbr * bi + lbi * br + z[:, H:]
        buf_b[rb, :] = jnp.concatenate([nbr, nbi], axis=1)
        return nfr, nfi, nbr, nbi

    fin = lax.fori_loop(0, S5_TC, step, (st_ref[0], st_ref[1], st_ref[2], st_ref[3]))
    for j in range(4):
        st_ref[j] = fin[j]

    for d, (y_ref, buf) in enumerate(((yf_ref, buf_f), (yb_ref, buf_b))):
        for rb in range(R // RB):
            rows = slice(rb * RB, (rb + 1) * RB)
            h = buf[rows, :].astype(BF16)
            y_ref[rows, :] = jnp.where(half0, _dot(h, cblk_ref[d, 0]), _dot(h, cblk_ref[d, 1]))


def s5_scan(u, bblk, cblk, lam, S, nctx):
    assert u.shape[0] == S * 8, "the scan packs (sample, half) pairs into the eight sublanes of a vreg"
    nch, nc0 = S // S5_TC, nctx // S5_TC

    def bwd(i):
        return jnp.where(i < nc0, nc0 - 1 - i, (nch - 1) - (i - nc0))
    H2 = S5_GROUPS * S5_STATE
    R, W = S5_TC * 8, u.shape[1]
    blk = lambda f: pl.BlockSpec((R, W), f)
    return pl.pallas_call(
        _s5_kernel, out_shape=[jax.ShapeDtypeStruct(u.shape, F32)] * 2, grid=(nch,),
        in_specs=[blk(lambda i: (i, 0)), blk(lambda i: (bwd(i), 0)),
                  _const_spec(bblk.shape), _const_spec(cblk.shape), _const_spec(lam.shape)],
        out_specs=[blk(lambda i: (i, 0)), blk(lambda i: (bwd(i), 0))],
        scratch_shapes=[pltpu.VMEM((R, H2), F32), pltpu.VMEM((R, H2), F32), pltpu.VMEM((4, 8, H2 // 2), F32)],
        compiler_params=_cparams(("arbitrary",)), name="s5_scan")(u, u, bblk, cblk, lam)


def _ml_prep_kernel(x_ref, prev_ref, next_ref, w_ref, b_ref, q_ref, k_ref, *, seg_starts, seg_ends):
    i = pl.program_id(1)
    x = x_ref[0]
    row = lax.broadcasted_iota(I32, x.shape, 0)
    first = functools.reduce(jnp.logical_or, [i == s for s in seg_starts])
    last = functools.reduce(jnp.logical_or, [i == s for s in seg_ends])
    pr = jnp.where(first, 0.0, prev_ref[0, 7:8, :])
    nx = jnp.where(last, 0.0, next_ref[0, 0:1, :])
    xp = jnp.where(row == 0, pr, pltpu.roll(x, 1, 0))
    xn = jnp.where(row == TM - 1, nx, pltpu.roll(x, TM - 1, 0))
    w = w_ref[...]
    y = b_ref[...] + xp * w[0:1, :] + x * w[1:2, :] + xn * w[2:3, :]
    y = y * _sigmoid(y)
    q_ref[0] = y[:, :512].astype(q_ref.dtype)
    k_ref[0] = (y[:, 512:] * (ML_HD ** -0.5)).astype(k_ref.dtype)


def ml_prep(qk, conv_w, conv_b, nctx):
    B, S, W = qk.shape
    nt, nct, r8 = S // TM, nctx // TM, TM // 8
    kern = functools.partial(_ml_prep_kernel, seg_starts=(0, nct), seg_ends=(nct - 1, nt - 1))
    return pl.pallas_call(
        kern, out_shape=[jax.ShapeDtypeStruct((B, S, 512), BF16)] * 2, grid=(B, nt),
        in_specs=[pl.BlockSpec((1, TM, W), lambda b, i: (b, i, 0)),
                  pl.BlockSpec((1, 8, W), lambda b, i: (b, jnp.maximum(i * r8 - 1, 0), 0)),
                  pl.BlockSpec((1, 8, W), lambda b, i: (b, jnp.minimum((i + 1) * r8, S // 8 - 1), 0)),
                  pl.BlockSpec((3, W), lambda b, i: (0, 0)), pl.BlockSpec((1, W), lambda b, i: (0, 0))],
        out_specs=[pl.BlockSpec((1, TM, 512), lambda b, i: (b, i, 0))] * 2,
        compiler_params=_cparams(("parallel", "parallel")), name="ml_prep")(qk, qk, qk, conv_w, conv_b)


def _log_sigmoid(x):
    return jnp.minimum(x, 0.0) - jnp.log1p(jnp.exp(-jnp.abs(x)))


def _mlstm_kernel(q_ref, k_ref, v_ref, g_ref, h_ref, c_sc, n_sc, m_sc, *, reverse):
    T = ML_T
    gi, gf = (8, 12) if reverse else (0, 4)

    @pl.when(pl.program_id(0) == 0)
    def _():
        c_sc[...] = jnp.zeros_like(c_sc)
        n_sc[...] = jnp.zeros_like(n_sc)
        m_sc[...] = jnp.zeros_like(m_sc)

    r = lax.broadcasted_iota(I32, (T, T), 0)
    c = lax.broadcasted_iota(I32, (T, T), 1)
    mask = (c >= r) if reverse else (c <= r)
    tri = jnp.where(mask, 1.0, 0.0)
    hp = lax.Precision.HIGHEST
    end = 0 if reverse else T - 1
    def gates(b):
        g = g_ref[b]
        gt = g.T
        bcol_all = jnp.dot(tri, _log_sigmoid(g), preferred_element_type=F32, precision=hp)
        brow_all = lax.dot_general(_log_sigmoid(gt[0:16]), tri, (((1,), (1,)), ((), ())),
                                   preferred_element_type=F32, precision=hp)
        return g, gt, bcol_all, brow_all

    def chain(b, hh, g, gt, bcol_all, brow_all):
        st = b * ML_HEADS + hh
        sl = slice(hh * ML_HD, (hh + 1) * ML_HD)
        q, k, v = q_ref[b, :, sl], k_ref[b, :, sl], v_ref[b, :, sl]
        bcol, brow = bcol_all[:, gf + hh:gf + hh + 1], brow_all[gf + hh:gf + hh + 1, :]
        icol, irow = g[:, gi + hh:gi + hh + 1], gt[gi + hh:gi + hh + 1, :]
        m_old = m_sc[st][:, 0:1]
        n_old = n_sc[st]
        c_old = c_sc[st]
        logw = jnp.where(mask, bcol - brow + irow, -jnp.inf)
        m_inter = bcol + m_old
        m_t = jnp.maximum(m_inter, jnp.max(logw, axis=1, keepdims=True))
        yield
        s = _dot_nt(q, k) * jnp.exp(logw - m_t)
        inter = jnp.exp(m_inter - m_t)
        yield
        num = _dot(s.astype(BF16), v.astype(BF16)) + inter * _dot_nt(q, c_old.astype(BF16))
        den = jnp.sum(s, axis=1, keepdims=True) + inter * jnp.sum(q.astype(F32) * n_old, axis=1, keepdims=True)
        h_ref[b, :, sl] = num / jnp.maximum(jnp.abs(den), jnp.exp(-m_t))
        yield
        b_end = bcol[end:end + 1, :]
        g_row, g_col = b_end - brow + irow, b_end - bcol + icol
        m_new = jnp.maximum(b_end + m_old, jnp.max(g_row, axis=1, keepdims=True))
        decay = jnp.exp(b_end + m_old - m_new)
        wk = jnp.exp(g_col - m_new)
        c_sc[st] = decay * c_old + _dot_tn((v * wk).astype(BF16), k)
        n_sc[st] = decay * n_old + jnp.sum(k.astype(F32) * wk, axis=0, keepdims=True)
        m_sc[st] = jnp.broadcast_to(m_new, (1, LANES))

    nb = q_ref.shape[0]
    for b0 in range(0, nb, ML_LOCKSTEP):
        chains = []
        for b in range(b0, min(b0 + ML_LOCKSTEP, nb)):
            shared = gates(b)
            chains += [chain(b, hh, *shared) for hh in range(ML_HEADS)]
        for _ in range(4):
            for ch in chains:
                next(ch, None)


def mlstm_scan(q, k, v, g, nctx, reverse):
    B, S, W = q.shape
    nch, nc0 = S // ML_T, nctx // ML_T

    def order(i):
        return jnp.where(i < nc0, nc0 - 1 - i, (nch - 1) - (i - nc0)) if reverse else i
    blk = lambda n: pl.BlockSpec((B, ML_T, n), lambda i: (0, order(i), 0))
    nst = B * ML_HEADS
    return pl.pallas_call(
        functools.partial(_mlstm_kernel, reverse=reverse),
        out_shape=jax.ShapeDtypeStruct((B, S, W), F32), grid=(nch,),
        in_specs=[blk(W), blk(W), blk(W), blk(LANES)], out_specs=blk(W),
        scratch_shapes=[pltpu.VMEM((nst, ML_HD, ML_HD), F32), pltpu.VMEM((nst, 1, ML_HD), F32),
                        pltpu.VMEM((nst, 1, LANES), F32)],
        compiler_params=_cparams(("arbitrary",)), name="mlstm_bwd" if reverse else "mlstm_fwd")(q, k, v, g)


def _merge_kernel(x_ref, mod_ref, yd_ref, yg_ref, u_ref, sf_ref, sb_ref, hf_ref, hb_ref, mo_ref,
                  s5d_ref, wglu_ref, bglu_ref, mlg_ref, wgate_ref, bgate_ref, wbr_ref, wout_ref,
                  lng_ref, lnb_ref, ones_ref, o_ref):
    x = x_ref[0]
    mod = mod_ref[0, 0]
    xm = (_ln(x) * (1.0 + mod[1:2, :]) + mod[0:1, :]).astype(BF16)
    ys = u_ref[...] * s5d_ref[...] + sf_ref[...] + sb_ref[...]
    z = _dot(_gelu(ys).astype(BF16), wglu_ref[...]) + bglu_ref[...]
    ys = z[:, :512] * _sigmoid(z[:, 512:])
    hm = hf_ref[0] + hb_ref[0]
    ss = _dot((hm * hm).astype(BF16), ones_ref[...])
    ym = hm * lax.rsqrt(ss * (1.0 / ML_HD) + LN_EPS) * mlg_ref[...] * _sigmoid(mo_ref[0])
    branches = (yd_ref[0], ys.astype(BF16), ym.astype(BF16), yg_ref[0])
    merged = None
    for j, yb in enumerate(branches):
        gate = _sigmoid(_dot(xm, wgate_ref[:, j * D_MODEL:(j + 1) * D_MODEL]) + bgate_ref[:, j * D_MODEL:(j + 1) * D_MODEL])
        term = gate * _dot(yb, wbr_ref[j])
        merged = term if merged is None else merged + term
    mix = _dot(merged.astype(BF16), wout_ref[...])
    o_ref[0] = _ln(ALPHA * x + mod[2:3, :] * mix) * lng_ref[...] + lnb_ref[...]


def merge(xa, mod, yd, yg, u, sf, sb, hf, hb, mo, s5d, wglu, bglu, mlg, wgate, bgate, wbr, wout, lng, lnb, ones_ml, nct):
    B, S, D = xa.shape
    tok = lambda n: pl.BlockSpec((1, TM, n), lambda b, i: (b, i, 0))
    consts = (s5d, wglu, bglu, mlg, wgate, bgate, wbr, wout, lng, lnb, ones_ml)
    return pl.pallas_call(
        _merge_kernel, out_shape=jax.ShapeDtypeStruct((B, S, D), F32), grid=(B, S // TM),
        in_specs=[tok(D), pl.BlockSpec((1, 1, 6, D), lambda b, i: (b, jnp.where(i >= nct, 1, 0), 0, 0))]
        + [tok(512)] * 2 + [pl.BlockSpec((TM, 512), lambda b, i: (i, b))] * 3 + [tok(512)] * 3
        + [_const_spec(a.shape) for a in consts],
        out_specs=tok(D),
        compiler_params=_cparams(("parallel", "parallel")), name="merge")(xa, mod, yd, yg, u, sf, sb, hf, hb, mo, *consts)


def _top16(*problems):
    iotas = {prob[0].shape: lax.broadcasted_iota(I32, prob[0].shape, 0).astype(F32) for prob in problems}

    def one(kk, s, val_ref, idx_ref, payload):
        R, ri = s.shape[0], iotas[s.shape]
        m = jnp.max(s, axis=0, keepdims=True)
        ix = jnp.min(jnp.where(s == m, ri, float(R)), axis=0, keepdims=True)
        hit = ri == ix
        val_ref[pl.ds(kk, 1), :] = m
        idx_ref[pl.ds(kk, 1), :] = ix if payload is None else jnp.max(jnp.where(hit, payload, -1.0), axis=0, keepdims=True)
        return jnp.where(hit, -jnp.inf, s)

    def body(kk, ss):
        return tuple(one(kk, s, *prob[1:]) for s, prob in zip(ss, problems))

    lax.fori_loop(0, PEER_TOPK, body, tuple(prob[0] for prob in problems))


_PEER_CAND_ROWS = ((0, 16),) + tuple((16 + 8 * (p - 1), 8) for p in range(1, 8))
_PEER_NCAND = _PEER_CAND_ROWS[-1][0] + 16


def _peer_route_kernel(x_ref, mod0_ref, mod1_ref, wq_ref, sk_ref, t_ref, ids_ref, gate_ref,
                       v1_sc, i1_sc, v2_sc, i2_sc, cand_sc, cidx_sc, top_sc, tid_sc):
    h = pl.program_id(1)

    @pl.when(h == 0)
    def _():
        for half, mod_ref in enumerate((mod0_ref, mod1_ref)):
            rows = slice(half * TM, (half + 1) * TM)
            mod = mod_ref[0, 0]
            t_ref[rows, :] = (_ln(x_ref[rows, :]) * (1.0 + mod[4:5, :]) + mod[3:4, :]).astype(BF16)

    q = _dot(t_ref[...], wq_ref[...]).astype(BF16)
    half = PEER_DQ // 2
    _top16((_dot_nt(sk_ref[0], q[:, :half]), v1_sc, i1_sc, None))
    _top16((_dot_nt(sk_ref[1], q[:, half:]), v2_sc, i2_sc, None))
    nk = float(PEER_NKEYS)
    for p, (lo, n) in enumerate(_PEER_CAND_ROWS):
        cand_sc[lo:lo + n, :] = v1_sc[p:p + 1, :] + v2_sc[0:n, :]
        cidx_sc[lo:lo + n, :] = i1_sc[p:p + 1, :] * nk + i2_sc[0:n, :]
    lo = _PEER_CAND_ROWS[-1][0] + _PEER_CAND_ROWS[-1][1]
    cand_sc[lo:lo + 8, :] = v1_sc[8:16, :] + v2_sc[0:1, :]
    cidx_sc[lo:lo + 8, :] = i1_sc[8:16, :] * nk + i2_sc[0:1, :]
    _top16((cand_sc[...], top_sc, tid_sc, cidx_sc[...]))
    top = top_sc[...]
    e = jnp.exp(top - top[0:1, :])
    gate_ref[...] = e / jnp.sum(e, axis=0, keepdims=True)
    ids_ref[...] = tid_sc[...].astype(I32)


def _half_tile_mod_spec(half, nt_per_sample, nct):
    def index(i, j):
        r = 2 * i + half
        return (r // nt_per_sample, jnp.where(r % nt_per_sample >= nct, 1, 0), 0, 0)
    return pl.BlockSpec((1, 1, 6, D_MODEL), index)


def peer_route(h1, mod, wq, subkeys, nt_per_sample, nct):
    T, D = h1.shape
    assert T % TM_E == 0 and TM_E == 2 * TM
    f = lambda n: pltpu.VMEM((n, TM_E), F32)
    return pl.pallas_call(
        _peer_route_kernel,
        out_shape=[jax.ShapeDtypeStruct((T, D), BF16),
                   jax.ShapeDtypeStruct((PEER_HEADS * PEER_TOPK, T), I32),
                   jax.ShapeDtypeStruct((PEER_HEADS * PEER_TOPK, T), F32)],
        grid=(T // TM_E, PEER_HEADS),
        in_specs=[pl.BlockSpec((TM_E, D), lambda i, h: (i, 0)),
                  _half_tile_mod_spec(0, nt_per_sample, nct), _half_tile_mod_spec(1, nt_per_sample, nct),
                  pl.BlockSpec((D, PEER_DQ), lambda i, h: (0, h)),
                  pl.BlockSpec((2, PEER_NKEYS, PEER_DQ // 2), lambda i, h: (0, 0, 0))],
        out_specs=[pl.BlockSpec((TM_E, D), lambda i, h: (i, 0)),
                   pl.BlockSpec((PEER_TOPK, TM_E), lambda i, h: (h, i)),
                   pl.BlockSpec((PEER_TOPK, TM_E), lambda i, h: (h, i))],
        scratch_shapes=[f(16), f(16), f(16), f(16), f(_PEER_NCAND), f(_PEER_NCAND), f(16), f(16)],
        compiler_params=_cparams(("parallel", "arbitrary")), name="peer_route")(h1, mod, mod, wq, subkeys)


def _peer_expert_kernel(t_ref, ids_ref, gate_ref, u_ref, v_ref, x_ref, mod0_ref, mod1_ref, lng_ref, lnb_ref, o_ref,
                        a_sc, raw_sc, wraw_sc, wgt_sc, *, nchunk):
    j = pl.program_id(1)
    NK, CA = PEER_NKEYS, PEER_CA
    w_sc = a_sc

    def dense(slot):
        raw_sc[slot] = _dot(t_ref[...], u_ref[0]).astype(BF16)

    def relayout_in(c, slot):
        a = raw_sc[slot]
        a3 = jnp.stack([a[:, al * NK:(al + 1) * NK] for al in range(CA)], axis=0)
        a_sc[:, pl.ds(pl.multiple_of(c * CA, CA), CA), :] = pltpu.einshape("atb->tab", a3)

    def relayout_out(c, slot):
        w3 = pltpu.einshape("tab->atb", w_sc[:, pl.ds(pl.multiple_of(c * CA, CA), CA), :])
        wraw_sc[slot] = jnp.concatenate([w3[al] for al in range(CA)], axis=1)

    def weighted_sum(slot):
        o_ref[...] += _dot(wraw_sc[slot], v_ref[...])

    @pl.when(j == 0)
    def _():
        dense(0)

    for slot in (0, 1):
        @pl.when(jnp.logical_and(jnp.logical_and(j > 0, j < nchunk), j % 2 == slot))
        def _():
            relayout_in(j - 1, 1 - slot)
            dense(slot)

    @pl.when(j == nchunk - 1)
    def _():
        relayout_in(j, (nchunk - 1) % 2)
        io_k = lax.broadcasted_iota(I32, (NK, NK), 0)
        io_a = lax.broadcasted_iota(I32, (NK, 2 * NK), 0)

        def masks(two):
            idp = ids_ref[two, :]
            ids = jnp.concatenate([idp[0:1], idp[1:2]], axis=1)
            blocks = [jnp.where(io_k == (idp[tok:tok + 1] & (NK - 1)), 1.0, 0.0).astype(BF16) for tok in range(2)]
            zero = jnp.zeros((NK, NK), BF16)
            onehot = jnp.concatenate([jnp.concatenate([blocks[0], zero], axis=1),
                                      jnp.concatenate([zero, blocks[1]], axis=1)], axis=0)
            return onehot, io_a == (ids >> 7)

        def pick(p, carry):
            two = pl.ds(pl.multiple_of(p * 2, 2), 2)
            onehot, r1 = masks(two)
            a2 = a_sc[two]
            at = jnp.concatenate([a2[0], a2[1]], axis=1)
            picked = _dot(at, onehot)
            act = jnp.sum(jnp.where(r1, picked, 0.0), axis=0, keepdims=True)
            gp = gate_ref[two, :]
            w = jnp.concatenate([gp[0:1], gp[1:2]], axis=1) * _gelu(act)
            wgt_sc[two, :] = jnp.concatenate([w[:, :NK], w[:, NK:]], axis=0)
            return carry

        def scatter(p, carry):
            two = pl.ds(pl.multiple_of(p * 2, 2), 2)
            onehot, r1 = masks(two)
            wp = wgt_sc[two, :]
            w = jnp.concatenate([wp[0:1], wp[1:2]], axis=1)
            wt = _dot_nt(jnp.where(r1, w, 0.0).astype(BF16), onehot)
            w_sc[two] = jnp.stack([wt[:, :NK], wt[:, NK:]], axis=0).astype(BF16)
            return carry

        lax.fori_loop(0, TM_E // 2, pick, 0, unroll=16)
        lax.fori_loop(0, TM_E // 2, scatter, 0, unroll=16)
        o_ref[...] = jnp.zeros_like(o_ref)
        relayout_out(0, 0)

    for slot in (0, 1):
        @pl.when(jnp.logical_and(jnp.logical_and(j >= nchunk, j < 2 * nchunk - 1), (j - nchunk) % 2 == slot))
        def _():
            weighted_sum(slot)
            relayout_out(j - nchunk + 1, 1 - slot)

    @pl.when(j == 2 * nchunk - 1)
    def _():
        weighted_sum((nchunk - 1) % 2)
        for half, mod_ref in enumerate((mod0_ref, mod1_ref)):
            rows = slice(half * TM, (half + 1) * TM)
            y = ALPHA * x_ref[rows, :] + mod_ref[0, 0][5:6, :] * o_ref[rows, :]
            o_ref[rows, :] = _ln(y) * lng_ref[...] + lnb_ref[...]


def peer_experts(t, ids, gates, emb_u, emb_v, h1, mod, lng, lnb, nt_per_sample, nct):
    T, D = h1.shape
    E = emb_v.shape[0]
    ce = PEER_CA * PEER_NKEYS
    nchunk = E // ce
    emb_ut = emb_u.reshape(nchunk, ce, D).transpose(0, 2, 1)
    assert T % TM_E == 0 and TM_E == 2 * TM
    tok = lambda n, **kw: pl.BlockSpec((TM_E, n), lambda i, j: (i, 0), **kw)
    mod_spec = lambda half: _half_tile_mod_spec(half, nt_per_sample, nct)
    once = dict(pipeline_mode=pl.Buffered(1))
    return pl.pallas_call(
        functools.partial(_peer_expert_kernel, nchunk=nchunk),
        out_shape=jax.ShapeDtypeStruct((T, D), F32), grid=(T // TM_E, 2 * nchunk),
        in_specs=[tok(D, **once), tok(LANES, **once), tok(LANES, **once),
                  pl.BlockSpec((1, D, ce), lambda i, j: (jnp.minimum(j, nchunk - 1), 0, 0)),
                  pl.BlockSpec((ce, D), lambda i, j: (jnp.maximum(j - nchunk, 0), 0)),
                  tok(D, **once), mod_spec(0), mod_spec(1),
                  pl.BlockSpec((1, D), lambda i, j: (0, 0)), pl.BlockSpec((1, D), lambda i, j: (0, 0))],
        out_specs=tok(D),
        scratch_shapes=[pltpu.VMEM((TM_E, PEER_NKEYS, PEER_NKEYS), BF16), pltpu.VMEM((2, TM_E, ce), BF16),
                        pltpu.VMEM((2, TM_E, ce), BF16), pltpu.VMEM((TM_E, LANES), F32)],
        compiler_params=_cparams(("parallel", "arbitrary")), name="peer_experts")(t, ids, gates, emb_ut, emb_v, h1, mod, mod, lng, lnb)


_ROPE_IDX = np.arange(HD)
_ROPE_PERM = np.where(_ROPE_IDX % 32 < 16, _ROPE_IDX + 16, _ROPE_IDX - 16)
_ROPE_SIGN = np.where(_ROPE_IDX % 32 < 16, -1.0, 1.0).astype(np.float32)


def _rope_partner(w):
    n = w.shape[1] // HD
    perm = np.concatenate([h * HD + _ROPE_PERM for h in range(n)])
    return w[:, perm] * jnp.asarray(np.tile(_ROPE_SIGN, n))


def _rope_tables(L, nctx, gq, gk):
    rows = L // GRID_W
    row = jnp.repeat(jnp.arange(rows, dtype=F32), GRID_W)
    col = jnp.tile(jnp.arange(GRID_W, dtype=F32), rows)
    nf = HD // 4
    inv = ROPE_THETA ** (-jnp.arange(nf, dtype=F32) / nf)
    ar, ac = row[:, None] * inv, col[:, None] * inv
    cos = jnp.concatenate([jnp.cos(ar), jnp.cos(ar), jnp.cos(ac), jnp.cos(ac)], axis=1)
    sin = jnp.concatenate([jnp.sin(ar), jnp.sin(ar), jnp.sin(ac), jnp.sin(ac)], axis=1)
    cos = jnp.concatenate([jnp.ones((nctx, HD), F32), cos], axis=0)
    sin = jnp.concatenate([jnp.zeros((nctx, HD), F32), sin], axis=0)
    scale = HD ** -0.5
    one = jnp.ones((HD,), F32)
    tabs = []
    for g, sc in ((one, scale), (one, 1.0), (gq, scale), (gk, 1.0)):
        tabs += [cos * (g * sc), sin * (g[_ROPE_PERM] * sc)]
    return jnp.tile(jnp.stack(tabs), (1, 1, 2))


def _s5_tables(a_re, a_im, log_dt, b_re, b_im, c_re, c_im, nb):
    dt = jnp.exp(log_dt)[..., None]
    mag = jnp.exp(a_re * dt)
    lr, li = mag * jnp.cos(a_im * dt), mag * jnp.sin(a_im * dt)
    den = a_re * a_re + a_im * a_im
    cr = ((lr - 1) * a_re + li * a_im) / den
    ci = (li * a_re - (lr - 1) * a_im) / den
    br = cr[..., None] * b_re - ci[..., None] * b_im
    bi = cr[..., None] * b_im + ci[..., None] * b_re
    G2, N, C = S5_GROUPS // 2, S5_STATE, S5_GROUP
    eye = jnp.eye(G2, dtype=F32)

    def blockdiag_in(m):
        m = m.reshape(2, 2, G2, N, C)
        return jnp.einsum('dhgnc,gk->dhgckn', m, eye).reshape(2, 2, G2 * C, G2 * N)

    def blockdiag_out(m):
        m = m.reshape(2, 2, G2, C, N)
        return jnp.einsum('dhgcn,gk->dhgnkc', m, eye).reshape(2, 2, G2 * N, G2 * C)

    bblk = jnp.concatenate([blockdiag_in(br), blockdiag_in(bi)], axis=-1).astype(BF16)
    cblk = jnp.concatenate([blockdiag_out(c_re), blockdiag_out(-c_im)], axis=-2).astype(BF16)
    lam = jnp.stack([lr, li], axis=1).reshape(2, 2, 2, G2 * N)
    lam = jnp.tile(lam[:, :, None], (1, 1, nb, 1, 1)).reshape(2, 2, 2 * nb, G2 * N)
    return bblk, cblk, lam


def _blockdiag_ones(group, n=LANES):
    i = np.arange(n) // group
    return jnp.asarray((i[:, None] == i[None, :]).astype(np.float32), dtype=BF16)


def kernel(x, c, ctx, c_ctx, ada_w, ada_b, w_in, b_gate, diff_lam, diff_norm_g, gqa_qnorm_g, gqa_knorm_g,
           s5_a_re, s5_a_im, s5_log_dt, s5_b_re, s5_b_im, s5_c_re, s5_c_im, s5_d, s5_w_glu, s5_b_glu,
           ml_conv_w, ml_conv_b, ml_gate_b, ml_norm_g, w_branch, w_out, ln_mix_g, ln_mix_b, ln_ffn_g, ln_ffn_b,
           peer_wq, peer_subkeys, peer_u, peer_v):
    B, L, D = x.shape
    nctx = ctx.shape[1]
    S = nctx + L
    assert D == D_MODEL and nctx % TM == 0 and L % TM == 0 and L % GRID_W == 0
    nt, nct = S // TM, nctx // TM
    depth = ada_w.shape[0]

    h = jnp.concatenate([ctx, x], axis=1)
    R = -(-(B + 1) // 8) * 8
    cond = jnp.zeros((R, D), F32).at[:B].set(c).at[B].set(c_ctx)
    ones64, ones128 = _blockdiag_ones(HD), _blockdiag_ones(ML_HD, ML_HEADS * ML_HD)
    o = IN_OFFS

    for l in range(depth):
        lam_init = 0.8 - 0.6 * math.exp(-0.3 * l)
        m = ada_modulation(cond, ada_w[l].astype(BF16), ada_b[l][None, :])
        mod = jnp.stack([jnp.broadcast_to(m[B], (B, 6 * D)), m[:B]], axis=1).reshape(B, 2, 6, D)

        w = w_in[l]
        seg = lambda i: w[:, o[i]:o[i + 1]]
        dup = lambda t: jnp.concatenate([t[:, :HD], t[:, :HD], t[:, HD:], t[:, HD:]], axis=1)
        def with_ones_slots(t, width):
            heads = t.reshape(D, -1, width)
            return jnp.concatenate([heads, jnp.zeros_like(heads)], axis=2).reshape(D, -1)
        w_a = jnp.concatenate([seg(0), _rope_partner(seg(0)), seg(1), _rope_partner(seg(1)), seg(2),
                               seg(9), _rope_partner(seg(9)), dup(seg(10)), dup(_rope_partner(seg(10))),
                               with_ones_slots(seg(11), HD)], axis=1).astype(BF16)
        w_b = jnp.concatenate([seg(3), seg(4), seg(5), seg(6), seg(7), seg(8),
                               jnp.zeros((D, LANES - 16), F32)], axis=1).astype(BF16)
        gate_b = jnp.concatenate([ml_gate_b[l], jnp.zeros((LANES - 16,), F32)])[None, :]
        tab = _rope_tables(L, nctx, gqa_qnorm_g[l], gqa_knorm_g[l])

        dq, dk, dv, gq, gk, gv = proj_attn(h, mod, w_a, tab, ones64, nct)
        u, mqk, mv, mo, mg = proj_seq(h, mod, w_b, gate_b, nct)

        yd = diff_attention(dq, dk, dv, diff_lam[l], diff_norm_g[l][None, :], nctx, lam_init)
        yg = gqa_attention(gq, gk, gv, nctx)

        bblk, cblk, lam = _s5_tables(s5_a_re[l], s5_a_im[l], s5_log_dt[l], s5_b_re[l], s5_b_im[l],
                                     s5_c_re[l], s5_c_im[l], B)
        sf, sb = (y.reshape(u.shape) for y in s5_scan(u.reshape(S * 2 * B, 256), bblk, cblk, lam, S, nctx))

        mq, mk = ml_prep(mqk, ml_conv_w[l], ml_conv_b[l][None, :], nctx)
        hf = mlstm_scan(mq, mk, mv, mg, nctx, False)
        hb = mlstm_scan(mq, mk, mv, mg, nctx, True)

        h1 = merge(h, mod, yd, yg, u, sf, sb, hf, hb, mo,
                   s5_d[l][None, :], s5_w_glu[l].astype(BF16), s5_b_glu[l][None, :], ml_norm_g[l][None, :],
                   seg(12).astype(BF16), b_gate[l][None, :], w_branch[l].astype(BF16), w_out[l].astype(BF16),
                   ln_mix_g[l][None, :], ln_mix_b[l][None, :], ones128, nct)

        h1 = h1.reshape(B * S, D)
        t, ids, gates = peer_route(h1, mod, peer_wq[l].astype(BF16), peer_subkeys[l].astype(BF16), nt, nct)
        h = peer_experts(t, ids.T, gates.T, peer_u[l].astype(BF16), peer_v[l].astype(BF16),
                         h1, mod, ln_ffn_g[l][None, :], ln_ffn_b[l][None, :], nt, nct).reshape(B, S, D)
    return h[:, nctx:]
```

```python
import functools
import math

import numpy as np
import jax
import jax.numpy as jnp
from jax import lax
from jax.experimental import pallas as pl
from jax.experimental.pallas import tpu as pltpu

F32 = jnp.float32
BF16 = jnp.bfloat16
I32 = jnp.int32

D_MODEL = 1024
DEPTH = 2
GRID_W = 64
ROPE_THETA = 10000.0
LN_EPS = 1e-6
HD = 64
DIFF_HEADS = 4
GQA_HEADS = 8
GQA_KV = 2
S5_GROUP = 16
S5_GROUPS = 32
S5_STATE = 64
ML_HEADS = 4
ML_HD = 128
N_BRANCH = 4
W_BRANCH = 512
PEER_HEADS = 8
PEER_NKEYS = 128
PEER_TOPK = 16
PEER_DQ = 256
ALPHA = (2 * DEPTH) ** 0.25

LANES = 128
TM = 256
TM_E = 2 * TM
TQ = 256
TK = 2048
S5_TC = 128
ML_T = 256
ML_LOCKSTEP = 4
PEER_CA = 8
VMEM_LIMIT = 56 * 1024 * 1024

IN_SPLITS = (512, 512, 512, 512, 512, 512, 512, 512, 16, 512, 128, 128, N_BRANCH * D_MODEL)
IN_OFFS = tuple(int(v) for v in np.cumsum((0,) + IN_SPLITS))


def _cparams(sem):
    return pltpu.CompilerParams(dimension_semantics=sem, vmem_limit_bytes=VMEM_LIMIT)


def _const_spec(shape):
    nd = len(shape)
    return pl.BlockSpec(shape, lambda *_: (0,) * nd, pipeline_mode=pl.Buffered(1))


def _ln(x):
    xc = x - jnp.mean(x, axis=-1, keepdims=True)
    return xc * lax.rsqrt(jnp.mean(xc * xc, axis=-1, keepdims=True) + LN_EPS)


def _sigmoid(x):
    return 1.0 / (1.0 + jnp.exp(-x))


def _gelu(x):
    return 0.5 * x * (1.0 + lax.erf(x * (2.0 ** -0.5)))


def _dot(a, b):
    return jnp.dot(a, b, preferred_element_type=F32)


def _dot_nt(a, b):
    return lax.dot_general(a, b, (((1,), (1,)), ((), ())), preferred_element_type=F32)


def _dot_tn(a, b):
    return lax.dot_general(a, b, (((0,), (0,)), ((), ())), preferred_element_type=F32)


def _ada_kernel(c_ref, w_ref, b_ref, o_ref):
    c = c_ref[...]
    o_ref[...] = _dot((c * _sigmoid(c)).astype(BF16), w_ref[...]) + b_ref[...]


def ada_modulation(cond, w, b):
    R, D = cond.shape
    N = w.shape[1]
    tn = 1536
    return pl.pallas_call(
        _ada_kernel, out_shape=jax.ShapeDtypeStruct((R, N), F32), grid=(N // tn,),
        in_specs=[pl.BlockSpec((R, D), lambda j: (0, 0)), pl.BlockSpec((D, tn), lambda j: (0, j)),
                  pl.BlockSpec((1, tn), lambda j: (0, j))],
        out_specs=pl.BlockSpec((R, tn), lambda j: (0, j)),
        compiler_params=_cparams(("arbitrary",)), name="ada")(cond, w, b)


def _proj_attn_kernel(x_ref, mod_ref, w_ref, tab_ref, ones_ref,
                      dq_ref, dk_ref, dv_ref, gq_ref, gk_ref, gv_ref):
    mod = mod_ref[0, 0]
    xm = (_ln(x_ref[0]) * (1.0 + mod[1:2, :]) + mod[0:1, :]).astype(BF16)

    def mm(lo, n):
        return _dot(xm, w_ref[:, lo:lo + n])

    def rope_store(ref, lo, n, ci, norm):
        t, tp = mm(lo, n), mm(lo + n, n)
        c, s = tab_ref[ci], tab_ref[ci + 1]
        for j in range(n // LANES):
            sl = slice(j * LANES, (j + 1) * LANES)
            tb = t[:, sl]
            y = tb * c + tp[:, sl] * s
            if norm:
                ss = _dot((tb * tb).astype(BF16), ones_ref[...])
                y = y * lax.rsqrt(ss * (1.0 / HD) + LN_EPS)
            ref[0, :, sl] = y.astype(ref.dtype)

    def value_store(ref, lo, n, width):
        lane = lax.broadcasted_iota(I32, (1, n), 1)
        ref[0] = (mm(lo, n) + jnp.where(lane % (2 * width) >= width, 1.0, 0.0)).astype(ref.dtype)

    rope_store(dq_ref, 0, 512, 0, False)
    rope_store(dk_ref, 1024, 512, 2, False)
    dv_ref[0] = mm(2048, 512).astype(dv_ref.dtype)
    rope_store(gq_ref, 2560, 512, 4, True)
    rope_store(gk_ref, 3584, 256, 6, True)
    value_store(gv_ref, 4096, 256, HD)


def proj_attn(xa, mod, w_a, tab, ones_bd, nct):
    B, S, D = xa.shape
    tok = lambda n: pl.BlockSpec((1, TM, n), lambda b, i: (b, i, 0))
    widths = (512, 512, 512, 512, 256, 256)
    outs = [jax.ShapeDtypeStruct((B, S, n), BF16) for n in widths]
    return pl.pallas_call(
        _proj_attn_kernel, out_shape=outs, grid=(B, S // TM),
        in_specs=[tok(D),
                  pl.BlockSpec((1, 1, 6, D), lambda b, i: (b, jnp.where(i >= nct, 1, 0), 0, 0)),
                  _const_spec(w_a.shape),
                  pl.BlockSpec((8, TM, LANES), lambda b, i: (0, i, 0)),
                  _const_spec(ones_bd.shape)],
        out_specs=[tok(n) for n in widths],
        compiler_params=_cparams(("parallel", "parallel")), name="proj_attn")(xa, mod, w_a, tab, ones_bd)


def _proj_seq_kernel(x_ref, mod_ref, w_ref, gb_ref, u_ref, qk_ref, v_ref, o_ref, g_ref):
    mod = mod_ref[0, 0]
    xm = (_ln(x_ref[0]) * (1.0 + mod[1:2, :]) + mod[0:1, :]).astype(BF16)
    u_ref[...] = _dot(xm, w_ref[:, 0:512])
    qk_ref[0] = _dot(xm, w_ref[:, 512:1536])
    v_ref[0] = _dot(xm, w_ref[:, 1536:2048])
    o_ref[0] = _dot(xm, w_ref[:, 2048:2560])
    g_ref[0] = _dot(xm, w_ref[:, 2560:2688]) + gb_ref[...]


def proj_seq(xa, mod, w_b, gate_b, nct):
    B, S, D = xa.shape
    tok = lambda n: pl.BlockSpec((1, TM, n), lambda b, i: (b, i, 0))
    widths = (1024, 512, 512, LANES)
    return pl.pallas_call(
        _proj_seq_kernel,
        out_shape=[jax.ShapeDtypeStruct((S, B * 512), F32)] + [jax.ShapeDtypeStruct((B, S, n), F32) for n in widths],
        grid=(B, S // TM),
        in_specs=[tok(D),
                  pl.BlockSpec((1, 1, 6, D), lambda b, i: (b, jnp.where(i >= nct, 1, 0), 0, 0)),
                  _const_spec(w_b.shape), _const_spec(gate_b.shape)],
        out_specs=[pl.BlockSpec((TM, 512), lambda b, i: (i, b))] + [tok(n) for n in widths],
        compiler_params=_cparams(("parallel", "parallel")), name="proj_seq")(xa, mod, w_b, gate_b)


def _flash(qq, k_ref, v_ref, nctx, n_lat, tk, dv):
    R, W = qq.shape[0], v_ref.shape[2]
    mxu_sums = dv < W

    def chunk(carry, rows):
        m, l, acc = carry
        s = _dot_nt(qq, k_ref[0, rows, :])
        m_new = jnp.maximum(m, jnp.max(s, axis=1, keepdims=True))
        alpha = jnp.exp(m - m_new)
        if mxu_sums:
            p = jnp.exp((s - m_new).astype(BF16))
        else:
            p = jnp.exp(s - m_new)
            l = alpha * l + jnp.sum(p, axis=1, keepdims=True)
        return m_new, l, alpha * acc + _dot(p.astype(BF16), v_ref[0, rows, :])

    carry = (jnp.full((R, 1), -jnp.inf, F32), jnp.zeros((R, 1), F32), jnp.zeros((R, W), F32))
    carry = chunk(carry, pl.ds(0, nctx))
    for c in range(n_lat):
        carry = chunk(carry, pl.ds(nctx + c * tk, tk))
    _, l, acc = carry
    return acc[:, :dv] / (acc[:, dv:dv + 1] if mxu_sums else l)


def _ctx_or_all(i, nctx, n_lat, attend):
    @pl.when(i < nctx // TQ)
    def _():
        attend(0)

    @pl.when(i >= nctx // TQ)
    def _():
        attend(n_lat)


def _diff_attn_kernel(q_ref, k_ref, v_ref, lam_ref, g_ref, o_ref, *, nctx, n_lat, tk, lam_init):
    i = pl.program_id(2)
    q = q_ref[0]
    lane = lax.broadcasted_iota(I32, q.shape, 1)
    zero = jnp.zeros_like(q)
    qq = jnp.concatenate([jnp.where(lane < HD, q, zero), jnp.where(lane >= HD, q, zero)], axis=0)
    lv = lam_ref[...]
    lam = (jnp.exp(jnp.sum(lv[0:1] * lv[1:2], axis=1, keepdims=True))
           - jnp.exp(jnp.sum(lv[2:3] * lv[3:4], axis=1, keepdims=True)) + lam_init)

    def attend(n):
        o = _flash(qq, k_ref, v_ref, nctx, n, tk, 2 * HD)
        d = o[:TQ] - lam * o[TQ:]
        y = d * lax.rsqrt(jnp.mean(d * d, axis=-1, keepdims=True) + LN_EPS)
        o_ref[0] = (y * g_ref[...] * (1.0 - lam_init)).astype(o_ref.dtype)

    _ctx_or_all(i, nctx, n_lat, attend)


def diff_attention(q, k, v, lam_vec, norm_g, nctx, lam_init):
    B, S, _ = q.shape
    tk = math.gcd(S - nctx, TK)
    kern = functools.partial(_diff_attn_kernel, nctx=nctx, n_lat=(S - nctx) // tk, tk=tk, lam_init=lam_init)
    return pl.pallas_call(
        kern, out_shape=jax.ShapeDtypeStruct((B, S, 512), BF16), grid=(B, DIFF_HEADS, S // TQ),
        in_specs=[pl.BlockSpec((1, TQ, LANES), lambda b, h, i: (b, i, h)),
                  pl.BlockSpec((1, S, LANES), lambda b, h, i: (b, 0, h)),
                  pl.BlockSpec((1, S, LANES), lambda b, h, i: (b, 0, h)),
                  pl.BlockSpec((4, HD), lambda b, h, i: (0, 0)),
                  pl.BlockSpec((1, LANES), lambda b, h, i: (0, 0))],
        out_specs=pl.BlockSpec((1, TQ, LANES), lambda b, h, i: (b, i, h)),
        compiler_params=_cparams(("parallel", "parallel", "parallel")), name="diff_attn")(q, k, v, lam_vec, norm_g)


def _gqa_attn_kernel(q_ref, k_ref, v_ref, o_ref, *, nctx, n_lat, tk):
    i = pl.program_id(2)
    q = q_ref[0]
    lane = lax.broadcasted_iota(I32, (TQ, LANES), 1)
    zero = jnp.zeros((TQ, LANES), q.dtype)
    parts = []
    for j in range(2):
        blk = q[:, j * LANES:(j + 1) * LANES]
        parts += [jnp.where(lane < HD, blk, zero), jnp.where(lane >= HD, blk, zero)]
    qq = jnp.concatenate(parts, axis=0)

    def attend(n):
        o = _flash(qq, k_ref, v_ref, nctx, n, tk, HD)
        o_ref[0] = jnp.concatenate([o[h * TQ:(h + 1) * TQ] for h in range(4)], axis=1).astype(o_ref.dtype)

    _ctx_or_all(i, nctx, n_lat, attend)


def gqa_attention(q, k, v, nctx):
    B, S, _ = q.shape
    tk = math.gcd(S - nctx, TK)
    kern = functools.partial(_gqa_attn_kernel, nctx=nctx, n_lat=(S - nctx) // tk, tk=tk)
    return pl.pallas_call(
        kern, out_shape=jax.ShapeDtypeStruct((B, S, 512), BF16), grid=(B, GQA_KV, S // TQ),
        in_specs=[pl.BlockSpec((1, TQ, 2 * LANES), lambda b, g, i: (b, i, g)),
                  pl.BlockSpec((1, S, LANES), lambda b, g, i: (b, 0, g)),
                  pl.BlockSpec((1, S, LANES), lambda b, g, i: (b, 0, g))],
        out_specs=pl.BlockSpec((1, TQ, 2 * LANES), lambda b, g, i: (b, i, g)),
        compiler_params=_cparams(("parallel", "parallel", "parallel")), name="gqa_attn")(q, k, v)


def _s5_kernel(uf_ref, ub_ref, bblk_ref, cblk_ref, lam_ref, yf_ref, yb_ref, buf_f, buf_b, st_ref):
    H = S5_GROUPS * S5_STATE // 2
    R = buf_f.shape[0]
    RB = 256

    @pl.when(pl.program_id(0) == 0)
    def _():
        st_ref[...] = jnp.zeros_like(st_ref)

    half0 = (lax.broadcasted_iota(I32, (RB, 1), 0) & 1) == 0

    for d, (u_ref, buf) in enumerate(((uf_ref, buf_f), (ub_ref, buf_b))):
        for rb in range(R // RB):
            rows = slice(rb * RB, (rb + 1) * RB)
            uh = u_ref[rows, :].astype(BF16)
            for cols in (slice(0, H), slice(H, 2 * H)):
                buf[rows, cols] = jnp.where(half0, _dot(uh, bblk_ref[d, 0, :, cols]), _dot(uh, bblk_ref[d, 1, :, cols]))

    lfr, lfi, lbr, lbi = lam_ref[0, 0], lam_ref[0, 1], lam_ref[1, 0], lam_ref[1, 1]

    def step(t, carry):
        fr, fi, br, bi = carry
        rf = pl.ds(pl.multiple_of(t * 8, 8), 8)
        x = buf_f[rf, :]
        nfr = lfr * fr - lfi * fi + x[:, :H]
        nfi = lfr * fi + lfi * fr + x[:, H:]
        buf_f[rf, :] = jnp.concatenate([nfr, nfi], axis=1)
        rb = pl.ds(pl.multiple_of((S5_TC - 1 - t) * 8, 8), 8)
        z = buf_b[rb, :]
        nbr = lbr * br - lbi * bi + z[:, :H]
        nbi = lbr * bi + lbi * br + z[:, H:]
        buf_b[rb, :] = jnp.concatenate([nbr, nbi], axis=1)
        return nfr, nfi, nbr, nbi

    fin = lax.fori_loop(0, S5_TC, step, (st_ref[0], st_ref[1], st_ref[2], st_ref[3]))
    for j in range(4):
        st_ref[j] = fin[j]

    for d, (y_ref, buf) in enumerate(((yf_ref, buf_f), (yb_ref, buf_b))):
        for rb in range(R // RB):
            rows = slice(rb * RB, (rb + 1) * RB)
            h = buf[rows, :].astype(BF16)
            y_ref[rows, :] = jnp.where(half0, _dot(h, cblk_ref[d, 0]), _dot(h, cblk_ref[d, 1]))


def s5_scan(u, bblk, cblk, lam, S, nctx):
    assert u.shape[0] == S * 8, "the scan packs (sample, half) pairs into the eight sublanes of a vreg"
    nch, nc0 = S // S5_TC, nctx // S5_TC

    def bwd(i):
        return jnp.where(i < nc0, nc0 - 1 - i, (nch - 1) - (i - nc0))
    H2 = S5_GROUPS * S5_STATE
    R, W = S5_TC * 8, u.shape[1]
    blk = lambda f: pl.BlockSpec((R, W), f)
    return pl.pallas_call(
        _s5_kernel, out_shape=[jax.ShapeDtypeStruct(u.shape, F32)] * 2, grid=(nch,),
        in_specs=[blk(lambda i: (i, 0)), blk(lambda i: (bwd(i), 0)),
                  _const_spec(bblk.shape), _const_spec(cblk.shape), _const_spec(lam.shape)],
        out_specs=[blk(lambda i: (i, 0)), blk(lambda i: (bwd(i), 0))],
        scratch_shapes=[pltpu.VMEM((R, H2), F32), pltpu.VMEM((R, H2), F32), pltpu.VMEM((4, 8, H2 // 2), F32)],
        compiler_params=_cparams(("arbitrary",)), name="s5_scan")(u, u, bblk, cblk, lam)


def _ml_prep_kernel(x_ref, prev_ref, next_ref, w_ref, b_ref, q_ref, k_ref, *, seg_starts, seg_ends):
    i = pl.program_id(1)
    x = x_ref[0]
    row = lax.broadcasted_iota(I32, x.shape, 0)
    first = functools.reduce(jnp.logical_or, [i == s for s in seg_starts])
    last = functools.reduce(jnp.logical_or, [i == s for s in seg_ends])
    pr = jnp.where(first, 0.0, prev_ref[0, 7:8, :])
    nx = jnp.where(last, 0.0, next_ref[0, 0:1, :])
    xp = jnp.where(row == 0, pr, pltpu.roll(x, 1, 0))
    xn = jnp.where(row == TM - 1, nx, pltpu.roll(x, TM - 1, 0))
    w = w_ref[...]
    y = b_ref[...] + xp * w[0:1, :] + x * w[1:2, :] + xn * w[2:3, :]
    y = y * _sigmoid(y)
    q_ref[0] = y[:, :512].astype(q_ref.dtype)
    k_ref[0] = (y[:, 512:] * (ML_HD ** -0.5)).astype(k_ref.dtype)


def ml_prep(qk, conv_w, conv_b, nctx):
    B, S, W = qk.shape
    nt, nct, r8 = S // TM, nctx // TM, TM // 8
    kern = functools.partial(_ml_prep_kernel, seg_starts=(0, nct), seg_ends=(nct - 1, nt - 1))
    return pl.pallas_call(
        kern, out_shape=[jax.ShapeDtypeStruct((B, S, 512), BF16)] * 2, grid=(B, nt),
        in_specs=[pl.BlockSpec((1, TM, W), lambda b, i: (b, i, 0)),
                  pl.BlockSpec((1, 8, W), lambda b, i: (b, jnp.maximum(i * r8 - 1, 0), 0)),
                  pl.BlockSpec((1, 8, W), lambda b, i: (b, jnp.minimum((i + 1) * r8, S // 8 - 1), 0)),
                  pl.BlockSpec((3, W), lambda b, i: (0, 0)), pl.BlockSpec((1, W), lambda b, i: (0, 0))],
        out_specs=[pl.BlockSpec((1, TM, 512), lambda b, i: (b, i, 0))] * 2,
        compiler_params=_cparams(("parallel", "parallel")), name="ml_prep")(qk, qk, qk, conv_w, conv_b)


def _log_sigmoid(x):
    return jnp.minimum(x, 0.0) - jnp.log1p(jnp.exp(-jnp.abs(x)))


def _mlstm_kernel(q_ref, k_ref, v_ref, g_ref, h_ref, c_sc, n_sc, m_sc, *, reverse):
    T = ML_T
    gi, gf = (8, 12) if reverse else (0, 4)

    @pl.when(pl.program_id(0) == 0)
    def _():
        c_sc[...] = jnp.zeros_like(c_sc)
        n_sc[...] = jnp.zeros_like(n_sc)
        m_sc[...] = jnp.zeros_like(m_sc)

    r = lax.broadcasted_iota(I32, (T, T), 0)
    c = lax.broadcasted_iota(I32, (T, T), 1)
    mask = (c >= r) if reverse else (c <= r)
    tri = jnp.where(mask, 1.0, 0.0)
    hp = lax.Precision.HIGHEST
    end = 0 if reverse else T - 1
    def gates(b):
        g = g_ref[b]
        gt = g.T
        bcol_all = jnp.dot(tri, _log_sigmoid(g), preferred_element_type=F32, precision=hp)
        brow_all = lax.dot_general(_log_sigmoid(gt[0:16]), tri, (((1,), (1,)), ((), ())),
                                   preferred_element_type=F32, precision=hp)
        return g, gt, bcol_all, brow_all

    def chain(b, hh, g, gt, bcol_all, brow_all):
        st = b * ML_HEADS + hh
        sl = slice(hh * ML_HD, (hh + 1) * ML_HD)
        q, k, v = q_ref[b, :, sl], k_ref[b, :, sl], v_ref[b, :, sl]
        bcol, brow = bcol_all[:, gf + hh:gf + hh + 1], brow_all[gf + hh:gf + hh + 1, :]
        icol, irow = g[:, gi + hh:gi + hh + 1], gt[gi + hh:gi + hh + 1, :]
        m_old = m_sc[st][:, 0:1]
        n_old = n_sc[st]
        c_old = c_sc[st]
        logw = jnp.where(mask, bcol - brow + irow, -jnp.inf)
        m_inter = bcol + m_old
        m_t = jnp.maximum(m_inter, jnp.max(logw, axis=1, keepdims=True))
        yield
        s = _dot_nt(q, k) * jnp.exp(logw - m_t)
        inter = jnp.exp(m_inter - m_t)
        yield
        num = _dot(s.astype(BF16), v.astype(BF16)) + inter * _dot_nt(q, c_old.astype(BF16))
        den = jnp.sum(s, axis=1, keepdims=True) + inter * jnp.sum(q.astype(F32) * n_old, axis=1, keepdims=True)
        h_ref[b, :, sl] = num / jnp.maximum(jnp.abs(den), jnp.exp(-m_t))
        yield
        b_end = bcol[end:end + 1, :]
        g_row, g_col = b_end - brow + irow, b_end - bcol + icol
        m_new = jnp.maximum(b_end + m_old, jnp.max(g_row, axis=1, keepdims=True))
        decay = jnp.exp(b_end + m_old - m_new)
        wk = jnp.exp(g_col - m_new)
        c_sc[st] = decay * c_old + _dot_tn((v * wk).astype(BF16), k)
        n_sc[st] = decay * n_old + jnp.sum(k.astype(F32) * wk, axis=0, keepdims=True)
        m_sc[st] = jnp.broadcast_to(m_new, (1, LANES))

    nb = q_ref.shape[0]
    for b0 in range(0, nb, ML_LOCKSTEP):
        chains = []
        for b in range(b0, min(b0 + ML_LOCKSTEP, nb)):
            shared = gates(b)
            chains += [chain(b, hh, *shared) for hh in range(ML_HEADS)]
        for _ in range(4):
            for ch in chains:
                next(ch, None)


def mlstm_scan(q, k, v, g, nctx, reverse):
    B, S, W = q.shape
    nch, nc0 = S // ML_T, nctx // ML_T

    def order(i):
        return jnp.where(i < nc0, nc0 - 1 - i, (nch - 1) - (i - nc0)) if reverse else i
    blk = lambda n: pl.BlockSpec((B, ML_T, n), lambda i: (0, order(i), 0))
    nst = B * ML_HEADS
    return pl.pallas_call(
        functools.partial(_mlstm_kernel, reverse=reverse),
        out_shape=jax.ShapeDtypeStruct((B, S, W), F32), grid=(nch,),
        in_specs=[blk(W), blk(W), blk(W), blk(LANES)], out_specs=blk(W),
        scratch_shapes=[pltpu.VMEM((nst, ML_HD, ML_HD), F32), pltpu.VMEM((nst, 1, ML_HD), F32),
                        pltpu.VMEM((nst, 1, LANES), F32)],
        compiler_params=_cparams(("arbitrary",)), name="mlstm_bwd" if reverse else "mlstm_fwd")(q, k, v, g)


def _merge_kernel(x_ref, mod_ref, yd_ref, yg_ref, u_ref, sf_ref, sb_ref, hf_ref, hb_ref, mo_ref,
                  s5d_ref, wglu_ref, bglu_ref, mlg_ref, wgate_ref, bgate_ref, wbr_ref, wout_ref,
                  lng_ref, lnb_ref, ones_ref, o_ref):
    x = x_ref[0]
    mod = mod_ref[0, 0]
    xm = (_ln(x) * (1.0 + mod[1:2, :]) + mod[0:1, :]).astype(BF16)
    ys = u_ref[...] * s5d_ref[...] + sf_ref[...] + sb_ref[...]
    z = _dot(_gelu(ys).astype(BF16), wglu_ref[...]) + bglu_ref[...]
    ys = z[:, :512] * _sigmoid(z[:, 512:])
    hm = hf_ref[0] + hb_ref[0]
    ss = _dot((hm * hm).astype(BF16), ones_ref[...])
    ym = hm * lax.rsqrt(ss * (1.0 / ML_HD) + LN_EPS) * mlg_ref[...] * _sigmoid(mo_ref[0])
    branches = (yd_ref[0], ys.astype(BF16), ym.astype(BF16), yg_ref[0])
    merged = None
    for j, yb in enumerate(branches):
        gate = _sigmoid(_dot(xm, wgate_ref[:, j * D_MODEL:(j + 1) * D_MODEL]) + bgate_ref[:, j * D_MODEL:(j + 1) * D_MODEL])
        term = gate * _dot(yb, wbr_ref[j])
        merged = term if merged is None else merged + term
    mix = _dot(merged.astype(BF16), wout_ref[...])
    o_ref[0] = _ln(ALPHA * x + mod[2:3, :] * mix) * lng_ref[...] + lnb_ref[...]


def merge(xa, mod, yd, yg, u, sf, sb, hf, hb, mo, s5d, wglu, bglu, mlg, wgate, bgate, wbr, wout, lng, lnb, ones_ml, nct):
    B, S, D = xa.shape
    tok = lambda n: pl.BlockSpec((1, TM, n), lambda b, i: (b, i, 0))
    consts = (s5d, wglu, bglu, mlg, wgate, bgate, wbr, wout, lng, lnb, ones_ml)
    return pl.pallas_call(
        _merge_kernel, out_shape=jax.ShapeDtypeStruct((B, S, D), F32), grid=(B, S // TM),
        in_specs=[tok(D), pl.BlockSpec((1, 1, 6, D), lambda b, i: (b, jnp.where(i >= nct, 1, 0), 0, 0))]
        + [tok(512)] * 2 + [pl.BlockSpec((TM, 512), lambda b, i: (i, b))] * 3 + [tok(512)] * 3
        + [_const_spec(a.shape) for a in consts],
        out_specs=tok(D),
        compiler_params=_cparams(("parallel", "parallel")), name="merge")(xa, mod, yd, yg, u, sf, sb, hf, hb, mo, *consts)


def _top16(*problems):
    iotas = {prob[0].shape: lax.broadcasted_iota(I32, prob[0].shape, 0).astype(F32) for prob in problems}

    def one(kk, s, val_ref, idx_ref, payload):
        R, ri = s.shape[0], iotas[s.shape]
        m = jnp.max(s, axis=0, keepdims=True)
        ix = jnp.min(jnp.where(s == m, ri, float(R)), axis=0, keepdims=True)
        hit = ri == ix
        val_ref[pl.ds(kk, 1), :] = m
        idx_ref[pl.ds(kk, 1), :] = ix if payload is None else jnp.max(jnp.where(hit, payload, -1.0), axis=0, keepdims=True)
        return jnp.where(hit, -jnp.inf, s)

    def body(kk, ss):
        return tuple(one(kk, s, *prob[1:]) for s, prob in zip(ss, problems))

    lax.fori_loop(0, PEER_TOPK, body, tuple(prob[0] for prob in problems))


_PEER_CAND_ROWS = ((0, 16),) + tuple((16 + 8 * (p - 1), 8) for p in range(1, 8))
_PEER_NCAND = _PEER_CAND_ROWS[-1][0] + 16


def _peer_route_kernel(x_ref, mod0_ref, mod1_ref, wq_ref, sk_ref, t_ref, ids_ref, gate_ref,
                       v1_sc, i1_sc, v2_sc, i2_sc, cand_sc, cidx_sc, top_sc, tid_sc):
    h = pl.program_id(1)

    @pl.when(h == 0)
    def _():
        for half, mod_ref in enumerate((mod0_ref, mod1_ref)):
            rows = slice(half * TM, (half + 1) * TM)
            mod = mod_ref[0, 0]
            t_ref[rows, :] = (_ln(x_ref[rows, :]) * (1.0 + mod[4:5, :]) + mod[3:4, :]).astype(BF16)

    q = _dot(t_ref[...], wq_ref[...]).astype(BF16)
    half = PEER_DQ // 2
    _top16((_dot_nt(sk_ref[0], q[:, :half]), v1_sc, i1_sc, None))
    _top16((_dot_nt(sk_ref[1], q[:, half:]), v2_sc, i2_sc, None))
    nk = float(PEER_NKEYS)
    for p, (lo, n) in enumerate(_PEER_CAND_ROWS):
        cand_sc[lo:lo + n, :] = v1_sc[p:p + 1, :] + v2_sc[0:n, :]
        cidx_sc[lo:lo + n, :] = i1_sc[p:p + 1, :] * nk + i2_sc[0:n, :]
    lo = _PEER_CAND_ROWS[-1][0] + _PEER_CAND_ROWS[-1][1]
    cand_sc[lo:lo + 8, :] = v1_sc[8:16, :] + v2_sc[0:1, :]
    cidx_sc[lo:lo + 8, :] = i1_sc[8:16, :] * nk + i2_sc[0:1, :]
    _top16((cand_sc[...], top_sc, tid_sc, cidx_sc[...]))
    top = top_sc[...]
    e = jnp.exp(top - top[0:1, :])
    gate_ref[...] = e / jnp.sum(e, axis=0, keepdims=True)
    ids_ref[...] = tid_sc[...].astype(I32)


def _half_tile_mod_spec(half, nt_per_sample, nct):
    def index(i, j):
        r = 2 * i + half
        return (r // nt_per_sample, jnp.where(r % nt_per_sample >= nct, 1, 0), 0, 0)
    return pl.BlockSpec((1, 1, 6, D_MODEL), index)


def peer_route(h1, mod, wq, subkeys, nt_per_sample, nct):
    T, D = h1.shape
    assert T % TM_E == 0 and TM_E == 2 * TM
    f = lambda n: pltpu.VMEM((n, TM_E), F32)
    return pl.pallas_call(
        _peer_route_kernel,
        out_shape=[jax.ShapeDtypeStruct((T, D), BF16),
                   jax.ShapeDtypeStruct((PEER_HEADS * PEER_TOPK, T), I32),
                   jax.ShapeDtypeStruct((PEER_HEADS * PEER_TOPK, T), F32)],
        grid=(T // TM_E, PEER_HEADS),
        in_specs=[pl.BlockSpec((TM_E, D), lambda i, h: (i, 0)),
                  _half_tile_mod_spec(0, nt_per_sample, nct), _half_tile_mod_spec(1, nt_per_sample, nct),
                  pl.BlockSpec((D, PEER_DQ), lambda i, h: (0, h)),
                  pl.BlockSpec((2, PEER_NKEYS, PEER_DQ // 2), lambda i, h: (0, 0, 0))],
        out_specs=[pl.BlockSpec((TM_E, D), lambda i, h: (i, 0)),
                   pl.BlockSpec((PEER_TOPK, TM_E), lambda i, h: (h, i)),
                   pl.BlockSpec((PEER_TOPK, TM_E), lambda i, h: (h, i))],
        scratch_shapes=[f(16), f(16), f(16), f(16), f(_PEER_NCAND), f(_PEER_NCAND), f(16), f(16)],
        compiler_params=_cparams(("parallel", "arbitrary")), name="peer_route")(h1, mod, mod, wq, subkeys)


def _peer_expert_kernel(t_ref, ids_ref, gate_ref, u_ref, v_ref, x_ref, mod0_ref, mod1_ref, lng_ref, lnb_ref, o_ref,
                        a_sc, wgt_sc, *, nchunk):
    j = pl.program_id(1)
    NK, CA = PEER_NKEYS, PEER_CA
    w_sc = a_sc

    @pl.when(j < nchunk)
    def _():
        a = _dot(t_ref[...], u_ref[0])
        a3 = jnp.stack([a[:, al * NK:(al + 1) * NK] for al in range(CA)], axis=0)
        a_sc[:, pl.ds(pl.multiple_of(j * CA, CA), CA), :] = pltpu.einshape("atb->tab", a3)

    @pl.when(j == nchunk - 1)
    def _():
        io_k = lax.broadcasted_iota(I32, (NK, NK), 0)
        io_a = lax.broadcasted_iota(I32, (NK, 2 * NK), 0)

        def masks(two):
            idp = ids_ref[two, :]
            ids = jnp.concatenate([idp[0:1], idp[1:2]], axis=1)
            blocks = [jnp.where(io_k == (idp[tok:tok + 1] & (NK - 1)), 1.0, 0.0).astype(BF16) for tok in range(2)]
            zero = jnp.zeros((NK, NK), BF16)
            onehot = jnp.concatenate([jnp.concatenate([blocks[0], zero], axis=1),
                                      jnp.concatenate([zero, blocks[1]], axis=1)], axis=0)
            return onehot, io_a == (ids >> 7)

        def pick(p, carry):
            two = pl.ds(pl.multiple_of(p * 2, 2), 2)
            onehot, r1 = masks(two)
            a2 = a_sc[two]
            at = jnp.concatenate([a2[0], a2[1]], axis=1).astype(BF16)
            picked = _dot(at, onehot)
            act = jnp.sum(jnp.where(r1, picked, 0.0), axis=0, keepdims=True)
            gp = gate_ref[two, :]
            w = jnp.concatenate([gp[0:1], gp[1:2]], axis=1) * _gelu(act)
            wgt_sc[two, :] = jnp.concatenate([w[:, :NK], w[:, NK:]], axis=0)
            return carry

        def scatter(p, carry):
            two = pl.ds(pl.multiple_of(p * 2, 2), 2)
            onehot, r1 = masks(two)
            wp = wgt_sc[two, :]
            w = jnp.concatenate([wp[0:1], wp[1:2]], axis=1)
            wt = _dot_nt(jnp.where(r1, w, 0.0).astype(BF16), onehot)
            w_sc[two] = jnp.stack([wt[:, :NK], wt[:, NK:]], axis=0)
            return carry

        lax.fori_loop(0, TM_E // 2, pick, 0, unroll=16)
        lax.fori_loop(0, TM_E // 2, scatter, 0, unroll=16)
        o_ref[...] = jnp.zeros_like(o_ref)

    @pl.when(j >= nchunk)
    def _():
        w3 = pltpu.einshape("tab->atb", w_sc[:, pl.ds(pl.multiple_of((j - nchunk) * CA, CA), CA), :])
        w = jnp.concatenate([w3[al] for al in range(CA)], axis=1).astype(BF16)
        o_ref[...] += _dot(w, v_ref[...])

    @pl.when(j == 2 * nchunk - 1)
    def _():
        for half, mod_ref in enumerate((mod0_ref, mod1_ref)):
            rows = slice(half * TM, (half + 1) * TM)
            y = ALPHA * x_ref[rows, :] + mod_ref[0, 0][5:6, :] * o_ref[rows, :]
            o_ref[rows, :] = _ln(y) * lng_ref[...] + lnb_ref[...]


def peer_experts(t, ids, gates, emb_u, emb_v, h1, mod, lng, lnb, nt_per_sample, nct):
    T, D = h1.shape
    E = emb_v.shape[0]
    ce = PEER_CA * PEER_NKEYS
    nchunk = E // ce
    emb_ut = emb_u.reshape(nchunk, ce, D).transpose(0, 2, 1)
    assert T % TM_E == 0 and TM_E == 2 * TM
    tok = lambda n, **kw: pl.BlockSpec((TM_E, n), lambda i, j: (i, 0), **kw)
    mod_spec = lambda half: _half_tile_mod_spec(half, nt_per_sample, nct)
    once = dict(pipeline_mode=pl.Buffered(1))
    return pl.pallas_call(
        functools.partial(_peer_expert_kernel, nchunk=nchunk),
        out_shape=jax.ShapeDtypeStruct((T, D), F32), grid=(T // TM_E, 2 * nchunk),
        in_specs=[tok(D, **once), tok(LANES, **once), tok(LANES, **once),
                  pl.BlockSpec((1, D, ce), lambda i, j: (jnp.minimum(j, nchunk - 1), 0, 0)),
                  pl.BlockSpec((ce, D), lambda i, j: (jnp.maximum(j - nchunk, 0), 0)),
                  tok(D, **once), mod_spec(0), mod_spec(1),
                  pl.BlockSpec((1, D), lambda i, j: (0, 0)), pl.BlockSpec((1, D), lambda i, j: (0, 0))],
        out_specs=tok(D),
        scratch_shapes=[pltpu.VMEM((TM_E, PEER_NKEYS, PEER_NKEYS), F32), pltpu.VMEM((TM_E, LANES), F32)],
        compiler_params=_cparams(("parallel", "arbitrary")), name="peer_experts")(t, ids, gates, emb_ut, emb_v, h1, mod, mod, lng, lnb)


_ROPE_IDX = np.arange(HD)
_ROPE_PERM = np.where(_ROPE_IDX % 32 < 16, _ROPE_IDX + 16, _ROPE_IDX - 16)
_ROPE_SIGN = np.where(_ROPE_IDX % 32 < 16, -1.0, 1.0).astype(np.float32)


def _rope_partner(w):
    n = w.shape[1] // HD
    perm = np.concatenate([h * HD + _ROPE_PERM for h in range(n)])
    return w[:, perm] * jnp.asarray(np.tile(_ROPE_SIGN, n))


def _rope_tables(L, nctx, gq, gk):
    rows = L // GRID_W
    row = jnp.repeat(jnp.arange(rows, dtype=F32), GRID_W)
    col = jnp.tile(jnp.arange(GRID_W, dtype=F32), rows)
    nf = HD // 4
    inv = ROPE_THETA ** (-jnp.arange(nf, dtype=F32) / nf)
    ar, ac = row[:, None] * inv, col[:, None] * inv
    cos = jnp.concatenate([jnp.cos(ar), jnp.cos(ar), jnp.cos(ac), jnp.cos(ac)], axis=1)
    sin = jnp.concatenate([jnp.sin(ar), jnp.sin(ar), jnp.sin(ac), jnp.sin(ac)], axis=1)
    cos = jnp.concatenate([jnp.ones((nctx, HD), F32), cos], axis=0)
    sin = jnp.concatenate([jnp.zeros((nctx, HD), F32), sin], axis=0)
    scale = HD ** -0.5
    one = jnp.ones((HD,), F32)
    tabs = []
    for g, sc in ((one, scale), (one, 1.0), (gq, scale), (gk, 1.0)):
        tabs += [cos * (g * sc), sin * (g[_ROPE_PERM] * sc)]
    return jnp.tile(jnp.stack(tabs), (1, 1, 2))


def _s5_tables(a_re, a_im, log_dt, b_re, b_im, c_re, c_im, nb):
    dt = jnp.exp(log_dt)[..., None]
    mag = jnp.exp(a_re * dt)
    lr, li = mag * jnp.cos(a_im * dt), mag * jnp.sin(a_im * dt)
    den = a_re * a_re + a_im * a_im
    cr = ((lr - 1) * a_re + li * a_im) / den
    ci = (li * a_re - (lr - 1) * a_im) / den
    br = cr[..., None] * b_re - ci[..., None] * b_im
    bi = cr[..., None] * b_im + ci[..., None] * b_re
    G2, N, C = S5_GROUPS // 2, S5_STATE, S5_GROUP
    eye = jnp.eye(G2, dtype=F32)

    def blockdiag_in(m):
        m = m.reshape(2, 2, G2, N, C)
        return jnp.einsum('dhgnc,gk->dhgckn', m, eye).reshape(2, 2, G2 * C, G2 * N)

    def blockdiag_out(m):
        m = m.reshape(2, 2, G2, C, N)
        return jnp.einsum('dhgcn,gk->dhgnkc', m, eye).reshape(2, 2, G2 * N, G2 * C)

    bblk = jnp.concatenate([blockdiag_in(br), blockdiag_in(bi)], axis=-1).astype(BF16)
    cblk = jnp.concatenate([blockdiag_out(c_re), blockdiag_out(-c_im)], axis=-2).astype(BF16)
    lam = jnp.stack([lr, li], axis=1).reshape(2, 2, 2, G2 * N)
    lam = jnp.tile(lam[:, :, None], (1, 1, nb, 1, 1)).reshape(2, 2, 2 * nb, G2 * N)
    return bblk, cblk, lam


def _blockdiag_ones(group, n=LANES):
    i = np.arange(n) // group
    return jnp.asarray((i[:, None] == i[None, :]).astype(np.float32), dtype=BF16)


def kernel(x, c, ctx, c_ctx, ada_w, ada_b, w_in, b_gate, diff_lam, diff_norm_g, gqa_qnorm_g, gqa_knorm_g,
           s5_a_re, s5_a_im, s5_log_dt, s5_b_re, s5_b_im, s5_c_re, s5_c_im, s5_d, s5_w_glu, s5_b_glu,
           ml_conv_w, ml_conv_b, ml_gate_b, ml_norm_g, w_branch, w_out, ln_mix_g, ln_mix_b, ln_ffn_g, ln_ffn_b,
           peer_wq, peer_subkeys, peer_u, peer_v):
    B, L, D = x.shape
    nctx = ctx.shape[1]
    S = nctx + L
    assert D == D_MODEL and nctx % TM == 0 and L % TM == 0 and L % GRID_W == 0
    nt, nct = S // TM, nctx // TM
    depth = ada_w.shape[0]

    h = jnp.concatenate([ctx, x], axis=1)
    R = -(-(B + 1) // 8) * 8
    cond = jnp.zeros((R, D), F32).at[:B].set(c).at[B].set(c_ctx)
    ones64, ones128 = _blockdiag_ones(HD), _blockdiag_ones(ML_HD, ML_HEADS * ML_HD)
    o = IN_OFFS

    for l in range(depth):
        lam_init = 0.8 - 0.6 * math.exp(-0.3 * l)
        m = ada_modulation(cond, ada_w[l].astype(BF16), ada_b[l][None, :])
        mod = jnp.stack([jnp.broadcast_to(m[B], (B, 6 * D)), m[:B]], axis=1).reshape(B, 2, 6, D)

        w = w_in[l]
        seg = lambda i: w[:, o[i]:o[i + 1]]
        dup = lambda t: jnp.concatenate([t[:, :HD], t[:, :HD], t[:, HD:], t[:, HD:]], axis=1)
        def with_ones_slots(t, width):
            heads = t.reshape(D, -1, width)
            return jnp.concatenate([heads, jnp.zeros_like(heads)], axis=2).reshape(D, -1)
        w_a = jnp.concatenate([seg(0), _rope_partner(seg(0)), seg(1), _rope_partner(seg(1)), seg(2),
                               seg(9), _rope_partner(seg(9)), dup(seg(10)), dup(_rope_partner(seg(10))),
                               with_ones_slots(seg(11), HD)], axis=1).astype(BF16)
        w_b = jnp.concatenate([seg(3), seg(4), seg(5), seg(6), seg(7), seg(8),
                               jnp.zeros((D, LANES - 16), F32)], axis=1).astype(BF16)
        gate_b = jnp.concatenate([ml_gate_b[l], jnp.zeros((LANES - 16,), F32)])[None, :]
        tab = _rope_tables(L, nctx, gqa_qnorm_g[l], gqa_knorm_g[l])

        dq, dk, dv, gq, gk, gv = proj_attn(h, mod, w_a, tab, ones64, nct)
        u, mqk, mv, mo, mg = proj_seq(h, mod, w_b, gate_b, nct)

        yd = diff_attention(dq, dk, dv, diff_lam[l], diff_norm_g[l][None, :], nctx, lam_init)
        yg = gqa_attention(gq, gk, gv, nctx)

        bblk, cblk, lam = _s5_tables(s5_a_re[l], s5_a_im[l], s5_log_dt[l], s5_b_re[l], s5_b_im[l],
                                     s5_c_re[l], s5_c_im[l], B)
        sf, sb = (y.reshape(u.shape) for y in s5_scan(u.reshape(S * 2 * B, 256), bblk, cblk, lam, S, nctx))

        mq, mk = ml_prep(mqk, ml_conv_w[l], ml_conv_b[l][None, :], nctx)
        hf = mlstm_scan(mq, mk, mv, mg, nctx, False)
        hb = mlstm_scan(mq, mk, mv, mg, nctx, True)

        h1 = merge(h, mod, yd, yg, u, sf, sb, hf, hb, mo,
                   s5_d[l][None, :], s5_w_glu[l].astype(BF16), s5_b_glu[l][None, :], ml_norm_g[l][None, :],
                   seg(12).astype(BF16), b_gate[l][None, :], w_branch[l].astype(BF16), w_out[l].astype(BF16),
                   ln_mix_g[l][None, :], ln_mix_b[l][None, :], ones128, nct)

        h1 = h1.reshape(B * S, D)
        t, ids, gates = peer_route(h1, mod, peer_wq[l].astype(BF16), peer_subkeys[l].astype(BF16), nt, nct)
        h = peer_experts(t, ids.T, gates.T, peer_u[l].astype(BF16), peer_v[l].astype(BF16),
                         h1, mod, ln_ffn_g[l][None, :], ln_ffn_b[l][None, :], nt, nct).reshape(B, S, D)
    return h[:, nctx:]
```

```python
import functools
import math

import numpy as np
import jax
import jax.numpy as jnp
from jax import lax
from jax.experimental import pallas as pl
from jax.experimental.pallas import tpu as pltpu

F32 = jnp.float32
BF16 = jnp.bfloat16
I32 = jnp.int32

D_MODEL = 1024
DEPTH = 2
GRID_W = 64
ROPE_THETA = 10000.0
LN_EPS = 1e-6
HD = 64
DIFF_HEADS = 4
DIFF_HPS = 2
GQA_HEADS = 8
GQA_KV = 2
S5_GROUP = 16
S5_GROUPS = 32
S5_STATE = 64
ML_HEADS = 4
ML_HD = 128
N_BRANCH = 4
W_BRANCH = 512
PEER_HEADS = 8
PEER_NKEYS = 128
PEER_TOPK = 16
PEER_DQ = 256
ALPHA = (2 * DEPTH) ** 0.25

LANES = 128
TM = 256
TM_E = 2 * TM
TQ = 256
TK = 2048
S5_TC = 128
ML_T = 256
ML_LOCKSTEP = 4
PEER_CA = 8
VMEM_LIMIT = 56 * 1024 * 1024

IN_SPLITS = (512, 512, 512, 512, 512, 512, 512, 512, 16, 512, 128, 128, N_BRANCH * D_MODEL)
IN_OFFS = tuple(int(v) for v in np.cumsum((0,) + IN_SPLITS))


def _cparams(sem):
    return pltpu.CompilerParams(dimension_semantics=sem, vmem_limit_bytes=VMEM_LIMIT)


def _const_spec(shape):
    nd = len(shape)
    return pl.BlockSpec(shape, lambda *_: (0,) * nd, pipeline_mode=pl.Buffered(1))


def _ln(x):
    xc = x - jnp.mean(x, axis=-1, keepdims=True)
    return xc * lax.rsqrt(jnp.mean(xc * xc, axis=-1, keepdims=True) + LN_EPS)


def _sigmoid(x):
    return 1.0 / (1.0 + jnp.exp(-x))


def _gelu(x):
    return 0.5 * x * (1.0 + lax.erf(x * (2.0 ** -0.5)))


def _dot(a, b):
    return jnp.dot(a, b, preferred_element_type=F32)


def _dot_nt(a, b):
    return lax.dot_general(a, b, (((1,), (1,)), ((), ())), preferred_element_type=F32)


def _dot_tn(a, b):
    return lax.dot_general(a, b, (((0,), (0,)), ((), ())), preferred_element_type=F32)


def _ada_kernel(c_ref, w_ref, b_ref, o_ref):
    c = c_ref[...]
    o_ref[...] = _dot((c * _sigmoid(c)).astype(BF16), w_ref[...]) + b_ref[...]


def ada_modulation(cond, w, b):
    R, D = cond.shape
    N = w.shape[1]
    tn = 1536
    return pl.pallas_call(
        _ada_kernel, out_shape=jax.ShapeDtypeStruct((R, N), F32), grid=(N // tn,),
        in_specs=[pl.BlockSpec((R, D), lambda j: (0, 0)), pl.BlockSpec((D, tn), lambda j: (0, j)),
                  pl.BlockSpec((1, tn), lambda j: (0, j))],
        out_specs=pl.BlockSpec((R, tn), lambda j: (0, j)),
        compiler_params=_cparams(("arbitrary",)), name="ada")(cond, w, b)


def _proj_attn_kernel(x_ref, mod_ref, w_ref, tab_ref, ones_ref,
                      dq_ref, dk_ref, dv_ref, gq_ref, gk_ref, gv_ref):
    mod = mod_ref[0, 0]
    xm = (_ln(x_ref[0]) * (1.0 + mod[1:2, :]) + mod[0:1, :]).astype(BF16)

    def mm(lo, n):
        return _dot(xm, w_ref[:, lo:lo + n])

    def rope_store(ref, lo, n, ci, norm):
        t, tp = mm(lo, n), mm(lo + n, n)
        c, s = tab_ref[ci], tab_ref[ci + 1]
        for j in range(n // LANES):
            sl = slice(j * LANES, (j + 1) * LANES)
            tb = t[:, sl]
            y = tb * c + tp[:, sl] * s
            if norm:
                ss = _dot((tb * tb).astype(BF16), ones_ref[...])
                y = y * lax.rsqrt(ss * (1.0 / HD) + LN_EPS)
            ref[0, :, sl] = y.astype(ref.dtype)

    def value_store(ref, lo, n, width):
        lane = lax.broadcasted_iota(I32, (1, n), 1)
        ref[0] = (mm(lo, n) + jnp.where(lane % (2 * width) >= width, 1.0, 0.0)).astype(ref.dtype)

    rope_store(dq_ref, 0, 512, 0, False)
    rope_store(dk_ref, 1024, 512, 2, False)
    dv_ref[0] = mm(2048, 512).astype(dv_ref.dtype)
    rope_store(gq_ref, 2560, 512, 4, True)
    rope_store(gk_ref, 3584, 256, 6, True)
    value_store(gv_ref, 4096, 256, HD)


def proj_attn(xa, mod, w_a, tab, ones_bd, nct):
    B, S, D = xa.shape
    tok = lambda n: pl.BlockSpec((1, TM, n), lambda b, i: (b, i, 0))
    widths = (512, 512, 512, 512, 256, 256)
    outs = [jax.ShapeDtypeStruct((B, S, n), BF16) for n in widths]
    return pl.pallas_call(
        _proj_attn_kernel, out_shape=outs, grid=(B, S // TM),
        in_specs=[tok(D),
                  pl.BlockSpec((1, 1, 6, D), lambda b, i: (b, jnp.where(i >= nct, 1, 0), 0, 0)),
                  _const_spec(w_a.shape),
                  pl.BlockSpec((8, TM, LANES), lambda b, i: (0, i, 0)),
                  _const_spec(ones_bd.shape)],
        out_specs=[tok(n) for n in widths],
        compiler_params=_cparams(("parallel", "parallel")), name="proj_attn")(xa, mod, w_a, tab, ones_bd)


def _proj_seq_kernel(x_ref, mod_ref, w_ref, gb_ref, u_ref, qk_ref, v_ref, o_ref, g_ref):
    mod = mod_ref[0, 0]
    xm = (_ln(x_ref[0]) * (1.0 + mod[1:2, :]) + mod[0:1, :]).astype(BF16)
    u_ref[...] = _dot(xm, w_ref[:, 0:512])
    qk_ref[0] = _dot(xm, w_ref[:, 512:1536])
    v_ref[0] = _dot(xm, w_ref[:, 1536:2048])
    o_ref[0] = _dot(xm, w_ref[:, 2048:2560])
    g_ref[0] = _dot(xm, w_ref[:, 2560:2688]) + gb_ref[...]


def proj_seq(xa, mod, w_b, gate_b, nct):
    B, S, D = xa.shape
    tok = lambda n: pl.BlockSpec((1, TM, n), lambda b, i: (b, i, 0))
    widths = (1024, 512, 512, LANES)
    return pl.pallas_call(
        _proj_seq_kernel,
        out_shape=[jax.ShapeDtypeStruct((S, B * 512), F32)] + [jax.ShapeDtypeStruct((B, S, n), F32) for n in widths],
        grid=(B, S // TM),
        in_specs=[tok(D),
                  pl.BlockSpec((1, 1, 6, D), lambda b, i: (b, jnp.where(i >= nct, 1, 0), 0, 0)),
                  _const_spec(w_b.shape), _const_spec(gate_b.shape)],
        out_specs=[pl.BlockSpec((TM, 512), lambda b, i: (i, b))] + [tok(n) for n in widths],
        compiler_params=_cparams(("parallel", "parallel")), name="proj_seq")(xa, mod, w_b, gate_b)


def _flash(qqs, k_ref, v_ref, nctx, n_lat, tk, dv):
    P = len(qqs)
    R, W = qqs[0].shape[0], v_ref.shape[2] // P
    mxu_sums = dv < W

    def finish(carry, s, v):
        m, l, acc = carry
        m_new = jnp.maximum(m, jnp.max(s, axis=1, keepdims=True))
        alpha = jnp.exp(m - m_new)
        if mxu_sums:
            p = jnp.exp((s - m_new).astype(BF16))
        else:
            p = jnp.exp(s - m_new)
            l = alpha * l + jnp.sum(p, axis=1, keepdims=True)
        return m_new, l, alpha * acc + _dot(p.astype(BF16), v)

    carries = [(jnp.full((R, 1), -jnp.inf, F32), jnp.zeros((R, 1), F32), jnp.zeros((R, W), F32))] * P
    chunks = [pl.ds(0, nctx)] + [pl.ds(nctx + c * tk, tk) for c in range(n_lat)]
    for rows in chunks:
        scores = [_dot_nt(qqs[p], k_ref[0, rows, p * LANES:(p + 1) * LANES]) for p in range(P)]
        carries = [finish(carries[p], scores[p], v_ref[0, rows, p * W:(p + 1) * W]) for p in range(P)]
    return [acc[:, :dv] / (acc[:, dv:dv + 1] if mxu_sums else l) for _, l, acc in carries]


def _ctx_or_all(i, nctx, n_lat, attend):
    @pl.when(i < nctx // TQ)
    def _():
        attend(0)

    @pl.when(i >= nctx // TQ)
    def _():
        attend(n_lat)


def _diff_attn_kernel(q_ref, k_ref, v_ref, lam_ref, g_ref, o_ref, *, nctx, n_lat, tk, lam_init):
    i = pl.program_id(2)
    lane = lax.broadcasted_iota(I32, (TQ, LANES), 1)
    zero = jnp.zeros((TQ, LANES), q_ref.dtype)
    qqs = []
    for h in range(DIFF_HPS):
        q = q_ref[0, :, h * LANES:(h + 1) * LANES]
        qqs.append(jnp.concatenate([jnp.where(lane < HD, q, zero), jnp.where(lane >= HD, q, zero)], axis=0))
    lv = lam_ref[...]
    lam = (jnp.exp(jnp.sum(lv[0:1] * lv[1:2], axis=1, keepdims=True))
           - jnp.exp(jnp.sum(lv[2:3] * lv[3:4], axis=1, keepdims=True)) + lam_init)

    def attend(n):
        for h, o in enumerate(_flash(qqs, k_ref, v_ref, nctx, n, tk, 2 * HD)):
            d = o[:TQ] - lam * o[TQ:]
            y = d * lax.rsqrt(jnp.mean(d * d, axis=-1, keepdims=True) + LN_EPS)
            o_ref[0, :, h * LANES:(h + 1) * LANES] = (y * g_ref[...] * (1.0 - lam_init)).astype(o_ref.dtype)

    _ctx_or_all(i, nctx, n_lat, attend)


def diff_attention(q, k, v, lam_vec, norm_g, nctx, lam_init):
    B, S, _ = q.shape
    tk = math.gcd(S - nctx, TK)
    kern = functools.partial(_diff_attn_kernel, nctx=nctx, n_lat=(S - nctx) // tk, tk=tk, lam_init=lam_init)
    return pl.pallas_call(
        kern, out_shape=jax.ShapeDtypeStruct((B, S, 512), BF16), grid=(B, DIFF_HEADS // DIFF_HPS, S // TQ),
        in_specs=[pl.BlockSpec((1, TQ, DIFF_HPS * LANES), lambda b, h, i: (b, i, h)),
                  pl.BlockSpec((1, S, DIFF_HPS * LANES), lambda b, h, i: (b, 0, h)),
                  pl.BlockSpec((1, S, DIFF_HPS * LANES), lambda b, h, i: (b, 0, h)),
                  pl.BlockSpec((4, HD), lambda b, h, i: (0, 0)),
                  pl.BlockSpec((1, LANES), lambda b, h, i: (0, 0))],
        out_specs=pl.BlockSpec((1, TQ, DIFF_HPS * LANES), lambda b, h, i: (b, i, h)),
        compiler_params=_cparams(("parallel", "parallel", "parallel")), name="diff_attn")(q, k, v, lam_vec, norm_g)


def _gqa_attn_kernel(q_ref, k_ref, v_ref, o_ref, *, nctx, n_lat, tk):
    i = pl.program_id(2)
    q = q_ref[0]
    lane = lax.broadcasted_iota(I32, (TQ, LANES), 1)
    zero = jnp.zeros((TQ, LANES), q.dtype)
    parts = []
    for j in range(2):
        blk = q[:, j * LANES:(j + 1) * LANES]
        parts += [jnp.where(lane < HD, blk, zero), jnp.where(lane >= HD, blk, zero)]
    qq = jnp.concatenate(parts, axis=0)

    def attend(n):
        o, = _flash([qq], k_ref, v_ref, nctx, n, tk, HD)
        o_ref[0] = jnp.concatenate([o[h * TQ:(h + 1) * TQ] for h in range(4)], axis=1).astype(o_ref.dtype)

    _ctx_or_all(i, nctx, n_lat, attend)


def gqa_attention(q, k, v, nctx):
    B, S, _ = q.shape
    tk = math.gcd(S - nctx, TK)
    kern = functools.partial(_gqa_attn_kernel, nctx=nctx, n_lat=(S - nctx) // tk, tk=tk)
    return pl.pallas_call(
        kern, out_shape=jax.ShapeDtypeStruct((B, S, 512), BF16), grid=(B, GQA_KV, S // TQ),
        in_specs=[pl.BlockSpec((1, TQ, 2 * LANES), lambda b, g, i: (b, i, g)),
                  pl.BlockSpec((1, S, LANES), lambda b, g, i: (b, 0, g)),
                  pl.BlockSpec((1, S, LANES), lambda b, g, i: (b, 0, g))],
        out_specs=pl.BlockSpec((1, TQ, 2 * LANES), lambda b, g, i: (b, i, g)),
        compiler_params=_cparams(("parallel", "parallel", "parallel")), name="gqa_attn")(q, k, v)


def _s5_kernel(uf_ref, ub_ref, bblk_ref, cblk_ref, lam_ref, yf_ref, yb_ref, buf_f, buf_b, st_ref):
    H = S5_GROUPS * S5_STATE // 2
    R = buf_f.shape[0]
    RB = 256

    @pl.when(pl.program_id(0) == 0)
    def _():
        st_ref[...] = jnp.zeros_like(st_ref)

    half0 = (lax.broadcasted_iota(I32, (RB, 1), 0) & 1) == 0

    for d, (u_ref, buf) in enumerate(((uf_ref, buf_f), (ub_ref, buf_b))):
        for rb in range(R // RB):
            rows = slice(rb * RB, (rb + 1) * RB)
            uh = u_ref[rows, :].astype(BF16)
            for cols in (slice(0, H), slice(H, 2 * H)):
                buf[rows, cols] = jnp.where(half0, _dot(uh, bblk_ref[d, 0, :, cols]), _dot(uh, bblk_ref[d, 1, :, cols]))

    lfr, lfi, lbr, lbi = lam_ref[0, 0], lam_ref[0, 1], lam_ref[1, 0], lam_ref[1, 1]

    def step(t, carry):
        fr, fi, br, bi = carry
        rf = pl.ds(pl.multiple_of(t * 8, 8), 8)
        x = buf_f[rf, :]
        nfr = lfr * fr - lfi * fi + x[:, :H]
        nfi = lfr * fi + lfi * fr + x[:, H:]
        buf_f[rf, :] = jnp.concatenate([nfr, nfi], axis=1)
        rb = pl.ds(pl.multiple_of((S5_TC - 1 - t) * 8, 8), 8)
        z = buf_b[rb, :]
        nbr = lbr * br - lbi * bi + z[:, :H]
        nbi = lbr * bi + lbi * br + z[:, H:]
        buf_b[rb, :] = jnp.concatenate([nbr, nbi], axis=1)
        return nfr, nfi, nbr, nbi

    fin = lax.fori_loop(0, S5_TC, step, (st_ref[0], st_ref[1], st_ref[2], st_ref[3]))
    for j in range(4):
        st_ref[j] = fin[j]

    for d, (y_ref, buf) in enumerate(((yf_ref, buf_f), (yb_ref, buf_b))):
        for rb in range(R // RB):
            rows = slice(rb * RB, (rb + 1) * RB)
            h = buf[rows, :].astype(BF16)
            y_ref[rows, :] = jnp.where(half0, _dot(h, cblk_ref[d, 0]), _dot(h, cblk_ref[d, 1]))


def s5_scan(u, bblk, cblk, lam, S, nctx):
    assert u.shape[0] == S * 8, "the scan packs (sample, half) pairs into the eight sublanes of a vreg"
    nch, nc0 = S // S5_TC, nctx // S5_TC

    def bwd(i):
        return jnp.where(i < nc0, nc0 - 1 - i, (nch - 1) - (i - nc0))
    H2 = S5_GROUPS * S5_STATE
    R, W = S5_TC * 8, u.shape[1]
    blk = lambda f: pl.BlockSpec((R, W), f)
    return pl.pallas_call(
        _s5_kernel, out_shape=[jax.ShapeDtypeStruct(u.shape, F32)] * 2, grid=(nch,),
        in_specs=[blk(lambda i: (i, 0)), blk(lambda i: (bwd(i), 0)),
                  _const_spec(bblk.shape), _const_spec(cblk.shape), _const_spec(lam.shape)],
        out_specs=[blk(lambda i: (i, 0)), blk(lambda i: (bwd(i), 0))],
        scratch_shapes=[pltpu.VMEM((R, H2), F32), pltpu.VMEM((R, H2), F32), pltpu.VMEM((4, 8, H2 // 2), F32)],
        compiler_params=_cparams(("arbitrary",)), name="s5_scan")(u, u, bblk, cblk, lam)


def _ml_prep_kernel(x_ref, prev_ref, next_ref, w_ref, b_ref, q_ref, k_ref, *, seg_starts, seg_ends):
    i = pl.program_id(1)
    x = x_ref[0]
    row = lax.broadcasted_iota(I32, x.shape, 0)
    first = functools.reduce(jnp.logical_or, [i == s for s in seg_starts])
    last = functools.reduce(jnp.logical_or, [i == s for s in seg_ends])
    pr = jnp.where(first, 0.0, prev_ref[0, 7:8, :])
    nx = jnp.where(last, 0.0, next_ref[0, 0:1, :])
    xp = jnp.where(row == 0, pr, pltpu.roll(x, 1, 0))
    xn = jnp.where(row == TM - 1, nx, pltpu.roll(x, TM - 1, 0))
    w = w_ref[...]
    y = b_ref[...] + xp * w[0:1, :] + x * w[1:2, :] + xn * w[2:3, :]
    y = y * _sigmoid(y)
    q_ref[0] = y[:, :512].astype(q_ref.dtype)
    k_ref[0] = (y[:, 512:] * (ML_HD ** -0.5)).astype(k_ref.dtype)


def ml_prep(qk, conv_w, conv_b, nctx):
    B, S, W = qk.shape
    nt, nct, r8 = S // TM, nctx // TM, TM // 8
    kern = functools.partial(_ml_prep_kernel, seg_starts=(0, nct), seg_ends=(nct - 1, nt - 1))
    return pl.pallas_call(
        kern, out_shape=[jax.ShapeDtypeStruct((B, S, 512), BF16)] * 2, grid=(B, nt),
        in_specs=[pl.BlockSpec((1, TM, W), lambda b, i: (b, i, 0)),
                  pl.BlockSpec((1, 8, W), lambda b, i: (b, jnp.maximum(i * r8 - 1, 0), 0)),
                  pl.BlockSpec((1, 8, W), lambda b, i: (b, jnp.minimum((i + 1) * r8, S // 8 - 1), 0)),
                  pl.BlockSpec((3, W), lambda b, i: (0, 0)), pl.BlockSpec((1, W), lambda b, i: (0, 0))],
        out_specs=[pl.BlockSpec((1, TM, 512), lambda b, i: (b, i, 0))] * 2,
        compiler_params=_cparams(("parallel", "parallel")), name="ml_prep")(qk, qk, qk, conv_w, conv_b)


def _log_sigmoid(x):
    return jnp.minimum(x, 0.0) - jnp.log1p(jnp.exp(-jnp.abs(x)))


def _mlstm_kernel(q_ref, k_ref, v_ref, g_ref, h_ref, c_sc, n_sc, m_sc, *, reverse):
    T = ML_T
    gi, gf = (8, 12) if reverse else (0, 4)

    @pl.when(pl.program_id(0) == 0)
    def _():
        c_sc[...] = jnp.zeros_like(c_sc)
        n_sc[...] = jnp.zeros_like(n_sc)
        m_sc[...] = jnp.zeros_like(m_sc)

    r = lax.broadcasted_iota(I32, (T, T), 0)
    c = lax.broadcasted_iota(I32, (T, T), 1)
    mask = (c >= r) if reverse else (c <= r)
    tri = jnp.where(mask, 1.0, 0.0)
    hp = lax.Precision.HIGHEST
    end = 0 if reverse else T - 1
    def gates(b):
        g = g_ref[b]
        gt = g.T
        bcol_all = jnp.dot(tri, _log_sigmoid(g), preferred_element_type=F32, precision=hp)
        brow_all = lax.dot_general(_log_sigmoid(gt[0:16]), tri, (((1,), (1,)), ((), ())),
                                   preferred_element_type=F32, precision=hp)
        return g, gt, bcol_all, brow_all

    def chain(b, hh, g, gt, bcol_all, brow_all):
        st = b * ML_HEADS + hh
        sl = slice(hh * ML_HD, (hh + 1) * ML_HD)
        q, k, v = q_ref[b, :, sl], k_ref[b, :, sl], v_ref[b, :, sl]
        bcol, brow = bcol_all[:, gf + hh:gf + hh + 1], brow_all[gf + hh:gf + hh + 1, :]
        icol, irow = g[:, gi + hh:gi + hh + 1], gt[gi + hh:gi + hh + 1, :]
        m_old = m_sc[st][:, 0:1]
        n_old = n_sc[st]
        c_old = c_sc[st]
        logw = jnp.where(mask, bcol - brow + irow, -jnp.inf)
        m_inter = bcol + m_old
        m_t = jnp.maximum(m_inter, jnp.max(logw, axis=1, keepdims=True))
        yield
        s = _dot_nt(q, k) * jnp.exp(logw - m_t)
        inter = jnp.exp(m_inter - m_t)
        yield
        num = _dot(s.astype(BF16), v.astype(BF16)) + inter * _dot_nt(q, c_old.astype(BF16))
        den = jnp.sum(s, axis=1, keepdims=True) + inter * jnp.sum(q.astype(F32) * n_old, axis=1, keepdims=True)
        h_ref[b, :, sl] = num / jnp.maximum(jnp.abs(den), jnp.exp(-m_t))
        yield
        b_end = bcol[end:end + 1, :]
        g_row, g_col = b_end - brow + irow, b_end - bcol + icol
        m_new = jnp.maximum(b_end + m_old, jnp.max(g_row, axis=1, keepdims=True))
        decay = jnp.exp(b_end + m_old - m_new)
        wk = jnp.exp(g_col - m_new)
        c_sc[st] = decay * c_old + _dot_tn((v * wk).astype(BF16), k)
        n_sc[st] = decay * n_old + jnp.sum(k.astype(F32) * wk, axis=0, keepdims=True)
        m_sc[st] = jnp.broadcast_to(m_new, (1, LANES))

    nb = q_ref.shape[0]
    for b0 in range(0, nb, ML_LOCKSTEP):
        chains = []
        for b in range(b0, min(b0 + ML_LOCKSTEP, nb)):
            shared = gates(b)
            chains += [chain(b, hh, *shared) for hh in range(ML_HEADS)]
        for _ in range(4):
            for ch in chains:
                next(ch, None)


def mlstm_scan(q, k, v, g, nctx, reverse):
    B, S, W = q.shape
    nch, nc0 = S // ML_T, nctx // ML_T

    def order(i):
        return jnp.where(i < nc0, nc0 - 1 - i, (nch - 1) - (i - nc0)) if reverse else i
    blk = lambda n: pl.BlockSpec((B, ML_T, n), lambda i: (0, order(i), 0))
    nst = B * ML_HEADS
    return pl.pallas_call(
        functools.partial(_mlstm_kernel, reverse=reverse),
        out_shape=jax.ShapeDtypeStruct((B, S, W), F32), grid=(nch,),
        in_specs=[blk(W), blk(W), blk(W), blk(LANES)], out_specs=blk(W),
        scratch_shapes=[pltpu.VMEM((nst, ML_HD, ML_HD), F32), pltpu.VMEM((nst, 1, ML_HD), F32),
                        pltpu.VMEM((nst, 1, LANES), F32)],
        compiler_params=_cparams(("arbitrary",)), name="mlstm_bwd" if reverse else "mlstm_fwd")(q, k, v, g)


def _merge_kernel(x_ref, mod_ref, yd_ref, yg_ref, u_ref, sf_ref, sb_ref, hf_ref, hb_ref, mo_ref,
                  s5d_ref, wglu_ref, bglu_ref, mlg_ref, wgate_ref, bgate_ref, wbr_ref, wout_ref,
                  lng_ref, lnb_ref, ones_ref, o_ref):
    x = x_ref[0]
    mod = mod_ref[0, 0]
    xm = (_ln(x) * (1.0 + mod[1:2, :]) + mod[0:1, :]).astype(BF16)
    ys = u_ref[...] * s5d_ref[...] + sf_ref[...] + sb_ref[...]
    z = _dot(_gelu(ys).astype(BF16), wglu_ref[...]) + bglu_ref[...]
    ys = z[:, :512] * _sigmoid(z[:, 512:])
    hm = hf_ref[0] + hb_ref[0]
    ss = _dot((hm * hm).astype(BF16), ones_ref[...])
    ym = hm * lax.rsqrt(ss * (1.0 / ML_HD) + LN_EPS) * mlg_ref[...] * _sigmoid(mo_ref[0])
    branches = (yd_ref[0], ys.astype(BF16), ym.astype(BF16), yg_ref[0])
    merged = None
    for j, yb in enumerate(branches):
        gate = _sigmoid(_dot(xm, wgate_ref[:, j * D_MODEL:(j + 1) * D_MODEL]) + bgate_ref[:, j * D_MODEL:(j + 1) * D_MODEL])
        term = gate * _dot(yb, wbr_ref[j])
        merged = term if merged is None else merged + term
    mix = _dot(merged.astype(BF16), wout_ref[...])
    o_ref[0] = _ln(ALPHA * x + mod[2:3, :] * mix) * lng_ref[...] + lnb_ref[...]


def merge(xa, mod, yd, yg, u, sf, sb, hf, hb, mo, s5d, wglu, bglu, mlg, wgate, bgate, wbr, wout, lng, lnb, ones_ml, nct):
    B, S, D = xa.shape
    tok = lambda n: pl.BlockSpec((1, TM, n), lambda b, i: (b, i, 0))
    consts = (s5d, wglu, bglu, mlg, wgate, bgate, wbr, wout, lng, lnb, ones_ml)
    return pl.pallas_call(
        _merge_kernel, out_shape=jax.ShapeDtypeStruct((B, S, D), F32), grid=(B, S // TM),
        in_specs=[tok(D), pl.BlockSpec((1, 1, 6, D), lambda b, i: (b, jnp.where(i >= nct, 1, 0), 0, 0))]
        + [tok(512)] * 2 + [pl.BlockSpec((TM, 512), lambda b, i: (i, b))] * 3 + [tok(512)] * 3
        + [_const_spec(a.shape) for a in consts],
        out_specs=tok(D),
        compiler_params=_cparams(("parallel", "parallel")), name="merge")(xa, mod, yd, yg, u, sf, sb, hf, hb, mo, *consts)


def _top16(*problems):
    iotas = {prob[0].shape: lax.broadcasted_iota(I32, prob[0].shape, 0).astype(F32) for prob in problems}

    def one(kk, s, val_ref, idx_ref, payload):
        R, ri = s.shape[0], iotas[s.shape]
        m = jnp.max(s, axis=0, keepdims=True)
        ix = jnp.min(jnp.where(s == m, ri, float(R)), axis=0, keepdims=True)
        hit = ri == ix
        val_ref[pl.ds(kk, 1), :] = m
        idx_ref[pl.ds(kk, 1), :] = ix if payload is None else jnp.max(jnp.where(hit, payload, -1.0), axis=0, keepdims=True)
        return jnp.where(hit, -jnp.inf, s)

    def body(kk, ss):
        return tuple(one(kk, s, *prob[1:]) for s, prob in zip(ss, problems))

    lax.fori_loop(0, PEER_TOPK, body, tuple(prob[0] for prob in problems))


_PEER_CAND_ROWS = ((0, 16),) + tuple((16 + 8 * (p - 1), 8) for p in range(1, 8))
_PEER_NCAND = _PEER_CAND_ROWS[-1][0] + 16


def _peer_route_kernel(x_ref, mod0_ref, mod1_ref, wq_ref, sk_ref, t_ref, ids_ref, gate_ref,
                       v1_sc, i1_sc, v2_sc, i2_sc, cand_sc, cidx_sc, top_sc, tid_sc):
    h = pl.program_id(1)

    @pl.when(h == 0)
    def _():
        for half, mod_ref in enumerate((mod0_ref, mod1_ref)):
            rows = slice(half * TM, (half + 1) * TM)
            mod = mod_ref[0, 0]
            t_ref[rows, :] = (_ln(x_ref[rows, :]) * (1.0 + mod[4:5, :]) + mod[3:4, :]).astype(BF16)

    q = _dot(t_ref[...], wq_ref[...]).astype(BF16)
    half = PEER_DQ // 2
    s1, s2 = _dot_nt(sk_ref[0], q[:, :half]), _dot_nt(sk_ref[1], q[:, half:])
    _top16((s1, v1_sc, i1_sc, None))
    _top16((s2, v2_sc, i2_sc, None))
    nk = float(PEER_NKEYS)
    for p, (lo, n) in enumerate(_PEER_CAND_ROWS):
        cand_sc[lo:lo + n, :] = v1_sc[p:p + 1, :] + v2_sc[0:n, :]
        cidx_sc[lo:lo + n, :] = i1_sc[p:p + 1, :] * nk + i2_sc[0:n, :]
    lo = _PEER_CAND_ROWS[-1][0] + _PEER_CAND_ROWS[-1][1]
    cand_sc[lo:lo + 8, :] = v1_sc[8:16, :] + v2_sc[0:1, :]
    cidx_sc[lo:lo + 8, :] = i1_sc[8:16, :] * nk + i2_sc[0:1, :]
    _top16((cand_sc[...], top_sc, tid_sc, cidx_sc[...]))
    top = top_sc[...]
    e = jnp.exp(top - top[0:1, :])
    gate_ref[...] = e / jnp.sum(e, axis=0, keepdims=True)
    ids_ref[...] = tid_sc[...].astype(I32)


def _half_tile_mod_spec(half, nt_per_sample, nct):
    def index(i, j):
        r = 2 * i + half
        return (r // nt_per_sample, jnp.where(r % nt_per_sample >= nct, 1, 0), 0, 0)
    return pl.BlockSpec((1, 1, 6, D_MODEL), index)


def peer_route(h1, mod, wq, subkeys, nt_per_sample, nct):
    T, D = h1.shape
    assert T % TM_E == 0 and TM_E == 2 * TM
    f = lambda n: pltpu.VMEM((n, TM_E), F32)
    return pl.pallas_call(
        _peer_route_kernel,
        out_shape=[jax.ShapeDtypeStruct((T, D), BF16),
                   jax.ShapeDtypeStruct((PEER_HEADS * PEER_TOPK, T), I32),
                   jax.ShapeDtypeStruct((PEER_HEADS * PEER_TOPK, T), F32)],
        grid=(T // TM_E, PEER_HEADS),
        in_specs=[pl.BlockSpec((TM_E, D), lambda i, h: (i, 0)),
                  _half_tile_mod_spec(0, nt_per_sample, nct), _half_tile_mod_spec(1, nt_per_sample, nct),
                  pl.BlockSpec((D, PEER_DQ), lambda i, h: (0, h)),
                  pl.BlockSpec((2, PEER_NKEYS, PEER_DQ // 2), lambda i, h: (0, 0, 0))],
        out_specs=[pl.BlockSpec((TM_E, D), lambda i, h: (i, 0)),
                   pl.BlockSpec((PEER_TOPK, TM_E), lambda i, h: (h, i)),
                   pl.BlockSpec((PEER_TOPK, TM_E), lambda i, h: (h, i))],
        scratch_shapes=[f(16), f(16), f(16), f(16), f(_PEER_NCAND), f(_PEER_NCAND), f(16), f(16)],
        compiler_params=_cparams(("parallel", "arbitrary")), name="peer_route")(h1, mod, mod, wq, subkeys)


def _peer_expert_kernel(t_ref, ids_ref, gate_ref, u_ref, v_ref, x_ref, mod0_ref, mod1_ref, lng_ref, lnb_ref, o_ref,
                        a_sc, wgt_sc, *, nchunk):
    j = pl.program_id(1)
    NK, CA = PEER_NKEYS, PEER_CA
    w_sc = a_sc

    @pl.when(j < nchunk)
    def _():
        a = _dot(t_ref[...], u_ref[0])
        a3 = jnp.stack([a[:, al * NK:(al + 1) * NK] for al in range(CA)], axis=0)
        a_sc[:, pl.ds(pl.multiple_of(j * CA, CA), CA), :] = pltpu.einshape("atb->tab", a3)

    @pl.when(j == nchunk - 1)
    def _():
        io_k = lax.broadcasted_iota(I32, (NK, NK), 0)
        io_a = lax.broadcasted_iota(I32, (NK, 2 * NK), 0)

        def masks(two):
            idp = ids_ref[two, :]
            ids = jnp.concatenate([idp[0:1], idp[1:2]], axis=1)
            blocks = [jnp.where(io_k == (idp[tok:tok + 1] & (NK - 1)), 1.0, 0.0).astype(BF16) for tok in range(2)]
            zero = jnp.zeros((NK, NK), BF16)
            onehot = jnp.concatenate([jnp.concatenate([blocks[0], zero], axis=1),
                                      jnp.concatenate([zero, blocks[1]], axis=1)], axis=0)
            return onehot, io_a == (ids >> 7)

        def pick(p, carry):
            two = pl.ds(pl.multiple_of(p * 2, 2), 2)
            onehot, r1 = masks(two)
            a2 = a_sc[two]
            at = jnp.concatenate([a2[0], a2[1]], axis=1).astype(BF16)
            picked = _dot(at, onehot)
            act = jnp.sum(jnp.where(r1, picked, 0.0), axis=0, keepdims=True)
            gp = gate_ref[two, :]
            w = jnp.concatenate([gp[0:1], gp[1:2]], axis=1) * _gelu(act)
            wgt_sc[two, :] = jnp.concatenate([w[:, :NK], w[:, NK:]], axis=0)
            return carry

        def scatter(p, carry):
            two = pl.ds(pl.multiple_of(p * 2, 2), 2)
            onehot, r1 = masks(two)
            wp = wgt_sc[two, :]
            w = jnp.concatenate([wp[0:1], wp[1:2]], axis=1)
            wt = _dot_nt(jnp.where(r1, w, 0.0).astype(BF16), onehot)
            w_sc[two] = jnp.stack([wt[:, :NK], wt[:, NK:]], axis=0)
            return carry

        lax.fori_loop(0, TM_E // 2, pick, 0, unroll=16)
        lax.fori_loop(0, TM_E // 2, scatter, 0, unroll=16)
        o_ref[...] = jnp.zeros_like(o_ref)

    @pl.when(j >= nchunk)
    def _():
        w3 = pltpu.einshape("tab->atb", w_sc[:, pl.ds(pl.multiple_of((j - nchunk) * CA, CA), CA), :])
        w = jnp.concatenate([w3[al] for al in range(CA)], axis=1).astype(BF16)
        o_ref[...] += _dot(w, v_ref[...])

    @pl.when(j == 2 * nchunk - 1)
    def _():
        for half, mod_ref in enumerate((mod0_ref, mod1_ref)):
            rows = slice(half * TM, (half + 1) * TM)
            y = ALPHA * x_ref[rows, :] + mod_ref[0, 0][5:6, :] * o_ref[rows, :]
            o_ref[rows, :] = _ln(y) * lng_ref[...] + lnb_ref[...]


def peer_experts(t, ids, gates, emb_u, emb_v, h1, mod, lng, lnb, nt_per_sample, nct):
    T, D = h1.shape
    E = emb_v.shape[0]
    ce = PEER_CA * PEER_NKEYS
    nchunk = E // ce
    emb_ut = emb_u.reshape(nchunk, ce, D).transpose(0, 2, 1)
    assert T % TM_E == 0 and TM_E == 2 * TM
    tok = lambda n, **kw: pl.BlockSpec((TM_E, n), lambda i, j: (i, 0), **kw)
    mod_spec = lambda half: _half_tile_mod_spec(half, nt_per_sample, nct)
    once = dict(pipeline_mode=pl.Buffered(1))
    return pl.pallas_call(
        functools.partial(_peer_expert_kernel, nchunk=nchunk),
        out_shape=jax.ShapeDtypeStruct((T, D), F32), grid=(T // TM_E, 2 * nchunk),
        in_specs=[tok(D, **once), tok(LANES, **once), tok(LANES, **once),
                  pl.BlockSpec((1, D, ce), lambda i, j: (jnp.minimum(j, nchunk - 1), 0, 0)),
                  pl.BlockSpec((ce, D), lambda i, j: (jnp.maximum(j - nchunk, 0), 0)),
                  tok(D, **once), mod_spec(0), mod_spec(1),
                  pl.BlockSpec((1, D), lambda i, j: (0, 0)), pl.BlockSpec((1, D), lambda i, j: (0, 0))],
        out_specs=tok(D),
        scratch_shapes=[pltpu.VMEM((TM_E, PEER_NKEYS, PEER_NKEYS), F32), pltpu.VMEM((TM_E, LANES), F32)],
        compiler_params=_cparams(("parallel", "arbitrary")), name="peer_experts")(t, ids, gates, emb_ut, emb_v, h1, mod, mod, lng, lnb)


_ROPE_IDX = np.arange(HD)
_ROPE_PERM = np.where(_ROPE_IDX % 32 < 16, _ROPE_IDX + 16, _ROPE_IDX - 16)
_ROPE_SIGN = np.where(_ROPE_IDX % 32 < 16, -1.0, 1.0).astype(np.float32)


def _rope_partner(w):
    n = w.shape[1] // HD
    perm = np.concatenate([h * HD + _ROPE_PERM for h in range(n)])
    return w[:, perm] * jnp.asarray(np.tile(_ROPE_SIGN, n))


def _rope_tables(L, nctx, gq, gk):
    rows = L // GRID_W
    row = jnp.repeat(jnp.arange(rows, dtype=F32), GRID_W)
    col = jnp.tile(jnp.arange(GRID_W, dtype=F32), rows)
    nf = HD // 4
    inv = ROPE_THETA ** (-jnp.arange(nf, dtype=F32) / nf)
    ar, ac = row[:, None] * inv, col[:, None] * inv
    cos = jnp.concatenate([jnp.cos(ar), jnp.cos(ar), jnp.cos(ac), jnp.cos(ac)], axis=1)
    sin = jnp.concatenate([jnp.sin(ar), jnp.sin(ar), jnp.sin(ac), jnp.sin(ac)], axis=1)
    cos = jnp.concatenate([jnp.ones((nctx, HD), F32), cos], axis=0)
    sin = jnp.concatenate([jnp.zeros((nctx, HD), F32), sin], axis=0)
    scale = HD ** -0.5
    one = jnp.ones((HD,), F32)
    tabs = []
    for g, sc in ((one, scale), (one, 1.0), (gq, scale), (gk, 1.0)):
        tabs += [cos * (g * sc), sin * (g[_ROPE_PERM] * sc)]
    return jnp.tile(jnp.stack(tabs), (1, 1, 2))


def _s5_tables(a_re, a_im, log_dt, b_re, b_im, c_re, c_im, nb):
    dt = jnp.exp(log_dt)[..., None]
    mag = jnp.exp(a_re * dt)
    lr, li = mag * jnp.cos(a_im * dt), mag * jnp.sin(a_im * dt)
    den = a_re * a_re + a_im * a_im
    cr = ((lr - 1) * a_re + li * a_im) / den
    ci = (li * a_re - (lr - 1) * a_im) / den
    br = cr[..., None] * b_re - ci[..., None] * b_im
    bi = cr[..., None] * b_im + ci[..., None] * b_re
    G2, N, C = S5_GROUPS // 2, S5_STATE, S5_GROUP
    eye = jnp.eye(G2, dtype=F32)

    def blockdiag_in(m):
        m = m.reshape(2, 2, G2, N, C)
        return jnp.einsum('dhgnc,gk->dhgckn', m, eye).reshape(2, 2, G2 * C, G2 * N)

    def blockdiag_out(m):
        m = m.reshape(2, 2, G2, C, N)
        return jnp.einsum('dhgcn,gk->dhgnkc', m, eye).reshape(2, 2, G2 * N, G2 * C)

    bblk = jnp.concatenate([blockdiag_in(br), blockdiag_in(bi)], axis=-1).astype(BF16)
    cblk = jnp.concatenate([blockdiag_out(c_re), blockdiag_out(-c_im)], axis=-2).astype(BF16)
    lam = jnp.stack([lr, li], axis=1).reshape(2, 2, 2, G2 * N)
    lam = jnp.tile(lam[:, :, None], (1, 1, nb, 1, 1)).reshape(2, 2, 2 * nb, G2 * N)
    return bblk, cblk, lam


def _blockdiag_ones(group, n=LANES):
    i = np.arange(n) // group
    return jnp.asarray((i[:, None] == i[None, :]).astype(np.float32), dtype=BF16)


def kernel(x, c, ctx, c_ctx, ada_w, ada_b, w_in, b_gate, diff_lam, diff_norm_g, gqa_qnorm_g, gqa_knorm_g,
           s5_a_re, s5_a_im, s5_log_dt, s5_b_re, s5_b_im, s5_c_re, s5_c_im, s5_d, s5_w_glu, s5_b_glu,
           ml_conv_w, ml_conv_b, ml_gate_b, ml_norm_g, w_branch, w_out, ln_mix_g, ln_mix_b, ln_ffn_g, ln_ffn_b,
           peer_wq, peer_subkeys, peer_u, peer_v):
    B, L, D = x.shape
    nctx = ctx.shape[1]
    S = nctx + L
    assert D == D_MODEL and nctx % TM == 0 and L % TM == 0 and L % GRID_W == 0
    nt, nct = S // TM, nctx // TM
    depth = ada_w.shape[0]

    h = jnp.concatenate([ctx, x], axis=1)
    R = -(-(B + 1) // 8) * 8
    cond = jnp.zeros((R, D), F32).at[:B].set(c).at[B].set(c_ctx)
    ones64, ones128 = _blockdiag_ones(HD), _blockdiag_ones(ML_HD, ML_HEADS * ML_HD)
    o = IN_OFFS

    for l in range(depth):
        lam_init = 0.8 - 0.6 * math.exp(-0.3 * l)
        m = ada_modulation(cond, ada_w[l].astype(BF16), ada_b[l][None, :])
        mod = jnp.stack([jnp.broadcast_to(m[B], (B, 6 * D)), m[:B]], axis=1).reshape(B, 2, 6, D)

        w = w_in[l]
        seg = lambda i: w[:, o[i]:o[i + 1]]
        dup = lambda t: jnp.concatenate([t[:, :HD], t[:, :HD], t[:, HD:], t[:, HD:]], axis=1)
        def with_ones_slots(t, width):
            heads = t.reshape(D, -1, width)
            return jnp.concatenate([heads, jnp.zeros_like(heads)], axis=2).reshape(D, -1)
        w_a = jnp.concatenate([seg(0), _rope_partner(seg(0)), seg(1), _rope_partner(seg(1)), seg(2),
                               seg(9), _rope_partner(seg(9)), dup(seg(10)), dup(_rope_partner(seg(10))),
                               with_ones_slots(seg(11), HD)], axis=1).astype(BF16)
        w_b = jnp.concatenate([seg(3), seg(4), seg(5), seg(6), seg(7), seg(8),
                               jnp.zeros((D, LANES - 16), F32)], axis=1).astype(BF16)
        gate_b = jnp.concatenate([ml_gate_b[l], jnp.zeros((LANES - 16,), F32)])[None, :]
        tab = _rope_tables(L, nctx, gqa_qnorm_g[l], gqa_knorm_g[l])

        dq, dk, dv, gq, gk, gv = proj_attn(h, mod, w_a, tab, ones64, nct)
        u, mqk, mv, mo, mg = proj_seq(h, mod, w_b, gate_b, nct)

        yd = diff_attention(dq, dk, dv, diff_lam[l], diff_norm_g[l][None, :], nctx, lam_init)
        yg = gqa_attention(gq, gk, gv, nctx)

        bblk, cblk, lam = _s5_tables(s5_a_re[l], s5_a_im[l], s5_log_dt[l], s5_b_re[l], s5_b_im[l],
                                     s5_c_re[l], s5_c_im[l], B)
        sf, sb = (y.reshape(u.shape) for y in s5_scan(u.reshape(S * 2 * B, 256), bblk, cblk, lam, S, nctx))

        mq, mk = ml_prep(mqk, ml_conv_w[l], ml_conv_b[l][None, :], nctx)
        hf = mlstm_scan(mq, mk, mv, mg, nctx, False)
        hb = mlstm_scan(mq, mk, mv, mg, nctx, True)

        h1 = merge(h, mod, yd, yg, u, sf, sb, hf, hb, mo,
                   s5_d[l][None, :], s5_w_glu[l].astype(BF16), s5_b_glu[l][None, :], ml_norm_g[l][None, :],
                   seg(12).astype(BF16), b_gate[l][None, :], w_branch[l].astype(BF16), w_out[l].astype(BF16),
                   ln_mix_g[l][None, :], ln_mix_b[l][None, :], ones128, nct)

        h1 = h1.reshape(B * S, D)
        t, ids, gates = peer_route(h1, mod, peer_wq[l].astype(BF16), peer_subkeys[l].astype(BF16), nt, nct)
        h = peer_experts(t, ids.T, gates.T, peer_u[l].astype(BF16), peer_v[l].astype(BF16),
                         h1, mod, ln_ffn_g[l][None, :], ln_ffn_b[l][None, :], nt, nct).reshape(B, S, D)
    return h[:, nctx:]
```

```python
import functools
import math

import numpy as np
import jax
import jax.numpy as jnp
from jax import lax
from jax.experimental import pallas as pl
from jax.experimental.pallas import tpu as pltpu

F32 = jnp.float32
BF16 = jnp.bfloat16
I32 = jnp.int32

D_MODEL = 1024
DEPTH = 2
GRID_W = 64
ROPE_THETA = 10000.0
LN_EPS = 1e-6
HD = 64
DIFF_HEADS = 4
DIFF_HPS = 2
GQA_HEADS = 8
GQA_KV = 2
GQA_GPS = 2
S5_GROUP = 16
S5_GROUPS = 32
S5_STATE = 64
ML_HEADS = 4
ML_HD = 128
N_BRANCH = 4
W_BRANCH = 512
PEER_HEADS = 8
PEER_NKEYS = 128
PEER_TOPK = 16
PEER_DQ = 256
ALPHA = (2 * DEPTH) ** 0.25

LANES = 128
TM = 256
TM_E = 2 * TM
TQ = 256
TK = 2048
S5_TC = 128
ML_T = 256
ML_LOCKSTEP = 4
PEER_CA = 8
VMEM_LIMIT = 56 * 1024 * 1024

IN_SPLITS = (512, 512, 512, 512, 512, 512, 512, 512, 16, 512, 128, 128, N_BRANCH * D_MODEL)
IN_OFFS = tuple(int(v) for v in np.cumsum((0,) + IN_SPLITS))


def _cparams(sem):
    return pltpu.CompilerParams(dimension_semantics=sem, vmem_limit_bytes=VMEM_LIMIT)


def _const_spec(shape):
    nd = len(shape)
    return pl.BlockSpec(shape, lambda *_: (0,) * nd, pipeline_mode=pl.Buffered(1))


def _ln(x):
    xc = x - jnp.mean(x, axis=-1, keepdims=True)
    return xc * lax.rsqrt(jnp.mean(xc * xc, axis=-1, keepdims=True) + LN_EPS)


def _sigmoid(x):
    return 1.0 / (1.0 + jnp.exp(-x))


def _gelu(x):
    return 0.5 * x * (1.0 + lax.erf(x * (2.0 ** -0.5)))


def _dot(a, b):
    return jnp.dot(a, b, preferred_element_type=F32)


def _dot_nt(a, b):
    return lax.dot_general(a, b, (((1,), (1,)), ((), ())), preferred_element_type=F32)


def _dot_tn(a, b):
    return lax.dot_general(a, b, (((0,), (0,)), ((), ())), preferred_element_type=F32)


def _ada_kernel(c_ref, w_ref, b_ref, o_ref):
    c = c_ref[...]
    o_ref[...] = _dot((c * _sigmoid(c)).astype(BF16), w_ref[...]) + b_ref[...]


def ada_modulation(cond, w, b):
    R, D = cond.shape
    N = w.shape[1]
    tn = 1536
    return pl.pallas_call(
        _ada_kernel, out_shape=jax.ShapeDtypeStruct((R, N), F32), grid=(N // tn,),
        in_specs=[pl.BlockSpec((R, D), lambda j: (0, 0)), pl.BlockSpec((D, tn), lambda j: (0, j)),
                  pl.BlockSpec((1, tn), lambda j: (0, j))],
        out_specs=pl.BlockSpec((R, tn), lambda j: (0, j)),
        compiler_params=_cparams(("arbitrary",)), name="ada")(cond, w, b)


def _proj_attn_kernel(x_ref, mod_ref, w_ref, tab_ref, ones_ref,
                      dq_ref, dk_ref, dv_ref, gq_ref, gk_ref, gv_ref):
    mod = mod_ref[0, 0]
    xm = (_ln(x_ref[0]) * (1.0 + mod[1:2, :]) + mod[0:1, :]).astype(BF16)

    def mm(lo, n):
        return _dot(xm, w_ref[:, lo:lo + n])

    def rope_store(ref, lo, n, ci, norm):
        t, tp = mm(lo, n), mm(lo + n, n)
        c, s = tab_ref[ci], tab_ref[ci + 1]
        for j in range(n // LANES):
            sl = slice(j * LANES, (j + 1) * LANES)
            tb = t[:, sl]
            y = tb * c + tp[:, sl] * s
            if norm:
                ss = _dot((tb * tb).astype(BF16), ones_ref[...])
                y = y * lax.rsqrt(ss * (1.0 / HD) + LN_EPS)
            ref[0, :, sl] = y.astype(ref.dtype)

    def value_store(ref, lo, n, width):
        lane = lax.broadcasted_iota(I32, (1, n), 1)
        ref[0] = (mm(lo, n) + jnp.where(lane % (2 * width) >= width, 1.0, 0.0)).astype(ref.dtype)

    rope_store(dq_ref, 0, 512, 0, False)
    rope_store(dk_ref, 1024, 512, 2, False)
    dv_ref[0] = mm(2048, 512).astype(dv_ref.dtype)
    rope_store(gq_ref, 2560, 512, 4, True)
    rope_store(gk_ref, 3584, 256, 6, True)
    value_store(gv_ref, 4096, 256, HD)


def proj_attn(xa, mod, w_a, tab, ones_bd, nct):
    B, S, D = xa.shape
    tok = lambda n: pl.BlockSpec((1, TM, n), lambda b, i: (b, i, 0))
    widths = (512, 512, 512, 512, 256, 256)
    outs = [jax.ShapeDtypeStruct((B, S, n), BF16) for n in widths]
    return pl.pallas_call(
        _proj_attn_kernel, out_shape=outs, grid=(B, S // TM),
        in_specs=[tok(D),
                  pl.BlockSpec((1, 1, 6, D), lambda b, i: (b, jnp.where(i >= nct, 1, 0), 0, 0)),
                  _const_spec(w_a.shape),
                  pl.BlockSpec((8, TM, LANES), lambda b, i: (0, i, 0)),
                  _const_spec(ones_bd.shape)],
        out_specs=[tok(n) for n in widths],
        compiler_params=_cparams(("parallel", "parallel")), name="proj_attn")(xa, mod, w_a, tab, ones_bd)


def _proj_seq_kernel(x_ref, mod_ref, w_ref, gb_ref, u_ref, qk_ref, v_ref, o_ref, g_ref):
    mod = mod_ref[0, 0]
    xm = (_ln(x_ref[0]) * (1.0 + mod[1:2, :]) + mod[0:1, :]).astype(BF16)
    u_ref[...] = _dot(xm, w_ref[:, 0:512])
    qk_ref[0] = _dot(xm, w_ref[:, 512:1536])
    v_ref[0] = _dot(xm, w_ref[:, 1536:2048])
    o_ref[0] = _dot(xm, w_ref[:, 2048:2560])
    g_ref[0] = _dot(xm, w_ref[:, 2560:2688]) + gb_ref[...]


def proj_seq(xa, mod, w_b, gate_b, nct):
    B, S, D = xa.shape
    tok = lambda n: pl.BlockSpec((1, TM, n), lambda b, i: (b, i, 0))
    widths = (1024, 512, 512, LANES)
    return pl.pallas_call(
        _proj_seq_kernel,
        out_shape=[jax.ShapeDtypeStruct((S, B * 512), F32)] + [jax.ShapeDtypeStruct((B, S, n), F32) for n in widths],
        grid=(B, S // TM),
        in_specs=[tok(D),
                  pl.BlockSpec((1, 1, 6, D), lambda b, i: (b, jnp.where(i >= nct, 1, 0), 0, 0)),
                  _const_spec(w_b.shape), _const_spec(gate_b.shape)],
        out_specs=[pl.BlockSpec((TM, 512), lambda b, i: (i, b))] + [tok(n) for n in widths],
        compiler_params=_cparams(("parallel", "parallel")), name="proj_seq")(xa, mod, w_b, gate_b)


def _flash(qqs, k_ref, v_ref, nctx, n_lat, tk, dv):
    P = len(qqs)
    R, W = qqs[0].shape[0], v_ref.shape[2] // P
    mxu_sums = dv < W

    def finish(carry, s, v):
        m, l, acc = carry
        m_new = jnp.maximum(m, jnp.max(s, axis=1, keepdims=True))
        alpha = jnp.exp(m - m_new)
        if mxu_sums:
            p = jnp.exp((s - m_new).astype(BF16))
        else:
            p = jnp.exp(s - m_new)
            l = alpha * l + jnp.sum(p, axis=1, keepdims=True)
        return m_new, l, alpha * acc + _dot(p.astype(BF16), v)

    carries = [(jnp.full((R, 1), -jnp.inf, F32), jnp.zeros((R, 1), F32), jnp.zeros((R, W), F32))] * P
    chunks = [pl.ds(0, nctx)] + [pl.ds(nctx + c * tk, tk) for c in range(n_lat)]
    for rows in chunks:
        scores = [_dot_nt(qqs[p], k_ref[0, rows, p * LANES:(p + 1) * LANES]) for p in range(P)]
        carries = [finish(carries[p], scores[p], v_ref[0, rows, p * W:(p + 1) * W]) for p in range(P)]
    return [acc[:, :dv] / (acc[:, dv:dv + 1] if mxu_sums else l) for _, l, acc in carries]


def _ctx_or_all(i, nctx, n_lat, attend):
    @pl.when(i < nctx // TQ)
    def _():
        attend(0)

    @pl.when(i >= nctx // TQ)
    def _():
        attend(n_lat)


def _diff_attn_kernel(q_ref, k_ref, v_ref, lam_ref, g_ref, o_ref, *, nctx, n_lat, tk, lam_init):
    i = pl.program_id(2)
    lane = lax.broadcasted_iota(I32, (TQ, LANES), 1)
    zero = jnp.zeros((TQ, LANES), q_ref.dtype)
    qqs = []
    for h in range(DIFF_HPS):
        q = q_ref[0, :, h * LANES:(h + 1) * LANES]
        qqs.append(jnp.concatenate([jnp.where(lane < HD, q, zero), jnp.where(lane >= HD, q, zero)], axis=0))
    lv = lam_ref[...]
    lam = (jnp.exp(jnp.sum(lv[0:1] * lv[1:2], axis=1, keepdims=True))
           - jnp.exp(jnp.sum(lv[2:3] * lv[3:4], axis=1, keepdims=True)) + lam_init)

    def attend(n):
        for h, o in enumerate(_flash(qqs, k_ref, v_ref, nctx, n, tk, 2 * HD)):
            d = o[:TQ] - lam * o[TQ:]
            y = d * lax.rsqrt(jnp.mean(d * d, axis=-1, keepdims=True) + LN_EPS)
            o_ref[0, :, h * LANES:(h + 1) * LANES] = (y * g_ref[...] * (1.0 - lam_init)).astype(o_ref.dtype)

    _ctx_or_all(i, nctx, n_lat, attend)


def diff_attention(q, k, v, lam_vec, norm_g, nctx, lam_init):
    B, S, _ = q.shape
    tk = math.gcd(S - nctx, TK)
    kern = functools.partial(_diff_attn_kernel, nctx=nctx, n_lat=(S - nctx) // tk, tk=tk, lam_init=lam_init)
    return pl.pallas_call(
        kern, out_shape=jax.ShapeDtypeStruct((B, S, 512), BF16), grid=(B, DIFF_HEADS // DIFF_HPS, S // TQ),
        in_specs=[pl.BlockSpec((1, TQ, DIFF_HPS * LANES), lambda b, h, i: (b, i, h)),
                  pl.BlockSpec((1, S, DIFF_HPS * LANES), lambda b, h, i: (b, 0, h)),
                  pl.BlockSpec((1, S, DIFF_HPS * LANES), lambda b, h, i: (b, 0, h)),
                  pl.BlockSpec((4, HD), lambda b, h, i: (0, 0)),
                  pl.BlockSpec((1, LANES), lambda b, h, i: (0, 0))],
        out_specs=pl.BlockSpec((1, TQ, DIFF_HPS * LANES), lambda b, h, i: (b, i, h)),
        compiler_params=_cparams(("parallel", "parallel", "parallel")), name="diff_attn")(q, k, v, lam_vec, norm_g)


def _gqa_attn_kernel(q_ref, k_ref, v_ref, o_ref, *, nctx, n_lat, tk):
    i = pl.program_id(2)
    lane = lax.broadcasted_iota(I32, (TQ, LANES), 1)
    zero = jnp.zeros((TQ, LANES), q_ref.dtype)
    qqs = []
    for g in range(GQA_GPS):
        parts = []
        for j in range(2 * g, 2 * g + 2):
            blk = q_ref[0, :, j * LANES:(j + 1) * LANES]
            parts += [jnp.where(lane < HD, blk, zero), jnp.where(lane >= HD, blk, zero)]
        qqs.append(jnp.concatenate(parts, axis=0))

    def attend(n):
        for g, o in enumerate(_flash(qqs, k_ref, v_ref, nctx, n, tk, HD)):
            heads = jnp.concatenate([o[h * TQ:(h + 1) * TQ] for h in range(4)], axis=1)
            o_ref[0, :, g * 2 * LANES:(g + 1) * 2 * LANES] = heads.astype(o_ref.dtype)

    _ctx_or_all(i, nctx, n_lat, attend)


def gqa_attention(q, k, v, nctx):
    B, S, _ = q.shape
    tk = math.gcd(S - nctx, TK)
    kern = functools.partial(_gqa_attn_kernel, nctx=nctx, n_lat=(S - nctx) // tk, tk=tk)
    return pl.pallas_call(
        kern, out_shape=jax.ShapeDtypeStruct((B, S, 512), BF16), grid=(B, GQA_KV // GQA_GPS, S // TQ),
        in_specs=[pl.BlockSpec((1, TQ, GQA_GPS * 2 * LANES), lambda b, g, i: (b, i, g)),
                  pl.BlockSpec((1, S, GQA_GPS * LANES), lambda b, g, i: (b, 0, g)),
                  pl.BlockSpec((1, S, GQA_GPS * LANES), lambda b, g, i: (b, 0, g))],
        out_specs=pl.BlockSpec((1, TQ, GQA_GPS * 2 * LANES), lambda b, g, i: (b, i, g)),
        compiler_params=_cparams(("parallel", "parallel", "parallel")), name="gqa_attn")(q, k, v)


def _s5_kernel(uf_ref, ub_ref, bblk_ref, cblk_ref, lam_ref, yf_ref, yb_ref, buf_f, buf_b, st_ref):
    H = S5_GROUPS * S5_STATE // 2
    R = buf_f.shape[0]
    RB = 256
    TB = RB // 8

    @pl.when(pl.program_id(0) == 0)
    def _():
        st_ref[...] = jnp.zeros_like(st_ref)

    half0 = (lax.broadcasted_iota(I32, (RB, 1), 0) & 1) == 0

    for d, (u_ref, buf) in enumerate(((uf_ref, buf_f), (ub_ref, buf_b))):
        for rb in range(R // RB):
            rows = slice(rb * RB, (rb + 1) * RB)
            ut = u_ref[rb * TB:(rb + 1) * TB, :]
            u3 = jnp.stack([ut[:, s * 256:(s + 1) * 256] for s in range(8)], axis=0)
            uh = pltpu.einshape("stc->tsc", u3).reshape(RB, 256).astype(BF16)
            for cols in (slice(0, H), slice(H, 2 * H)):
                buf[rows, cols] = jnp.where(half0, _dot(uh, bblk_ref[d, 0, :, cols]), _dot(uh, bblk_ref[d, 1, :, cols]))

    lfr, lfi, lbr, lbi = lam_ref[0, 0], lam_ref[0, 1], lam_ref[1, 0], lam_ref[1, 1]

    def step(t, carry):
        fr, fi, br, bi = carry
        rf = pl.ds(pl.multiple_of(t * 8, 8), 8)
        x = buf_f[rf, :]
        nfr = lfr * fr - lfi * fi + x[:, :H]
        nfi = lfr * fi + lfi * fr + x[:, H:]
        buf_f[rf, :] = jnp.concatenate([nfr, nfi], axis=1)
        rb = pl.ds(pl.multiple_of((S5_TC - 1 - t) * 8, 8), 8)
        z = buf_b[rb, :]
        nbr = lbr * br - lbi * bi + z[:, :H]
        nbi = lbr * bi + lbi * br + z[:, H:]
        buf_b[rb, :] = jnp.concatenate([nbr, nbi], axis=1)
        return nfr, nfi, nbr, nbi

    fin = lax.fori_loop(0, S5_TC, step, (st_ref[0], st_ref[1], st_ref[2], st_ref[3]))
    for j in range(4):
        st_ref[j] = fin[j]

    for d, (y_ref, buf) in enumerate(((yf_ref, buf_f), (yb_ref, buf_b))):
        for rb in range(R // RB):
            rows = slice(rb * RB, (rb + 1) * RB)
            h = buf[rows, :].astype(BF16)
            y = jnp.where(half0, _dot(h, cblk_ref[d, 0]), _dot(h, cblk_ref[d, 1]))
            y3 = pltpu.einshape("tsc->stc", y.reshape(TB, 8, 256))
            y_ref[rb * TB:(rb + 1) * TB, :] = jnp.concatenate([y3[s] for s in range(8)], axis=1)


def s5_scan(u, bblk, cblk, lam, nctx):
    S, W = u.shape
    assert W == 8 * 256, "the scan packs (sample, half) pairs into the eight sublanes of a vreg"
    nch, nc0 = S // S5_TC, nctx // S5_TC

    def bwd(i):
        return jnp.where(i < nc0, nc0 - 1 - i, (nch - 1) - (i - nc0))
    H2 = S5_GROUPS * S5_STATE
    R = S5_TC * 8
    blk = lambda f: pl.BlockSpec((S5_TC, W), f)
    return pl.pallas_call(
        _s5_kernel, out_shape=[jax.ShapeDtypeStruct(u.shape, F32)] * 2, grid=(nch,),
        in_specs=[blk(lambda i: (i, 0)), blk(lambda i: (bwd(i), 0)),
                  _const_spec(bblk.shape), _const_spec(cblk.shape), _const_spec(lam.shape)],
        out_specs=[blk(lambda i: (i, 0)), blk(lambda i: (bwd(i), 0))],
        scratch_shapes=[pltpu.VMEM((R, H2), F32), pltpu.VMEM((R, H2), F32), pltpu.VMEM((4, 8, H2 // 2), F32)],
        compiler_params=_cparams(("arbitrary",)), name="s5_scan")(u, u, bblk, cblk, lam)


def _ml_prep_kernel(x_ref, prev_ref, next_ref, w_ref, b_ref, q_ref, k_ref, *, seg_starts, seg_ends):
    i = pl.program_id(1)
    x = x_ref[0]
    row = lax.broadcasted_iota(I32, x.shape, 0)
    first = functools.reduce(jnp.logical_or, [i == s for s in seg_starts])
    last = functools.reduce(jnp.logical_or, [i == s for s in seg_ends])
    pr = jnp.where(first, 0.0, prev_ref[0, 7:8, :])
    nx = jnp.where(last, 0.0, next_ref[0, 0:1, :])
    xp = jnp.where(row == 0, pr, pltpu.roll(x, 1, 0))
    xn = jnp.where(row == TM - 1, nx, pltpu.roll(x, TM - 1, 0))
    w = w_ref[...]
    y = b_ref[...] + xp * w[0:1, :] + x * w[1:2, :] + xn * w[2:3, :]
    y = y * _sigmoid(y)
    q_ref[0] = y[:, :512].astype(q_ref.dtype)
    k_ref[0] = (y[:, 512:] * (ML_HD ** -0.5)).astype(k_ref.dtype)


def ml_prep(qk, conv_w, conv_b, nctx):
    B, S, W = qk.shape
    nt, nct, r8 = S // TM, nctx // TM, TM // 8
    kern = functools.partial(_ml_prep_kernel, seg_starts=(0, nct), seg_ends=(nct - 1, nt - 1))
    return pl.pallas_call(
        kern, out_shape=[jax.ShapeDtypeStruct((B, S, 512), BF16)] * 2, grid=(B, nt),
        in_specs=[pl.BlockSpec((1, TM, W), lambda b, i: (b, i, 0)),
                  pl.BlockSpec((1, 8, W), lambda b, i: (b, jnp.maximum(i * r8 - 1, 0), 0)),
                  pl.BlockSpec((1, 8, W), lambda b, i: (b, jnp.minimum((i + 1) * r8, S // 8 - 1), 0)),
                  pl.BlockSpec((3, W), lambda b, i: (0, 0)), pl.BlockSpec((1, W), lambda b, i: (0, 0))],
        out_specs=[pl.BlockSpec((1, TM, 512), lambda b, i: (b, i, 0))] * 2,
        compiler_params=_cparams(("parallel", "parallel")), name="ml_prep")(qk, qk, qk, conv_w, conv_b)


def _log_sigmoid(x):
    return jnp.minimum(x, 0.0) - jnp.log1p(jnp.exp(-jnp.abs(x)))


def _mlstm_kernel(q_ref, k_ref, v_ref, g_ref, h_ref, c_sc, n_sc, m_sc, *, reverse):
    T = ML_T
    gi, gf = (8, 12) if reverse else (0, 4)

    @pl.when(pl.program_id(0) == 0)
    def _():
        c_sc[...] = jnp.zeros_like(c_sc)
        n_sc[...] = jnp.zeros_like(n_sc)
        m_sc[...] = jnp.zeros_like(m_sc)

    r = lax.broadcasted_iota(I32, (T, T), 0)
    c = lax.broadcasted_iota(I32, (T, T), 1)
    mask = (c >= r) if reverse else (c <= r)
    tri = jnp.where(mask, 1.0, 0.0)
    hp = lax.Precision.HIGHEST
    end = 0 if reverse else T - 1
    def gates(b):
        g = g_ref[b]
        gt = g.T
        bcol_all = jnp.dot(tri, _log_sigmoid(g), preferred_element_type=F32, precision=hp)
        brow_all = lax.dot_general(_log_sigmoid(gt[0:16]), tri, (((1,), (1,)), ((), ())),
                                   preferred_element_type=F32, precision=hp)
        return g, gt, bcol_all, brow_all

    def chain(b, hh, g, gt, bcol_all, brow_all):
        st = b * ML_HEADS + hh
        sl = slice(hh * ML_HD, (hh + 1) * ML_HD)
        q, k, v = q_ref[b, :, sl], k_ref[b, :, sl], v_ref[b, :, sl]
        bcol, brow = bcol_all[:, gf + hh:gf + hh + 1], brow_all[gf + hh:gf + hh + 1, :]
        icol, irow = g[:, gi + hh:gi + hh + 1], gt[gi + hh:gi + hh + 1, :]
        m_old = m_sc[st][:, 0:1]
        n_old = n_sc[st]
        c_old = c_sc[st]
        logw = jnp.where(mask, bcol - brow + irow, -jnp.inf)
        m_inter = bcol + m_old
        m_t = jnp.maximum(m_inter, jnp.max(logw, axis=1, keepdims=True))
        yield
        s = _dot_nt(q, k) * jnp.exp(logw - m_t)
        inter = jnp.exp(m_inter - m_t)
        yield
        num = _dot(s.astype(BF16), v.astype(BF16)) + inter * _dot_nt(q, c_old.astype(BF16))
        den = jnp.sum(s, axis=1, keepdims=True) + inter * jnp.sum(q.astype(F32) * n_old, axis=1, keepdims=True)
        h_ref[b, :, sl] = num / jnp.maximum(jnp.abs(den), jnp.exp(-m_t))
        yield
        b_end = bcol[end:end + 1, :]
        g_row, g_col = b_end - brow + irow, b_end - bcol + icol
        m_new = jnp.maximum(b_end + m_old, jnp.max(g_row, axis=1, keepdims=True))
        decay = jnp.exp(b_end + m_old - m_new)
        wk = jnp.exp(g_col - m_new)
        c_sc[st] = decay * c_old + _dot_tn((v * wk).astype(BF16), k)
        n_sc[st] = decay * n_old + jnp.sum(k.astype(F32) * wk, axis=0, keepdims=True)
        m_sc[st] = jnp.broadcast_to(m_new, (1, LANES))

    nb = q_ref.shape[0]
    for b0 in range(0, nb, ML_LOCKSTEP):
        chains = []
        for b in range(b0, min(b0 + ML_LOCKSTEP, nb)):
            shared = gates(b)
            chains += [chain(b, hh, *shared) for hh in range(ML_HEADS)]
        for _ in range(4):
            for ch in chains:
                next(ch, None)


def mlstm_scan(q, k, v, g, nctx, reverse):
    B, S, W = q.shape
    nch, nc0 = S // ML_T, nctx // ML_T

    def order(i):
        return jnp.where(i < nc0, nc0 - 1 - i, (nch - 1) - (i - nc0)) if reverse else i
    blk = lambda n: pl.BlockSpec((B, ML_T, n), lambda i: (0, order(i), 0))
    nst = B * ML_HEADS
    return pl.pallas_call(
        functools.partial(_mlstm_kernel, reverse=reverse),
        out_shape=jax.ShapeDtypeStruct((B, S, W), F32), grid=(nch,),
        in_specs=[blk(W), blk(W), blk(W), blk(LANES)], out_specs=blk(W),
        scratch_shapes=[pltpu.VMEM((nst, ML_HD, ML_HD), F32), pltpu.VMEM((nst, 1, ML_HD), F32),
                        pltpu.VMEM((nst, 1, LANES), F32)],
        compiler_params=_cparams(("arbitrary",)), name="mlstm_bwd" if reverse else "mlstm_fwd")(q, k, v, g)


def _merge_kernel(x_ref, mod_ref, yd_ref, yg_ref, u_ref, sf_ref, sb_ref, hf_ref, hb_ref, mo_ref,
                  s5d_ref, wglu_ref, bglu_ref, mlg_ref, wgate_ref, bgate_ref, wbr_ref, wout_ref,
                  lng_ref, lnb_ref, ones_ref, o_ref):
    x = x_ref[0]
    mod = mod_ref[0, 0]
    xm = (_ln(x) * (1.0 + mod[1:2, :]) + mod[0:1, :]).astype(BF16)
    ys = u_ref[...] * s5d_ref[...] + sf_ref[...] + sb_ref[...]
    z = _dot(_gelu(ys).astype(BF16), wglu_ref[...]) + bglu_ref[...]
    ys = z[:, :512] * _sigmoid(z[:, 512:])
    hm = hf_ref[0] + hb_ref[0]
    ss = _dot((hm * hm).astype(BF16), ones_ref[...])
    ym = hm * lax.rsqrt(ss * (1.0 / ML_HD) + LN_EPS) * mlg_ref[...] * _sigmoid(mo_ref[0])
    branches = (yd_ref[0], ys.astype(BF16), ym.astype(BF16), yg_ref[0])
    merged = None
    for j, yb in enumerate(branches):
        gate = _sigmoid(_dot(xm, wgate_ref[:, j * D_MODEL:(j + 1) * D_MODEL]) + bgate_ref[:, j * D_MODEL:(j + 1) * D_MODEL])
        term = gate * _dot(yb, wbr_ref[j])
        merged = term if merged is None else merged + term
    mix = _dot(merged.astype(BF16), wout_ref[...])
    o_ref[0] = _ln(ALPHA * x + mod[2:3, :] * mix) * lng_ref[...] + lnb_ref[...]


def merge(xa, mod, yd, yg, u, sf, sb, hf, hb, mo, s5d, wglu, bglu, mlg, wgate, bgate, wbr, wout, lng, lnb, ones_ml, nct):
    B, S, D = xa.shape
    tok = lambda n: pl.BlockSpec((1, TM, n), lambda b, i: (b, i, 0))
    consts = (s5d, wglu, bglu, mlg, wgate, bgate, wbr, wout, lng, lnb, ones_ml)
    return pl.pallas_call(
        _merge_kernel, out_shape=jax.ShapeDtypeStruct((B, S, D), F32), grid=(B, S // TM),
        in_specs=[tok(D), pl.BlockSpec((1, 1, 6, D), lambda b, i: (b, jnp.where(i >= nct, 1, 0), 0, 0))]
        + [tok(512)] * 2 + [pl.BlockSpec((TM, 512), lambda b, i: (i, b))] * 3 + [tok(512)] * 3
        + [_const_spec(a.shape) for a in consts],
        out_specs=tok(D),
        compiler_params=_cparams(("parallel", "parallel")), name="merge")(xa, mod, yd, yg, u, sf, sb, hf, hb, mo, *consts)


def _top16(*problems):
    iotas = {prob[0].shape: lax.broadcasted_iota(I32, prob[0].shape, 0).astype(F32) for prob in problems}

    def one(kk, s, val_ref, idx_ref, payload):
        R, ri = s.shape[0], iotas[s.shape]
        m = jnp.max(s, axis=0, keepdims=True)
        ix = jnp.min(jnp.where(s == m, ri, float(R)), axis=0, keepdims=True)
        hit = ri == ix
        val_ref[pl.ds(kk, 1), :] = m
        idx_ref[pl.ds(kk, 1), :] = ix if payload is None else jnp.max(jnp.where(hit, payload, -1.0), axis=0, keepdims=True)
        return jnp.where(hit, -jnp.inf, s)

    def body(kk, ss):
        return tuple(one(kk, s, *prob[1:]) for s, prob in zip(ss, problems))

    lax.fori_loop(0, PEER_TOPK, body, tuple(prob[0] for prob in problems))


_PEER_CAND_ROWS = ((0, 16),) + tuple((16 + 8 * (p - 1), 8) for p in range(1, 8))
_PEER_NCAND = _PEER_CAND_ROWS[-1][0] + 16


def _peer_route_kernel(x_ref, mod0_ref, mod1_ref, wq_ref, sk_ref, t_ref, ids_ref, gate_ref,
                       v1_sc, i1_sc, v2_sc, i2_sc, cand_sc, cidx_sc, top_sc, tid_sc):
    h = pl.program_id(1)

    @pl.when(h == 0)
    def _():
        for half, mod_ref in enumerate((mod0_ref, mod1_ref)):
            rows = slice(half * TM, (half + 1) * TM)
            mod = mod_ref[0, 0]
            t_ref[rows, :] = (_ln(x_ref[rows, :]) * (1.0 + mod[4:5, :]) + mod[3:4, :]).astype(BF16)

    q = _dot(t_ref[...], wq_ref[...]).astype(BF16)
    half = PEER_DQ // 2
    s1, s2 = _dot_nt(sk_ref[0], q[:, :half]), _dot_nt(sk_ref[1], q[:, half:])
    _top16((s1, v1_sc, i1_sc, None))
    _top16((s2, v2_sc, i2_sc, None))
    nk = float(PEER_NKEYS)
    for p, (lo, n) in enumerate(_PEER_CAND_ROWS):
        cand_sc[lo:lo + n, :] = v1_sc[p:p + 1, :] + v2_sc[0:n, :]
        cidx_sc[lo:lo + n, :] = i1_sc[p:p + 1, :] * nk + i2_sc[0:n, :]
    lo = _PEER_CAND_ROWS[-1][0] + _PEER_CAND_ROWS[-1][1]
    cand_sc[lo:lo + 8, :] = v1_sc[8:16, :] + v2_sc[0:1, :]
    cidx_sc[lo:lo + 8, :] = i1_sc[8:16, :] * nk + i2_sc[0:1, :]
    _top16((cand_sc[...], top_sc, tid_sc, cidx_sc[...]))
    top = top_sc[...]
    e = jnp.exp(top - top[0:1, :])
    gate_ref[...] = e / jnp.sum(e, axis=0, keepdims=True)
    ids_ref[...] = tid_sc[...].astype(I32)


def _half_tile_mod_spec(half, nt_per_sample, nct):
    def index(i, j):
        r = 2 * i + half
        return (r // nt_per_sample, jnp.where(r % nt_per_sample >= nct, 1, 0), 0, 0)
    return pl.BlockSpec((1, 1, 6, D_MODEL), index)


def peer_route(h1, mod, wq, subkeys, nt_per_sample, nct):
    T, D = h1.shape
    assert T % TM_E == 0 and TM_E == 2 * TM
    f = lambda n: pltpu.VMEM((n, TM_E), F32)
    return pl.pallas_call(
        _peer_route_kernel,
        out_shape=[jax.ShapeDtypeStruct((T, D), BF16),
                   jax.ShapeDtypeStruct((PEER_HEADS * PEER_TOPK, T), I32),
                   jax.ShapeDtypeStruct((PEER_HEADS * PEER_TOPK, T), F32)],
        grid=(T // TM_E, PEER_HEADS),
        in_specs=[pl.BlockSpec((TM_E, D), lambda i, h: (i, 0)),
                  _half_tile_mod_spec(0, nt_per_sample, nct), _half_tile_mod_spec(1, nt_per_sample, nct),
                  pl.BlockSpec((D, PEER_DQ), lambda i, h: (0, h)),
                  pl.BlockSpec((2, PEER_NKEYS, PEER_DQ // 2), lambda i, h: (0, 0, 0))],
        out_specs=[pl.BlockSpec((TM_E, D), lambda i, h: (i, 0)),
                   pl.BlockSpec((PEER_TOPK, TM_E), lambda i, h: (h, i)),
                   pl.BlockSpec((PEER_TOPK, TM_E), lambda i, h: (h, i))],
        scratch_shapes=[f(16), f(16), f(16), f(16), f(_PEER_NCAND), f(_PEER_NCAND), f(16), f(16)],
        compiler_params=_cparams(("parallel", "arbitrary")), name="peer_route")(h1, mod, mod, wq, subkeys)


def _peer_expert_kernel(t_ref, ids_ref, gate_ref, u_ref, v_ref, x_ref, mod0_ref, mod1_ref, lng_ref, lnb_ref, o_ref,
                        a_sc, wgt_sc, *, nchunk):
    j = pl.program_id(1)
    NK, CA = PEER_NKEYS, PEER_CA
    w_sc = a_sc

    @pl.when(j < nchunk)
    def _():
        a = _dot(t_ref[...], u_ref[0])
        a3 = jnp.stack([a[:, al * NK:(al + 1) * NK] for al in range(CA)], axis=0)
        a_sc[:, pl.ds(pl.multiple_of(j * CA, CA), CA), :] = pltpu.einshape("atb->tab", a3)

    @pl.when(j == nchunk - 1)
    def _():
        io_k = lax.broadcasted_iota(I32, (NK, NK), 0)
        io_a = lax.broadcasted_iota(I32, (NK, 2 * NK), 0)

        def masks(two):
            idp = ids_ref[two, :]
            ids = jnp.concatenate([idp[0:1], idp[1:2]], axis=1)
            blocks = [jnp.where(io_k == (idp[tok:tok + 1] & (NK - 1)), 1.0, 0.0).astype(BF16) for tok in range(2)]
            zero = jnp.zeros((NK, NK), BF16)
            onehot = jnp.concatenate([jnp.concatenate([blocks[0], zero], axis=1),
                                      jnp.concatenate([zero, blocks[1]], axis=1)], axis=0)
            return onehot, io_a == (ids >> 7)

        def pick(p, carry):
            two = pl.ds(pl.multiple_of(p * 2, 2), 2)
            onehot, r1 = masks(two)
            a2 = a_sc[two]
            at = jnp.concatenate([a2[0], a2[1]], axis=1).astype(BF16)
            picked = _dot(at, onehot)
            act = jnp.sum(jnp.where(r1, picked, 0.0), axis=0, keepdims=True)
            gp = gate_ref[two, :]
            w = jnp.concatenate([gp[0:1], gp[1:2]], axis=1) * _gelu(act)
            wgt_sc[two, :] = jnp.concatenate([w[:, :NK], w[:, NK:]], axis=0)
            return carry

        def scatter(p, carry):
            two = pl.ds(pl.multiple_of(p * 2, 2), 2)
            onehot, r1 = masks(two)
            wp = wgt_sc[two, :]
            w = jnp.concatenate([wp[0:1], wp[1:2]], axis=1)
            wt = _dot_nt(jnp.where(r1, w, 0.0).astype(BF16), onehot)
            w_sc[two] = jnp.stack([wt[:, :NK], wt[:, NK:]], axis=0)
            return carry

        lax.fori_loop(0, TM_E // 2, pick, 0, unroll=16)
        lax.fori_loop(0, TM_E // 2, scatter, 0, unroll=16)
        o_ref[...] = jnp.zeros_like(o_ref)

    @pl.when(j >= nchunk)
    def _():
        w3 = pltpu.einshape("tab->atb", w_sc[:, pl.ds(pl.multiple_of((j - nchunk) * CA, CA), CA), :])
        w = jnp.concatenate([w3[al] for al in range(CA)], axis=1).astype(BF16)
        o_ref[...] += _dot(w, v_ref[...])

    @pl.when(j == 2 * nchunk - 1)
    def _():
        for half, mod_ref in enumerate((mod0_ref, mod1_ref)):
            rows = slice(half * TM, (half + 1) * TM)
            y = ALPHA * x_ref[rows, :] + mod_ref[0, 0][5:6, :] * o_ref[rows, :]
            o_ref[rows, :] = _ln(y) * lng_ref[...] + lnb_ref[...]


def peer_experts(t, ids, gates, emb_u, emb_v, h1, mod, lng, lnb, nt_per_sample, nct):
    T, D = h1.shape
    E = emb_v.shape[0]
    ce = PEER_CA * PEER_NKEYS
    nchunk = E // ce
    emb_ut = emb_u.reshape(nchunk, ce, D).transpose(0, 2, 1)
    assert T % TM_E == 0 and TM_E == 2 * TM
    tok = lambda n, **kw: pl.BlockSpec((TM_E, n), lambda i, j: (i, 0), **kw)
    mod_spec = lambda half: _half_tile_mod_spec(half, nt_per_sample, nct)
    once = dict(pipeline_mode=pl.Buffered(1))
    return pl.pallas_call(
        functools.partial(_peer_expert_kernel, nchunk=nchunk),
        out_shape=jax.ShapeDtypeStruct((T, D), F32), grid=(T // TM_E, 2 * nchunk),
        in_specs=[tok(D, **once), tok(LANES, **once), tok(LANES, **once),
                  pl.BlockSpec((1, D, ce), lambda i, j: (jnp.minimum(j, nchunk - 1), 0, 0)),
                  pl.BlockSpec((ce, D), lambda i, j: (jnp.maximum(j - nchunk, 0), 0)),
                  tok(D, **once), mod_spec(0), mod_spec(1),
                  pl.BlockSpec((1, D), lambda i, j: (0, 0)), pl.BlockSpec((1, D), lambda i, j: (0, 0))],
        out_specs=tok(D),
        scratch_shapes=[pltpu.VMEM((TM_E, PEER_NKEYS, PEER_NKEYS), F32), pltpu.VMEM((TM_E, LANES), F32)],
        compiler_params=_cparams(("parallel", "arbitrary")), name="peer_experts")(t, ids, gates, emb_ut, emb_v, h1, mod, mod, lng, lnb)


_ROPE_IDX = np.arange(HD)
_ROPE_PERM = np.where(_ROPE_IDX % 32 < 16, _ROPE_IDX + 16, _ROPE_IDX - 16)
_ROPE_SIGN = np.where(_ROPE_IDX % 32 < 16, -1.0, 1.0).astype(np.float32)


def _rope_partner(w):
    n = w.shape[1] // HD
    perm = np.concatenate([h * HD + _ROPE_PERM for h in range(n)])
    return w[:, perm] * jnp.asarray(np.tile(_ROPE_SIGN, n))


def _rope_tables(L, nctx, gq, gk):
    rows = L // GRID_W
    row = jnp.repeat(jnp.arange(rows, dtype=F32), GRID_W)
    col = jnp.tile(jnp.arange(GRID_W, dtype=F32), rows)
    nf = HD // 4
    inv = ROPE_THETA ** (-jnp.arange(nf, dtype=F32) / nf)
    ar, ac = row[:, None] * inv, col[:, None] * inv
    cos = jnp.concatenate([jnp.cos(ar), jnp.cos(ar), jnp.cos(ac), jnp.cos(ac)], axis=1)
    sin = jnp.concatenate([jnp.sin(ar), jnp.sin(ar), jnp.sin(ac), jnp.sin(ac)], axis=1)
    cos = jnp.concatenate([jnp.ones((nctx, HD), F32), cos], axis=0)
    sin = jnp.concatenate([jnp.zeros((nctx, HD), F32), sin], axis=0)
    scale = HD ** -0.5
    one = jnp.ones((HD,), F32)
    tabs = []
    for g, sc in ((one, scale), (one, 1.0), (gq, scale), (gk, 1.0)):
        tabs += [cos * (g * sc), sin * (g[_ROPE_PERM] * sc)]
    return jnp.tile(jnp.stack(tabs), (1, 1, 2))


def _s5_tables(a_re, a_im, log_dt, b_re, b_im, c_re, c_im, nb):
    dt = jnp.exp(log_dt)[..., None]
    mag = jnp.exp(a_re * dt)
    lr, li = mag * jnp.cos(a_im * dt), mag * jnp.sin(a_im * dt)
    den = a_re * a_re + a_im * a_im
    cr = ((lr - 1) * a_re + li * a_im) / den
    ci = (li * a_re - (lr - 1) * a_im) / den
    br = cr[..., None] * b_re - ci[..., None] * b_im
    bi = cr[..., None] * b_im + ci[..., None] * b_re
    G2, N, C = S5_GROUPS // 2, S5_STATE, S5_GROUP
    eye = jnp.eye(G2, dtype=F32)

    def blockdiag_in(m):
        m = m.reshape(2, 2, G2, N, C)
        return jnp.einsum('dhgnc,gk->dhgckn', m, eye).reshape(2, 2, G2 * C, G2 * N)

    def blockdiag_out(m):
        m = m.reshape(2, 2, G2, C, N)
        return jnp.einsum('dhgcn,gk->dhgnkc', m, eye).reshape(2, 2, G2 * N, G2 * C)

    bblk = jnp.concatenate([blockdiag_in(br), blockdiag_in(bi)], axis=-1).astype(BF16)
    cblk = jnp.concatenate([blockdiag_out(c_re), blockdiag_out(-c_im)], axis=-2).astype(BF16)
    lam = jnp.stack([lr, li], axis=1).reshape(2, 2, 2, G2 * N)
    lam = jnp.tile(lam[:, :, None], (1, 1, nb, 1, 1)).reshape(2, 2, 2 * nb, G2 * N)
    return bblk, cblk, lam


def _blockdiag_ones(group, n=LANES):
    i = np.arange(n) // group
    return jnp.asarray((i[:, None] == i[None, :]).astype(np.float32), dtype=BF16)


def kernel(x, c, ctx, c_ctx, ada_w, ada_b, w_in, b_gate, diff_lam, diff_norm_g, gqa_qnorm_g, gqa_knorm_g,
           s5_a_re, s5_a_im, s5_log_dt, s5_b_re, s5_b_im, s5_c_re, s5_c_im, s5_d, s5_w_glu, s5_b_glu,
           ml_conv_w, ml_conv_b, ml_gate_b, ml_norm_g, w_branch, w_out, ln_mix_g, ln_mix_b, ln_ffn_g, ln_ffn_b,
           peer_wq, peer_subkeys, peer_u, peer_v):
    B, L, D = x.shape
    nctx = ctx.shape[1]
    S = nctx + L
    assert D == D_MODEL and nctx % TM == 0 and L % TM == 0 and L % GRID_W == 0
    nt, nct = S // TM, nctx // TM
    depth = ada_w.shape[0]

    h = jnp.concatenate([ctx, x], axis=1)
    R = -(-(B + 1) // 8) * 8
    cond = jnp.zeros((R, D), F32).at[:B].set(c).at[B].set(c_ctx)
    ones64, ones128 = _blockdiag_ones(HD), _blockdiag_ones(ML_HD, ML_HEADS * ML_HD)
    o = IN_OFFS

    for l in range(depth):
        lam_init = 0.8 - 0.6 * math.exp(-0.3 * l)
        m = ada_modulation(cond, ada_w[l].astype(BF16), ada_b[l][None, :])
        mod = jnp.stack([jnp.broadcast_to(m[B], (B, 6 * D)), m[:B]], axis=1).reshape(B, 2, 6, D)

        w = w_in[l]
        seg = lambda i: w[:, o[i]:o[i + 1]]
        dup = lambda t: jnp.concatenate([t[:, :HD], t[:, :HD], t[:, HD:], t[:, HD:]], axis=1)
        def with_ones_slots(t, width):
            heads = t.reshape(D, -1, width)
            return jnp.concatenate([heads, jnp.zeros_like(heads)], axis=2).reshape(D, -1)
        w_a = jnp.concatenate([seg(0), _rope_partner(seg(0)), seg(1), _rope_partner(seg(1)), seg(2),
                               seg(9), _rope_partner(seg(9)), dup(seg(10)), dup(_rope_partner(seg(10))),
                               with_ones_slots(seg(11), HD)], axis=1).astype(BF16)
        w_b = jnp.concatenate([seg(3), seg(4), seg(5), seg(6), seg(7), seg(8),
                               jnp.zeros((D, LANES - 16), F32)], axis=1).astype(BF16)
        gate_b = jnp.concatenate([ml_gate_b[l], jnp.zeros((LANES - 16,), F32)])[None, :]
        tab = _rope_tables(L, nctx, gqa_qnorm_g[l], gqa_knorm_g[l])

        dq, dk, dv, gq, gk, gv = proj_attn(h, mod, w_a, tab, ones64, nct)
        u, mqk, mv, mo, mg = proj_seq(h, mod, w_b, gate_b, nct)

        yd = diff_attention(dq, dk, dv, diff_lam[l], diff_norm_g[l][None, :], nctx, lam_init)
        yg = gqa_attention(gq, gk, gv, nctx)

        bblk, cblk, lam = _s5_tables(s5_a_re[l], s5_a_im[l], s5_log_dt[l], s5_b_re[l], s5_b_im[l],
                                     s5_c_re[l], s5_c_im[l], B)
        sf, sb = s5_scan(u, bblk, cblk, lam, nctx)

        mq, mk = ml_prep(mqk, ml_conv_w[l], ml_conv_b[l][None, :], nctx)
        hf = mlstm_scan(mq, mk, mv, mg, nctx, False)
        hb = mlstm_scan(mq, mk, mv, mg, nctx, True)

        h1 = merge(h, mod, yd, yg, u, sf, sb, hf, hb, mo,
                   s5_d[l][None, :], s5_w_glu[l].astype(BF16), s5_b_glu[l][None, :], ml_norm_g[l][None, :],
                   seg(12).astype(BF16), b_gate[l][None, :], w_branch[l].astype(BF16), w_out[l].astype(BF16),
                   ln_mix_g[l][None, :], ln_mix_b[l][None, :], ones128, nct)

        h1 = h1.reshape(B * S, D)
        t, ids, gates = peer_route(h1, mod, peer_wq[l].astype(BF16), peer_subkeys[l].astype(BF16), nt, nct)
        h = peer_experts(t, ids.T, gates.T, peer_u[l].astype(BF16), peer_v[l].astype(BF16),
                         h1, mod, ln_ffn_g[l][None, :], ln_ffn_b[l][None, :], nt, nct).reshape(B, S, D)
    return h[:, nctx:]
```

```python
import functools
import math

import numpy as np
import jax
import jax.numpy as jnp
from jax import lax
from jax.experimental import pallas as pl
from jax.experimental.pallas import tpu as pltpu

F32 = jnp.float32
BF16 = jnp.bfloat16
I32 = jnp.int32

D_MODEL = 1024
DEPTH = 2
GRID_W = 64
ROPE_THETA = 10000.0
LN_EPS = 1e-6
HD = 64
DIFF_HEADS = 4
DIFF_HPS = 2
GQA_HEADS = 8
GQA_KV = 2
GQA_GPS = 2
S5_GROUP = 16
S5_GROUPS = 32
S5_STATE = 64
ML_HEADS = 4
ML_HD = 128
N_BRANCH = 4
W_BRANCH = 512
PEER_HEADS = 8
PEER_NKEYS = 128
PEER_TOPK = 16
PEER_DQ = 256
ALPHA = (2 * DEPTH) ** 0.25

LANES = 128
TM = 256
TM_E = 2 * TM
TQ = 256
TK = 2048
S5_TC = 128
ML_T = 256
ML_LOCKSTEP = 4
PEER_CA = 8
VMEM_LIMIT = 56 * 1024 * 1024

IN_SPLITS = (512, 512, 512, 512, 512, 512, 512, 512, 16, 512, 128, 128, N_BRANCH * D_MODEL)
IN_OFFS = tuple(int(v) for v in np.cumsum((0,) + IN_SPLITS))


def _cparams(sem):
    return pltpu.CompilerParams(dimension_semantics=sem, vmem_limit_bytes=VMEM_LIMIT)


def _const_spec(shape):
    nd = len(shape)
    return pl.BlockSpec(shape, lambda *_: (0,) * nd, pipeline_mode=pl.Buffered(1))


def _ln(x):
    xc = x - jnp.mean(x, axis=-1, keepdims=True)
    return xc * lax.rsqrt(jnp.mean(xc * xc, axis=-1, keepdims=True) + LN_EPS)


def _sigmoid(x):
    return 1.0 / (1.0 + jnp.exp(-x))


def _gelu(x):
    return 0.5 * x * (1.0 + lax.erf(x * (2.0 ** -0.5)))


def _dot(a, b):
    return jnp.dot(a, b, preferred_element_type=F32)


def _dot_nt(a, b):
    return lax.dot_general(a, b, (((1,), (1,)), ((), ())), preferred_element_type=F32)


def _dot_tn(a, b):
    return lax.dot_general(a, b, (((0,), (0,)), ((), ())), preferred_element_type=F32)


def _ada_kernel(c_ref, w_ref, b_ref, o_ref):
    c = c_ref[...]
    o_ref[...] = _dot((c * _sigmoid(c)).astype(BF16), w_ref[...]) + b_ref[...]


def ada_modulation(cond, w, b):
    R, D = cond.shape
    N = w.shape[1]
    tn = 1536
    return pl.pallas_call(
        _ada_kernel, out_shape=jax.ShapeDtypeStruct((R, N), F32), grid=(N // tn,),
        in_specs=[pl.BlockSpec((R, D), lambda j: (0, 0)), pl.BlockSpec((D, tn), lambda j: (0, j)),
                  pl.BlockSpec((1, tn), lambda j: (0, j))],
        out_specs=pl.BlockSpec((R, tn), lambda j: (0, j)),
        compiler_params=_cparams(("arbitrary",)), name="ada")(cond, w, b)


def _proj_attn_kernel(x_ref, mod_ref, w_ref, tab_ref, ones_ref,
                      dq_ref, dk_ref, dv_ref, gq_ref, gk_ref, gv_ref):
    mod = mod_ref[0, 0]
    xm = (_ln(x_ref[0]) * (1.0 + mod[1:2, :]) + mod[0:1, :]).astype(BF16)

    def mm(lo, n):
        return _dot(xm, w_ref[:, lo:lo + n])

    def rope_store(ref, lo, n, ci, norm):
        t, tp = mm(lo, n), mm(lo + n, n)
        c, s = tab_ref[ci], tab_ref[ci + 1]
        for j in range(n // LANES):
            sl = slice(j * LANES, (j + 1) * LANES)
            tb = t[:, sl]
            y = tb * c + tp[:, sl] * s
            if norm:
                ss = _dot((tb * tb).astype(BF16), ones_ref[...])
                y = y * lax.rsqrt(ss * (1.0 / HD) + LN_EPS)
            ref[0, :, sl] = y.astype(ref.dtype)

    def value_store(ref, lo, n, width):
        lane = lax.broadcasted_iota(I32, (1, n), 1)
        ref[0] = (mm(lo, n) + jnp.where(lane % (2 * width) >= width, 1.0, 0.0)).astype(ref.dtype)

    rope_store(dq_ref, 0, 512, 0, False)
    rope_store(dk_ref, 1024, 512, 2, False)
    dv_ref[0] = mm(2048, 512).astype(dv_ref.dtype)
    rope_store(gq_ref, 2560, 512, 4, True)
    rope_store(gk_ref, 3584, 256, 6, True)
    value_store(gv_ref, 4096, 256, HD)


def proj_attn(xa, mod, w_a, tab, ones_bd, nct):
    B, S, D = xa.shape
    tok = lambda n: pl.BlockSpec((1, TM, n), lambda b, i: (b, i, 0))
    widths = (512, 512, 512, 512, 256, 256)
    outs = [jax.ShapeDtypeStruct((B, S, n), BF16) for n in widths]
    return pl.pallas_call(
        _proj_attn_kernel, out_shape=outs, grid=(B, S // TM),
        in_specs=[tok(D),
                  pl.BlockSpec((1, 1, 6, D), lambda b, i: (b, jnp.where(i >= nct, 1, 0), 0, 0)),
                  _const_spec(w_a.shape),
                  pl.BlockSpec((8, TM, LANES), lambda b, i: (0, i, 0)),
                  _const_spec(ones_bd.shape)],
        out_specs=[tok(n) for n in widths],
        compiler_params=_cparams(("parallel", "parallel")), name="proj_attn")(xa, mod, w_a, tab, ones_bd)


def _proj_seq_kernel(x_ref, mod_ref, w_ref, gb_ref, u_ref, qk_ref, v_ref, o_ref, g_ref):
    mod = mod_ref[0, 0]
    xm = (_ln(x_ref[0]) * (1.0 + mod[1:2, :]) + mod[0:1, :]).astype(BF16)
    u_ref[...] = _dot(xm, w_ref[:, 0:512])
    qk_ref[0] = _dot(xm, w_ref[:, 512:1536])
    v_ref[0] = _dot(xm, w_ref[:, 1536:2048])
    o_ref[0] = _dot(xm, w_ref[:, 2048:2560])
    g_ref[0] = _dot(xm, w_ref[:, 2560:2688]) + gb_ref[...]


def proj_seq(xa, mod, w_b, gate_b, nct):
    B, S, D = xa.shape
    tok = lambda n: pl.BlockSpec((1, TM, n), lambda b, i: (b, i, 0))
    widths = (1024, 512, 512, LANES)
    return pl.pallas_call(
        _proj_seq_kernel,
        out_shape=[jax.ShapeDtypeStruct((S, B * 512), F32)] + [jax.ShapeDtypeStruct((B, S, n), F32) for n in widths],
        grid=(B, S // TM),
        in_specs=[tok(D),
                  pl.BlockSpec((1, 1, 6, D), lambda b, i: (b, jnp.where(i >= nct, 1, 0), 0, 0)),
                  _const_spec(w_b.shape), _const_spec(gate_b.shape)],
        out_specs=[pl.BlockSpec((TM, 512), lambda b, i: (i, b))] + [tok(n) for n in widths],
        compiler_params=_cparams(("parallel", "parallel")), name="proj_seq")(xa, mod, w_b, gate_b)


def _flash(qqs, k_ref, v_ref, nctx, n_lat, tk, dv):
    P = len(qqs)
    R, W = qqs[0].shape[0], v_ref.shape[2] // P
    mxu_sums = dv < W

    def finish(carry, s, v):
        m, l, acc = carry
        m_new = jnp.maximum(m, jnp.max(s, axis=1, keepdims=True))
        alpha = jnp.exp(m - m_new)
        if mxu_sums:
            p = jnp.exp((s - m_new).astype(BF16))
        else:
            p = jnp.exp(s - m_new)
            l = alpha * l + jnp.sum(p, axis=1, keepdims=True)
        return m_new, l, alpha * acc + _dot(p.astype(BF16), v)

    carries = [(jnp.full((R, 1), -jnp.inf, F32), jnp.zeros((R, 1), F32), jnp.zeros((R, W), F32))] * P
    chunks = [pl.ds(0, nctx)] + [pl.ds(nctx + c * tk, tk) for c in range(n_lat)]
    for rows in chunks:
        scores = [_dot_nt(qqs[p], k_ref[0, rows, p * LANES:(p + 1) * LANES]) for p in range(P)]
        carries = [finish(carries[p], scores[p], v_ref[0, rows, p * W:(p + 1) * W]) for p in range(P)]
    return [acc[:, :dv] / (acc[:, dv:dv + 1] if mxu_sums else l) for _, l, acc in carries]


def _ctx_or_all(i, nctx, n_lat, attend):
    @pl.when(i < nctx // TQ)
    def _():
        attend(0)

    @pl.when(i >= nctx // TQ)
    def _():
        attend(n_lat)


def _diff_attn_kernel(q_ref, k_ref, v_ref, lam_ref, g_ref, o_ref, *, nctx, n_lat, tk, lam_init):
    i = pl.program_id(2)
    lane = lax.broadcasted_iota(I32, (TQ, LANES), 1)
    zero = jnp.zeros((TQ, LANES), q_ref.dtype)
    qqs = []
    for h in range(DIFF_HPS):
        q = q_ref[0, :, h * LANES:(h + 1) * LANES]
        qqs.append(jnp.concatenate([jnp.where(lane < HD, q, zero), jnp.where(lane >= HD, q, zero)], axis=0))
    lv = lam_ref[...]
    lam = (jnp.exp(jnp.sum(lv[0:1] * lv[1:2], axis=1, keepdims=True))
           - jnp.exp(jnp.sum(lv[2:3] * lv[3:4], axis=1, keepdims=True)) + lam_init)

    def attend(n):
        for h, o in enumerate(_flash(qqs, k_ref, v_ref, nctx, n, tk, 2 * HD)):
            d = o[:TQ] - lam * o[TQ:]
            y = d * lax.rsqrt(jnp.mean(d * d, axis=-1, keepdims=True) + LN_EPS)
            o_ref[0, :, h * LANES:(h + 1) * LANES] = (y * g_ref[...] * (1.0 - lam_init)).astype(o_ref.dtype)

    _ctx_or_all(i, nctx, n_lat, attend)


def diff_attention(q, k, v, lam_vec, norm_g, nctx, lam_init):
    B, S, _ = q.shape
    tk = math.gcd(S - nctx, TK)
    kern = functools.partial(_diff_attn_kernel, nctx=nctx, n_lat=(S - nctx) // tk, tk=tk, lam_init=lam_init)
    return pl.pallas_call(
        kern, out_shape=jax.ShapeDtypeStruct((B, S, 512), BF16), grid=(B, DIFF_HEADS // DIFF_HPS, S // TQ),
        in_specs=[pl.BlockSpec((1, TQ, DIFF_HPS * LANES), lambda b, h, i: (b, i, h)),
                  pl.BlockSpec((1, S, DIFF_HPS * LANES), lambda b, h, i: (b, 0, h)),
                  pl.BlockSpec((1, S, DIFF_HPS * LANES), lambda b, h, i: (b, 0, h)),
                  pl.BlockSpec((4, HD), lambda b, h, i: (0, 0)),
                  pl.BlockSpec((1, LANES), lambda b, h, i: (0, 0))],
        out_specs=pl.BlockSpec((1, TQ, DIFF_HPS * LANES), lambda b, h, i: (b, i, h)),
        compiler_params=_cparams(("parallel", "parallel", "parallel")), name="diff_attn")(q, k, v, lam_vec, norm_g)


def _gqa_attn_kernel(q_ref, k_ref, v_ref, o_ref, *, nctx, n_lat, tk):
    i = pl.program_id(2)
    lane = lax.broadcasted_iota(I32, (TQ, LANES), 1)
    zero = jnp.zeros((TQ, LANES), q_ref.dtype)
    qqs = []
    for g in range(GQA_GPS):
        parts = []
        for j in range(2 * g, 2 * g + 2):
            blk = q_ref[0, :, j * LANES:(j + 1) * LANES]
            parts += [jnp.where(lane < HD, blk, zero), jnp.where(lane >= HD, blk, zero)]
        qqs.append(jnp.concatenate(parts, axis=0))

    def attend(n):
        for g, o in enumerate(_flash(qqs, k_ref, v_ref, nctx, n, tk, HD)):
            heads = jnp.concatenate([o[h * TQ:(h + 1) * TQ] for h in range(4)], axis=1)
            o_ref[0, :, g * 2 * LANES:(g + 1) * 2 * LANES] = heads.astype(o_ref.dtype)

    _ctx_or_all(i, nctx, n_lat, attend)


def gqa_attention(q, k, v, nctx):
    B, S, _ = q.shape
    tk = math.gcd(S - nctx, TK)
    kern = functools.partial(_gqa_attn_kernel, nctx=nctx, n_lat=(S - nctx) // tk, tk=tk)
    return pl.pallas_call(
        kern, out_shape=jax.ShapeDtypeStruct((B, S, 512), BF16), grid=(B, GQA_KV // GQA_GPS, S // TQ),
        in_specs=[pl.BlockSpec((1, TQ, GQA_GPS * 2 * LANES), lambda b, g, i: (b, i, g)),
                  pl.BlockSpec((1, S, GQA_GPS * LANES), lambda b, g, i: (b, 0, g)),
                  pl.BlockSpec((1, S, GQA_GPS * LANES), lambda b, g, i: (b, 0, g))],
        out_specs=pl.BlockSpec((1, TQ, GQA_GPS * 2 * LANES), lambda b, g, i: (b, i, g)),
        compiler_params=_cparams(("parallel", "parallel", "parallel")), name="gqa_attn")(q, k, v)


def _s5_kernel(uf_ref, ub_ref, bblk_ref, cblk_ref, lam_ref, yf_ref, yb_ref, buf_f, buf_b, st_ref):
    H = S5_GROUPS * S5_STATE // 2
    R = buf_f.shape[0]
    RB = 256
    TB = RB // 8

    @pl.when(pl.program_id(0) == 0)
    def _():
        st_ref[...] = jnp.zeros_like(st_ref)

    half0 = (lax.broadcasted_iota(I32, (RB, 1), 0) & 1) == 0

    for d, (u_ref, buf) in enumerate(((uf_ref, buf_f), (ub_ref, buf_b))):
        for rb in range(R // RB):
            rows = slice(rb * RB, (rb + 1) * RB)
            ut = u_ref[rb * TB:(rb + 1) * TB, :]
            u3 = jnp.stack([ut[:, s * 256:(s + 1) * 256] for s in range(8)], axis=0)
            uh = pltpu.einshape("stc->tsc", u3).reshape(RB, 256).astype(BF16)
            for cols in (slice(0, H), slice(H, 2 * H)):
                buf[rows, cols] = jnp.where(half0, _dot(uh, bblk_ref[d, 0, :, cols]), _dot(uh, bblk_ref[d, 1, :, cols]))

    lfr, lfi, lbr, lbi = lam_ref[0, 0], lam_ref[0, 1], lam_ref[1, 0], lam_ref[1, 1]

    def step(t, carry):
        fr, fi, br, bi = carry
        rf = pl.ds(pl.multiple_of(t * 8, 8), 8)
        x = buf_f[rf, :]
        nfr = lfr * fr - lfi * fi + x[:, :H]
        nfi = lfr * fi + lfi * fr + x[:, H:]
        buf_f[rf, :] = jnp.concatenate([nfr, nfi], axis=1)
        rb = pl.ds(pl.multiple_of((S5_TC - 1 - t) * 8, 8), 8)
        z = buf_b[rb, :]
        nbr = lbr * br - lbi * bi + z[:, :H]
        nbi = lbr * bi + lbi * br + z[:, H:]
        buf_b[rb, :] = jnp.concatenate([nbr, nbi], axis=1)
        return nfr, nfi, nbr, nbi

    fin = lax.fori_loop(0, S5_TC, step, (st_ref[0], st_ref[1], st_ref[2], st_ref[3]))
    for j in range(4):
        st_ref[j] = fin[j]

    for d, (y_ref, buf) in enumerate(((yf_ref, buf_f), (yb_ref, buf_b))):
        for rb in range(R // RB):
            rows = slice(rb * RB, (rb + 1) * RB)
            h = buf[rows, :].astype(BF16)
            y = jnp.where(half0, _dot(h, cblk_ref[d, 0]), _dot(h, cblk_ref[d, 1]))
            y3 = pltpu.einshape("tsc->stc", y.reshape(TB, 8, 256))
            y_ref[rb * TB:(rb + 1) * TB, :] = jnp.concatenate([y3[s] for s in range(8)], axis=1)


def s5_scan(u, bblk, cblk, lam, nctx):
    S, W = u.shape
    assert W == 8 * 256, "the scan packs (sample, half) pairs into the eight sublanes of a vreg"
    nch, nc0 = S // S5_TC, nctx // S5_TC

    def bwd(i):
        return jnp.where(i < nc0, nc0 - 1 - i, (nch - 1) - (i - nc0))
    H2 = S5_GROUPS * S5_STATE
    R = S5_TC * 8
    blk = lambda f: pl.BlockSpec((S5_TC, W), f)
    return pl.pallas_call(
        _s5_kernel, out_shape=[jax.ShapeDtypeStruct(u.shape, F32)] * 2, grid=(nch,),
        in_specs=[blk(lambda i: (i, 0)), blk(lambda i: (bwd(i), 0)),
                  _const_spec(bblk.shape), _const_spec(cblk.shape), _const_spec(lam.shape)],
        out_specs=[blk(lambda i: (i, 0)), blk(lambda i: (bwd(i), 0))],
        scratch_shapes=[pltpu.VMEM((R, H2), F32), pltpu.VMEM((R, H2), F32), pltpu.VMEM((4, 8, H2 // 2), F32)],
        compiler_params=_cparams(("arbitrary",)), name="s5_scan")(u, u, bblk, cblk, lam)


def _ml_prep_kernel(x_ref, prev_ref, next_ref, w_ref, b_ref, q_ref, k_ref, *, seg_starts, seg_ends):
    i = pl.program_id(1)
    x = x_ref[0]
    row = lax.broadcasted_iota(I32, x.shape, 0)
    first = functools.reduce(jnp.logical_or, [i == s for s in seg_starts])
    last = functools.reduce(jnp.logical_or, [i == s for s in seg_ends])
    pr = jnp.where(first, 0.0, prev_ref[0, 7:8, :])
    nx = jnp.where(last, 0.0, next_ref[0, 0:1, :])
    xp = jnp.where(row == 0, pr, pltpu.roll(x, 1, 0))
    xn = jnp.where(row == TM - 1, nx, pltpu.roll(x, TM - 1, 0))
    w = w_ref[...]
    y = b_ref[...] + xp * w[0:1, :] + x * w[1:2, :] + xn * w[2:3, :]
    y = y * _sigmoid(y)
    q_ref[0] = y[:, :512].astype(q_ref.dtype)
    k_ref[0] = (y[:, 512:] * (ML_HD ** -0.5)).astype(k_ref.dtype)


def ml_prep(qk, conv_w, conv_b, nctx):
    B, S, W = qk.shape
    nt, nct, r8 = S // TM, nctx // TM, TM // 8
    kern = functools.partial(_ml_prep_kernel, seg_starts=(0, nct), seg_ends=(nct - 1, nt - 1))
    return pl.pallas_call(
        kern, out_shape=[jax.ShapeDtypeStruct((B, S, 512), BF16)] * 2, grid=(B, nt),
        in_specs=[pl.BlockSpec((1, TM, W), lambda b, i: (b, i, 0)),
                  pl.BlockSpec((1, 8, W), lambda b, i: (b, jnp.maximum(i * r8 - 1, 0), 0)),
                  pl.BlockSpec((1, 8, W), lambda b, i: (b, jnp.minimum((i + 1) * r8, S // 8 - 1), 0)),
                  pl.BlockSpec((3, W), lambda b, i: (0, 0)), pl.BlockSpec((1, W), lambda b, i: (0, 0))],
        out_specs=[pl.BlockSpec((1, TM, 512), lambda b, i: (b, i, 0))] * 2,
        compiler_params=_cparams(("parallel", "parallel")), name="ml_prep")(qk, qk, qk, conv_w, conv_b)


def _log_sigmoid(x):
    return jnp.minimum(x, 0.0) - jnp.log1p(jnp.exp(-jnp.abs(x)))


def _mlstm_kernel(q_ref, k_ref, v_ref, g_ref, h_ref, c_sc, n_sc, m_sc, *, reverse):
    T = ML_T
    gi, gf = (8, 12) if reverse else (0, 4)

    @pl.when(pl.program_id(0) == 0)
    def _():
        c_sc[...] = jnp.zeros_like(c_sc)
        n_sc[...] = jnp.zeros_like(n_sc)
        m_sc[...] = jnp.zeros_like(m_sc)

    r = lax.broadcasted_iota(I32, (T, T), 0)
    c = lax.broadcasted_iota(I32, (T, T), 1)
    mask = (c >= r) if reverse else (c <= r)
    tri = jnp.where(mask, 1.0, 0.0)
    hp = lax.Precision.HIGHEST
    end = 0 if reverse else T - 1
    def gates(b):
        g = g_ref[b]
        gt = g.T
        bcol_all = jnp.dot(tri, _log_sigmoid(g), preferred_element_type=F32, precision=hp)
        brow_all = lax.dot_general(_log_sigmoid(gt[0:16]), tri, (((1,), (1,)), ((), ())),
                                   preferred_element_type=F32, precision=hp)
        return g, gt, bcol_all, brow_all

    def chain(b, hh, g, gt, bcol_all, brow_all):
        st = b * ML_HEADS + hh
        sl = slice(hh * ML_HD, (hh + 1) * ML_HD)
        q, k, v = q_ref[b, :, sl], k_ref[b, :, sl], v_ref[b, :, sl]
        bcol, brow = bcol_all[:, gf + hh:gf + hh + 1], brow_all[gf + hh:gf + hh + 1, :]
        icol, irow = g[:, gi + hh:gi + hh + 1], gt[gi + hh:gi + hh + 1, :]
        m_old = m_sc[st][:, 0:1]
        n_old = n_sc[st]
        c_old = c_sc[st]
        logw = jnp.where(mask, bcol - brow + irow, -jnp.inf)
        m_inter = bcol + m_old
        m_t = jnp.maximum(m_inter, jnp.max(logw, axis=1, keepdims=True))
        yield
        s = _dot_nt(q, k) * jnp.exp(logw - m_t)
        inter = jnp.exp(m_inter - m_t)
        yield
        num = _dot(s.astype(BF16), v.astype(BF16)) + inter * _dot_nt(q, c_old.astype(BF16))
        den = jnp.sum(s, axis=1, keepdims=True) + inter * jnp.sum(q.astype(F32) * n_old, axis=1, keepdims=True)
        h_ref[b, :, sl] = num / jnp.maximum(jnp.abs(den), jnp.exp(-m_t))
        yield
        b_end = bcol[end:end + 1, :]
        g_row, g_col = b_end - brow + irow, b_end - bcol + icol
        m_new = jnp.maximum(b_end + m_old, jnp.max(g_row, axis=1, keepdims=True))
        decay = jnp.exp(b_end + m_old - m_new)
        wk = jnp.exp(g_col - m_new)
        c_sc[st] = decay * c_old + _dot_tn((v * wk).astype(BF16), k)
        n_sc[st] = decay * n_old + jnp.sum(k.astype(F32) * wk, axis=0, keepdims=True)
        m_sc[st] = jnp.broadcast_to(m_new, (1, LANES))

    nb = q_ref.shape[0]
    for b0 in range(0, nb, ML_LOCKSTEP):
        chains = []
        for b in range(b0, min(b0 + ML_LOCKSTEP, nb)):
            shared = gates(b)
            chains += [chain(b, hh, *shared) for hh in range(ML_HEADS)]
        for _ in range(4):
            for ch in chains:
                next(ch, None)


def mlstm_scan(q, k, v, g, nctx, reverse):
    B, S, W = q.shape
    nch, nc0 = S // ML_T, nctx // ML_T

    def order(i):
        return jnp.where(i < nc0, nc0 - 1 - i, (nch - 1) - (i - nc0)) if reverse else i
    blk = lambda n: pl.BlockSpec((B, ML_T, n), lambda i: (0, order(i), 0))
    nst = B * ML_HEADS
    return pl.pallas_call(
        functools.partial(_mlstm_kernel, reverse=reverse),
        out_shape=jax.ShapeDtypeStruct((B, S, W), F32), grid=(nch,),
        in_specs=[blk(W), blk(W), blk(W), blk(LANES)], out_specs=blk(W),
        scratch_shapes=[pltpu.VMEM((nst, ML_HD, ML_HD), F32), pltpu.VMEM((nst, 1, ML_HD), F32),
                        pltpu.VMEM((nst, 1, LANES), F32)],
        compiler_params=_cparams(("arbitrary",)), name="mlstm_bwd" if reverse else "mlstm_fwd")(q, k, v, g)


def _merge_kernel(x_ref, mod_ref, yd_ref, yg_ref, u_ref, sf_ref, sb_ref, hf_ref, hb_ref, mo_ref,
                  s5d_ref, wglu_ref, bglu_ref, mlg_ref, wgate_ref, bgate_ref, wbr_ref, wout_ref,
                  lng_ref, lnb_ref, ones_ref, o_ref):
    x = x_ref[0]
    mod = mod_ref[0, 0]
    xm = (_ln(x) * (1.0 + mod[1:2, :]) + mod[0:1, :]).astype(BF16)
    ys = u_ref[...] * s5d_ref[...] + sf_ref[...] + sb_ref[...]
    z = _dot(_gelu(ys).astype(BF16), wglu_ref[...]) + bglu_ref[...]
    ys = z[:, :512] * _sigmoid(z[:, 512:])
    hm = hf_ref[0] + hb_ref[0]
    ss = _dot((hm * hm).astype(BF16), ones_ref[...])
    ym = hm * lax.rsqrt(ss * (1.0 / ML_HD) + LN_EPS) * mlg_ref[...] * _sigmoid(mo_ref[0])
    branches = (yd_ref[0], ys.astype(BF16), ym.astype(BF16), yg_ref[0])
    merged = None
    for j, yb in enumerate(branches):
        gate = _sigmoid(_dot(xm, wgate_ref[:, j * D_MODEL:(j + 1) * D_MODEL]) + bgate_ref[:, j * D_MODEL:(j + 1) * D_MODEL])
        term = gate * _dot(yb, wbr_ref[j])
        merged = term if merged is None else merged + term
    mix = _dot(merged.astype(BF16), wout_ref[...])
    o_ref[...] = _ln(ALPHA * x + mod[2:3, :] * mix) * lng_ref[...] + lnb_ref[...]


def merge(xa, mod, yd, yg, u, sf, sb, hf, hb, mo, s5d, wglu, bglu, mlg, wgate, bgate, wbr, wout, lng, lnb, ones_ml, nct,
          latent_only):
    B, S, D = xa.shape
    first = nct if latent_only else 0
    nti = S // TM - first
    tok = lambda n: pl.BlockSpec((1, TM, n), lambda b, i: (b, i + first, 0))
    consts = (s5d, wglu, bglu, mlg, wgate, bgate, wbr, wout, lng, lnb, ones_ml)
    return pl.pallas_call(
        _merge_kernel, out_shape=jax.ShapeDtypeStruct((B * nti * TM, D), F32), grid=(B, nti),
        in_specs=[tok(D), pl.BlockSpec((1, 1, 6, D), lambda b, i: (b, jnp.where(i + first >= nct, 1, 0), 0, 0))]
        + [tok(512)] * 2 + [pl.BlockSpec((TM, 512), lambda b, i: (i + first, b))] * 3 + [tok(512)] * 3
        + [_const_spec(a.shape) for a in consts],
        out_specs=pl.BlockSpec((TM, D), lambda b, i: (b * nti + i, 0)),
        compiler_params=_cparams(("parallel", "parallel")), name="merge")(xa, mod, yd, yg, u, sf, sb, hf, hb, mo, *consts)


def _top16(*problems):
    iotas = {prob[0].shape: lax.broadcasted_iota(I32, prob[0].shape, 0).astype(F32) for prob in problems}

    def one(kk, s, val_ref, idx_ref, payload):
        R, ri = s.shape[0], iotas[s.shape]
        m = jnp.max(s, axis=0, keepdims=True)
        ix = jnp.min(jnp.where(s == m, ri, float(R)), axis=0, keepdims=True)
        hit = ri == ix
        val_ref[pl.ds(kk, 1), :] = m
        idx_ref[pl.ds(kk, 1), :] = ix if payload is None else jnp.max(jnp.where(hit, payload, -1.0), axis=0, keepdims=True)
        return jnp.where(hit, -jnp.inf, s)

    def body(kk, ss):
        return tuple(one(kk, s, *prob[1:]) for s, prob in zip(ss, problems))

    lax.fori_loop(0, PEER_TOPK, body, tuple(prob[0] for prob in problems))


_PEER_CAND_COUNTS = tuple(PEER_TOPK // (p + 1) for p in range(8))
_PEER_CAND_ROWS = tuple((sum(_PEER_CAND_COUNTS[:p]), n) for p, n in enumerate(_PEER_CAND_COUNTS))
_PEER_NCAND = -(-(sum(_PEER_CAND_COUNTS) + 8) // 8) * 8


def _peer_route_kernel(x_ref, mod0_ref, mod1_ref, wq_ref, sk_ref, t_ref, ids_ref, gate_ref,
                       v1_sc, i1_sc, v2_sc, i2_sc, cand_sc, cidx_sc, top_sc, tid_sc):
    h = pl.program_id(1)

    @pl.when(h == 0)
    def _():
        for half, mod_ref in enumerate((mod0_ref, mod1_ref)):
            rows = slice(half * TM, (half + 1) * TM)
            mod = mod_ref[0, 0]
            t_ref[rows, :] = (_ln(x_ref[rows, :]) * (1.0 + mod[4:5, :]) + mod[3:4, :]).astype(BF16)

    q = _dot(t_ref[...], wq_ref[...]).astype(BF16)
    half = PEER_DQ // 2
    s1, s2 = _dot_nt(sk_ref[0], q[:, :half]), _dot_nt(sk_ref[1], q[:, half:])
    _top16((s1, v1_sc, i1_sc, None))
    _top16((s2, v2_sc, i2_sc, None))
    nk = float(PEER_NKEYS)
    for p, (lo, n) in enumerate(_PEER_CAND_ROWS):
        cand_sc[lo:lo + n, :] = v1_sc[p:p + 1, :] + v2_sc[0:n, :]
        cidx_sc[lo:lo + n, :] = i1_sc[p:p + 1, :] * nk + i2_sc[0:n, :]
    lo = _PEER_CAND_ROWS[-1][0] + _PEER_CAND_ROWS[-1][1]
    cand_sc[lo:lo + 8, :] = v1_sc[8:16, :] + v2_sc[0:1, :]
    cidx_sc[lo:lo + 8, :] = i1_sc[8:16, :] * nk + i2_sc[0:1, :]
    pad = _PEER_NCAND - (lo + 8)
    cand_sc[lo + 8:, :] = jnp.full((pad, cand_sc.shape[1]), -jnp.inf, F32)
    cidx_sc[lo + 8:, :] = jnp.zeros((pad, cand_sc.shape[1]), F32)
    _top16((cand_sc[...], top_sc, tid_sc, cidx_sc[...]))
    top = top_sc[...]
    e = jnp.exp(top - top[0:1, :])
    gate_ref[...] = e / jnp.sum(e, axis=0, keepdims=True)
    ids_ref[...] = tid_sc[...].astype(I32)


def _half_tile_mod_spec(half, nt_per_sample, nct):
    def index(i, j):
        r = 2 * i + half
        return (r // nt_per_sample, jnp.where(r % nt_per_sample >= nct, 1, 0), 0, 0)
    return pl.BlockSpec((1, 1, 6, D_MODEL), index)


def peer_route(h1, mod, wq, subkeys, nt_per_sample, nct):
    T, D = h1.shape
    assert T % TM_E == 0 and TM_E == 2 * TM
    f = lambda n: pltpu.VMEM((n, TM_E), F32)
    return pl.pallas_call(
        _peer_route_kernel,
        out_shape=[jax.ShapeDtypeStruct((T, D), BF16),
                   jax.ShapeDtypeStruct((PEER_HEADS * PEER_TOPK, T), I32),
                   jax.ShapeDtypeStruct((PEER_HEADS * PEER_TOPK, T), F32)],
        grid=(T // TM_E, PEER_HEADS),
        in_specs=[pl.BlockSpec((TM_E, D), lambda i, h: (i, 0)),
                  _half_tile_mod_spec(0, nt_per_sample, nct), _half_tile_mod_spec(1, nt_per_sample, nct),
                  pl.BlockSpec((D, PEER_DQ), lambda i, h: (0, h)),
                  pl.BlockSpec((2, PEER_NKEYS, PEER_DQ // 2), lambda i, h: (0, 0, 0))],
        out_specs=[pl.BlockSpec((TM_E, D), lambda i, h: (i, 0)),
                   pl.BlockSpec((PEER_TOPK, TM_E), lambda i, h: (h, i)),
                   pl.BlockSpec((PEER_TOPK, TM_E), lambda i, h: (h, i))],
        scratch_shapes=[f(16), f(16), f(16), f(16), f(_PEER_NCAND), f(_PEER_NCAND), f(16), f(16)],
        compiler_params=_cparams(("parallel", "arbitrary")), name="peer_route")(h1, mod, mod, wq, subkeys)


def _peer_expert_kernel(t_ref, ids_ref, gate_ref, u_ref, v_ref, x_ref, mod0_ref, mod1_ref, lng_ref, lnb_ref, o_ref,
                        a_sc, wgt_sc, *, nchunk):
    j = pl.program_id(1)
    NK, CA = PEER_NKEYS, PEER_CA
    w_sc = a_sc

    @pl.when(j < nchunk)
    def _():
        a = _dot(t_ref[...], u_ref[0])
        a3 = jnp.stack([a[:, al * NK:(al + 1) * NK] for al in range(CA)], axis=0)
        a_sc[:, pl.ds(pl.multiple_of(j * CA, CA), CA), :] = pltpu.einshape("atb->tab", a3)

    @pl.when(j == nchunk - 1)
    def _():
        io_k = lax.broadcasted_iota(I32, (NK, NK), 0)
        io_a = lax.broadcasted_iota(I32, (NK, 2 * NK), 0)

        def masks(two):
            idp = ids_ref[two, :]
            ids = jnp.concatenate([idp[0:1], idp[1:2]], axis=1)
            blocks = [jnp.where(io_k == (idp[tok:tok + 1] & (NK - 1)), 1.0, 0.0).astype(BF16) for tok in range(2)]
            zero = jnp.zeros((NK, NK), BF16)
            onehot = jnp.concatenate([jnp.concatenate([blocks[0], zero], axis=1),
                                      jnp.concatenate([zero, blocks[1]], axis=1)], axis=0)
            return onehot, io_a == (ids >> 7)

        def pick(p, carry):
            two = pl.ds(pl.multiple_of(p * 2, 2), 2)
            onehot, r1 = masks(two)
            a2 = a_sc[two]
            at = jnp.concatenate([a2[0], a2[1]], axis=1).astype(BF16)
            picked = _dot(at, onehot)
            act = jnp.sum(jnp.where(r1, picked, 0.0), axis=0, keepdims=True)
            gp = gate_ref[two, :]
            w = jnp.concatenate([gp[0:1], gp[1:2]], axis=1) * _gelu(act)
            wgt_sc[two, :] = jnp.concatenate([w[:, :NK], w[:, NK:]], axis=0)
            return carry

        def scatter(p, carry):
            two = pl.ds(pl.multiple_of(p * 2, 2), 2)
            onehot, r1 = masks(two)
            wp = wgt_sc[two, :]
            w = jnp.concatenate([wp[0:1], wp[1:2]], axis=1)
            wt = _dot_nt(jnp.where(r1, w, 0.0).astype(BF16), onehot)
            w_sc[two] = jnp.stack([wt[:, :NK], wt[:, NK:]], axis=0)
            return carry

        lax.fori_loop(0, TM_E // 2, pick, 0, unroll=16)
        lax.fori_loop(0, TM_E // 2, scatter, 0, unroll=16)
        o_ref[...] = jnp.zeros_like(o_ref)

    @pl.when(j >= nchunk)
    def _():
        w3 = pltpu.einshape("tab->atb", w_sc[:, pl.ds(pl.multiple_of((j - nchunk) * CA, CA), CA), :])
        w = jnp.concatenate([w3[al] for al in range(CA)], axis=1).astype(BF16)
        o_ref[...] += _dot(w, v_ref[...])

    @pl.when(j == 2 * nchunk - 1)
    def _():
        for half, mod_ref in enumerate((mod0_ref, mod1_ref)):
            rows = slice(half * TM, (half + 1) * TM)
            y = ALPHA * x_ref[rows, :] + mod_ref[0, 0][5:6, :] * o_ref[rows, :]
            o_ref[rows, :] = _ln(y) * lng_ref[...] + lnb_ref[...]


def peer_experts(t, ids, gates, emb_u, emb_v, h1, mod, lng, lnb, nt_per_sample, nct):
    T, D = h1.shape
    E = emb_v.shape[0]
    ce = PEER_CA * PEER_NKEYS
    nchunk = E // ce
    emb_ut = emb_u.reshape(nchunk, ce, D).transpose(0, 2, 1)
    assert T % TM_E == 0 and TM_E == 2 * TM
    tok = lambda n, **kw: pl.BlockSpec((TM_E, n), lambda i, j: (i, 0), **kw)
    mod_spec = lambda half: _half_tile_mod_spec(half, nt_per_sample, nct)
    once = dict(pipeline_mode=pl.Buffered(1))
    return pl.pallas_call(
        functools.partial(_peer_expert_kernel, nchunk=nchunk),
        out_shape=jax.ShapeDtypeStruct((T, D), F32), grid=(T // TM_E, 2 * nchunk),
        in_specs=[tok(D, **once), tok(LANES, **once), tok(LANES, **once),
                  pl.BlockSpec((1, D, ce), lambda i, j: (jnp.minimum(j, nchunk - 1), 0, 0)),
                  pl.BlockSpec((ce, D), lambda i, j: (jnp.maximum(j - nchunk, 0), 0)),
                  tok(D, **once), mod_spec(0), mod_spec(1),
                  pl.BlockSpec((1, D), lambda i, j: (0, 0)), pl.BlockSpec((1, D), lambda i, j: (0, 0))],
        out_specs=tok(D),
        scratch_shapes=[pltpu.VMEM((TM_E, PEER_NKEYS, PEER_NKEYS), F32), pltpu.VMEM((TM_E, LANES), F32)],
        compiler_params=_cparams(("parallel", "arbitrary")), name="peer_experts")(t, ids, gates, emb_ut, emb_v, h1, mod, mod, lng, lnb)


_ROPE_IDX = np.arange(HD)
_ROPE_PERM = np.where(_ROPE_IDX % 32 < 16, _ROPE_IDX + 16, _ROPE_IDX - 16)
_ROPE_SIGN = np.where(_ROPE_IDX % 32 < 16, -1.0, 1.0).astype(np.float32)


def _rope_partner(w):
    n = w.shape[1] // HD
    perm = np.concatenate([h * HD + _ROPE_PERM for h in range(n)])
    return w[:, perm] * jnp.asarray(np.tile(_ROPE_SIGN, n))


def _rope_tables(L, nctx, gq, gk):
    rows = L // GRID_W
    row = jnp.repeat(jnp.arange(rows, dtype=F32), GRID_W)
    col = jnp.tile(jnp.arange(GRID_W, dtype=F32), rows)
    nf = HD // 4
    inv = ROPE_THETA ** (-jnp.arange(nf, dtype=F32) / nf)
    ar, ac = row[:, None] * inv, col[:, None] * inv
    cos = jnp.concatenate([jnp.cos(ar), jnp.cos(ar), jnp.cos(ac), jnp.cos(ac)], axis=1)
    sin = jnp.concatenate([jnp.sin(ar), jnp.sin(ar), jnp.sin(ac), jnp.sin(ac)], axis=1)
    cos = jnp.concatenate([jnp.ones((nctx, HD), F32), cos], axis=0)
    sin = jnp.concatenate([jnp.zeros((nctx, HD), F32), sin], axis=0)
    scale = HD ** -0.5
    one = jnp.ones((HD,), F32)
    tabs = []
    for g, sc in ((one, scale), (one, 1.0), (gq, scale), (gk, 1.0)):
        tabs += [cos * (g * sc), sin * (g[_ROPE_PERM] * sc)]
    return jnp.tile(jnp.stack(tabs), (1, 1, 2))


def _s5_tables(a_re, a_im, log_dt, b_re, b_im, c_re, c_im, nb):
    dt = jnp.exp(log_dt)[..., None]
    mag = jnp.exp(a_re * dt)
    lr, li = mag * jnp.cos(a_im * dt), mag * jnp.sin(a_im * dt)
    den = a_re * a_re + a_im * a_im
    cr = ((lr - 1) * a_re + li * a_im) / den
    ci = (li * a_re - (lr - 1) * a_im) / den
    br = cr[..., None] * b_re - ci[..., None] * b_im
    bi = cr[..., None] * b_im + ci[..., None] * b_re
    G2, N, C = S5_GROUPS // 2, S5_STATE, S5_GROUP
    eye = jnp.eye(G2, dtype=F32)

    def blockdiag_in(m):
        m = m.reshape(2, 2, G2, N, C)
        return jnp.einsum('dhgnc,gk->dhgckn', m, eye).reshape(2, 2, G2 * C, G2 * N)

    def blockdiag_out(m):
        m = m.reshape(2, 2, G2, C, N)
        return jnp.einsum('dhgcn,gk->dhgnkc', m, eye).reshape(2, 2, G2 * N, G2 * C)

    bblk = jnp.concatenate([blockdiag_in(br), blockdiag_in(bi)], axis=-1).astype(BF16)
    cblk = jnp.concatenate([blockdiag_out(c_re), blockdiag_out(-c_im)], axis=-2).astype(BF16)
    lam = jnp.stack([lr, li], axis=1).reshape(2, 2, 2, G2 * N)
    lam = jnp.tile(lam[:, :, None], (1, 1, nb, 1, 1)).reshape(2, 2, 2 * nb, G2 * N)
    return bblk, cblk, lam


def _blockdiag_ones(group, n=LANES):
    i = np.arange(n) // group
    return jnp.asarray((i[:, None] == i[None, :]).astype(np.float32), dtype=BF16)


def kernel(x, c, ctx, c_ctx, ada_w, ada_b, w_in, b_gate, diff_lam, diff_norm_g, gqa_qnorm_g, gqa_knorm_g,
           s5_a_re, s5_a_im, s5_log_dt, s5_b_re, s5_b_im, s5_c_re, s5_c_im, s5_d, s5_w_glu, s5_b_glu,
           ml_conv_w, ml_conv_b, ml_gate_b, ml_norm_g, w_branch, w_out, ln_mix_g, ln_mix_b, ln_ffn_g, ln_ffn_b,
           peer_wq, peer_subkeys, peer_u, peer_v):
    B, L, D = x.shape
    nctx = ctx.shape[1]
    S = nctx + L
    assert D == D_MODEL and nctx % TM == 0 and L % TM == 0 and L % GRID_W == 0
    nt, nct = S // TM, nctx // TM
    depth = ada_w.shape[0]

    h = jnp.concatenate([ctx, x], axis=1)
    R = -(-(B + 1) // 8) * 8
    cond = jnp.zeros((R, D), F32).at[:B].set(c).at[B].set(c_ctx)
    ones64, ones128 = _blockdiag_ones(HD), _blockdiag_ones(ML_HD, ML_HEADS * ML_HD)
    o = IN_OFFS

    for l in range(depth):
        last = l == depth - 1
        lam_init = 0.8 - 0.6 * math.exp(-0.3 * l)
        m = ada_modulation(cond, ada_w[l].astype(BF16), ada_b[l][None, :])
        mod = jnp.stack([jnp.broadcast_to(m[B], (B, 6 * D)), m[:B]], axis=1).reshape(B, 2, 6, D)

        w = w_in[l]
        seg = lambda i: w[:, o[i]:o[i + 1]]
        dup = lambda t: jnp.concatenate([t[:, :HD], t[:, :HD], t[:, HD:], t[:, HD:]], axis=1)
        def with_ones_slots(t, width):
            heads = t.reshape(D, -1, width)
            return jnp.concatenate([heads, jnp.zeros_like(heads)], axis=2).reshape(D, -1)
        w_a = jnp.concatenate([seg(0), _rope_partner(seg(0)), seg(1), _rope_partner(seg(1)), seg(2),
                               seg(9), _rope_partner(seg(9)), dup(seg(10)), dup(_rope_partner(seg(10))),
                               with_ones_slots(seg(11), HD)], axis=1).astype(BF16)
        w_b = jnp.concatenate([seg(3), seg(4), seg(5), seg(6), seg(7), seg(8),
                               jnp.zeros((D, LANES - 16), F32)], axis=1).astype(BF16)
        gate_b = jnp.concatenate([ml_gate_b[l], jnp.zeros((LANES - 16,), F32)])[None, :]
        tab = _rope_tables(L, nctx, gqa_qnorm_g[l], gqa_knorm_g[l])

        dq, dk, dv, gq, gk, gv = proj_attn(h, mod, w_a, tab, ones64, nct)
        u, mqk, mv, mo, mg = proj_seq(h, mod, w_b, gate_b, nct)

        yd = diff_attention(dq, dk, dv, diff_lam[l], diff_norm_g[l][None, :], nctx, lam_init)
        yg = gqa_attention(gq, gk, gv, nctx)

        bblk, cblk, lam = _s5_tables(s5_a_re[l], s5_a_im[l], s5_log_dt[l], s5_b_re[l], s5_b_im[l],
                                     s5_c_re[l], s5_c_im[l], B)
        sf, sb = s5_scan(u, bblk, cblk, lam, nctx)

        mq, mk = ml_prep(mqk, ml_conv_w[l], ml_conv_b[l][None, :], nctx)
        hf = mlstm_scan(mq, mk, mv, mg, nctx, False)
        hb = mlstm_scan(mq, mk, mv, mg, nctx, True)

        h1 = merge(h, mod, yd, yg, u, sf, sb, hf, hb, mo,
                   s5_d[l][None, :], s5_w_glu[l].astype(BF16), s5_b_glu[l][None, :], ml_norm_g[l][None, :],
                   seg(12).astype(BF16), b_gate[l][None, :], w_branch[l].astype(BF16), w_out[l].astype(BF16),
                   ln_mix_g[l][None, :], ln_mix_b[l][None, :], ones128, nct, last)

        tiles, ctx_tiles = (nt - nct, 0) if last else (nt, nct)
        t, ids, gates = peer_route(h1, mod, peer_wq[l].astype(BF16), peer_subkeys[l].astype(BF16), tiles, ctx_tiles)
        h = peer_experts(t, ids.T, gates.T, peer_u[l].astype(BF16), peer_v[l].astype(BF16),
                         h1, mod, ln_ffn_g[l][None, :], ln_ffn_b[l][None, :], tiles, ctx_tiles)
        h = h.reshape(B, -1, D)
    return h
```

```python
import functools
import math

import numpy as np
import jax
import jax.numpy as jnp
from jax import lax
from jax.experimental import pallas as pl
from jax.experimental.pallas import tpu as pltpu

F32 = jnp.float32
BF16 = jnp.bfloat16
I32 = jnp.int32

D_MODEL = 1024
DEPTH = 2
GRID_W = 64
ROPE_THETA = 10000.0
LN_EPS = 1e-6
HD = 64
DIFF_HEADS = 4
DIFF_HPS = 2
GQA_KV = 2
GQA_GPS = 2
S5_GROUP = 16
S5_GROUPS = 32
S5_STATE = 64
ML_HEADS = 4
ML_HD = 128
N_BRANCH = 4
PEER_HEADS = 8
PEER_NKEYS = 128
PEER_TOPK = 16
PEER_DQ = 256
ALPHA = (2 * DEPTH) ** 0.25

LANES = 128
TM = 256
TM_E = 2 * TM
TQ = 256
TK = 2048
S5_TC = 128
ML_T = 256
ML_LOCKSTEP = 4
PEER_CA = 8
VMEM_LIMIT = 56 * 1024 * 1024

IN_SPLITS = (512, 512, 512, 512, 512, 512, 512, 512, 16, 512, 128, 128, N_BRANCH * D_MODEL)
IN_OFFS = tuple(int(v) for v in np.cumsum((0,) + IN_SPLITS))


def _cparams(sem):
    return pltpu.CompilerParams(dimension_semantics=sem, vmem_limit_bytes=VMEM_LIMIT)


def _const_spec(shape):
    nd = len(shape)
    return pl.BlockSpec(shape, lambda *_: (0,) * nd, pipeline_mode=pl.Buffered(1))


def _ln(x):
    xc = x - jnp.mean(x, axis=-1, keepdims=True)
    return xc * lax.rsqrt(jnp.mean(xc * xc, axis=-1, keepdims=True) + LN_EPS)


def _sigmoid(x):
    return 1.0 / (1.0 + jnp.exp(-x))


def _gelu(x):
    return 0.5 * x * (1.0 + lax.erf(x * (2.0 ** -0.5)))


def _dot(a, b):
    return jnp.dot(a, b, preferred_element_type=F32)


def _dot_nt(a, b):
    return lax.dot_general(a, b, (((1,), (1,)), ((), ())), preferred_element_type=F32)


def _dot_tn(a, b):
    return lax.dot_general(a, b, (((0,), (0,)), ((), ())), preferred_element_type=F32)


def _ada_kernel(c_ref, w_ref, b_ref, o_ref):
    c = c_ref[...]
    o_ref[...] = _dot((c * _sigmoid(c)).astype(BF16), w_ref[...]) + b_ref[...]


def ada_modulation(cond, w, b):
    R, D = cond.shape
    N = w.shape[1]
    tn = 1536
    return pl.pallas_call(
        _ada_kernel, out_shape=jax.ShapeDtypeStruct((R, N), F32), grid=(N // tn,),
        in_specs=[pl.BlockSpec((R, D), lambda j: (0, 0)), pl.BlockSpec((D, tn), lambda j: (0, j)),
                  pl.BlockSpec((1, tn), lambda j: (0, j))],
        out_specs=pl.BlockSpec((R, tn), lambda j: (0, j)),
        compiler_params=_cparams(("arbitrary",)), name="ada")(cond, w, b)


def _proj_attn_kernel(x_ref, mod_ref, w_ref, tab_ref, ones_ref,
                      dq_ref, dk_ref, dv_ref, gq_ref, gk_ref, gv_ref):
    mod = mod_ref[0, 0]
    xm = (_ln(x_ref[0]) * (1.0 + mod[1:2, :]) + mod[0:1, :]).astype(BF16)

    def mm(lo, n):
        return _dot(xm, w_ref[:, lo:lo + n])

    def rope_store(ref, lo, n, ci, norm):
        t, tp = mm(lo, n), mm(lo + n, n)
        c, s = tab_ref[ci], tab_ref[ci + 1]
        for j in range(n // LANES):
            sl = slice(j * LANES, (j + 1) * LANES)
            tb = t[:, sl]
            y = tb * c + tp[:, sl] * s
            if norm:
                ss = _dot((tb * tb).astype(BF16), ones_ref[...])
                y = y * lax.rsqrt(ss * (1.0 / HD) + LN_EPS)
            ref[0, :, sl] = y.astype(ref.dtype)

    def value_store(ref, lo, n, width):
        lane = lax.broadcasted_iota(I32, (1, n), 1)
        ref[0] = (mm(lo, n) + jnp.where(lane % (2 * width) >= width, 1.0, 0.0)).astype(ref.dtype)

    rope_store(dq_ref, 0, 512, 0, False)
    rope_store(dk_ref, 1024, 512, 2, False)
    dv_ref[0] = mm(2048, 512).astype(dv_ref.dtype)
    rope_store(gq_ref, 2560, 512, 4, True)
    rope_store(gk_ref, 3584, 256, 6, True)
    value_store(gv_ref, 4096, 256, HD)


def proj_attn(xa, mod, w_a, tab, ones_bd, nct):
    B, S, D = xa.shape
    tok = lambda n: pl.BlockSpec((1, TM, n), lambda b, i: (b, i, 0))
    widths = (512, 512, 512, 512, 256, 256)
    outs = [jax.ShapeDtypeStruct((B, S, n), BF16) for n in widths]
    return pl.pallas_call(
        _proj_attn_kernel, out_shape=outs, grid=(B, S // TM),
        in_specs=[tok(D),
                  pl.BlockSpec((1, 1, 6, D), lambda b, i: (b, jnp.where(i >= nct, 1, 0), 0, 0)),
                  _const_spec(w_a.shape),
                  pl.BlockSpec((8, TM, LANES), lambda b, i: (0, i, 0)),
                  _const_spec(ones_bd.shape)],
        out_specs=[tok(n) for n in widths],
        compiler_params=_cparams(("parallel", "parallel")), name="proj_attn")(xa, mod, w_a, tab, ones_bd)


def _proj_seq_kernel(x_ref, mod_ref, w_ref, gb_ref, u_ref, qk_ref, v_ref, o_ref, g_ref):
    mod = mod_ref[0, 0]
    xm = (_ln(x_ref[0]) * (1.0 + mod[1:2, :]) + mod[0:1, :]).astype(BF16)
    u_ref[...] = _dot(xm, w_ref[:, 0:512])
    qk_ref[0] = _dot(xm, w_ref[:, 512:1536])
    v_ref[0] = _dot(xm, w_ref[:, 1536:2048])
    o_ref[0] = _dot(xm, w_ref[:, 2048:2560])
    g_ref[0] = _dot(xm, w_ref[:, 2560:2688]) + gb_ref[...]


def proj_seq(xa, mod, w_b, gate_b, nct):
    B, S, D = xa.shape
    tok = lambda n: pl.BlockSpec((1, TM, n), lambda b, i: (b, i, 0))
    widths = (1024, 512, 512, LANES)
    return pl.pallas_call(
        _proj_seq_kernel,
        out_shape=[jax.ShapeDtypeStruct((S, B * 512), F32)] + [jax.ShapeDtypeStruct((B, S, n), F32) for n in widths],
        grid=(B, S // TM),
        in_specs=[tok(D),
                  pl.BlockSpec((1, 1, 6, D), lambda b, i: (b, jnp.where(i >= nct, 1, 0), 0, 0)),
                  _const_spec(w_b.shape), _const_spec(gate_b.shape)],
        out_specs=[pl.BlockSpec((TM, 512), lambda b, i: (i, b))] + [tok(n) for n in widths],
        compiler_params=_cparams(("parallel", "parallel")), name="proj_seq")(xa, mod, w_b, gate_b)


def _flash(qqs, k_ref, v_ref, nctx, n_lat, tk, dv):
    P = len(qqs)
    R, W = qqs[0].shape[0], v_ref.shape[2] // P
    mxu_sums = dv < W

    def finish(carry, s, v):
        m, l, acc = carry
        m_new = jnp.maximum(m, jnp.max(s, axis=1, keepdims=True))
        alpha = jnp.exp(m - m_new)
        if mxu_sums:
            p = jnp.exp((s - m_new).astype(BF16))
        else:
            p = jnp.exp(s - m_new)
            l = alpha * l + jnp.sum(p, axis=1, keepdims=True)
        return m_new, l, alpha * acc + _dot(p.astype(BF16), v)

    carries = [(jnp.full((R, 1), -jnp.inf, F32), jnp.zeros((R, 1), F32), jnp.zeros((R, W), F32))] * P
    chunks = [pl.ds(0, nctx)] + [pl.ds(nctx + c * tk, tk) for c in range(n_lat)]
    for rows in chunks:
        scores = [_dot_nt(qqs[p], k_ref[0, rows, p * LANES:(p + 1) * LANES]) for p in range(P)]
        carries = [finish(carries[p], scores[p], v_ref[0, rows, p * W:(p + 1) * W]) for p in range(P)]
    return [acc[:, :dv] / (acc[:, dv:dv + 1] if mxu_sums else l) for _, l, acc in carries]


def _ctx_or_all(i, nctx, n_lat, attend):
    @pl.when(i < nctx // TQ)
    def _():
        attend(0)

    @pl.when(i >= nctx // TQ)
    def _():
        attend(n_lat)


def _diff_attn_kernel(q_ref, k_ref, v_ref, lam_ref, g_ref, o_ref, *, nctx, n_lat, tk, lam_init):
    i = pl.program_id(2)
    lane = lax.broadcasted_iota(I32, (TQ, LANES), 1)
    zero = jnp.zeros((TQ, LANES), q_ref.dtype)
    qqs = []
    for h in range(DIFF_HPS):
        q = q_ref[0, :, h * LANES:(h + 1) * LANES]
        qqs.append(jnp.concatenate([jnp.where(lane < HD, q, zero), jnp.where(lane >= HD, q, zero)], axis=0))
    lv = lam_ref[...]
    lam = (jnp.exp(jnp.sum(lv[0:1] * lv[1:2], axis=1, keepdims=True))
           - jnp.exp(jnp.sum(lv[2:3] * lv[3:4], axis=1, keepdims=True)) + lam_init)

    def attend(n):
        for h, o in enumerate(_flash(qqs, k_ref, v_ref, nctx, n, tk, 2 * HD)):
            d = o[:TQ] - lam * o[TQ:]
            y = d * lax.rsqrt(jnp.mean(d * d, axis=-1, keepdims=True) + LN_EPS)
            o_ref[0, :, h * LANES:(h + 1) * LANES] = (y * g_ref[...] * (1.0 - lam_init)).astype(o_ref.dtype)

    _ctx_or_all(i, nctx, n_lat, attend)


def diff_attention(q, k, v, lam_vec, norm_g, nctx, lam_init):
    B, S, _ = q.shape
    tk = math.gcd(S - nctx, TK)
    kern = functools.partial(_diff_attn_kernel, nctx=nctx, n_lat=(S - nctx) // tk, tk=tk, lam_init=lam_init)
    return pl.pallas_call(
        kern, out_shape=jax.ShapeDtypeStruct((B, S, 512), BF16), grid=(B, DIFF_HEADS // DIFF_HPS, S // TQ),
        in_specs=[pl.BlockSpec((1, TQ, DIFF_HPS * LANES), lambda b, h, i: (b, i, h)),
                  pl.BlockSpec((1, S, DIFF_HPS * LANES), lambda b, h, i: (b, 0, h)),
                  pl.BlockSpec((1, S, DIFF_HPS * LANES), lambda b, h, i: (b, 0, h)),
                  pl.BlockSpec((4, HD), lambda b, h, i: (0, 0)),
                  pl.BlockSpec((1, LANES), lambda b, h, i: (0, 0))],
        out_specs=pl.BlockSpec((1, TQ, DIFF_HPS * LANES), lambda b, h, i: (b, i, h)),
        compiler_params=_cparams(("parallel", "parallel", "parallel")), name="diff_attn")(q, k, v, lam_vec, norm_g)


def _gqa_attn_kernel(q_ref, k_ref, v_ref, o_ref, *, nctx, n_lat, tk):
    i = pl.program_id(2)
    lane = lax.broadcasted_iota(I32, (TQ, LANES), 1)
    zero = jnp.zeros((TQ, LANES), q_ref.dtype)
    qqs = []
    for g in range(GQA_GPS):
        parts = []
        for j in range(2 * g, 2 * g + 2):
            blk = q_ref[0, :, j * LANES:(j + 1) * LANES]
            parts += [jnp.where(lane < HD, blk, zero), jnp.where(lane >= HD, blk, zero)]
        qqs.append(jnp.concatenate(parts, axis=0))

    def attend(n):
        for g, o in enumerate(_flash(qqs, k_ref, v_ref, nctx, n, tk, HD)):
            heads = jnp.concatenate([o[h * TQ:(h + 1) * TQ] for h in range(4)], axis=1)
            o_ref[0, :, g * 2 * LANES:(g + 1) * 2 * LANES] = heads.astype(o_ref.dtype)

    _ctx_or_all(i, nctx, n_lat, attend)


def gqa_attention(q, k, v, nctx):
    B, S, _ = q.shape
    tk = math.gcd(S - nctx, TK)
    kern = functools.partial(_gqa_attn_kernel, nctx=nctx, n_lat=(S - nctx) // tk, tk=tk)
    return pl.pallas_call(
        kern, out_shape=jax.ShapeDtypeStruct((B, S, 512), BF16), grid=(B, GQA_KV // GQA_GPS, S // TQ),
        in_specs=[pl.BlockSpec((1, TQ, GQA_GPS * 2 * LANES), lambda b, g, i: (b, i, g)),
                  pl.BlockSpec((1, S, GQA_GPS * LANES), lambda b, g, i: (b, 0, g)),
                  pl.BlockSpec((1, S, GQA_GPS * LANES), lambda b, g, i: (b, 0, g))],
        out_specs=pl.BlockSpec((1, TQ, GQA_GPS * 2 * LANES), lambda b, g, i: (b, i, g)),
        compiler_params=_cparams(("parallel", "parallel", "parallel")), name="gqa_attn")(q, k, v)


def _s5_kernel(uf_ref, ub_ref, bblk_ref, cblk_ref, lam_ref, yf_ref, yb_ref, buf_f, buf_b, st_ref):
    H = S5_GROUPS * S5_STATE // 2
    R = buf_f.shape[0]
    RB = 256
    TB = RB // 8

    @pl.when(pl.program_id(0) == 0)
    def _():
        st_ref[...] = jnp.zeros_like(st_ref)

    half0 = (lax.broadcasted_iota(I32, (RB, 1), 0) & 1) == 0

    for d, (u_ref, buf) in enumerate(((uf_ref, buf_f), (ub_ref, buf_b))):
        for rb in range(R // RB):
            rows = slice(rb * RB, (rb + 1) * RB)
            ut = u_ref[rb * TB:(rb + 1) * TB, :]
            u3 = jnp.stack([ut[:, s * 256:(s + 1) * 256] for s in range(8)], axis=0)
            uh = pltpu.einshape("stc->tsc", u3).reshape(RB, 256).astype(BF16)
            for cols in (slice(0, H), slice(H, 2 * H)):
                buf[rows, cols] = jnp.where(half0, _dot(uh, bblk_ref[d, 0, :, cols]), _dot(uh, bblk_ref[d, 1, :, cols]))

    lfr, lfi, lbr, lbi = lam_ref[0, 0], lam_ref[0, 1], lam_ref[1, 0], lam_ref[1, 1]

    def step(t, carry):
        fr, fi, br, bi = carry
        rf = pl.ds(pl.multiple_of(t * 8, 8), 8)
        x = buf_f[rf, :]
        nfr = lfr * fr - lfi * fi + x[:, :H]
        nfi = lfr * fi + lfi * fr + x[:, H:]
        buf_f[rf, :] = jnp.concatenate([nfr, nfi], axis=1)
        rb = pl.ds(pl.multiple_of((S5_TC - 1 - t) * 8, 8), 8)
        z = buf_b[rb, :]
        nbr = lbr * br - lbi * bi + z[:, :H]
        nbi = lbr * bi + lbi * br + z[:, H:]
        buf_b[rb, :] = jnp.concatenate([nbr, nbi], axis=1)
        return nfr, nfi, nbr, nbi

    fin = lax.fori_loop(0, S5_TC, step, (st_ref[0], st_ref[1], st_ref[2], st_ref[3]))
    for j in range(4):
        st_ref[j] = fin[j]

    for d, (y_ref, buf) in enumerate(((yf_ref, buf_f), (yb_ref, buf_b))):
        for rb in range(R // RB):
            rows = slice(rb * RB, (rb + 1) * RB)
            h = buf[rows, :].astype(BF16)
            y = jnp.where(half0, _dot(h, cblk_ref[d, 0]), _dot(h, cblk_ref[d, 1]))
            y3 = pltpu.einshape("tsc->stc", y.reshape(TB, 8, 256))
            y_ref[rb * TB:(rb + 1) * TB, :] = jnp.concatenate([y3[s] for s in range(8)], axis=1)


def s5_scan(u, bblk, cblk, lam, nctx):
    S, W = u.shape
    assert W == 8 * 256, "the scan packs (sample, half) pairs into the eight sublanes of a vreg"
    nch, nc0 = S // S5_TC, nctx // S5_TC

    def bwd(i):
        return jnp.where(i < nc0, nc0 - 1 - i, (nch - 1) - (i - nc0))
    H2 = S5_GROUPS * S5_STATE
    R = S5_TC * 8
    blk = lambda f: pl.BlockSpec((S5_TC, W), f)
    return pl.pallas_call(
        _s5_kernel, out_shape=[jax.ShapeDtypeStruct(u.shape, F32)] * 2, grid=(nch,),
        in_specs=[blk(lambda i: (i, 0)), blk(lambda i: (bwd(i), 0)),
                  _const_spec(bblk.shape), _const_spec(cblk.shape), _const_spec(lam.shape)],
        out_specs=[blk(lambda i: (i, 0)), blk(lambda i: (bwd(i), 0))],
        scratch_shapes=[pltpu.VMEM((R, H2), F32), pltpu.VMEM((R, H2), F32), pltpu.VMEM((4, 8, H2 // 2), F32)],
        compiler_params=_cparams(("arbitrary",)), name="s5_scan")(u, u, bblk, cblk, lam)


def _ml_prep_kernel(x_ref, prev_ref, next_ref, w_ref, b_ref, q_ref, k_ref, *, seg_starts, seg_ends):
    i = pl.program_id(1)
    x = x_ref[0]
    row = lax.broadcasted_iota(I32, x.shape, 0)
    first = functools.reduce(jnp.logical_or, [i == s for s in seg_starts])
    last = functools.reduce(jnp.logical_or, [i == s for s in seg_ends])
    pr = jnp.where(first, 0.0, prev_ref[0, 7:8, :])
    nx = jnp.where(last, 0.0, next_ref[0, 0:1, :])
    xp = jnp.where(row == 0, pr, pltpu.roll(x, 1, 0))
    xn = jnp.where(row == TM - 1, nx, pltpu.roll(x, TM - 1, 0))
    w = w_ref[...]
    y = b_ref[...] + xp * w[0:1, :] + x * w[1:2, :] + xn * w[2:3, :]
    y = y * _sigmoid(y)
    q_ref[0] = y[:, :512].astype(q_ref.dtype)
    k_ref[0] = (y[:, 512:] * (ML_HD ** -0.5)).astype(k_ref.dtype)


def ml_prep(qk, conv_w, conv_b, nctx):
    B, S, W = qk.shape
    nt, nct, r8 = S // TM, nctx // TM, TM // 8
    kern = functools.partial(_ml_prep_kernel, seg_starts=(0, nct), seg_ends=(nct - 1, nt - 1))
    return pl.pallas_call(
        kern, out_shape=[jax.ShapeDtypeStruct((B, S, 512), BF16)] * 2, grid=(B, nt),
        in_specs=[pl.BlockSpec((1, TM, W), lambda b, i: (b, i, 0)),
                  pl.BlockSpec((1, 8, W), lambda b, i: (b, jnp.maximum(i * r8 - 1, 0), 0)),
                  pl.BlockSpec((1, 8, W), lambda b, i: (b, jnp.minimum((i + 1) * r8, S // 8 - 1), 0)),
                  pl.BlockSpec((3, W), lambda b, i: (0, 0)), pl.BlockSpec((1, W), lambda b, i: (0, 0))],
        out_specs=[pl.BlockSpec((1, TM, 512), lambda b, i: (b, i, 0))] * 2,
        compiler_params=_cparams(("parallel", "parallel")), name="ml_prep")(qk, qk, qk, conv_w, conv_b)


def _log_sigmoid(x):
    return jnp.minimum(x, 0.0) - jnp.log1p(jnp.exp(-jnp.abs(x)))


def _mlstm_kernel(q_ref, k_ref, v_ref, g_ref, h_ref, c_sc, n_sc, m_sc, *, reverse):
    T = ML_T
    gi, gf = (8, 12) if reverse else (0, 4)

    @pl.when(pl.program_id(0) == 0)
    def _():
        c_sc[...] = jnp.zeros_like(c_sc)
        n_sc[...] = jnp.zeros_like(n_sc)
        m_sc[...] = jnp.zeros_like(m_sc)

    r = lax.broadcasted_iota(I32, (T, T), 0)
    c = lax.broadcasted_iota(I32, (T, T), 1)
    mask = (c >= r) if reverse else (c <= r)
    tri = jnp.where(mask, 1.0, 0.0)
    hp = lax.Precision.HIGHEST
    end = 0 if reverse else T - 1
    def gates(b):
        g = g_ref[b]
        gt = g.T
        bcol_all = jnp.dot(tri, _log_sigmoid(g), preferred_element_type=F32, precision=hp)
        brow_all = lax.dot_general(_log_sigmoid(gt[0:16]), tri, (((1,), (1,)), ((), ())),
                                   preferred_element_type=F32, precision=hp)
        return g, gt, bcol_all, brow_all

    def chain(b, hh, g, gt, bcol_all, brow_all):
        st = b * ML_HEADS + hh
        sl = slice(hh * ML_HD, (hh + 1) * ML_HD)
        q, k, v = q_ref[b, :, sl], k_ref[b, :, sl], v_ref[b, :, sl]
        bcol, brow = bcol_all[:, gf + hh:gf + hh + 1], brow_all[gf + hh:gf + hh + 1, :]
        icol, irow = g[:, gi + hh:gi + hh + 1], gt[gi + hh:gi + hh + 1, :]
        m_old = m_sc[st][:, 0:1]
        n_old = n_sc[st]
        c_old = c_sc[st]
        logw = jnp.where(mask, bcol - brow + irow, -jnp.inf)
        m_inter = bcol + m_old
        m_t = jnp.maximum(m_inter, jnp.max(logw, axis=1, keepdims=True))
        yield
        s = _dot_nt(q, k) * jnp.exp(logw - m_t)
        inter = jnp.exp(m_inter - m_t)
        yield
        num = _dot(s.astype(BF16), v.astype(BF16)) + inter * _dot_nt(q, c_old.astype(BF16))
        den = jnp.sum(s, axis=1, keepdims=True) + inter * jnp.sum(q.astype(F32) * n_old, axis=1, keepdims=True)
        h_ref[b, :, sl] = num / jnp.maximum(jnp.abs(den), jnp.exp(-m_t))
        yield
        b_end = bcol[end:end + 1, :]
        g_row, g_col = b_end - brow + irow, b_end - bcol + icol
        m_new = jnp.maximum(b_end + m_old, jnp.max(g_row, axis=1, keepdims=True))
        decay = jnp.exp(b_end + m_old - m_new)
        wk = jnp.exp(g_col - m_new)
        c_sc[st] = decay * c_old + _dot_tn((v * wk).astype(BF16), k)
        n_sc[st] = decay * n_old + jnp.sum(k.astype(F32) * wk, axis=0, keepdims=True)
        m_sc[st] = jnp.broadcast_to(m_new, (1, LANES))

    nb = q_ref.shape[0]
    for b0 in range(0, nb, ML_LOCKSTEP):
        chains = []
        for b in range(b0, min(b0 + ML_LOCKSTEP, nb)):
            shared = gates(b)
            chains += [chain(b, hh, *shared) for hh in range(ML_HEADS)]
        for _ in range(4):
            for ch in chains:
                next(ch, None)


def mlstm_scan(q, k, v, g, nctx, reverse):
    B, S, W = q.shape
    nch, nc0 = S // ML_T, nctx // ML_T

    def order(i):
        return jnp.where(i < nc0, nc0 - 1 - i, (nch - 1) - (i - nc0)) if reverse else i
    blk = lambda n: pl.BlockSpec((B, ML_T, n), lambda i: (0, order(i), 0))
    nst = B * ML_HEADS
    return pl.pallas_call(
        functools.partial(_mlstm_kernel, reverse=reverse),
        out_shape=jax.ShapeDtypeStruct((B, S, W), F32), grid=(nch,),
        in_specs=[blk(W), blk(W), blk(W), blk(LANES)], out_specs=blk(W),
        scratch_shapes=[pltpu.VMEM((nst, ML_HD, ML_HD), F32), pltpu.VMEM((nst, 1, ML_HD), F32),
                        pltpu.VMEM((nst, 1, LANES), F32)],
        compiler_params=_cparams(("arbitrary",)), name="mlstm_bwd" if reverse else "mlstm_fwd")(q, k, v, g)


def _merge_kernel(x_ref, mod_ref, yd_ref, yg_ref, u_ref, sf_ref, sb_ref, hf_ref, hb_ref, mo_ref,
                  s5d_ref, wglu_ref, bglu_ref, mlg_ref, wgate_ref, bgate_ref, wbr_ref, wout_ref,
                  lng_ref, lnb_ref, ones_ref, o_ref):
    x = x_ref[0]
    mod = mod_ref[0, 0]
    xm = (_ln(x) * (1.0 + mod[1:2, :]) + mod[0:1, :]).astype(BF16)
    ys = u_ref[...] * s5d_ref[...] + sf_ref[...] + sb_ref[...]
    z = _dot(_gelu(ys).astype(BF16), wglu_ref[...]) + bglu_ref[...]
    ys = z[:, :512] * _sigmoid(z[:, 512:])
    hm = hf_ref[0] + hb_ref[0]
    ss = _dot((hm * hm).astype(BF16), ones_ref[...])
    ym = hm * lax.rsqrt(ss * (1.0 / ML_HD) + LN_EPS) * mlg_ref[...] * _sigmoid(mo_ref[0])
    branches = (yd_ref[0], ys.astype(BF16), ym.astype(BF16), yg_ref[0])
    merged = None
    for j, yb in enumerate(branches):
        gate = _sigmoid(_dot(xm, wgate_ref[:, j * D_MODEL:(j + 1) * D_MODEL]) + bgate_ref[:, j * D_MODEL:(j + 1) * D_MODEL])
        term = gate * _dot(yb, wbr_ref[j])
        merged = term if merged is None else merged + term
    mix = _dot(merged.astype(BF16), wout_ref[...])
    o_ref[...] = _ln(ALPHA * x + mod[2:3, :] * mix) * lng_ref[...] + lnb_ref[...]


def merge(xa, mod, yd, yg, u, sf, sb, hf, hb, mo, s5d, wglu, bglu, mlg, wgate, bgate, wbr, wout, lng, lnb, ones_ml, nct,
          latent_only):
    B, S, D = xa.shape
    first = nct if latent_only else 0
    nti = S // TM - first
    tok = lambda n: pl.BlockSpec((1, TM, n), lambda b, i: (b, i + first, 0))
    consts = (s5d, wglu, bglu, mlg, wgate, bgate, wbr, wout, lng, lnb, ones_ml)
    return pl.pallas_call(
        _merge_kernel, out_shape=jax.ShapeDtypeStruct((B * nti * TM, D), F32), grid=(B, nti),
        in_specs=[tok(D), pl.BlockSpec((1, 1, 6, D), lambda b, i: (b, jnp.where(i + first >= nct, 1, 0), 0, 0))]
        + [tok(512)] * 2 + [pl.BlockSpec((TM, 512), lambda b, i: (i + first, b))] * 3 + [tok(512)] * 3
        + [_const_spec(a.shape) for a in consts],
        out_specs=pl.BlockSpec((TM, D), lambda b, i: (b * nti + i, 0)),
        compiler_params=_cparams(("parallel", "parallel")), name="merge")(xa, mod, yd, yg, u, sf, sb, hf, hb, mo, *consts)


def _top16(*problems):
    iotas = {prob[0].shape: lax.broadcasted_iota(I32, prob[0].shape, 0).astype(F32) for prob in problems}

    def one(kk, s, val_ref, idx_ref, payload):
        R, ri = s.shape[0], iotas[s.shape]
        m = jnp.max(s, axis=0, keepdims=True)
        ix = jnp.min(jnp.where(s == m, ri, float(R)), axis=0, keepdims=True)
        hit = ri == ix
        val_ref[pl.ds(kk, 1), :] = m
        idx_ref[pl.ds(kk, 1), :] = ix if payload is None else jnp.max(jnp.where(hit, payload, -1.0), axis=0, keepdims=True)
        return jnp.where(hit, -jnp.inf, s)

    def body(kk, ss):
        return tuple(one(kk, s, *prob[1:]) for s, prob in zip(ss, problems))

    lax.fori_loop(0, PEER_TOPK, body, tuple(prob[0] for prob in problems))


_PEER_CAND_COUNTS = tuple(PEER_TOPK // (p + 1) for p in range(8))
_PEER_CAND_ROWS = tuple((sum(_PEER_CAND_COUNTS[:p]), n) for p, n in enumerate(_PEER_CAND_COUNTS))
_PEER_NCAND = -(-(sum(_PEER_CAND_COUNTS) + 8) // 8) * 8


def _peer_route_kernel(x_ref, mod0_ref, mod1_ref, wq_ref, sk_ref, t_ref, ids_ref, gate_ref,
                       v1_sc, i1_sc, v2_sc, i2_sc, cand_sc, cidx_sc, top_sc, tid_sc, ids_sc, gates_sc):
    h = pl.program_id(1)

    @pl.when(h == 0)
    def _():
        for half, mod_ref in enumerate((mod0_ref, mod1_ref)):
            rows = slice(half * TM, (half + 1) * TM)
            mod = mod_ref[0, 0]
            t_ref[rows, :] = (_ln(x_ref[rows, :]) * (1.0 + mod[4:5, :]) + mod[3:4, :]).astype(BF16)

    q = _dot(t_ref[...], wq_ref[...]).astype(BF16)
    half = PEER_DQ // 2
    s1, s2 = _dot_nt(sk_ref[0], q[:, :half]), _dot_nt(sk_ref[1], q[:, half:])
    _top16((s1, v1_sc, i1_sc, None))
    _top16((s2, v2_sc, i2_sc, None))
    nk = float(PEER_NKEYS)
    for p, (lo, n) in enumerate(_PEER_CAND_ROWS):
        cand_sc[lo:lo + n, :] = v1_sc[p:p + 1, :] + v2_sc[0:n, :]
        cidx_sc[lo:lo + n, :] = i1_sc[p:p + 1, :] * nk + i2_sc[0:n, :]
    lo = _PEER_CAND_ROWS[-1][0] + _PEER_CAND_ROWS[-1][1]
    cand_sc[lo:lo + 8, :] = v1_sc[8:16, :] + v2_sc[0:1, :]
    cidx_sc[lo:lo + 8, :] = i1_sc[8:16, :] * nk + i2_sc[0:1, :]
    pad = _PEER_NCAND - (lo + 8)
    cand_sc[lo + 8:, :] = jnp.full((pad, cand_sc.shape[1]), -jnp.inf, F32)
    cidx_sc[lo + 8:, :] = jnp.zeros((pad, cand_sc.shape[1]), F32)
    _top16((cand_sc[...], top_sc, tid_sc, cidx_sc[...]))
    top = top_sc[...]
    e = jnp.exp(top - top[0:1, :])
    entries = pl.ds(pl.multiple_of(h * PEER_TOPK, PEER_TOPK), PEER_TOPK)
    gates_sc[entries, :] = e / jnp.sum(e, axis=0, keepdims=True)
    ids_sc[entries, :] = tid_sc[...]

    @pl.when(h == PEER_HEADS - 1)
    def _():
        gate_ref[...] = gates_sc[...].T
        ids_ref[...] = ids_sc[...].T.astype(I32)


def _half_tile_mod_spec(half, nt_per_sample, nct):
    def index(i, j):
        r = 2 * i + half
        return (r // nt_per_sample, jnp.where(r % nt_per_sample >= nct, 1, 0), 0, 0)
    return pl.BlockSpec((1, 1, 6, D_MODEL), index)


def peer_route(h1, mod, wq, subkeys, nt_per_sample, nct):
    T, D = h1.shape
    assert T % TM_E == 0 and TM_E == 2 * TM
    f = lambda n: pltpu.VMEM((n, TM_E), F32)
    entries = PEER_HEADS * PEER_TOPK
    return pl.pallas_call(
        _peer_route_kernel,
        out_shape=[jax.ShapeDtypeStruct((T, D), BF16),
                   jax.ShapeDtypeStruct((T, entries), I32),
                   jax.ShapeDtypeStruct((T, entries), F32)],
        grid=(T // TM_E, PEER_HEADS),
        in_specs=[pl.BlockSpec((TM_E, D), lambda i, h: (i, 0)),
                  _half_tile_mod_spec(0, nt_per_sample, nct), _half_tile_mod_spec(1, nt_per_sample, nct),
                  pl.BlockSpec((D, PEER_DQ), lambda i, h: (0, h)),
                  pl.BlockSpec((2, PEER_NKEYS, PEER_DQ // 2), lambda i, h: (0, 0, 0))],
        out_specs=[pl.BlockSpec((TM_E, D), lambda i, h: (i, 0)),
                   pl.BlockSpec((TM_E, entries), lambda i, h: (i, 0)),
                   pl.BlockSpec((TM_E, entries), lambda i, h: (i, 0))],
        scratch_shapes=[f(16), f(16), f(16), f(16), f(_PEER_NCAND), f(_PEER_NCAND), f(16), f(16),
                        f(entries), f(entries)],
        compiler_params=_cparams(("parallel", "arbitrary")), name="peer_route")(h1, mod, mod, wq, subkeys)


def _peer_expert_kernel(t_ref, ids_ref, gate_ref, u_ref, v_ref, x_ref, mod0_ref, mod1_ref, lng_ref, lnb_ref, o_ref,
                        a_sc, wgt_sc, *, nchunk):
    j = pl.program_id(1)
    NK, CA = PEER_NKEYS, PEER_CA
    w_sc = a_sc

    @pl.when(j < nchunk)
    def _():
        a = _dot(t_ref[...], u_ref[0])
        a3 = jnp.stack([a[:, al * NK:(al + 1) * NK] for al in range(CA)], axis=0)
        a_sc[:, pl.ds(pl.multiple_of(j * CA, CA), CA), :] = pltpu.einshape("atb->tab", a3)

    @pl.when(j == nchunk - 1)
    def _():
        io_k = lax.broadcasted_iota(I32, (NK, NK), 0)
        io_a = lax.broadcasted_iota(I32, (NK, 2 * NK), 0)

        def masks(two):
            idp = ids_ref[two, :]
            ids = jnp.concatenate([idp[0:1], idp[1:2]], axis=1)
            blocks = [jnp.where(io_k == (idp[tok:tok + 1] & (NK - 1)), 1.0, 0.0).astype(BF16) for tok in range(2)]
            zero = jnp.zeros((NK, NK), BF16)
            onehot = jnp.concatenate([jnp.concatenate([blocks[0], zero], axis=1),
                                      jnp.concatenate([zero, blocks[1]], axis=1)], axis=0)
            return onehot, io_a == (ids >> 7)

        def pick(p, carry):
            two = pl.ds(pl.multiple_of(p * 2, 2), 2)
            onehot, r1 = masks(two)
            a2 = a_sc[two]
            at = jnp.concatenate([a2[0], a2[1]], axis=1).astype(BF16)
            picked = _dot(at, onehot)
            act = jnp.sum(jnp.where(r1, picked, 0.0), axis=0, keepdims=True)
            gp = gate_ref[two, :]
            w = jnp.concatenate([gp[0:1], gp[1:2]], axis=1) * _gelu(act)
            wgt_sc[two, :] = jnp.concatenate([w[:, :NK], w[:, NK:]], axis=0)
            return carry

        def scatter(p, carry):
            two = pl.ds(pl.multiple_of(p * 2, 2), 2)
            onehot, r1 = masks(two)
            wp = wgt_sc[two, :]
            w = jnp.concatenate([wp[0:1], wp[1:2]], axis=1)
            wt = _dot_nt(jnp.where(r1, w, 0.0).astype(BF16), onehot)
            w_sc[two] = jnp.stack([wt[:, :NK], wt[:, NK:]], axis=0)
            return carry

        lax.fori_loop(0, TM_E // 2, pick, 0, unroll=16)
        lax.fori_loop(0, TM_E // 2, scatter, 0, unroll=16)
        o_ref[...] = jnp.zeros_like(o_ref)

    @pl.when(j >= nchunk)
    def _():
        w3 = pltpu.einshape("tab->atb", w_sc[:, pl.ds(pl.multiple_of((j - nchunk) * CA, CA), CA), :])
        w = jnp.concatenate([w3[al] for al in range(CA)], axis=1).astype(BF16)
        o_ref[...] += _dot(w, v_ref[...])

    @pl.when(j == 2 * nchunk - 1)
    def _():
        for half, mod_ref in enumerate((mod0_ref, mod1_ref)):
            rows = slice(half * TM, (half + 1) * TM)
            y = ALPHA * x_ref[rows, :] + mod_ref[0, 0][5:6, :] * o_ref[rows, :]
            o_ref[rows, :] = _ln(y) * lng_ref[...] + lnb_ref[...]


def peer_experts(t, ids, gates, emb_u, emb_v, h1, mod, lng, lnb, nt_per_sample, nct):
    T, D = h1.shape
    E = emb_v.shape[0]
    ce = PEER_CA * PEER_NKEYS
    nchunk = E // ce
    emb_ut = emb_u.reshape(nchunk, ce, D).transpose(0, 2, 1)
    assert T % TM_E == 0 and TM_E == 2 * TM
    tok = lambda n, **kw: pl.BlockSpec((TM_E, n), lambda i, j: (i, 0), **kw)
    mod_spec = lambda half: _half_tile_mod_spec(half, nt_per_sample, nct)
    once = dict(pipeline_mode=pl.Buffered(1))
    return pl.pallas_call(
        functools.partial(_peer_expert_kernel, nchunk=nchunk),
        out_shape=jax.ShapeDtypeStruct((T, D), F32), grid=(T // TM_E, 2 * nchunk),
        in_specs=[tok(D, **once), tok(LANES, **once), tok(LANES, **once),
                  pl.BlockSpec((1, D, ce), lambda i, j: (jnp.minimum(j, nchunk - 1), 0, 0)),
                  pl.BlockSpec((ce, D), lambda i, j: (jnp.maximum(j - nchunk, 0), 0)),
                  tok(D, **once), mod_spec(0), mod_spec(1),
                  pl.BlockSpec((1, D), lambda i, j: (0, 0)), pl.BlockSpec((1, D), lambda i, j: (0, 0))],
        out_specs=tok(D),
        scratch_shapes=[pltpu.VMEM((TM_E, PEER_NKEYS, PEER_NKEYS), F32), pltpu.VMEM((TM_E, LANES), F32)],
        compiler_params=_cparams(("parallel", "arbitrary")), name="peer_experts")(t, ids, gates, emb_ut, emb_v, h1, mod, mod, lng, lnb)


_ROPE_IDX = np.arange(HD)
_ROPE_PERM = np.where(_ROPE_IDX % 32 < 16, _ROPE_IDX + 16, _ROPE_IDX - 16)
_ROPE_SIGN = np.where(_ROPE_IDX % 32 < 16, -1.0, 1.0).astype(np.float32)


def _rope_partner(w):
    n = w.shape[1] // HD
    perm = np.concatenate([h * HD + _ROPE_PERM for h in range(n)])
    return w[:, perm] * jnp.asarray(np.tile(_ROPE_SIGN, n))


def _rope_tables(L, nctx, gq, gk):
    rows = L // GRID_W
    row = jnp.repeat(jnp.arange(rows, dtype=F32), GRID_W)
    col = jnp.tile(jnp.arange(GRID_W, dtype=F32), rows)
    nf = HD // 4
    inv = ROPE_THETA ** (-jnp.arange(nf, dtype=F32) / nf)
    ar, ac = row[:, None] * inv, col[:, None] * inv
    cos = jnp.concatenate([jnp.cos(ar), jnp.cos(ar), jnp.cos(ac), jnp.cos(ac)], axis=1)
    sin = jnp.concatenate([jnp.sin(ar), jnp.sin(ar), jnp.sin(ac), jnp.sin(ac)], axis=1)
    cos = jnp.concatenate([jnp.ones((nctx, HD), F32), cos], axis=0)
    sin = jnp.concatenate([jnp.zeros((nctx, HD), F32), sin], axis=0)
    scale = HD ** -0.5
    one = jnp.ones((HD,), F32)
    tabs = []
    for g, sc in ((one, scale), (one, 1.0), (gq, scale), (gk, 1.0)):
        tabs += [cos * (g * sc), sin * (g[_ROPE_PERM] * sc)]
    return jnp.tile(jnp.stack(tabs), (1, 1, 2))


def _s5_tables(a_re, a_im, log_dt, b_re, b_im, c_re, c_im, nb):
    dt = jnp.exp(log_dt)[..., None]
    mag = jnp.exp(a_re * dt)
    lr, li = mag * jnp.cos(a_im * dt), mag * jnp.sin(a_im * dt)
    den = a_re * a_re + a_im * a_im
    cr = ((lr - 1) * a_re + li * a_im) / den
    ci = (li * a_re - (lr - 1) * a_im) / den
    br = cr[..., None] * b_re - ci[..., None] * b_im
    bi = cr[..., None] * b_im + ci[..., None] * b_re
    G2, N, C = S5_GROUPS // 2, S5_STATE, S5_GROUP
    eye = jnp.eye(G2, dtype=F32)

    def blockdiag_in(m):
        m = m.reshape(2, 2, G2, N, C)
        return jnp.einsum('dhgnc,gk->dhgckn', m, eye).reshape(2, 2, G2 * C, G2 * N)

    def blockdiag_out(m):
        m = m.reshape(2, 2, G2, C, N)
        return jnp.einsum('dhgcn,gk->dhgnkc', m, eye).reshape(2, 2, G2 * N, G2 * C)

    bblk = jnp.concatenate([blockdiag_in(br), blockdiag_in(bi)], axis=-1).astype(BF16)
    cblk = jnp.concatenate([blockdiag_out(c_re), blockdiag_out(-c_im)], axis=-2).astype(BF16)
    lam = jnp.stack([lr, li], axis=1).reshape(2, 2, 2, G2 * N)
    lam = jnp.tile(lam[:, :, None], (1, 1, nb, 1, 1)).reshape(2, 2, 2 * nb, G2 * N)
    return bblk, cblk, lam


def _blockdiag_ones(group, n=LANES):
    i = np.arange(n) // group
    return jnp.asarray((i[:, None] == i[None, :]).astype(np.float32), dtype=BF16)


def kernel(x, c, ctx, c_ctx, ada_w, ada_b, w_in, b_gate, diff_lam, diff_norm_g, gqa_qnorm_g, gqa_knorm_g,
           s5_a_re, s5_a_im, s5_log_dt, s5_b_re, s5_b_im, s5_c_re, s5_c_im, s5_d, s5_w_glu, s5_b_glu,
           ml_conv_w, ml_conv_b, ml_gate_b, ml_norm_g, w_branch, w_out, ln_mix_g, ln_mix_b, ln_ffn_g, ln_ffn_b,
           peer_wq, peer_subkeys, peer_u, peer_v):
    B, L, D = x.shape
    nctx = ctx.shape[1]
    S = nctx + L
    assert D == D_MODEL and nctx % TM == 0 and L % TM == 0 and L % GRID_W == 0
    nt, nct = S // TM, nctx // TM
    depth = ada_w.shape[0]

    h = jnp.concatenate([ctx, x], axis=1)
    R = -(-(B + 1) // 8) * 8
    cond = jnp.zeros((R, D), F32).at[:B].set(c).at[B].set(c_ctx)
    ones64, ones128 = _blockdiag_ones(HD), _blockdiag_ones(ML_HD, ML_HEADS * ML_HD)
    o = IN_OFFS

    for l in range(depth):
        last = l == depth - 1
        lam_init = 0.8 - 0.6 * math.exp(-0.3 * l)
        m = ada_modulation(cond, ada_w[l].astype(BF16), ada_b[l][None, :])
        mod = jnp.stack([jnp.broadcast_to(m[B], (B, 6 * D)), m[:B]], axis=1).reshape(B, 2, 6, D)

        w = w_in[l]
        seg = lambda i: w[:, o[i]:o[i + 1]]
        dup = lambda t: jnp.concatenate([t[:, :HD], t[:, :HD], t[:, HD:], t[:, HD:]], axis=1)
        def with_ones_slots(t, width):
            heads = t.reshape(D, -1, width)
            return jnp.concatenate([heads, jnp.zeros_like(heads)], axis=2).reshape(D, -1)
        w_a = jnp.concatenate([seg(0), _rope_partner(seg(0)), seg(1), _rope_partner(seg(1)), seg(2),
                               seg(9), _rope_partner(seg(9)), dup(seg(10)), dup(_rope_partner(seg(10))),
                               with_ones_slots(seg(11), HD)], axis=1).astype(BF16)
        w_b = jnp.concatenate([seg(3), seg(4), seg(5), seg(6), seg(7), seg(8),
                               jnp.zeros((D, LANES - 16), F32)], axis=1).astype(BF16)
        gate_b = jnp.concatenate([ml_gate_b[l], jnp.zeros((LANES - 16,), F32)])[None, :]
        tab = _rope_tables(L, nctx, gqa_qnorm_g[l], gqa_knorm_g[l])

        dq, dk, dv, gq, gk, gv = proj_attn(h, mod, w_a, tab, ones64, nct)
        u, mqk, mv, mo, mg = proj_seq(h, mod, w_b, gate_b, nct)

        yd = diff_attention(dq, dk, dv, diff_lam[l], diff_norm_g[l][None, :], nctx, lam_init)
        yg = gqa_attention(gq, gk, gv, nctx)

        bblk, cblk, lam = _s5_tables(s5_a_re[l], s5_a_im[l], s5_log_dt[l], s5_b_re[l], s5_b_im[l],
                                     s5_c_re[l], s5_c_im[l], B)
        sf, sb = s5_scan(u, bblk, cblk, lam, nctx)

        mq, mk = ml_prep(mqk, ml_conv_w[l], ml_conv_b[l][None, :], nctx)
        hf = mlstm_scan(mq, mk, mv, mg, nctx, False)
        hb = mlstm_scan(mq, mk, mv, mg, nctx, True)

        h1 = merge(h, mod, yd, yg, u, sf, sb, hf, hb, mo,
                   s5_d[l][None, :], s5_w_glu[l].astype(BF16), s5_b_glu[l][None, :], ml_norm_g[l][None, :],
                   seg(12).astype(BF16), b_gate[l][None, :], w_branch[l].astype(BF16), w_out[l].astype(BF16),
                   ln_mix_g[l][None, :], ln_mix_b[l][None, :], ones128, nct, last)

        tiles, ctx_tiles = (nt - nct, 0) if last else (nt, nct)
        t, ids, gates = peer_route(h1, mod, peer_wq[l].astype(BF16), peer_subkeys[l].astype(BF16), tiles, ctx_tiles)
        h = peer_experts(t, ids, gates, peer_u[l].astype(BF16), peer_v[l].astype(BF16),
                         h1, mod, ln_ffn_g[l][None, :], ln_ffn_b[l][None, :], tiles, ctx_tiles)
        h = h.reshape(B, -1, D)
    return h
```

```python
import functools
import math

import numpy as np
import jax
import jax.numpy as jnp
from jax import lax
from jax.experimental import pallas as pl
from jax.experimental.pallas import tpu as pltpu

F32 = jnp.float32
BF16 = jnp.bfloat16
I32 = jnp.int32

D_MODEL = 1024
DEPTH = 2
GRID_W = 64
ROPE_THETA = 10000.0
LN_EPS = 1e-6
HD = 64
DIFF_HEADS = 4
DIFF_HPS = 2
GQA_KV = 2
GQA_GPS = 2
S5_GROUP = 16
S5_GROUPS = 32
S5_STATE = 64
ML_HEADS = 4
ML_HD = 128
N_BRANCH = 4
PEER_HEADS = 8
PEER_NKEYS = 128
PEER_TOPK = 16
PEER_DQ = 256
ALPHA = (2 * DEPTH) ** 0.25

LANES = 128
TM = 256
TM_E = 2 * TM
TQ = 256
TK = 2048
S5_TC = 128
ML_T = 256
ML_LOCKSTEP = 4
PEER_CA = 16
VMEM_LIMIT = 56 * 1024 * 1024

IN_SPLITS = (512, 512, 512, 512, 512, 512, 512, 512, 16, 512, 128, 128, N_BRANCH * D_MODEL)
IN_OFFS = tuple(int(v) for v in np.cumsum((0,) + IN_SPLITS))


def _cparams(sem):
    return pltpu.CompilerParams(dimension_semantics=sem, vmem_limit_bytes=VMEM_LIMIT)


def _const_spec(shape):
    nd = len(shape)
    return pl.BlockSpec(shape, lambda *_: (0,) * nd, pipeline_mode=pl.Buffered(1))


def _ln(x):
    xc = x - jnp.mean(x, axis=-1, keepdims=True)
    return xc * lax.rsqrt(jnp.mean(xc * xc, axis=-1, keepdims=True) + LN_EPS)


def _sigmoid(x):
    return 1.0 / (1.0 + jnp.exp(-x))


def _gelu(x):
    return 0.5 * x * (1.0 + lax.erf(x * (2.0 ** -0.5)))


def _dot(a, b):
    return jnp.dot(a, b, preferred_element_type=F32)


def _dot_nt(a, b):
    return lax.dot_general(a, b, (((1,), (1,)), ((), ())), preferred_element_type=F32)


def _dot_tn(a, b):
    return lax.dot_general(a, b, (((0,), (0,)), ((), ())), preferred_element_type=F32)


def _ada_kernel(c_ref, w_ref, b_ref, o_ref):
    c = c_ref[...]
    o_ref[...] = _dot((c * _sigmoid(c)).astype(BF16), w_ref[...]) + b_ref[...]


def ada_modulation(cond, w, b):
    R, D = cond.shape
    N = w.shape[1]
    tn = 1536
    return pl.pallas_call(
        _ada_kernel, out_shape=jax.ShapeDtypeStruct((R, N), F32), grid=(N // tn,),
        in_specs=[pl.BlockSpec((R, D), lambda j: (0, 0)), pl.BlockSpec((D, tn), lambda j: (0, j)),
                  pl.BlockSpec((1, tn), lambda j: (0, j))],
        out_specs=pl.BlockSpec((R, tn), lambda j: (0, j)),
        compiler_params=_cparams(("arbitrary",)), name="ada")(cond, w, b)


def _proj_attn_kernel(x_ref, mod_ref, w_ref, tab_ref, ones_ref,
                      dq_ref, dk_ref, dv_ref, gq_ref, gk_ref, gv_ref):
    mod = mod_ref[0, 0]
    xm = (_ln(x_ref[0]) * (1.0 + mod[1:2, :]) + mod[0:1, :]).astype(BF16)

    def mm(lo, n):
        return _dot(xm, w_ref[:, lo:lo + n])

    def rope_store(ref, lo, n, ci, norm):
        t, tp = mm(lo, n), mm(lo + n, n)
        c, s = tab_ref[ci], tab_ref[ci + 1]
        for j in range(n // LANES):
            sl = slice(j * LANES, (j + 1) * LANES)
            tb = t[:, sl]
            y = tb * c + tp[:, sl] * s
            if norm:
                ss = _dot((tb * tb).astype(BF16), ones_ref[...])
                y = y * lax.rsqrt(ss * (1.0 / HD) + LN_EPS)
            ref[0, :, sl] = y.astype(ref.dtype)

    def value_store(ref, lo, n, width):
        lane = lax.broadcasted_iota(I32, (1, n), 1)
        ref[0] = (mm(lo, n) + jnp.where(lane % (2 * width) >= width, 1.0, 0.0)).astype(ref.dtype)

    rope_store(dq_ref, 0, 512, 0, False)
    rope_store(dk_ref, 1024, 512, 2, False)
    dv_ref[0] = mm(2048, 512).astype(dv_ref.dtype)
    rope_store(gq_ref, 2560, 512, 4, True)
    rope_store(gk_ref, 3584, 256, 6, True)
    value_store(gv_ref, 4096, 256, HD)


def proj_attn(xa, mod, w_a, tab, ones_bd, nct):
    B, S, D = xa.shape
    tok = lambda n: pl.BlockSpec((1, TM, n), lambda b, i: (b, i, 0))
    widths = (512, 512, 512, 512, 256, 256)
    outs = [jax.ShapeDtypeStruct((B, S, n), BF16) for n in widths]
    return pl.pallas_call(
        _proj_attn_kernel, out_shape=outs, grid=(B, S // TM),
        in_specs=[tok(D),
                  pl.BlockSpec((1, 1, 6, D), lambda b, i: (b, jnp.where(i >= nct, 1, 0), 0, 0)),
                  _const_spec(w_a.shape),
                  pl.BlockSpec((8, TM, LANES), lambda b, i: (0, i, 0)),
                  _const_spec(ones_bd.shape)],
        out_specs=[tok(n) for n in widths],
        compiler_params=_cparams(("parallel", "parallel")), name="proj_attn")(xa, mod, w_a, tab, ones_bd)


def _proj_seq_kernel(x_ref, mod_ref, w_ref, gb_ref, u_ref, qk_ref, v_ref, o_ref, g_ref):
    mod = mod_ref[0, 0]
    xm = (_ln(x_ref[0]) * (1.0 + mod[1:2, :]) + mod[0:1, :]).astype(BF16)
    u_ref[...] = _dot(xm, w_ref[:, 0:512])
    qk_ref[0] = _dot(xm, w_ref[:, 512:1536])
    v_ref[0] = _dot(xm, w_ref[:, 1536:2048])
    o_ref[0] = _dot(xm, w_ref[:, 2048:2560])
    g_ref[0] = _dot(xm, w_ref[:, 2560:2688]) + gb_ref[...]


def proj_seq(xa, mod, w_b, gate_b, nct):
    B, S, D = xa.shape
    tok = lambda n: pl.BlockSpec((1, TM, n), lambda b, i: (b, i, 0))
    widths = (1024, 512, 512, LANES)
    return pl.pallas_call(
        _proj_seq_kernel,
        out_shape=[jax.ShapeDtypeStruct((S, B * 512), F32)] + [jax.ShapeDtypeStruct((B, S, n), F32) for n in widths],
        grid=(B, S // TM),
        in_specs=[tok(D),
                  pl.BlockSpec((1, 1, 6, D), lambda b, i: (b, jnp.where(i >= nct, 1, 0), 0, 0)),
                  _const_spec(w_b.shape), _const_spec(gate_b.shape)],
        out_specs=[pl.BlockSpec((TM, 512), lambda b, i: (i, b))] + [tok(n) for n in widths],
        compiler_params=_cparams(("parallel", "parallel")), name="proj_seq")(xa, mod, w_b, gate_b)


def _flash(qqs, k_ref, v_ref, nctx, n_lat, tk, dv):
    P = len(qqs)
    R, W = qqs[0].shape[0], v_ref.shape[2] // P
    mxu_sums = dv < W

    def finish(carry, s, v):
        m, l, acc = carry
        m_new = jnp.maximum(m, jnp.max(s, axis=1, keepdims=True))
        alpha = jnp.exp(m - m_new)
        if mxu_sums:
            p = jnp.exp((s - m_new).astype(BF16))
        else:
            p = jnp.exp(s - m_new)
            l = alpha * l + jnp.sum(p, axis=1, keepdims=True)
        return m_new, l, alpha * acc + _dot(p.astype(BF16), v)

    carries = [(jnp.full((R, 1), -jnp.inf, F32), jnp.zeros((R, 1), F32), jnp.zeros((R, W), F32))] * P
    chunks = [pl.ds(0, nctx)] + [pl.ds(nctx + c * tk, tk) for c in range(n_lat)]
    for rows in chunks:
        scores = [_dot_nt(qqs[p], k_ref[0, rows, p * LANES:(p + 1) * LANES]) for p in range(P)]
        carries = [finish(carries[p], scores[p], v_ref[0, rows, p * W:(p + 1) * W]) for p in range(P)]
    return [acc[:, :dv] / (acc[:, dv:dv + 1] if mxu_sums else l) for _, l, acc in carries]


def _ctx_or_all(i, nctx, n_lat, attend):
    @pl.when(i < nctx // TQ)
    def _():
        attend(0)

    @pl.when(i >= nctx // TQ)
    def _():
        attend(n_lat)


def _diff_attn_kernel(q_ref, k_ref, v_ref, lam_ref, g_ref, o_ref, *, nctx, n_lat, tk, lam_init):
    i = pl.program_id(2)
    lane = lax.broadcasted_iota(I32, (TQ, LANES), 1)
    zero = jnp.zeros((TQ, LANES), q_ref.dtype)
    qqs = []
    for h in range(DIFF_HPS):
        q = q_ref[0, :, h * LANES:(h + 1) * LANES]
        qqs.append(jnp.concatenate([jnp.where(lane < HD, q, zero), jnp.where(lane >= HD, q, zero)], axis=0))
    lv = lam_ref[...]
    lam = (jnp.exp(jnp.sum(lv[0:1] * lv[1:2], axis=1, keepdims=True))
           - jnp.exp(jnp.sum(lv[2:3] * lv[3:4], axis=1, keepdims=True)) + lam_init)

    def attend(n):
        for h, o in enumerate(_flash(qqs, k_ref, v_ref, nctx, n, tk, 2 * HD)):
            d = o[:TQ] - lam * o[TQ:]
            y = d * lax.rsqrt(jnp.mean(d * d, axis=-1, keepdims=True) + LN_EPS)
            o_ref[0, :, h * LANES:(h + 1) * LANES] = (y * g_ref[...] * (1.0 - lam_init)).astype(o_ref.dtype)

    _ctx_or_all(i, nctx, n_lat, attend)


def diff_attention(q, k, v, lam_vec, norm_g, nctx, lam_init):
    B, S, _ = q.shape
    tk = math.gcd(S - nctx, TK)
    kern = functools.partial(_diff_attn_kernel, nctx=nctx, n_lat=(S - nctx) // tk, tk=tk, lam_init=lam_init)
    return pl.pallas_call(
        kern, out_shape=jax.ShapeDtypeStruct((B, S, 512), BF16), grid=(B, DIFF_HEADS // DIFF_HPS, S // TQ),
        in_specs=[pl.BlockSpec((1, TQ, DIFF_HPS * LANES), lambda b, h, i: (b, i, h)),
                  pl.BlockSpec((1, S, DIFF_HPS * LANES), lambda b, h, i: (b, 0, h)),
                  pl.BlockSpec((1, S, DIFF_HPS * LANES), lambda b, h, i: (b, 0, h)),
                  pl.BlockSpec((4, HD), lambda b, h, i: (0, 0)),
                  pl.BlockSpec((1, LANES), lambda b, h, i: (0, 0))],
        out_specs=pl.BlockSpec((1, TQ, DIFF_HPS * LANES), lambda b, h, i: (b, i, h)),
        compiler_params=_cparams(("parallel", "parallel", "parallel")), name="diff_attn")(q, k, v, lam_vec, norm_g)


def _gqa_attn_kernel(q_ref, k_ref, v_ref, o_ref, *, nctx, n_lat, tk):
    i = pl.program_id(2)
    lane = lax.broadcasted_iota(I32, (TQ, LANES), 1)
    zero = jnp.zeros((TQ, LANES), q_ref.dtype)
    qqs = []
    for g in range(GQA_GPS):
        parts = []
        for j in range(2 * g, 2 * g + 2):
            blk = q_ref[0, :, j * LANES:(j + 1) * LANES]
            parts += [jnp.where(lane < HD, blk, zero), jnp.where(lane >= HD, blk, zero)]
        qqs.append(jnp.concatenate(parts, axis=0))

    def attend(n):
        for g, o in enumerate(_flash(qqs, k_ref, v_ref, nctx, n, tk, HD)):
            heads = jnp.concatenate([o[h * TQ:(h + 1) * TQ] for h in range(4)], axis=1)
            o_ref[0, :, g * 2 * LANES:(g + 1) * 2 * LANES] = heads.astype(o_ref.dtype)

    _ctx_or_all(i, nctx, n_lat, attend)


def gqa_attention(q, k, v, nctx):
    B, S, _ = q.shape
    tk = math.gcd(S - nctx, TK)
    kern = functools.partial(_gqa_attn_kernel, nctx=nctx, n_lat=(S - nctx) // tk, tk=tk)
    return pl.pallas_call(
        kern, out_shape=jax.ShapeDtypeStruct((B, S, 512), BF16), grid=(B, GQA_KV // GQA_GPS, S // TQ),
        in_specs=[pl.BlockSpec((1, TQ, GQA_GPS * 2 * LANES), lambda b, g, i: (b, i, g)),
                  pl.BlockSpec((1, S, GQA_GPS * LANES), lambda b, g, i: (b, 0, g)),
                  pl.BlockSpec((1, S, GQA_GPS * LANES), lambda b, g, i: (b, 0, g))],
        out_specs=pl.BlockSpec((1, TQ, GQA_GPS * 2 * LANES), lambda b, g, i: (b, i, g)),
        compiler_params=_cparams(("parallel", "parallel", "parallel")), name="gqa_attn")(q, k, v)


def _s5_kernel(uf_ref, ub_ref, bblk_ref, cblk_ref, lam_ref, yf_ref, yb_ref, buf_f, buf_b, st_ref):
    H = S5_GROUPS * S5_STATE // 2
    R = buf_f.shape[0]
    RB = 256
    TB = RB // 8

    @pl.when(pl.program_id(0) == 0)
    def _():
        st_ref[...] = jnp.zeros_like(st_ref)

    half0 = (lax.broadcasted_iota(I32, (RB, 1), 0) & 1) == 0

    for d, (u_ref, buf) in enumerate(((uf_ref, buf_f), (ub_ref, buf_b))):
        for rb in range(R // RB):
            rows = slice(rb * RB, (rb + 1) * RB)
            ut = u_ref[rb * TB:(rb + 1) * TB, :]
            u3 = jnp.stack([ut[:, s * 256:(s + 1) * 256] for s in range(8)], axis=0)
            uh = pltpu.einshape("stc->tsc", u3).reshape(RB, 256).astype(BF16)
            for cols in (slice(0, H), slice(H, 2 * H)):
                buf[rows, cols] = jnp.where(half0, _dot(uh, bblk_ref[d, 0, :, cols]), _dot(uh, bblk_ref[d, 1, :, cols]))

    lfr, lfi, lbr, lbi = lam_ref[0, 0], lam_ref[0, 1], lam_ref[1, 0], lam_ref[1, 1]

    def step(t, carry):
        fr, fi, br, bi = carry
        rf = pl.ds(pl.multiple_of(t * 8, 8), 8)
        x = buf_f[rf, :]
        nfr = lfr * fr - lfi * fi + x[:, :H]
        nfi = lfr * fi + lfi * fr + x[:, H:]
        buf_f[rf, :] = jnp.concatenate([nfr, nfi], axis=1)
        rb = pl.ds(pl.multiple_of((S5_TC - 1 - t) * 8, 8), 8)
        z = buf_b[rb, :]
        nbr = lbr * br - lbi * bi + z[:, :H]
        nbi = lbr * bi + lbi * br + z[:, H:]
        buf_b[rb, :] = jnp.concatenate([nbr, nbi], axis=1)
        return nfr, nfi, nbr, nbi

    fin = lax.fori_loop(0, S5_TC, step, (st_ref[0], st_ref[1], st_ref[2], st_ref[3]))
    for j in range(4):
        st_ref[j] = fin[j]

    for d, (y_ref, buf) in enumerate(((yf_ref, buf_f), (yb_ref, buf_b))):
        for rb in range(R // RB):
            rows = slice(rb * RB, (rb + 1) * RB)
            h = buf[rows, :].astype(BF16)
            y = jnp.where(half0, _dot(h, cblk_ref[d, 0]), _dot(h, cblk_ref[d, 1]))
            y3 = pltpu.einshape("tsc->stc", y.reshape(TB, 8, 256))
            y_ref[rb * TB:(rb + 1) * TB, :] = jnp.concatenate([y3[s] for s in range(8)], axis=1)


def s5_scan(u, bblk, cblk, lam, nctx):
    S, W = u.shape
    assert W == 8 * 256, "the scan packs (sample, half) pairs into the eight sublanes of a vreg"
    nch, nc0 = S // S5_TC, nctx // S5_TC

    def bwd(i):
        return jnp.where(i < nc0, nc0 - 1 - i, (nch - 1) - (i - nc0))
    H2 = S5_GROUPS * S5_STATE
    R = S5_TC * 8
    blk = lambda f: pl.BlockSpec((S5_TC, W), f)
    return pl.pallas_call(
        _s5_kernel, out_shape=[jax.ShapeDtypeStruct(u.shape, F32)] * 2, grid=(nch,),
        in_specs=[blk(lambda i: (i, 0)), blk(lambda i: (bwd(i), 0)),
                  _const_spec(bblk.shape), _const_spec(cblk.shape), _const_spec(lam.shape)],
        out_specs=[blk(lambda i: (i, 0)), blk(lambda i: (bwd(i), 0))],
        scratch_shapes=[pltpu.VMEM((R, H2), F32), pltpu.VMEM((R, H2), F32), pltpu.VMEM((4, 8, H2 // 2), F32)],
        compiler_params=_cparams(("arbitrary",)), name="s5_scan")(u, u, bblk, cblk, lam)


def _ml_prep_kernel(x_ref, prev_ref, next_ref, w_ref, b_ref, q_ref, k_ref, *, seg_starts, seg_ends):
    i = pl.program_id(1)
    x = x_ref[0]
    row = lax.broadcasted_iota(I32, x.shape, 0)
    first = functools.reduce(jnp.logical_or, [i == s for s in seg_starts])
    last = functools.reduce(jnp.logical_or, [i == s for s in seg_ends])
    pr = jnp.where(first, 0.0, prev_ref[0, 7:8, :])
    nx = jnp.where(last, 0.0, next_ref[0, 0:1, :])
    xp = jnp.where(row == 0, pr, pltpu.roll(x, 1, 0))
    xn = jnp.where(row == TM - 1, nx, pltpu.roll(x, TM - 1, 0))
    w = w_ref[...]
    y = b_ref[...] + xp * w[0:1, :] + x * w[1:2, :] + xn * w[2:3, :]
    y = y * _sigmoid(y)
    q_ref[0] = y[:, :512].astype(q_ref.dtype)
    k_ref[0] = (y[:, 512:] * (ML_HD ** -0.5)).astype(k_ref.dtype)


def ml_prep(qk, conv_w, conv_b, nctx):
    B, S, W = qk.shape
    nt, nct, r8 = S // TM, nctx // TM, TM // 8
    kern = functools.partial(_ml_prep_kernel, seg_starts=(0, nct), seg_ends=(nct - 1, nt - 1))
    return pl.pallas_call(
        kern, out_shape=[jax.ShapeDtypeStruct((B, S, 512), BF16)] * 2, grid=(B, nt),
        in_specs=[pl.BlockSpec((1, TM, W), lambda b, i: (b, i, 0)),
                  pl.BlockSpec((1, 8, W), lambda b, i: (b, jnp.maximum(i * r8 - 1, 0), 0)),
                  pl.BlockSpec((1, 8, W), lambda b, i: (b, jnp.minimum((i + 1) * r8, S // 8 - 1), 0)),
                  pl.BlockSpec((3, W), lambda b, i: (0, 0)), pl.BlockSpec((1, W), lambda b, i: (0, 0))],
        out_specs=[pl.BlockSpec((1, TM, 512), lambda b, i: (b, i, 0))] * 2,
        compiler_params=_cparams(("parallel", "parallel")), name="ml_prep")(qk, qk, qk, conv_w, conv_b)


def _log_sigmoid(x):
    return jnp.minimum(x, 0.0) - jnp.log1p(jnp.exp(-jnp.abs(x)))


def _mlstm_kernel(q_ref, k_ref, v_ref, g_ref, h_ref, c_sc, n_sc, m_sc, *, reverse):
    T = ML_T
    gi, gf = (8, 12) if reverse else (0, 4)

    @pl.when(pl.program_id(0) == 0)
    def _():
        c_sc[...] = jnp.zeros_like(c_sc)
        n_sc[...] = jnp.zeros_like(n_sc)
        m_sc[...] = jnp.zeros_like(m_sc)

    r = lax.broadcasted_iota(I32, (T, T), 0)
    c = lax.broadcasted_iota(I32, (T, T), 1)
    mask = (c >= r) if reverse else (c <= r)
    tri = jnp.where(mask, 1.0, 0.0)
    hp = lax.Precision.HIGHEST
    end = 0 if reverse else T - 1
    def gates(b):
        g = g_ref[b]
        gt = g.T
        bcol_all = jnp.dot(tri, _log_sigmoid(g), preferred_element_type=F32, precision=hp)
        brow_all = lax.dot_general(_log_sigmoid(gt[0:16]), tri, (((1,), (1,)), ((), ())),
                                   preferred_element_type=F32, precision=hp)
        return g, gt, bcol_all, brow_all

    def chain(b, hh, g, gt, bcol_all, brow_all):
        st = b * ML_HEADS + hh
        sl = slice(hh * ML_HD, (hh + 1) * ML_HD)
        q, k, v = q_ref[b, :, sl], k_ref[b, :, sl], v_ref[b, :, sl]
        bcol, brow = bcol_all[:, gf + hh:gf + hh + 1], brow_all[gf + hh:gf + hh + 1, :]
        icol, irow = g[:, gi + hh:gi + hh + 1], gt[gi + hh:gi + hh + 1, :]
        m_old = m_sc[st][:, 0:1]
        n_old = n_sc[st]
        c_old = c_sc[st]
        logw = jnp.where(mask, bcol - brow + irow, -jnp.inf)
        m_inter = bcol + m_old
        m_t = jnp.maximum(m_inter, jnp.max(logw, axis=1, keepdims=True))
        yield
        s = _dot_nt(q, k) * jnp.exp(logw - m_t)
        inter = jnp.exp(m_inter - m_t)
        yield
        num = _dot(s.astype(BF16), v.astype(BF16)) + inter * _dot_nt(q, c_old.astype(BF16))
        den = jnp.sum(s, axis=1, keepdims=True) + inter * jnp.sum(q.astype(F32) * n_old, axis=1, keepdims=True)
        h_ref[b, :, sl] = num / jnp.maximum(jnp.abs(den), jnp.exp(-m_t))
        yield
        b_end = bcol[end:end + 1, :]
        g_row, g_col = b_end - brow + irow, b_end - bcol + icol
        m_new = jnp.maximum(b_end + m_old, jnp.max(g_row, axis=1, keepdims=True))
        decay = jnp.exp(b_end + m_old - m_new)
        wk = jnp.exp(g_col - m_new)
        c_sc[st] = decay * c_old + _dot_tn((v * wk).astype(BF16), k)
        n_sc[st] = decay * n_old + jnp.sum(k.astype(F32) * wk, axis=0, keepdims=True)
        m_sc[st] = jnp.broadcast_to(m_new, (1, LANES))

    nb = q_ref.shape[0]
    for b0 in range(0, nb, ML_LOCKSTEP):
        chains = []
        for b in range(b0, min(b0 + ML_LOCKSTEP, nb)):
            shared = gates(b)
            chains += [chain(b, hh, *shared) for hh in range(ML_HEADS)]
        for _ in range(4):
            for ch in chains:
                next(ch, None)


def mlstm_scan(q, k, v, g, nctx, reverse):
    B, S, W = q.shape
    nch, nc0 = S // ML_T, nctx // ML_T

    def order(i):
        return jnp.where(i < nc0, nc0 - 1 - i, (nch - 1) - (i - nc0)) if reverse else i
    blk = lambda n: pl.BlockSpec((B, ML_T, n), lambda i: (0, order(i), 0))
    nst = B * ML_HEADS
    return pl.pallas_call(
        functools.partial(_mlstm_kernel, reverse=reverse),
        out_shape=jax.ShapeDtypeStruct((B, S, W), F32), grid=(nch,),
        in_specs=[blk(W), blk(W), blk(W), blk(LANES)], out_specs=blk(W),
        scratch_shapes=[pltpu.VMEM((nst, ML_HD, ML_HD), F32), pltpu.VMEM((nst, 1, ML_HD), F32),
                        pltpu.VMEM((nst, 1, LANES), F32)],
        compiler_params=_cparams(("arbitrary",)), name="mlstm_bwd" if reverse else "mlstm_fwd")(q, k, v, g)


def _merge_kernel(x_ref, mod_ref, yd_ref, yg_ref, u_ref, sf_ref, sb_ref, hf_ref, hb_ref, mo_ref,
                  s5d_ref, wglu_ref, bglu_ref, mlg_ref, wgate_ref, bgate_ref, wbr_ref, wout_ref,
                  lng_ref, lnb_ref, ones_ref, o_ref):
    x = x_ref[0]
    mod = mod_ref[0, 0]
    xm = (_ln(x) * (1.0 + mod[1:2, :]) + mod[0:1, :]).astype(BF16)
    ys = u_ref[...] * s5d_ref[...] + sf_ref[...] + sb_ref[...]
    z = _dot(_gelu(ys).astype(BF16), wglu_ref[...]) + bglu_ref[...]
    ys = z[:, :512] * _sigmoid(z[:, 512:])
    hm = hf_ref[0] + hb_ref[0]
    ss = _dot((hm * hm).astype(BF16), ones_ref[...])
    ym = hm * lax.rsqrt(ss * (1.0 / ML_HD) + LN_EPS) * mlg_ref[...] * _sigmoid(mo_ref[0])
    branches = (yd_ref[0], ys.astype(BF16), ym.astype(BF16), yg_ref[0])
    merged = None
    for j, yb in enumerate(branches):
        gate = _sigmoid(_dot(xm, wgate_ref[:, j * D_MODEL:(j + 1) * D_MODEL]) + bgate_ref[:, j * D_MODEL:(j + 1) * D_MODEL])
        term = gate * _dot(yb, wbr_ref[j])
        merged = term if merged is None else merged + term
    mix = _dot(merged.astype(BF16), wout_ref[...])
    o_ref[...] = _ln(ALPHA * x + mod[2:3, :] * mix) * lng_ref[...] + lnb_ref[...]


def merge(xa, mod, yd, yg, u, sf, sb, hf, hb, mo, s5d, wglu, bglu, mlg, wgate, bgate, wbr, wout, lng, lnb, ones_ml, nct,
          latent_only):
    B, S, D = xa.shape
    first = nct if latent_only else 0
    nti = S // TM - first
    tok = lambda n: pl.BlockSpec((1, TM, n), lambda b, i: (b, i + first, 0))
    consts = (s5d, wglu, bglu, mlg, wgate, bgate, wbr, wout, lng, lnb, ones_ml)
    return pl.pallas_call(
        _merge_kernel, out_shape=jax.ShapeDtypeStruct((B * nti * TM, D), F32), grid=(B, nti),
        in_specs=[tok(D), pl.BlockSpec((1, 1, 6, D), lambda b, i: (b, jnp.where(i + first >= nct, 1, 0), 0, 0))]
        + [tok(512)] * 2 + [pl.BlockSpec((TM, 512), lambda b, i: (i + first, b))] * 3 + [tok(512)] * 3
        + [_const_spec(a.shape) for a in consts],
        out_specs=pl.BlockSpec((TM, D), lambda b, i: (b * nti + i, 0)),
        compiler_params=_cparams(("parallel", "parallel")), name="merge")(xa, mod, yd, yg, u, sf, sb, hf, hb, mo, *consts)


def _top16(*problems):
    iotas = {prob[0].shape: lax.broadcasted_iota(I32, prob[0].shape, 0).astype(F32) for prob in problems}

    def one(kk, s, val_ref, idx_ref, payload):
        R, ri = s.shape[0], iotas[s.shape]
        m = jnp.max(s, axis=0, keepdims=True)
        ix = jnp.min(jnp.where(s == m, ri, float(R)), axis=0, keepdims=True)
        hit = ri == ix
        val_ref[pl.ds(kk, 1), :] = m
        idx_ref[pl.ds(kk, 1), :] = ix if payload is None else jnp.max(jnp.where(hit, payload, -1.0), axis=0, keepdims=True)
        return jnp.where(hit, -jnp.inf, s)

    def body(kk, ss):
        return tuple(one(kk, s, *prob[1:]) for s, prob in zip(ss, problems))

    lax.fori_loop(0, PEER_TOPK, body, tuple(prob[0] for prob in problems))


_PEER_CAND_COUNTS = tuple(PEER_TOPK // (p + 1) for p in range(8))
_PEER_CAND_ROWS = tuple((sum(_PEER_CAND_COUNTS[:p]), n) for p, n in enumerate(_PEER_CAND_COUNTS))
_PEER_NCAND = -(-(sum(_PEER_CAND_COUNTS) + 8) // 8) * 8


def _peer_route_kernel(x_ref, mod0_ref, mod1_ref, wq_ref, sk_ref, t_ref, ids_ref, gate_ref,
                       v1_sc, i1_sc, v2_sc, i2_sc, cand_sc, cidx_sc, top_sc, tid_sc, ids_sc, gates_sc):
    h = pl.program_id(1)

    @pl.when(h == 0)
    def _():
        for half, mod_ref in enumerate((mod0_ref, mod1_ref)):
            rows = slice(half * TM, (half + 1) * TM)
            mod = mod_ref[0, 0]
            t_ref[rows, :] = (_ln(x_ref[rows, :]) * (1.0 + mod[4:5, :]) + mod[3:4, :]).astype(BF16)

    q = _dot(t_ref[...], wq_ref[...]).astype(BF16)
    half = PEER_DQ // 2
    s1, s2 = _dot_nt(sk_ref[0], q[:, :half]), _dot_nt(sk_ref[1], q[:, half:])
    _top16((s1, v1_sc, i1_sc, None))
    _top16((s2, v2_sc, i2_sc, None))
    nk = float(PEER_NKEYS)
    for p, (lo, n) in enumerate(_PEER_CAND_ROWS):
        cand_sc[lo:lo + n, :] = v1_sc[p:p + 1, :] + v2_sc[0:n, :]
        cidx_sc[lo:lo + n, :] = i1_sc[p:p + 1, :] * nk + i2_sc[0:n, :]
    lo = _PEER_CAND_ROWS[-1][0] + _PEER_CAND_ROWS[-1][1]
    cand_sc[lo:lo + 8, :] = v1_sc[8:16, :] + v2_sc[0:1, :]
    cidx_sc[lo:lo + 8, :] = i1_sc[8:16, :] * nk + i2_sc[0:1, :]
    pad = _PEER_NCAND - (lo + 8)
    cand_sc[lo + 8:, :] = jnp.full((pad, cand_sc.shape[1]), -jnp.inf, F32)
    cidx_sc[lo + 8:, :] = jnp.zeros((pad, cand_sc.shape[1]), F32)
    _top16((cand_sc[...], top_sc, tid_sc, cidx_sc[...]))
    top = top_sc[...]
    e = jnp.exp(top - top[0:1, :])
    entries = pl.ds(pl.multiple_of(h * PEER_TOPK, PEER_TOPK), PEER_TOPK)
    gates_sc[entries, :] = e / jnp.sum(e, axis=0, keepdims=True)
    ids_sc[entries, :] = tid_sc[...]

    @pl.when(h == PEER_HEADS - 1)
    def _():
        gate_ref[...] = gates_sc[...].T
        ids_ref[...] = ids_sc[...].T.astype(I32)


def _half_tile_mod_spec(half, nt_per_sample, nct):
    def index(i, j):
        r = 2 * i + half
        return (r // nt_per_sample, jnp.where(r % nt_per_sample >= nct, 1, 0), 0, 0)
    return pl.BlockSpec((1, 1, 6, D_MODEL), index)


def peer_route(h1, mod, wq, subkeys, nt_per_sample, nct):
    T, D = h1.shape
    assert T % TM_E == 0 and TM_E == 2 * TM
    f = lambda n: pltpu.VMEM((n, TM_E), F32)
    entries = PEER_HEADS * PEER_TOPK
    return pl.pallas_call(
        _peer_route_kernel,
        out_shape=[jax.ShapeDtypeStruct((T, D), BF16),
                   jax.ShapeDtypeStruct((T, entries), I32),
                   jax.ShapeDtypeStruct((T, entries), F32)],
        grid=(T // TM_E, PEER_HEADS),
        in_specs=[pl.BlockSpec((TM_E, D), lambda i, h: (i, 0)),
                  _half_tile_mod_spec(0, nt_per_sample, nct), _half_tile_mod_spec(1, nt_per_sample, nct),
                  pl.BlockSpec((D, PEER_DQ), lambda i, h: (0, h)),
                  pl.BlockSpec((2, PEER_NKEYS, PEER_DQ // 2), lambda i, h: (0, 0, 0))],
        out_specs=[pl.BlockSpec((TM_E, D), lambda i, h: (i, 0)),
                   pl.BlockSpec((TM_E, entries), lambda i, h: (i, 0)),
                   pl.BlockSpec((TM_E, entries), lambda i, h: (i, 0))],
        scratch_shapes=[f(16), f(16), f(16), f(16), f(_PEER_NCAND), f(_PEER_NCAND), f(16), f(16),
                        f(entries), f(entries)],
        compiler_params=_cparams(("parallel", "arbitrary")), name="peer_route")(h1, mod, mod, wq, subkeys)


def _peer_expert_kernel(t_ref, ids_ref, gate_ref, u_ref, v_ref, x_ref, mod0_ref, mod1_ref, lng_ref, lnb_ref, o_ref,
                        a_sc, wgt_sc, *, nchunk):
    j = pl.program_id(1)
    NK, CA = PEER_NKEYS, PEER_CA
    w_sc = a_sc

    @pl.when(j < nchunk)
    def _():
        a = _dot(t_ref[...], u_ref[0]).astype(a_sc.dtype)
        a3 = jnp.stack([a[:, al * NK:(al + 1) * NK] for al in range(CA)], axis=0)
        a_sc[:, pl.ds(pl.multiple_of(j * CA, CA), CA), :] = pltpu.einshape("atb->tab", a3)

    @pl.when(j == nchunk - 1)
    def _():
        io_k = lax.broadcasted_iota(I32, (NK, NK), 0)
        io_a = lax.broadcasted_iota(I32, (NK, 2 * NK), 0)

        def masks(two):
            idp = ids_ref[two, :]
            ids = jnp.concatenate([idp[0:1], idp[1:2]], axis=1)
            blocks = [jnp.where(io_k == (idp[tok:tok + 1] & (NK - 1)), 1.0, 0.0).astype(BF16) for tok in range(2)]
            zero = jnp.zeros((NK, NK), BF16)
            onehot = jnp.concatenate([jnp.concatenate([blocks[0], zero], axis=1),
                                      jnp.concatenate([zero, blocks[1]], axis=1)], axis=0)
            return onehot, io_a == (ids >> 7)

        def pick(p, carry):
            two = pl.ds(pl.multiple_of(p * 2, 2), 2)
            onehot, r1 = masks(two)
            a2 = a_sc[two]
            at = jnp.concatenate([a2[0], a2[1]], axis=1).astype(BF16)
            picked = _dot(at, onehot)
            act = jnp.sum(jnp.where(r1, picked, 0.0), axis=0, keepdims=True)
            gp = gate_ref[two, :]
            w = jnp.concatenate([gp[0:1], gp[1:2]], axis=1) * _gelu(act)
            wgt_sc[two, :] = jnp.concatenate([w[:, :NK], w[:, NK:]], axis=0)
            return carry

        def scatter(p, carry):
            two = pl.ds(pl.multiple_of(p * 2, 2), 2)
            onehot, r1 = masks(two)
            wp = wgt_sc[two, :]
            w = jnp.concatenate([wp[0:1], wp[1:2]], axis=1)
            wt = _dot_nt(jnp.where(r1, w, 0.0).astype(BF16), onehot)
            w_sc[two] = jnp.stack([wt[:, :NK], wt[:, NK:]], axis=0).astype(w_sc.dtype)
            return carry

        lax.fori_loop(0, TM_E // 2, pick, 0, unroll=16)
        lax.fori_loop(0, TM_E // 2, scatter, 0, unroll=16)
        o_ref[...] = jnp.zeros_like(o_ref)

    @pl.when(j >= nchunk)
    def _():
        w3 = pltpu.einshape("tab->atb", w_sc[:, pl.ds(pl.multiple_of((j - nchunk) * CA, CA), CA), :])
        w = jnp.concatenate([w3[al] for al in range(CA)], axis=1).astype(BF16)
        o_ref[...] += _dot(w, v_ref[...])

    @pl.when(j == 2 * nchunk - 1)
    def _():
        for half, mod_ref in enumerate((mod0_ref, mod1_ref)):
            rows = slice(half * TM, (half + 1) * TM)
            y = ALPHA * x_ref[rows, :] + mod_ref[0, 0][5:6, :] * o_ref[rows, :]
            o_ref[rows, :] = _ln(y) * lng_ref[...] + lnb_ref[...]


def peer_experts(t, ids, gates, emb_u, emb_v, h1, mod, lng, lnb, nt_per_sample, nct):
    T, D = h1.shape
    E = emb_v.shape[0]
    ce = PEER_CA * PEER_NKEYS
    nchunk = E // ce
    emb_ut = emb_u.reshape(nchunk, ce, D).transpose(0, 2, 1)
    assert T % TM_E == 0 and TM_E == 2 * TM
    tok = lambda n, **kw: pl.BlockSpec((TM_E, n), lambda i, j: (i, 0), **kw)
    mod_spec = lambda half: _half_tile_mod_spec(half, nt_per_sample, nct)
    once = dict(pipeline_mode=pl.Buffered(1))
    return pl.pallas_call(
        functools.partial(_peer_expert_kernel, nchunk=nchunk),
        out_shape=jax.ShapeDtypeStruct((T, D), F32), grid=(T // TM_E, 2 * nchunk),
        in_specs=[tok(D, **once), tok(LANES, **once), tok(LANES, **once),
                  pl.BlockSpec((1, D, ce), lambda i, j: (jnp.minimum(j, nchunk - 1), 0, 0)),
                  pl.BlockSpec((ce, D), lambda i, j: (jnp.maximum(j - nchunk, 0), 0)),
                  tok(D, **once), mod_spec(0), mod_spec(1),
                  pl.BlockSpec((1, D), lambda i, j: (0, 0)), pl.BlockSpec((1, D), lambda i, j: (0, 0))],
        out_specs=tok(D),
        scratch_shapes=[pltpu.VMEM((TM_E, PEER_NKEYS, PEER_NKEYS), BF16), pltpu.VMEM((TM_E, LANES), F32)],
        compiler_params=_cparams(("parallel", "arbitrary")), name="peer_experts")(t, ids, gates, emb_ut, emb_v, h1, mod, mod, lng, lnb)


_ROPE_IDX = np.arange(HD)
_ROPE_PERM = np.where(_ROPE_IDX % 32 < 16, _ROPE_IDX + 16, _ROPE_IDX - 16)
_ROPE_SIGN = np.where(_ROPE_IDX % 32 < 16, -1.0, 1.0).astype(np.float32)


def _rope_partner(w):
    n = w.shape[1] // HD
    perm = np.concatenate([h * HD + _ROPE_PERM for h in range(n)])
    return w[:, perm] * jnp.asarray(np.tile(_ROPE_SIGN, n))


def _rope_tables(L, nctx, gq, gk):
    rows = L // GRID_W
    row = jnp.repeat(jnp.arange(rows, dtype=F32), GRID_W)
    col = jnp.tile(jnp.arange(GRID_W, dtype=F32), rows)
    nf = HD // 4
    inv = ROPE_THETA ** (-jnp.arange(nf, dtype=F32) / nf)
    ar, ac = row[:, None] * inv, col[:, None] * inv
    cos = jnp.concatenate([jnp.cos(ar), jnp.cos(ar), jnp.cos(ac), jnp.cos(ac)], axis=1)
    sin = jnp.concatenate([jnp.sin(ar), jnp.sin(ar), jnp.sin(ac), jnp.sin(ac)], axis=1)
    cos = jnp.concatenate([jnp.ones((nctx, HD), F32), cos], axis=0)
    sin = jnp.concatenate([jnp.zeros((nctx, HD), F32), sin], axis=0)
    scale = HD ** -0.5
    one = jnp.ones((HD,), F32)
    tabs = []
    for g, sc in ((one, scale), (one, 1.0), (gq, scale), (gk, 1.0)):
        tabs += [cos * (g * sc), sin * (g[_ROPE_PERM] * sc)]
    return jnp.tile(jnp.stack(tabs), (1, 1, 2))


def _s5_tables(a_re, a_im, log_dt, b_re, b_im, c_re, c_im, nb):
    dt = jnp.exp(log_dt)[..., None]
    mag = jnp.exp(a_re * dt)
    lr, li = mag * jnp.cos(a_im * dt), mag * jnp.sin(a_im * dt)
    den = a_re * a_re + a_im * a_im
    cr = ((lr - 1) * a_re + li * a_im) / den
    ci = (li * a_re - (lr - 1) * a_im) / den
    br = cr[..., None] * b_re - ci[..., None] * b_im
    bi = cr[..., None] * b_im + ci[..., None] * b_re
    G2, N, C = S5_GROUPS // 2, S5_STATE, S5_GROUP
    eye = jnp.eye(G2, dtype=F32)

    def blockdiag_in(m):
        m = m.reshape(2, 2, G2, N, C)
        return jnp.einsum('dhgnc,gk->dhgckn', m, eye).reshape(2, 2, G2 * C, G2 * N)

    def blockdiag_out(m):
        m = m.reshape(2, 2, G2, C, N)
        return jnp.einsum('dhgcn,gk->dhgnkc', m, eye).reshape(2, 2, G2 * N, G2 * C)

    bblk = jnp.concatenate([blockdiag_in(br), blockdiag_in(bi)], axis=-1).astype(BF16)
    cblk = jnp.concatenate([blockdiag_out(c_re), blockdiag_out(-c_im)], axis=-2).astype(BF16)
    lam = jnp.stack([lr, li], axis=1).reshape(2, 2, 2, G2 * N)
    lam = jnp.tile(lam[:, :, None], (1, 1, nb, 1, 1)).reshape(2, 2, 2 * nb, G2 * N)
    return bblk, cblk, lam


def _blockdiag_ones(group, n=LANES):
    i = np.arange(n) // group
    return jnp.asarray((i[:, None] == i[None, :]).astype(np.float32), dtype=BF16)


def kernel(x, c, ctx, c_ctx, ada_w, ada_b, w_in, b_gate, diff_lam, diff_norm_g, gqa_qnorm_g, gqa_knorm_g,
           s5_a_re, s5_a_im, s5_log_dt, s5_b_re, s5_b_im, s5_c_re, s5_c_im, s5_d, s5_w_glu, s5_b_glu,
           ml_conv_w, ml_conv_b, ml_gate_b, ml_norm_g, w_branch, w_out, ln_mix_g, ln_mix_b, ln_ffn_g, ln_ffn_b,
           peer_wq, peer_subkeys, peer_u, peer_v):
    B, L, D = x.shape
    nctx = ctx.shape[1]
    S = nctx + L
    assert D == D_MODEL and nctx % TM == 0 and L % TM == 0 and L % GRID_W == 0
    nt, nct = S // TM, nctx // TM
    depth = ada_w.shape[0]

    h = jnp.concatenate([ctx, x], axis=1)
    R = -(-(B + 1) // 8) * 8
    cond = jnp.zeros((R, D), F32).at[:B].set(c).at[B].set(c_ctx)
    ones64, ones128 = _blockdiag_ones(HD), _blockdiag_ones(ML_HD, ML_HEADS * ML_HD)
    o = IN_OFFS

    for l in range(depth):
        last = l == depth - 1
        lam_init = 0.8 - 0.6 * math.exp(-0.3 * l)
        m = ada_modulation(cond, ada_w[l].astype(BF16), ada_b[l][None, :])
        mod = jnp.stack([jnp.broadcast_to(m[B], (B, 6 * D)), m[:B]], axis=1).reshape(B, 2, 6, D)

        w = w_in[l]
        seg = lambda i: w[:, o[i]:o[i + 1]]
        dup = lambda t: jnp.concatenate([t[:, :HD], t[:, :HD], t[:, HD:], t[:, HD:]], axis=1)
        def with_ones_slots(t, width):
            heads = t.reshape(D, -1, width)
            return jnp.concatenate([heads, jnp.zeros_like(heads)], axis=2).reshape(D, -1)
        w_a = jnp.concatenate([seg(0), _rope_partner(seg(0)), seg(1), _rope_partner(seg(1)), seg(2),
                               seg(9), _rope_partner(seg(9)), dup(seg(10)), dup(_rope_partner(seg(10))),
                               with_ones_slots(seg(11), HD)], axis=1).astype(BF16)
        w_b = jnp.concatenate([seg(3), seg(4), seg(5), seg(6), seg(7), seg(8),
                               jnp.zeros((D, LANES - 16), F32)], axis=1).astype(BF16)
        gate_b = jnp.concatenate([ml_gate_b[l], jnp.zeros((LANES - 16,), F32)])[None, :]
        tab = _rope_tables(L, nctx, gqa_qnorm_g[l], gqa_knorm_g[l])

        dq, dk, dv, gq, gk, gv = proj_attn(h, mod, w_a, tab, ones64, nct)
        u, mqk, mv, mo, mg = proj_seq(h, mod, w_b, gate_b, nct)

        yd = diff_attention(dq, dk, dv, diff_lam[l], diff_norm_g[l][None, :], nctx, lam_init)
        yg = gqa_attention(gq, gk, gv, nctx)

        bblk, cblk, lam = _s5_tables(s5_a_re[l], s5_a_im[l], s5_log_dt[l], s5_b_re[l], s5_b_im[l],
                                     s5_c_re[l], s5_c_im[l], B)
        sf, sb = s5_scan(u, bblk, cblk, lam, nctx)

        mq, mk = ml_prep(mqk, ml_conv_w[l], ml_conv_b[l][None, :], nctx)
        hf = mlstm_scan(mq, mk, mv, mg, nctx, False)
        hb = mlstm_scan(mq, mk, mv, mg, nctx, True)

        h1 = merge(h, mod, yd, yg, u, sf, sb, hf, hb, mo,
                   s5_d[l][None, :], s5_w_glu[l].astype(BF16), s5_b_glu[l][None, :], ml_norm_g[l][None, :],
                   seg(12).astype(BF16), b_gate[l][None, :], w_branch[l].astype(BF16), w_out[l].astype(BF16),
                   ln_mix_g[l][None, :], ln_mix_b[l][None, :], ones128, nct, last)

        tiles, ctx_tiles = (nt - nct, 0) if last else (nt, nct)
        t, ids, gates = peer_route(h1, mod, peer_wq[l].astype(BF16), peer_subkeys[l].astype(BF16), tiles, ctx_tiles)
        h = peer_experts(t, ids, gates, peer_u[l].astype(BF16), peer_v[l].astype(BF16),
                         h1, mod, ln_ffn_g[l][None, :], ln_ffn_b[l][None, :], tiles, ctx_tiles)
        h = h.reshape(B, -1, D)
    return h
```

```python
import functools
import math

import numpy as np
import jax
import jax.numpy as jnp
from jax import lax
from jax.experimental import pallas as pl
from jax.experimental.pallas import tpu as pltpu

F32 = jnp.float32
BF16 = jnp.bfloat16
I32 = jnp.int32

D_MODEL = 1024
DEPTH = 2
GRID_W = 64
ROPE_THETA = 10000.0
LN_EPS = 1e-6
HD = 64
DIFF_HEADS = 4
DIFF_HPS = 2
GQA_KV = 2
GQA_GPS = 2
S5_GROUP = 16
S5_GROUPS = 32
S5_STATE = 64
ML_HEADS = 4
ML_HD = 128
N_BRANCH = 4
PEER_HEADS = 8
PEER_NKEYS = 128
PEER_TOPK = 16
PEER_DQ = 256
ALPHA = (2 * DEPTH) ** 0.25

LANES = 128
TM = 256
TM_E = 2 * TM
TQ = 256
TK = 2048
S5_TC = 128
ML_T = 256
ML_LOCKSTEP = 4
PEER_CA = 16
VMEM_LIMIT = 56 * 1024 * 1024

IN_SPLITS = (512, 512, 512, 512, 512, 512, 512, 512, 16, 512, 128, 128, N_BRANCH * D_MODEL)
IN_OFFS = tuple(int(v) for v in np.cumsum((0,) + IN_SPLITS))


def _cparams(sem):
    return pltpu.CompilerParams(dimension_semantics=sem, vmem_limit_bytes=VMEM_LIMIT)


def _const_spec(shape):
    nd = len(shape)
    return pl.BlockSpec(shape, lambda *_: (0,) * nd, pipeline_mode=pl.Buffered(1))


def _ln(x):
    xc = x - jnp.mean(x, axis=-1, keepdims=True)
    return xc * lax.rsqrt(jnp.mean(xc * xc, axis=-1, keepdims=True) + LN_EPS)


def _sigmoid(x):
    return 1.0 / (1.0 + jnp.exp(-x))


def _gelu(x):
    return 0.5 * x * (1.0 + lax.erf(x * (2.0 ** -0.5)))


def _dot(a, b):
    return jnp.dot(a, b, preferred_element_type=F32)


def _dot_nt(a, b):
    return lax.dot_general(a, b, (((1,), (1,)), ((), ())), preferred_element_type=F32)


def _dot_tn(a, b):
    return lax.dot_general(a, b, (((0,), (0,)), ((), ())), preferred_element_type=F32)


def _ada_kernel(c_ref, w_ref, b_ref, o_ref):
    c = c_ref[...]
    o_ref[...] = _dot((c * _sigmoid(c)).astype(BF16), w_ref[...]) + b_ref[...]


def ada_modulation(cond, w, b):
    R, D = cond.shape
    N = w.shape[1]
    tn = 1536
    return pl.pallas_call(
        _ada_kernel, out_shape=jax.ShapeDtypeStruct((R, N), F32), grid=(N // tn,),
        in_specs=[pl.BlockSpec((R, D), lambda j: (0, 0)), pl.BlockSpec((D, tn), lambda j: (0, j)),
                  pl.BlockSpec((1, tn), lambda j: (0, j))],
        out_specs=pl.BlockSpec((R, tn), lambda j: (0, j)),
        compiler_params=_cparams(("arbitrary",)), name="ada")(cond, w, b)


def _proj_attn_kernel(x_ref, mod_ref, w_ref, tab_ref, ones_ref,
                      dq_ref, dk_ref, dv_ref, gq_ref, gk_ref, gv_ref):
    mod = mod_ref[0, 0]
    xm = (_ln(x_ref[0]) * (1.0 + mod[1:2, :]) + mod[0:1, :]).astype(BF16)

    def mm(lo, n):
        return _dot(xm, w_ref[:, lo:lo + n])

    def rope_store(ref, lo, n, ci, norm):
        t, tp = mm(lo, n), mm(lo + n, n)
        c, s = tab_ref[ci], tab_ref[ci + 1]
        for j in range(n // LANES):
            sl = slice(j * LANES, (j + 1) * LANES)
            tb = t[:, sl]
            y = tb * c + tp[:, sl] * s
            if norm:
                ss = _dot((tb * tb).astype(BF16), ones_ref[...])
                y = y * lax.rsqrt(ss * (1.0 / HD) + LN_EPS)
            ref[0, :, sl] = y.astype(ref.dtype)

    def value_store(ref, lo, n, width):
        lane = lax.broadcasted_iota(I32, (1, n), 1)
        ref[0] = (mm(lo, n) + jnp.where(lane % (2 * width) >= width, 1.0, 0.0)).astype(ref.dtype)

    rope_store(dq_ref, 0, 512, 0, False)
    rope_store(dk_ref, 1024, 512, 2, False)
    dv_ref[0] = mm(2048, 512).astype(dv_ref.dtype)
    rope_store(gq_ref, 2560, 512, 4, True)
    rope_store(gk_ref, 3584, 256, 6, True)
    value_store(gv_ref, 4096, 256, HD)


def proj_attn(xa, mod, w_a, tab, ones_bd, nct):
    B, S, D = xa.shape
    tok = lambda n: pl.BlockSpec((1, TM, n), lambda b, i: (b, i, 0))
    widths = (512, 512, 512, 512, 256, 256)
    outs = [jax.ShapeDtypeStruct((B, S, n), BF16) for n in widths]
    return pl.pallas_call(
        _proj_attn_kernel, out_shape=outs, grid=(B, S // TM),
        in_specs=[tok(D),
                  pl.BlockSpec((1, 1, 6, D), lambda b, i: (b, jnp.where(i >= nct, 1, 0), 0, 0)),
                  _const_spec(w_a.shape),
                  pl.BlockSpec((8, TM, LANES), lambda b, i: (0, i, 0)),
                  _const_spec(ones_bd.shape)],
        out_specs=[tok(n) for n in widths],
        compiler_params=_cparams(("parallel", "parallel")), name="proj_attn")(xa, mod, w_a, tab, ones_bd)


def _proj_seq_kernel(x_ref, mod_ref, w_ref, gb_ref, u_ref, qk_ref, v_ref, o_ref, g_ref):
    mod = mod_ref[0, 0]
    xm = (_ln(x_ref[0]) * (1.0 + mod[1:2, :]) + mod[0:1, :]).astype(BF16)
    u_ref[...] = _dot(xm, w_ref[:, 0:512])
    qk_ref[0] = _dot(xm, w_ref[:, 512:1536])
    v_ref[0] = _dot(xm, w_ref[:, 1536:2048])
    o_ref[0] = _dot(xm, w_ref[:, 2048:2560])
    g_ref[0] = _dot(xm, w_ref[:, 2560:2688]) + gb_ref[...]


def proj_seq(xa, mod, w_b, gate_b, nct):
    B, S, D = xa.shape
    tok = lambda n: pl.BlockSpec((1, TM, n), lambda b, i: (b, i, 0))
    widths = (1024, 512, 512, LANES)
    return pl.pallas_call(
        _proj_seq_kernel,
        out_shape=[jax.ShapeDtypeStruct((S, B * 512), F32)] + [jax.ShapeDtypeStruct((B, S, n), F32) for n in widths],
        grid=(B, S // TM),
        in_specs=[tok(D),
                  pl.BlockSpec((1, 1, 6, D), lambda b, i: (b, jnp.where(i >= nct, 1, 0), 0, 0)),
                  _const_spec(w_b.shape), _const_spec(gate_b.shape)],
        out_specs=[pl.BlockSpec((TM, 512), lambda b, i: (i, b))] + [tok(n) for n in widths],
        compiler_params=_cparams(("parallel", "parallel")), name="proj_seq")(xa, mod, w_b, gate_b)


def _flash(qqs, k_ref, v_ref, nctx, n_lat, tk, dv):
    P = len(qqs)
    R, W = qqs[0].shape[0], v_ref.shape[2] // P
    mxu_sums = dv < W

    def finish(carry, s, v):
        m, l, acc = carry
        m_new = jnp.maximum(m, jnp.max(s, axis=1, keepdims=True))
        alpha = jnp.exp(m - m_new)
        if mxu_sums:
            p = jnp.exp((s - m_new).astype(BF16))
        else:
            p = jnp.exp(s - m_new)
            l = alpha * l + jnp.sum(p, axis=1, keepdims=True)
        return m_new, l, alpha * acc + _dot(p.astype(BF16), v)

    carries = [(jnp.full((R, 1), -jnp.inf, F32), jnp.zeros((R, 1), F32), jnp.zeros((R, W), F32))] * P
    chunks = [pl.ds(0, nctx)] + [pl.ds(nctx + c * tk, tk) for c in range(n_lat)]
    for rows in chunks:
        scores = [_dot_nt(qqs[p], k_ref[0, rows, p * LANES:(p + 1) * LANES]) for p in range(P)]
        carries = [finish(carries[p], scores[p], v_ref[0, rows, p * W:(p + 1) * W]) for p in range(P)]
    return [acc[:, :dv] / (acc[:, dv:dv + 1] if mxu_sums else l) for _, l, acc in carries]


def _ctx_or_all(i, nctx, n_lat, attend):
    @pl.when(i < nctx // TQ)
    def _():
        attend(0)

    @pl.when(i >= nctx // TQ)
    def _():
        attend(n_lat)


def _diff_attn_kernel(q_ref, k_ref, v_ref, lam_ref, g_ref, o_ref, *, nctx, n_lat, tk, lam_init):
    i = pl.program_id(2)
    lane = lax.broadcasted_iota(I32, (TQ, LANES), 1)
    zero = jnp.zeros((TQ, LANES), q_ref.dtype)
    qqs = []
    for h in range(DIFF_HPS):
        q = q_ref[0, :, h * LANES:(h + 1) * LANES]
        qqs.append(jnp.concatenate([jnp.where(lane < HD, q, zero), jnp.where(lane >= HD, q, zero)], axis=0))
    lv = lam_ref[...]
    lam = (jnp.exp(jnp.sum(lv[0:1] * lv[1:2], axis=1, keepdims=True))
           - jnp.exp(jnp.sum(lv[2:3] * lv[3:4], axis=1, keepdims=True)) + lam_init)

    def attend(n):
        for h, o in enumerate(_flash(qqs, k_ref, v_ref, nctx, n, tk, 2 * HD)):
            d = o[:TQ] - lam * o[TQ:]
            y = d * lax.rsqrt(jnp.mean(d * d, axis=-1, keepdims=True) + LN_EPS)
            o_ref[0, :, h * LANES:(h + 1) * LANES] = (y * g_ref[...] * (1.0 - lam_init)).astype(o_ref.dtype)

    _ctx_or_all(i, nctx, n_lat, attend)


def diff_attention(q, k, v, lam_vec, norm_g, nctx, lam_init):
    B, S, _ = q.shape
    tk = math.gcd(S - nctx, TK)
    kern = functools.partial(_diff_attn_kernel, nctx=nctx, n_lat=(S - nctx) // tk, tk=tk, lam_init=lam_init)
    return pl.pallas_call(
        kern, out_shape=jax.ShapeDtypeStruct((B, S, 512), BF16), grid=(B, DIFF_HEADS // DIFF_HPS, S // TQ),
        in_specs=[pl.BlockSpec((1, TQ, DIFF_HPS * LANES), lambda b, h, i: (b, i, h)),
                  pl.BlockSpec((1, S, DIFF_HPS * LANES), lambda b, h, i: (b, 0, h)),
                  pl.BlockSpec((1, S, DIFF_HPS * LANES), lambda b, h, i: (b, 0, h)),
                  pl.BlockSpec((4, HD), lambda b, h, i: (0, 0)),
                  pl.BlockSpec((1, LANES), lambda b, h, i: (0, 0))],
        out_specs=pl.BlockSpec((1, TQ, DIFF_HPS * LANES), lambda b, h, i: (b, i, h)),
        compiler_params=_cparams(("parallel", "parallel", "parallel")), name="diff_attn")(q, k, v, lam_vec, norm_g)


def _gqa_attn_kernel(q_ref, k_ref, v_ref, o_ref, *, nctx, n_lat, tk):
    i = pl.program_id(2)
    lane = lax.broadcasted_iota(I32, (TQ, LANES), 1)
    zero = jnp.zeros((TQ, LANES), q_ref.dtype)
    qqs = []
    for g in range(GQA_GPS):
        parts = []
        for j in range(2 * g, 2 * g + 2):
            blk = q_ref[0, :, j * LANES:(j + 1) * LANES]
            parts += [jnp.where(lane < HD, blk, zero), jnp.where(lane >= HD, blk, zero)]
        qqs.append(jnp.concatenate(parts, axis=0))

    def attend(n):
        for g, o in enumerate(_flash(qqs, k_ref, v_ref, nctx, n, tk, HD)):
            heads = jnp.concatenate([o[h * TQ:(h + 1) * TQ] for h in range(4)], axis=1)
            o_ref[0, :, g * 2 * LANES:(g + 1) * 2 * LANES] = heads.astype(o_ref.dtype)

    _ctx_or_all(i, nctx, n_lat, attend)


def gqa_attention(q, k, v, nctx):
    B, S, _ = q.shape
    tk = math.gcd(S - nctx, TK)
    kern = functools.partial(_gqa_attn_kernel, nctx=nctx, n_lat=(S - nctx) // tk, tk=tk)
    return pl.pallas_call(
        kern, out_shape=jax.ShapeDtypeStruct((B, S, 512), BF16), grid=(B, GQA_KV // GQA_GPS, S // TQ),
        in_specs=[pl.BlockSpec((1, TQ, GQA_GPS * 2 * LANES), lambda b, g, i: (b, i, g)),
                  pl.BlockSpec((1, S, GQA_GPS * LANES), lambda b, g, i: (b, 0, g)),
                  pl.BlockSpec((1, S, GQA_GPS * LANES), lambda b, g, i: (b, 0, g))],
        out_specs=pl.BlockSpec((1, TQ, GQA_GPS * 2 * LANES), lambda b, g, i: (b, i, g)),
        compiler_params=_cparams(("parallel", "parallel", "parallel")), name="gqa_attn")(q, k, v)


def _s5_kernel(uf_ref, ub_ref, bblk_ref, cblk_ref, lam_ref, yf_ref, yb_ref, buf_f, buf_b, st_ref):
    H = S5_GROUPS * S5_STATE // 2
    R = buf_f.shape[0]
    RB = 256
    TB = RB // 8

    @pl.when(pl.program_id(0) == 0)
    def _():
        st_ref[...] = jnp.zeros_like(st_ref)

    half0 = (lax.broadcasted_iota(I32, (RB, 1), 0) & 1) == 0

    for d, (u_ref, buf) in enumerate(((uf_ref, buf_f), (ub_ref, buf_b))):
        for rb in range(R // RB):
            rows = slice(rb * RB, (rb + 1) * RB)
            ut = u_ref[rb * TB:(rb + 1) * TB, :]
            u3 = jnp.stack([ut[:, s * 256:(s + 1) * 256] for s in range(8)], axis=0)
            uh = pltpu.einshape("stc->tsc", u3).reshape(RB, 256).astype(BF16)
            for cols in (slice(0, H), slice(H, 2 * H)):
                buf[rows, cols] = jnp.where(half0, _dot(uh, bblk_ref[d, 0, :, cols]), _dot(uh, bblk_ref[d, 1, :, cols]))

    lfr, lfi, lbr, lbi = lam_ref[0, 0], lam_ref[0, 1], lam_ref[1, 0], lam_ref[1, 1]

    def step(t, carry):
        fr, fi, br, bi = carry
        rf = pl.ds(pl.multiple_of(t * 8, 8), 8)
        x = buf_f[rf, :]
        nfr = lfr * fr - lfi * fi + x[:, :H]
        nfi = lfr * fi + lfi * fr + x[:, H:]
        buf_f[rf, :] = jnp.concatenate([nfr, nfi], axis=1)
        rb = pl.ds(pl.multiple_of((S5_TC - 1 - t) * 8, 8), 8)
        z = buf_b[rb, :]
        nbr = lbr * br - lbi * bi + z[:, :H]
        nbi = lbr * bi + lbi * br + z[:, H:]
        buf_b[rb, :] = jnp.concatenate([nbr, nbi], axis=1)
        return nfr, nfi, nbr, nbi

    fin = lax.fori_loop(0, S5_TC, step, (st_ref[0], st_ref[1], st_ref[2], st_ref[3]))
    for j in range(4):
        st_ref[j] = fin[j]

    for d, (y_ref, buf) in enumerate(((yf_ref, buf_f), (yb_ref, buf_b))):
        for rb in range(R // RB):
            rows = slice(rb * RB, (rb + 1) * RB)
            h = buf[rows, :].astype(BF16)
            y = jnp.where(half0, _dot(h, cblk_ref[d, 0]), _dot(h, cblk_ref[d, 1]))
            y3 = pltpu.einshape("tsc->stc", y.reshape(TB, 8, 256))
            y_ref[rb * TB:(rb + 1) * TB, :] = jnp.concatenate([y3[s] for s in range(8)], axis=1)


def s5_scan(u, bblk, cblk, lam, nctx):
    S, W = u.shape
    assert W == 8 * 256, "the scan packs (sample, half) pairs into the eight sublanes of a vreg"
    nch, nc0 = S // S5_TC, nctx // S5_TC

    def bwd(i):
        return jnp.where(i < nc0, nc0 - 1 - i, (nch - 1) - (i - nc0))
    H2 = S5_GROUPS * S5_STATE
    R = S5_TC * 8
    blk = lambda f: pl.BlockSpec((S5_TC, W), f)
    return pl.pallas_call(
        _s5_kernel, out_shape=[jax.ShapeDtypeStruct(u.shape, F32)] * 2, grid=(nch,),
        in_specs=[blk(lambda i: (i, 0)), blk(lambda i: (bwd(i), 0)),
                  _const_spec(bblk.shape), _const_spec(cblk.shape), _const_spec(lam.shape)],
        out_specs=[blk(lambda i: (i, 0)), blk(lambda i: (bwd(i), 0))],
        scratch_shapes=[pltpu.VMEM((R, H2), F32), pltpu.VMEM((R, H2), F32), pltpu.VMEM((4, 8, H2 // 2), F32)],
        compiler_params=_cparams(("arbitrary",)), name="s5_scan")(u, u, bblk, cblk, lam)


def _ml_prep_kernel(x_ref, prev_ref, next_ref, w_ref, b_ref, q_ref, k_ref, *, seg_starts, seg_ends):
    i = pl.program_id(1)
    x = x_ref[0]
    row = lax.broadcasted_iota(I32, x.shape, 0)
    first = functools.reduce(jnp.logical_or, [i == s for s in seg_starts])
    last = functools.reduce(jnp.logical_or, [i == s for s in seg_ends])
    pr = jnp.where(first, 0.0, prev_ref[0, 7:8, :])
    nx = jnp.where(last, 0.0, next_ref[0, 0:1, :])
    xp = jnp.where(row == 0, pr, pltpu.roll(x, 1, 0))
    xn = jnp.where(row == TM - 1, nx, pltpu.roll(x, TM - 1, 0))
    w = w_ref[...]
    y = b_ref[...] + xp * w[0:1, :] + x * w[1:2, :] + xn * w[2:3, :]
    y = y * _sigmoid(y)
    q_ref[0] = y[:, :512].astype(q_ref.dtype)
    k_ref[0] = (y[:, 512:] * (ML_HD ** -0.5)).astype(k_ref.dtype)


def ml_prep(qk, conv_w, conv_b, nctx):
    B, S, W = qk.shape
    nt, nct, r8 = S // TM, nctx // TM, TM // 8
    kern = functools.partial(_ml_prep_kernel, seg_starts=(0, nct), seg_ends=(nct - 1, nt - 1))
    return pl.pallas_call(
        kern, out_shape=[jax.ShapeDtypeStruct((B, S, 512), BF16)] * 2, grid=(B, nt),
        in_specs=[pl.BlockSpec((1, TM, W), lambda b, i: (b, i, 0)),
                  pl.BlockSpec((1, 8, W), lambda b, i: (b, jnp.maximum(i * r8 - 1, 0), 0)),
                  pl.BlockSpec((1, 8, W), lambda b, i: (b, jnp.minimum((i + 1) * r8, S // 8 - 1), 0)),
                  pl.BlockSpec((3, W), lambda b, i: (0, 0)), pl.BlockSpec((1, W), lambda b, i: (0, 0))],
        out_specs=[pl.BlockSpec((1, TM, 512), lambda b, i: (b, i, 0))] * 2,
        compiler_params=_cparams(("parallel", "parallel")), name="ml_prep")(qk, qk, qk, conv_w, conv_b)


def _log_sigmoid(x):
    return jnp.minimum(x, 0.0) - jnp.log1p(jnp.exp(-jnp.abs(x)))


def _mlstm_kernel(q_ref, k_ref, v_ref, g_ref, h_ref, c_sc, n_sc, m_sc, *, reverse):
    T = ML_T
    gi, gf = (8, 12) if reverse else (0, 4)

    @pl.when(pl.program_id(0) == 0)
    def _():
        c_sc[...] = jnp.zeros_like(c_sc)
        n_sc[...] = jnp.zeros_like(n_sc)
        m_sc[...] = jnp.zeros_like(m_sc)

    r = lax.broadcasted_iota(I32, (T, T), 0)
    c = lax.broadcasted_iota(I32, (T, T), 1)
    mask = (c >= r) if reverse else (c <= r)
    tri = jnp.where(mask, 1.0, 0.0)
    hp = lax.Precision.HIGHEST
    end = 0 if reverse else T - 1
    def gates(b):
        g = g_ref[b]
        gt = g.T
        bcol_all = jnp.dot(tri, _log_sigmoid(g), preferred_element_type=F32, precision=hp)
        brow_all = lax.dot_general(_log_sigmoid(gt[0:16]), tri, (((1,), (1,)), ((), ())),
                                   preferred_element_type=F32, precision=hp)
        return g, gt, bcol_all, brow_all

    def chain(b, hh, g, gt, bcol_all, brow_all):
        st = b * ML_HEADS + hh
        sl = slice(hh * ML_HD, (hh + 1) * ML_HD)
        q, k, v = q_ref[b, :, sl], k_ref[b, :, sl], v_ref[b, :, sl]
        bcol, brow = bcol_all[:, gf + hh:gf + hh + 1], brow_all[gf + hh:gf + hh + 1, :]
        icol, irow = g[:, gi + hh:gi + hh + 1], gt[gi + hh:gi + hh + 1, :]
        m_old = m_sc[st][:, 0:1]
        n_old = n_sc[st]
        c_old = c_sc[st]
        logw = jnp.where(mask, bcol - brow + irow, -jnp.inf)
        m_inter = bcol + m_old
        m_t = jnp.maximum(m_inter, jnp.max(logw, axis=1, keepdims=True))
        yield
        s = _dot_nt(q, k) * jnp.exp(logw - m_t)
        inter = jnp.exp(m_inter - m_t)
        yield
        num = _dot(s.astype(BF16), v.astype(BF16)) + inter * _dot_nt(q, c_old.astype(BF16))
        den = jnp.sum(s, axis=1, keepdims=True) + inter * jnp.sum(q.astype(F32) * n_old, axis=1, keepdims=True)
        h_ref[b, :, sl] = num / jnp.maximum(jnp.abs(den), jnp.exp(-m_t))
        yield
        b_end = bcol[end:end + 1, :]
        g_row, g_col = b_end - brow + irow, b_end - bcol + icol
        m_new = jnp.maximum(b_end + m_old, jnp.max(g_row, axis=1, keepdims=True))
        decay = jnp.exp(b_end + m_old - m_new)
        wk = jnp.exp(g_col - m_new)
        c_sc[st] = decay * c_old + _dot_tn((v * wk).astype(BF16), k)
        n_sc[st] = decay * n_old + jnp.sum(k.astype(F32) * wk, axis=0, keepdims=True)
        m_sc[st] = jnp.broadcast_to(m_new, (1, LANES))

    nb = q_ref.shape[0]
    for b0 in range(0, nb, ML_LOCKSTEP):
        chains = []
        for b in range(b0, min(b0 + ML_LOCKSTEP, nb)):
            shared = gates(b)
            chains += [chain(b, hh, *shared) for hh in range(ML_HEADS)]
        for _ in range(4):
            for ch in chains:
                next(ch, None)


def mlstm_scan(q, k, v, g, nctx, reverse):
    B, S, W = q.shape
    nch, nc0 = S // ML_T, nctx // ML_T

    def order(i):
        return jnp.where(i < nc0, nc0 - 1 - i, (nch - 1) - (i - nc0)) if reverse else i
    blk = lambda n: pl.BlockSpec((B, ML_T, n), lambda i: (0, order(i), 0))
    nst = B * ML_HEADS
    return pl.pallas_call(
        functools.partial(_mlstm_kernel, reverse=reverse),
        out_shape=jax.ShapeDtypeStruct((B, S, W), F32), grid=(nch,),
        in_specs=[blk(W), blk(W), blk(W), blk(LANES)], out_specs=blk(W),
        scratch_shapes=[pltpu.VMEM((nst, ML_HD, ML_HD), F32), pltpu.VMEM((nst, 1, ML_HD), F32),
                        pltpu.VMEM((nst, 1, LANES), F32)],
        compiler_params=_cparams(("arbitrary",)), name="mlstm_bwd" if reverse else "mlstm_fwd")(q, k, v, g)


def _merge_kernel(x_ref, mod_ref, yd_ref, yg_ref, u_ref, sf_ref, sb_ref, hf_ref, hb_ref, mo_ref,
                  s5d_ref, wglu_ref, bglu_ref, mlg_ref, wgate_ref, bgate_ref, wbr_ref, wout_ref,
                  lng_ref, lnb_ref, ones_ref, o_ref):
    x = x_ref[0]
    mod = mod_ref[0, 0]
    xm = (_ln(x) * (1.0 + mod[1:2, :]) + mod[0:1, :]).astype(BF16)
    ys = u_ref[...] * s5d_ref[...] + sf_ref[...] + sb_ref[...]
    z = _dot(_gelu(ys).astype(BF16), wglu_ref[...]) + bglu_ref[...]
    ys = z[:, :512] * _sigmoid(z[:, 512:])
    hm = hf_ref[0] + hb_ref[0]
    ss = _dot((hm * hm).astype(BF16), ones_ref[...])
    ym = hm * lax.rsqrt(ss * (1.0 / ML_HD) + LN_EPS) * mlg_ref[...] * _sigmoid(mo_ref[0])
    branches = (yd_ref[0], ys.astype(BF16), ym.astype(BF16), yg_ref[0])
    merged = None
    for j, yb in enumerate(branches):
        gate = _sigmoid(_dot(xm, wgate_ref[:, j * D_MODEL:(j + 1) * D_MODEL]) + bgate_ref[:, j * D_MODEL:(j + 1) * D_MODEL])
        term = gate * _dot(yb, wbr_ref[j])
        merged = term if merged is None else merged + term
    mix = _dot(merged.astype(BF16), wout_ref[...])
    o_ref[...] = _ln(ALPHA * x + mod[2:3, :] * mix) * lng_ref[...] + lnb_ref[...]


def merge(xa, mod, yd, yg, u, sf, sb, hf, hb, mo, s5d, wglu, bglu, mlg, wgate, bgate, wbr, wout, lng, lnb, ones_ml, nct,
          latent_only):
    B, S, D = xa.shape
    first = nct if latent_only else 0
    nti = S // TM - first
    tok = lambda n: pl.BlockSpec((1, TM, n), lambda b, i: (b, i + first, 0))
    consts = (s5d, wglu, bglu, mlg, wgate, bgate, wbr, wout, lng, lnb, ones_ml)
    return pl.pallas_call(
        _merge_kernel, out_shape=jax.ShapeDtypeStruct((B * nti * TM, D), F32), grid=(B, nti),
        in_specs=[tok(D), pl.BlockSpec((1, 1, 6, D), lambda b, i: (b, jnp.where(i + first >= nct, 1, 0), 0, 0))]
        + [tok(512)] * 2 + [pl.BlockSpec((TM, 512), lambda b, i: (i + first, b))] * 3 + [tok(512)] * 3
        + [_const_spec(a.shape) for a in consts],
        out_specs=pl.BlockSpec((TM, D), lambda b, i: (b * nti + i, 0)),
        compiler_params=_cparams(("parallel", "parallel")), name="merge")(xa, mod, yd, yg, u, sf, sb, hf, hb, mo, *consts)


def _top16(*problems):
    iotas = {prob[0].shape: lax.broadcasted_iota(I32, prob[0].shape, 0).astype(F32) for prob in problems}

    def one(kk, s, val_ref, idx_ref, payload):
        R, ri = s.shape[0], iotas[s.shape]
        m = jnp.max(s, axis=0, keepdims=True)
        ix = jnp.min(jnp.where(s == m, ri, float(R)), axis=0, keepdims=True)
        hit = ri == ix
        val_ref[pl.ds(kk, 1), :] = m
        idx_ref[pl.ds(kk, 1), :] = ix if payload is None else jnp.max(jnp.where(hit, payload, -1.0), axis=0, keepdims=True)
        return jnp.where(hit, -jnp.inf, s)

    def body(kk, ss):
        return tuple(one(kk, s, *prob[1:]) for s, prob in zip(ss, problems))

    lax.fori_loop(0, PEER_TOPK, body, tuple(prob[0] for prob in problems))


_PEER_CAND_COUNTS = tuple(PEER_TOPK // (p + 1) for p in range(8))
_PEER_CAND_ROWS = tuple((sum(_PEER_CAND_COUNTS[:p]), n) for p, n in enumerate(_PEER_CAND_COUNTS))
_PEER_NCAND = -(-(sum(_PEER_CAND_COUNTS) + 8) // 8) * 8


def _peer_route_kernel(x_ref, mod0_ref, mod1_ref, wq_ref, sk_ref, t_ref, ids_ref, gate_ref,
                       v1_sc, i1_sc, v2_sc, i2_sc, cand_sc, cidx_sc, top_sc, tid_sc, ids_sc, gates_sc):
    h = pl.program_id(1)

    @pl.when(h == 0)
    def _():
        for half, mod_ref in enumerate((mod0_ref, mod1_ref)):
            rows = slice(half * TM, (half + 1) * TM)
            mod = mod_ref[0, 0]
            t_ref[rows, :] = (_ln(x_ref[rows, :]) * (1.0 + mod[4:5, :]) + mod[3:4, :]).astype(BF16)

    q = _dot(t_ref[...], wq_ref[...]).astype(BF16)
    half = PEER_DQ // 2
    s1, s2 = _dot_nt(sk_ref[0], q[:, :half]), _dot_nt(sk_ref[1], q[:, half:])
    _top16((s1, v1_sc, i1_sc, None))
    _top16((s2, v2_sc, i2_sc, None))
    nk = float(PEER_NKEYS)
    for p, (lo, n) in enumerate(_PEER_CAND_ROWS):
        cand_sc[lo:lo + n, :] = v1_sc[p:p + 1, :] + v2_sc[0:n, :]
        cidx_sc[lo:lo + n, :] = i1_sc[p:p + 1, :] * nk + i2_sc[0:n, :]
    lo = _PEER_CAND_ROWS[-1][0] + _PEER_CAND_ROWS[-1][1]
    cand_sc[lo:lo + 8, :] = v1_sc[8:16, :] + v2_sc[0:1, :]
    cidx_sc[lo:lo + 8, :] = i1_sc[8:16, :] * nk + i2_sc[0:1, :]
    pad = _PEER_NCAND - (lo + 8)
    cand_sc[lo + 8:, :] = jnp.full((pad, cand_sc.shape[1]), -jnp.inf, F32)
    cidx_sc[lo + 8:, :] = jnp.zeros((pad, cand_sc.shape[1]), F32)
    _top16((cand_sc[...], top_sc, tid_sc, cidx_sc[...]))
    top = top_sc[...]
    e = jnp.exp(top - top[0:1, :])
    entries = pl.ds(pl.multiple_of(h * PEER_TOPK, PEER_TOPK), PEER_TOPK)
    gates_sc[entries, :] = e / jnp.sum(e, axis=0, keepdims=True)
    ids_sc[entries, :] = tid_sc[...]

    @pl.when(h == PEER_HEADS - 1)
    def _():
        gate_ref[...] = gates_sc[...].T
        ids_ref[...] = ids_sc[...].T.astype(I32)


def _half_tile_mod_spec(half, nt_per_sample, nct):
    def index(i, j):
        r = 2 * i + half
        return (r // nt_per_sample, jnp.where(r % nt_per_sample >= nct, 1, 0), 0, 0)
    return pl.BlockSpec((1, 1, 6, D_MODEL), index)


def peer_route(h1, mod, wq, subkeys, nt_per_sample, nct):
    T, D = h1.shape
    assert T % TM_E == 0 and TM_E == 2 * TM
    f = lambda n: pltpu.VMEM((n, TM_E), F32)
    entries = PEER_HEADS * PEER_TOPK
    return pl.pallas_call(
        _peer_route_kernel,
        out_shape=[jax.ShapeDtypeStruct((T, D), BF16),
                   jax.ShapeDtypeStruct((T, entries), I32),
                   jax.ShapeDtypeStruct((T, entries), F32)],
        grid=(T // TM_E, PEER_HEADS),
        in_specs=[pl.BlockSpec((TM_E, D), lambda i, h: (i, 0)),
                  _half_tile_mod_spec(0, nt_per_sample, nct), _half_tile_mod_spec(1, nt_per_sample, nct),
                  pl.BlockSpec((D, PEER_DQ), lambda i, h: (0, h)),
                  pl.BlockSpec((2, PEER_NKEYS, PEER_DQ // 2), lambda i, h: (0, 0, 0))],
        out_specs=[pl.BlockSpec((TM_E, D), lambda i, h: (i, 0)),
                   pl.BlockSpec((TM_E, entries), lambda i, h: (i, 0)),
                   pl.BlockSpec((TM_E, entries), lambda i, h: (i, 0))],
        scratch_shapes=[f(16), f(16), f(16), f(16), f(_PEER_NCAND), f(_PEER_NCAND), f(16), f(16),
                        f(entries), f(entries)],
        compiler_params=_cparams(("parallel", "arbitrary")), name="peer_route")(h1, mod, mod, wq, subkeys)


def _peer_expert_kernel(t_ref, ids_ref, gate_ref, u_ref, v_ref, x_ref, mod0_ref, mod1_ref, lng_ref, lnb_ref, o_ref,
                        a_sc, wgt_sc, *, nchunk):
    j = pl.program_id(1)
    NK, CA = PEER_NKEYS, PEER_CA
    w_sc = a_sc

    @pl.when(j < nchunk)
    def _():
        a = _dot(t_ref[...], u_ref[0]).astype(a_sc.dtype)
        a3 = jnp.stack([a[:, al * NK:(al + 1) * NK] for al in range(CA)], axis=0)
        a_sc[:, pl.ds(pl.multiple_of(j * CA, CA), CA), :] = pltpu.einshape("atb->tab", a3)

    @pl.when(j == nchunk - 1)
    def _():
        io_k = lax.broadcasted_iota(I32, (NK, NK), 0)
        io_a = lax.broadcasted_iota(I32, (NK, 2 * NK), 0)

        def masks(two):
            idp = ids_ref[two, :]
            ids = jnp.concatenate([idp[0:1], idp[1:2]], axis=1)
            blocks = [jnp.where(io_k == (idp[tok:tok + 1] & (NK - 1)), 1.0, 0.0).astype(BF16) for tok in range(2)]
            zero = jnp.zeros((NK, NK), BF16)
            onehot = jnp.concatenate([jnp.concatenate([blocks[0], zero], axis=1),
                                      jnp.concatenate([zero, blocks[1]], axis=1)], axis=0)
            return onehot, io_a == (ids >> 7)

        def pick(p, carry):
            two = pl.ds(pl.multiple_of(p * 2, 2), 2)
            onehot, r1 = masks(two)
            a2 = a_sc[two]
            at = jnp.concatenate([a2[0], a2[1]], axis=1).astype(BF16)
            picked = _dot(at, onehot)
            act = jnp.sum(jnp.where(r1, picked, 0.0), axis=0, keepdims=True)
            gp = gate_ref[two, :]
            w = jnp.concatenate([gp[0:1], gp[1:2]], axis=1) * _gelu(act)
            wgt_sc[two, :] = jnp.concatenate([w[:, :NK], w[:, NK:]], axis=0)
            return carry

        def scatter(p, carry):
            two = pl.ds(pl.multiple_of(p * 2, 2), 2)
            onehot, r1 = masks(two)
            wp = wgt_sc[two, :]
            w = jnp.concatenate([wp[0:1], wp[1:2]], axis=1)
            wt = _dot_nt(jnp.where(r1, w, 0.0).astype(BF16), onehot)
            w_sc[two] = jnp.stack([wt[:, :NK], wt[:, NK:]], axis=0).astype(w_sc.dtype)
            return carry

        lax.fori_loop(0, TM_E // 2, pick, 0, unroll=16)
        lax.fori_loop(0, TM_E // 2, scatter, 0, unroll=16)
        o_ref[...] = jnp.zeros_like(o_ref)

    @pl.when(j >= nchunk)
    def _():
        w3 = pltpu.einshape("tab->atb", w_sc[:, pl.ds(pl.multiple_of((j - nchunk) * CA, CA), CA), :])
        w = jnp.concatenate([w3[al] for al in range(CA)], axis=1).astype(BF16)
        o_ref[...] += _dot(w, v_ref[...])

    @pl.when(j == 2 * nchunk - 1)
    def _():
        for half, mod_ref in enumerate((mod0_ref, mod1_ref)):
            rows = slice(half * TM, (half + 1) * TM)
            y = ALPHA * x_ref[rows, :] + mod_ref[0, 0][5:6, :] * o_ref[rows, :]
            o_ref[rows, :] = _ln(y) * lng_ref[...] + lnb_ref[...]


def peer_experts(t, ids, gates, emb_u, emb_v, h1, mod, lng, lnb, nt_per_sample, nct):
    T, D = h1.shape
    E = emb_v.shape[0]
    ce = PEER_CA * PEER_NKEYS
    nchunk = E // ce
    emb_ut = emb_u.reshape(nchunk, ce, D).transpose(0, 2, 1)
    assert T % TM_E == 0 and TM_E == 2 * TM
    tok = lambda n, **kw: pl.BlockSpec((TM_E, n), lambda i, j: (i, 0), **kw)
    mod_spec = lambda half: _half_tile_mod_spec(half, nt_per_sample, nct)
    return pl.pallas_call(
        functools.partial(_peer_expert_kernel, nchunk=nchunk),
        out_shape=jax.ShapeDtypeStruct((T, D), F32), grid=(T // TM_E, 2 * nchunk),
        in_specs=[tok(D), tok(LANES), tok(LANES),
                  pl.BlockSpec((1, D, ce), lambda i, j: (jnp.minimum(j, nchunk - 1), 0, 0)),
                  pl.BlockSpec((ce, D), lambda i, j: (jnp.maximum(j - nchunk, 0), 0)),
                  tok(D), mod_spec(0), mod_spec(1),
                  pl.BlockSpec((1, D), lambda i, j: (0, 0)), pl.BlockSpec((1, D), lambda i, j: (0, 0))],
        out_specs=tok(D),
        scratch_shapes=[pltpu.VMEM((TM_E, PEER_NKEYS, PEER_NKEYS), BF16), pltpu.VMEM((TM_E, LANES), F32)],
        compiler_params=_cparams(("parallel", "arbitrary")), name="peer_experts")(t, ids, gates, emb_ut, emb_v, h1, mod, mod, lng, lnb)


_ROPE_IDX = np.arange(HD)
_ROPE_PERM = np.where(_ROPE_IDX % 32 < 16, _ROPE_IDX + 16, _ROPE_IDX - 16)
_ROPE_SIGN = np.where(_ROPE_IDX % 32 < 16, -1.0, 1.0).astype(np.float32)


def _rope_partner(w):
    n = w.shape[1] // HD
    perm = np.concatenate([h * HD + _ROPE_PERM for h in range(n)])
    return w[:, perm] * jnp.asarray(np.tile(_ROPE_SIGN, n))


def _rope_tables(L, nctx, gq, gk):
    rows = L // GRID_W
    row = jnp.repeat(jnp.arange(rows, dtype=F32), GRID_W)
    col = jnp.tile(jnp.arange(GRID_W, dtype=F32), rows)
    nf = HD // 4
    inv = ROPE_THETA ** (-jnp.arange(nf, dtype=F32) / nf)
    ar, ac = row[:, None] * inv, col[:, None] * inv
    cos = jnp.concatenate([jnp.cos(ar), jnp.cos(ar), jnp.cos(ac), jnp.cos(ac)], axis=1)
    sin = jnp.concatenate([jnp.sin(ar), jnp.sin(ar), jnp.sin(ac), jnp.sin(ac)], axis=1)
    cos = jnp.concatenate([jnp.ones((nctx, HD), F32), cos], axis=0)
    sin = jnp.concatenate([jnp.zeros((nctx, HD), F32), sin], axis=0)
    scale = HD ** -0.5
    one = jnp.ones((HD,), F32)
    tabs = []
    for g, sc in ((one, scale), (one, 1.0), (gq, scale), (gk, 1.0)):
        tabs += [cos * (g * sc), sin * (g[_ROPE_PERM] * sc)]
    return jnp.tile(jnp.stack(tabs), (1, 1, 2))


def _s5_tables(a_re, a_im, log_dt, b_re, b_im, c_re, c_im, nb):
    dt = jnp.exp(log_dt)[..., None]
    mag = jnp.exp(a_re * dt)
    lr, li = mag * jnp.cos(a_im * dt), mag * jnp.sin(a_im * dt)
    den = a_re * a_re + a_im * a_im
    cr = ((lr - 1) * a_re + li * a_im) / den
    ci = (li * a_re - (lr - 1) * a_im) / den
    br = cr[..., None] * b_re - ci[..., None] * b_im
    bi = cr[..., None] * b_im + ci[..., None] * b_re
    G2, N, C = S5_GROUPS // 2, S5_STATE, S5_GROUP
    eye = jnp.eye(G2, dtype=F32)

    def blockdiag_in(m):
        m = m.reshape(2, 2, G2, N, C)
        return jnp.einsum('dhgnc,gk->dhgckn', m, eye).reshape(2, 2, G2 * C, G2 * N)

    def blockdiag_out(m):
        m = m.reshape(2, 2, G2, C, N)
        return jnp.einsum('dhgcn,gk->dhgnkc', m, eye).reshape(2, 2, G2 * N, G2 * C)

    bblk = jnp.concatenate([blockdiag_in(br), blockdiag_in(bi)], axis=-1).astype(BF16)
    cblk = jnp.concatenate([blockdiag_out(c_re), blockdiag_out(-c_im)], axis=-2).astype(BF16)
    lam = jnp.stack([lr, li], axis=1).reshape(2, 2, 2, G2 * N)
    lam = jnp.tile(lam[:, :, None], (1, 1, nb, 1, 1)).reshape(2, 2, 2 * nb, G2 * N)
    return bblk, cblk, lam


def _blockdiag_ones(group, n=LANES):
    i = np.arange(n) // group
    return jnp.asarray((i[:, None] == i[None, :]).astype(np.float32), dtype=BF16)


def kernel(x, c, ctx, c_ctx, ada_w, ada_b, w_in, b_gate, diff_lam, diff_norm_g, gqa_qnorm_g, gqa_knorm_g,
           s5_a_re, s5_a_im, s5_log_dt, s5_b_re, s5_b_im, s5_c_re, s5_c_im, s5_d, s5_w_glu, s5_b_glu,
           ml_conv_w, ml_conv_b, ml_gate_b, ml_norm_g, w_branch, w_out, ln_mix_g, ln_mix_b, ln_ffn_g, ln_ffn_b,
           peer_wq, peer_subkeys, peer_u, peer_v):
    B, L, D = x.shape
    nctx = ctx.shape[1]
    S = nctx + L
    assert D == D_MODEL and nctx % TM == 0 and L % TM == 0 and L % GRID_W == 0
    nt, nct = S // TM, nctx // TM
    depth = ada_w.shape[0]

    h = jnp.concatenate([ctx, x], axis=1)
    R = -(-(B + 1) // 8) * 8
    cond = jnp.zeros((R, D), F32).at[:B].set(c).at[B].set(c_ctx)
    ones64, ones128 = _blockdiag_ones(HD), _blockdiag_ones(ML_HD, ML_HEADS * ML_HD)
    o = IN_OFFS

    for l in range(depth):
        last = l == depth - 1
        lam_init = 0.8 - 0.6 * math.exp(-0.3 * l)
        m = ada_modulation(cond, ada_w[l].astype(BF16), ada_b[l][None, :])
        mod = jnp.stack([jnp.broadcast_to(m[B], (B, 6 * D)), m[:B]], axis=1).reshape(B, 2, 6, D)

        w = w_in[l]
        seg = lambda i: w[:, o[i]:o[i + 1]]
        dup = lambda t: jnp.concatenate([t[:, :HD], t[:, :HD], t[:, HD:], t[:, HD:]], axis=1)
        def with_ones_slots(t, width):
            heads = t.reshape(D, -1, width)
            return jnp.concatenate([heads, jnp.zeros_like(heads)], axis=2).reshape(D, -1)
        w_a = jnp.concatenate([seg(0), _rope_partner(seg(0)), seg(1), _rope_partner(seg(1)), seg(2),
                               seg(9), _rope_partner(seg(9)), dup(seg(10)), dup(_rope_partner(seg(10))),
                               with_ones_slots(seg(11), HD)], axis=1).astype(BF16)
        w_b = jnp.concatenate([seg(3), seg(4), seg(5), seg(6), seg(7), seg(8),
                               jnp.zeros((D, LANES - 16), F32)], axis=1).astype(BF16)
        gate_b = jnp.concatenate([ml_gate_b[l], jnp.zeros((LANES - 16,), F32)])[None, :]
        tab = _rope_tables(L, nctx, gqa_qnorm_g[l], gqa_knorm_g[l])

        dq, dk, dv, gq, gk, gv = proj_attn(h, mod, w_a, tab, ones64, nct)
        u, mqk, mv, mo, mg = proj_seq(h, mod, w_b, gate_b, nct)

        yd = diff_attention(dq, dk, dv, diff_lam[l], diff_norm_g[l][None, :], nctx, lam_init)
        yg = gqa_attention(gq, gk, gv, nctx)

        bblk, cblk, lam = _s5_tables(s5_a_re[l], s5_a_im[l], s5_log_dt[l], s5_b_re[l], s5_b_im[l],
                                     s5_c_re[l], s5_c_im[l], B)
        sf, sb = s5_scan(u, bblk, cblk, lam, nctx)

        mq, mk = ml_prep(mqk, ml_conv_w[l], ml_conv_b[l][None, :], nctx)
        hf = mlstm_scan(mq, mk, mv, mg, nctx, False)
        hb = mlstm_scan(mq, mk, mv, mg, nctx, True)

        h1 = merge(h, mod, yd, yg, u, sf, sb, hf, hb, mo,
                   s5_d[l][None, :], s5_w_glu[l].astype(BF16), s5_b_glu[l][None, :], ml_norm_g[l][None, :],
                   seg(12).astype(BF16), b_gate[l][None, :], w_branch[l].astype(BF16), w_out[l].astype(BF16),
                   ln_mix_g[l][None, :], ln_mix_b[l][None, :], ones128, nct, last)

        tiles, ctx_tiles = (nt - nct, 0) if last else (nt, nct)
        t, ids, gates = peer_route(h1, mod, peer_wq[l].astype(BF16), peer_subkeys[l].astype(BF16), tiles, ctx_tiles)
        h = peer_experts(t, ids, gates, peer_u[l].astype(BF16), peer_v[l].astype(BF16),
                         h1, mod, ln_ffn_g[l][None, :], ln_ffn_b[l][None, :], tiles, ctx_tiles)
        h = h.reshape(B, -1, D)
    return h
```

```python
import functools
import math

import numpy as np
import jax
import jax.numpy as jnp
from jax import lax
from jax.experimental import pallas as pl
from jax.experimental.pallas import tpu as pltpu

F32 = jnp.float32
BF16 = jnp.bfloat16
I32 = jnp.int32

D_MODEL = 1024
DEPTH = 2
GRID_W = 64
ROPE_THETA = 10000.0
LN_EPS = 1e-6
HD = 64
DIFF_HEADS = 4
DIFF_HPS = 2
GQA_KV = 2
GQA_GPS = 2
S5_GROUP = 16
S5_GROUPS = 32
S5_STATE = 64
ML_HEADS = 4
ML_HD = 128
N_BRANCH = 4
PEER_HEADS = 8
PEER_NKEYS = 128
PEER_TOPK = 16
PEER_DQ = 256
ALPHA = (2 * DEPTH) ** 0.25

LANES = 128
TM = 256
TM_E = 2 * TM
TQ = 256
TK = 2048
S5_TC = 128
ML_T = 256
ML_LOCKSTEP = 4
PEER_CA = 16
VMEM_LIMIT = 56 * 1024 * 1024

IN_SPLITS = (512, 512, 512, 512, 512, 512, 512, 512, 16, 512, 128, 128, N_BRANCH * D_MODEL)
IN_OFFS = tuple(int(v) for v in np.cumsum((0,) + IN_SPLITS))


def _cparams(sem):
    return pltpu.CompilerParams(dimension_semantics=sem, vmem_limit_bytes=VMEM_LIMIT)


def _const_spec(shape):
    nd = len(shape)
    return pl.BlockSpec(shape, lambda *_: (0,) * nd, pipeline_mode=pl.Buffered(1))


def _ln(x):
    xc = x - jnp.mean(x, axis=-1, keepdims=True)
    return xc * lax.rsqrt(jnp.mean(xc * xc, axis=-1, keepdims=True) + LN_EPS)


def _sigmoid(x):
    return 1.0 / (1.0 + jnp.exp(-x))


def _gelu(x):
    return 0.5 * x * (1.0 + lax.erf(x * (2.0 ** -0.5)))


def _dot(a, b):
    return jnp.dot(a, b, preferred_element_type=F32)


def _dot_nt(a, b):
    return lax.dot_general(a, b, (((1,), (1,)), ((), ())), preferred_element_type=F32)


def _dot_tn(a, b):
    return lax.dot_general(a, b, (((0,), (0,)), ((), ())), preferred_element_type=F32)


def _ada_kernel(c_ref, w_ref, b_ref, o_ref):
    c = c_ref[...]
    o_ref[...] = _dot((c * _sigmoid(c)).astype(BF16), w_ref[...]) + b_ref[...]


def ada_modulation(cond, w, b):
    R, D = cond.shape
    N = w.shape[1]
    tn = 1536
    return pl.pallas_call(
        _ada_kernel, out_shape=jax.ShapeDtypeStruct((R, N), F32), grid=(N // tn,),
        in_specs=[pl.BlockSpec((R, D), lambda j: (0, 0)), pl.BlockSpec((D, tn), lambda j: (0, j)),
                  pl.BlockSpec((1, tn), lambda j: (0, j))],
        out_specs=pl.BlockSpec((R, tn), lambda j: (0, j)),
        compiler_params=_cparams(("arbitrary",)), name="ada")(cond, w, b)


def _proj_attn_kernel(x_ref, mod_ref, w_ref, tab_ref, ones_ref,
                      dq_ref, dk_ref, dv_ref, gq_ref, gk_ref, gv_ref):
    mod = mod_ref[0, 0]
    xm = (_ln(x_ref[0]) * (1.0 + mod[1:2, :]) + mod[0:1, :]).astype(BF16)

    def mm(lo, n):
        return _dot(xm, w_ref[:, lo:lo + n])

    def rope_store(ref, lo, n, ci, norm):
        t, tp = mm(lo, n), mm(lo + n, n)
        c, s = tab_ref[ci], tab_ref[ci + 1]
        for j in range(n // LANES):
            sl = slice(j * LANES, (j + 1) * LANES)
            tb = t[:, sl]
            y = tb * c + tp[:, sl] * s
            if norm:
                ss = _dot((tb * tb).astype(BF16), ones_ref[...])
                y = y * lax.rsqrt(ss * (1.0 / HD) + LN_EPS)
            ref[0, :, sl] = y.astype(ref.dtype)

    def value_store(ref, lo, n, width):
        lane = lax.broadcasted_iota(I32, (1, n), 1)
        ref[0] = (mm(lo, n) + jnp.where(lane % (2 * width) >= width, 1.0, 0.0)).astype(ref.dtype)

    rope_store(dq_ref, 0, 512, 0, False)
    rope_store(dk_ref, 1024, 512, 2, False)
    dv_ref[0] = mm(2048, 512).astype(dv_ref.dtype)
    rope_store(gq_ref, 2560, 512, 4, True)
    rope_store(gk_ref, 3584, 256, 6, True)
    value_store(gv_ref, 4096, 256, HD)


def proj_attn(xa, mod, w_a, tab, ones_bd, nct):
    B, S, D = xa.shape
    tok = lambda n: pl.BlockSpec((1, TM, n), lambda b, i: (b, i, 0))
    widths = (512, 512, 512, 512, 256, 256)
    outs = [jax.ShapeDtypeStruct((B, S, n), BF16) for n in widths]
    return pl.pallas_call(
        _proj_attn_kernel, out_shape=outs, grid=(B, S // TM),
        in_specs=[tok(D),
                  pl.BlockSpec((1, 1, 6, D), lambda b, i: (b, jnp.where(i >= nct, 1, 0), 0, 0)),
                  _const_spec(w_a.shape),
                  pl.BlockSpec((8, TM, LANES), lambda b, i: (0, i, 0)),
                  _const_spec(ones_bd.shape)],
        out_specs=[tok(n) for n in widths],
        compiler_params=_cparams(("parallel", "parallel")), name="proj_attn")(xa, mod, w_a, tab, ones_bd)


def _proj_seq_kernel(x_ref, mod_ref, w_ref, gb_ref, u_ref, qk_ref, v_ref, o_ref, g_ref):
    mod = mod_ref[0, 0]
    xm = (_ln(x_ref[0]) * (1.0 + mod[1:2, :]) + mod[0:1, :]).astype(BF16)
    u_ref[...] = _dot(xm, w_ref[:, 0:512])
    qk_ref[0] = _dot(xm, w_ref[:, 512:1536])
    v_ref[0] = _dot(xm, w_ref[:, 1536:2048])
    o_ref[0] = _dot(xm, w_ref[:, 2048:2560])
    g_ref[0] = _dot(xm, w_ref[:, 2560:2688]) + gb_ref[...]


def proj_seq(xa, mod, w_b, gate_b, nct):
    B, S, D = xa.shape
    tok = lambda n: pl.BlockSpec((1, TM, n), lambda b, i: (b, i, 0))
    widths = (1024, 512, 512, LANES)
    return pl.pallas_call(
        _proj_seq_kernel,
        out_shape=[jax.ShapeDtypeStruct((S, B * 512), F32)] + [jax.ShapeDtypeStruct((B, S, n), F32) for n in widths],
        grid=(B, S // TM),
        in_specs=[tok(D),
                  pl.BlockSpec((1, 1, 6, D), lambda b, i: (b, jnp.where(i >= nct, 1, 0), 0, 0)),
                  _const_spec(w_b.shape), _const_spec(gate_b.shape)],
        out_specs=[pl.BlockSpec((TM, 512), lambda b, i: (i, b))] + [tok(n) for n in widths],
        compiler_params=_cparams(("parallel", "parallel")), name="proj_seq")(xa, mod, w_b, gate_b)


def _flash(qqs, k_ref, v_ref, nctx, n_lat, tk, dv):
    P = len(qqs)
    R, W = qqs[0].shape[0], v_ref.shape[2] // P
    mxu_sums = dv < W

    def finish(carry, s, v):
        m, l, acc = carry
        m_new = jnp.maximum(m, jnp.max(s, axis=1, keepdims=True))
        alpha = jnp.exp(m - m_new)
        if mxu_sums:
            p = jnp.exp((s - m_new).astype(BF16))
        else:
            p = jnp.exp(s - m_new)
            l = alpha * l + jnp.sum(p, axis=1, keepdims=True)
        return m_new, l, alpha * acc + _dot(p.astype(BF16), v)

    carries = [(jnp.full((R, 1), -jnp.inf, F32), jnp.zeros((R, 1), F32), jnp.zeros((R, W), F32))] * P
    chunks = [pl.ds(0, nctx)] + [pl.ds(nctx + c * tk, tk) for c in range(n_lat)]
    for rows in chunks:
        scores = [_dot_nt(qqs[p], k_ref[0, rows, p * LANES:(p + 1) * LANES]) for p in range(P)]
        carries = [finish(carries[p], scores[p], v_ref[0, rows, p * W:(p + 1) * W]) for p in range(P)]
    return [acc[:, :dv] / (acc[:, dv:dv + 1] if mxu_sums else l) for _, l, acc in carries]


def _ctx_or_all(i, nctx, n_lat, attend):
    @pl.when(i < nctx // TQ)
    def _():
        attend(0)

    @pl.when(i >= nctx // TQ)
    def _():
        attend(n_lat)


def _diff_attn_kernel(q_ref, k_ref, v_ref, lam_ref, g_ref, o_ref, *, nctx, n_lat, tk, lam_init):
    i = pl.program_id(2)
    lane = lax.broadcasted_iota(I32, (TQ, LANES), 1)
    zero = jnp.zeros((TQ, LANES), q_ref.dtype)
    qqs = []
    for h in range(DIFF_HPS):
        q = q_ref[0, :, h * LANES:(h + 1) * LANES]
        qqs.append(jnp.concatenate([jnp.where(lane < HD, q, zero), jnp.where(lane >= HD, q, zero)], axis=0))
    lv = lam_ref[...]
    lam = (jnp.exp(jnp.sum(lv[0:1] * lv[1:2], axis=1, keepdims=True))
           - jnp.exp(jnp.sum(lv[2:3] * lv[3:4], axis=1, keepdims=True)) + lam_init)

    def attend(n):
        for h, o in enumerate(_flash(qqs, k_ref, v_ref, nctx, n, tk, 2 * HD)):
            d = o[:TQ] - lam * o[TQ:]
            y = d * lax.rsqrt(jnp.mean(d * d, axis=-1, keepdims=True) + LN_EPS)
            o_ref[0, :, h * LANES:(h + 1) * LANES] = (y * g_ref[...] * (1.0 - lam_init)).astype(o_ref.dtype)

    _ctx_or_all(i, nctx, n_lat, attend)


def diff_attention(q, k, v, lam_vec, norm_g, nctx, lam_init):
    B, S, _ = q.shape
    tk = math.gcd(S - nctx, TK)
    kern = functools.partial(_diff_attn_kernel, nctx=nctx, n_lat=(S - nctx) // tk, tk=tk, lam_init=lam_init)
    return pl.pallas_call(
        kern, out_shape=jax.ShapeDtypeStruct((B, S, 512), BF16), grid=(B, DIFF_HEADS // DIFF_HPS, S // TQ),
        in_specs=[pl.BlockSpec((1, TQ, DIFF_HPS * LANES), lambda b, h, i: (b, i, h)),
                  pl.BlockSpec((1, S, DIFF_HPS * LANES), lambda b, h, i: (b, 0, h)),
                  pl.BlockSpec((1, S, DIFF_HPS * LANES), lambda b, h, i: (b, 0, h)),
                  pl.BlockSpec((4, HD), lambda b, h, i: (0, 0)),
                  pl.BlockSpec((1, LANES), lambda b, h, i: (0, 0))],
        out_specs=pl.BlockSpec((1, TQ, DIFF_HPS * LANES), lambda b, h, i: (b, i, h)),
        compiler_params=_cparams(("parallel", "parallel", "parallel")), name="diff_attn")(q, k, v, lam_vec, norm_g)


def _gqa_attn_kernel(q_ref, k_ref, v_ref, o_ref, *, nctx, n_lat, tk):
    i = pl.program_id(2)
    lane = lax.broadcasted_iota(I32, (TQ, LANES), 1)
    zero = jnp.zeros((TQ, LANES), q_ref.dtype)
    qqs = []
    for g in range(GQA_GPS):
        parts = []
        for j in range(2 * g, 2 * g + 2):
            blk = q_ref[0, :, j * LANES:(j + 1) * LANES]
            parts += [jnp.where(lane < HD, blk, zero), jnp.where(lane >= HD, blk, zero)]
        qqs.append(jnp.concatenate(parts, axis=0))

    def attend(n):
        for g, o in enumerate(_flash(qqs, k_ref, v_ref, nctx, n, tk, HD)):
            heads = jnp.concatenate([o[h * TQ:(h + 1) * TQ] for h in range(4)], axis=1)
            o_ref[0, :, g * 2 * LANES:(g + 1) * 2 * LANES] = heads.astype(o_ref.dtype)

    _ctx_or_all(i, nctx, n_lat, attend)


def gqa_attention(q, k, v, nctx):
    B, S, _ = q.shape
    tk = math.gcd(S - nctx, TK)
    kern = functools.partial(_gqa_attn_kernel, nctx=nctx, n_lat=(S - nctx) // tk, tk=tk)
    return pl.pallas_call(
        kern, out_shape=jax.ShapeDtypeStruct((B, S, 512), BF16), grid=(B, GQA_KV // GQA_GPS, S // TQ),
        in_specs=[pl.BlockSpec((1, TQ, GQA_GPS * 2 * LANES), lambda b, g, i: (b, i, g)),
                  pl.BlockSpec((1, S, GQA_GPS * LANES), lambda b, g, i: (b, 0, g)),
                  pl.BlockSpec((1, S, GQA_GPS * LANES), lambda b, g, i: (b, 0, g))],
        out_specs=pl.BlockSpec((1, TQ, GQA_GPS * 2 * LANES), lambda b, g, i: (b, i, g)),
        compiler_params=_cparams(("parallel", "parallel", "parallel")), name="gqa_attn")(q, k, v)


def _s5_kernel(uf_ref, ub_ref, bblk_ref, cblk_ref, lam_ref, yf_ref, yb_ref, buf_f, buf_b, st_ref):
    H = S5_GROUPS * S5_STATE // 2
    R = buf_f.shape[0]
    RB = 256
    TB = RB // 8

    @pl.when(pl.program_id(0) == 0)
    def _():
        st_ref[...] = jnp.zeros_like(st_ref)

    half0 = (lax.broadcasted_iota(I32, (RB, 1), 0) & 1) == 0

    for d, (u_ref, buf) in enumerate(((uf_ref, buf_f), (ub_ref, buf_b))):
        for rb in range(R // RB):
            rows = slice(rb * RB, (rb + 1) * RB)
            ut = u_ref[rb * TB:(rb + 1) * TB, :]
            u3 = jnp.stack([ut[:, s * 256:(s + 1) * 256] for s in range(8)], axis=0)
            uh = pltpu.einshape("stc->tsc", u3).reshape(RB, 256).astype(BF16)
            for cols in (slice(0, H), slice(H, 2 * H)):
                buf[rows, cols] = jnp.where(half0, _dot(uh, bblk_ref[d, 0, :, cols]), _dot(uh, bblk_ref[d, 1, :, cols]))

    lfr, lfi, lbr, lbi = lam_ref[0, 0], lam_ref[0, 1], lam_ref[1, 0], lam_ref[1, 1]

    def step(t, carry):
        fr, fi, br, bi = carry
        rf = pl.ds(pl.multiple_of(t * 8, 8), 8)
        x = buf_f[rf, :]
        nfr = lfr * fr - lfi * fi + x[:, :H]
        nfi = lfr * fi + lfi * fr + x[:, H:]
        buf_f[rf, :] = jnp.concatenate([nfr, nfi], axis=1)
        rb = pl.ds(pl.multiple_of((S5_TC - 1 - t) * 8, 8), 8)
        z = buf_b[rb, :]
        nbr = lbr * br - lbi * bi + z[:, :H]
        nbi = lbr * bi + lbi * br + z[:, H:]
        buf_b[rb, :] = jnp.concatenate([nbr, nbi], axis=1)
        return nfr, nfi, nbr, nbi

    fin = lax.fori_loop(0, S5_TC, step, (st_ref[0], st_ref[1], st_ref[2], st_ref[3]))
    for j in range(4):
        st_ref[j] = fin[j]

    for d, (y_ref, buf) in enumerate(((yf_ref, buf_f), (yb_ref, buf_b))):
        for rb in range(R // RB):
            rows = slice(rb * RB, (rb + 1) * RB)
            h = buf[rows, :].astype(BF16)
            y = jnp.where(half0, _dot(h, cblk_ref[d, 0]), _dot(h, cblk_ref[d, 1]))
            y3 = pltpu.einshape("tsc->stc", y.reshape(TB, 8, 256))
            y_ref[rb * TB:(rb + 1) * TB, :] = jnp.concatenate([y3[s] for s in range(8)], axis=1)


def s5_scan(u, bblk, cblk, lam, nctx):
    S, W = u.shape
    assert W == 8 * 256, "the scan packs (sample, half) pairs into the eight sublanes of a vreg"
    nch, nc0 = S // S5_TC, nctx // S5_TC

    def bwd(i):
        return jnp.where(i < nc0, nc0 - 1 - i, (nch - 1) - (i - nc0))
    H2 = S5_GROUPS * S5_STATE
    R = S5_TC * 8
    blk = lambda f: pl.BlockSpec((S5_TC, W), f)
    return pl.pallas_call(
        _s5_kernel, out_shape=[jax.ShapeDtypeStruct(u.shape, F32)] * 2, grid=(nch,),
        in_specs=[blk(lambda i: (i, 0)), blk(lambda i: (bwd(i), 0)),
                  _const_spec(bblk.shape), _const_spec(cblk.shape), _const_spec(lam.shape)],
        out_specs=[blk(lambda i: (i, 0)), blk(lambda i: (bwd(i), 0))],
        scratch_shapes=[pltpu.VMEM((R, H2), F32), pltpu.VMEM((R, H2), F32), pltpu.VMEM((4, 8, H2 // 2), F32)],
        compiler_params=_cparams(("arbitrary",)), name="s5_scan")(u, u, bblk, cblk, lam)


def _ml_prep_kernel(x_ref, prev_ref, next_ref, w_ref, b_ref, q_ref, k_ref, *, seg_starts, seg_ends):
    i = pl.program_id(1)
    x = x_ref[0]
    row = lax.broadcasted_iota(I32, x.shape, 0)
    first = functools.reduce(jnp.logical_or, [i == s for s in seg_starts])
    last = functools.reduce(jnp.logical_or, [i == s for s in seg_ends])
    pr = jnp.where(first, 0.0, prev_ref[0, 7:8, :])
    nx = jnp.where(last, 0.0, next_ref[0, 0:1, :])
    xp = jnp.where(row == 0, pr, pltpu.roll(x, 1, 0))
    xn = jnp.where(row == TM - 1, nx, pltpu.roll(x, TM - 1, 0))
    w = w_ref[...]
    y = b_ref[...] + xp * w[0:1, :] + x * w[1:2, :] + xn * w[2:3, :]
    y = y * _sigmoid(y)
    q_ref[0] = y[:, :512].astype(q_ref.dtype)
    k_ref[0] = (y[:, 512:] * (ML_HD ** -0.5)).astype(k_ref.dtype)


def ml_prep(qk, conv_w, conv_b, nctx):
    B, S, W = qk.shape
    nt, nct, r8 = S // TM, nctx // TM, TM // 8
    kern = functools.partial(_ml_prep_kernel, seg_starts=(0, nct), seg_ends=(nct - 1, nt - 1))
    return pl.pallas_call(
        kern, out_shape=[jax.ShapeDtypeStruct((B, S, 512), BF16)] * 2, grid=(B, nt),
        in_specs=[pl.BlockSpec((1, TM, W), lambda b, i: (b, i, 0)),
                  pl.BlockSpec((1, 8, W), lambda b, i: (b, jnp.maximum(i * r8 - 1, 0), 0)),
                  pl.BlockSpec((1, 8, W), lambda b, i: (b, jnp.minimum((i + 1) * r8, S // 8 - 1), 0)),
                  pl.BlockSpec((3, W), lambda b, i: (0, 0)), pl.BlockSpec((1, W), lambda b, i: (0, 0))],
        out_specs=[pl.BlockSpec((1, TM, 512), lambda b, i: (b, i, 0))] * 2,
        compiler_params=_cparams(("parallel", "parallel")), name="ml_prep")(qk, qk, qk, conv_w, conv_b)


def _log_sigmoid(x):
    return jnp.minimum(x, 0.0) - jnp.log1p(jnp.exp(-jnp.abs(x)))


def _mlstm_kernel(q_ref, k_ref, v_ref, g_ref, h_ref, c_sc, n_sc, m_sc, *, reverse):
    T = ML_T
    gi, gf = (8, 12) if reverse else (0, 4)

    @pl.when(pl.program_id(0) == 0)
    def _():
        c_sc[...] = jnp.zeros_like(c_sc)
        n_sc[...] = jnp.zeros_like(n_sc)
        m_sc[...] = jnp.zeros_like(m_sc)

    r = lax.broadcasted_iota(I32, (T, T), 0)
    c = lax.broadcasted_iota(I32, (T, T), 1)
    mask = (c >= r) if reverse else (c <= r)
    tri = jnp.where(mask, 1.0, 0.0)
    hp = lax.Precision.HIGHEST
    end = 0 if reverse else T - 1
    def gates(b):
        g = g_ref[b]
        gt = g.T
        bcol_all = jnp.dot(tri, _log_sigmoid(g), preferred_element_type=F32, precision=hp)
        brow_all = lax.dot_general(_log_sigmoid(gt[0:16]), tri, (((1,), (1,)), ((), ())),
                                   preferred_element_type=F32, precision=hp)
        return g, gt, bcol_all, brow_all

    def chain(b, hh, g, gt, bcol_all, brow_all):
        st = b * ML_HEADS + hh
        sl = slice(hh * ML_HD, (hh + 1) * ML_HD)
        q, k, v = q_ref[b, :, sl], k_ref[b, :, sl], v_ref[b, :, sl]
        bcol, brow = bcol_all[:, gf + hh:gf + hh + 1], brow_all[gf + hh:gf + hh + 1, :]
        icol, irow = g[:, gi + hh:gi + hh + 1], gt[gi + hh:gi + hh + 1, :]
        m_old = m_sc[st][:, 0:1]
        n_old = n_sc[st]
        c_old = c_sc[st]
        logw = jnp.where(mask, bcol - brow + irow, -jnp.inf)
        m_inter = bcol + m_old
        m_t = jnp.maximum(m_inter, jnp.max(logw, axis=1, keepdims=True))
        yield
        s = _dot_nt(q, k) * jnp.exp(logw - m_t)
        inter = jnp.exp(m_inter - m_t)
        yield
        num = _dot(s.astype(BF16), v.astype(BF16)) + inter * _dot_nt(q, c_old.astype(BF16))
        den = jnp.sum(s, axis=1, keepdims=True) + inter * jnp.sum(q.astype(F32) * n_old, axis=1, keepdims=True)
        h_ref[b, :, sl] = num / jnp.maximum(jnp.abs(den), jnp.exp(-m_t))
        yield
        b_end = bcol[end:end + 1, :]
        g_row, g_col = b_end - brow + irow, b_end - bcol + icol
        m_new = jnp.maximum(b_end + m_old, jnp.max(g_row, axis=1, keepdims=True))
        decay = jnp.exp(b_end + m_old - m_new)
        wk = jnp.exp(g_col - m_new)
        c_sc[st] = decay * c_old + _dot_tn((v * wk).astype(BF16), k)
        n_sc[st] = decay * n_old + jnp.sum(k.astype(F32) * wk, axis=0, keepdims=True)
        m_sc[st] = jnp.broadcast_to(m_new, (1, LANES))

    nb = q_ref.shape[0]
    for b0 in range(0, nb, ML_LOCKSTEP):
        chains = []
        for b in range(b0, min(b0 + ML_LOCKSTEP, nb)):
            shared = gates(b)
            chains += [chain(b, hh, *shared) for hh in range(ML_HEADS)]
        for _ in range(4):
            for ch in chains:
                next(ch, None)


def mlstm_scan(q, k, v, g, nctx, reverse):
    B, S, W = q.shape
    nch, nc0 = S // ML_T, nctx // ML_T

    def order(i):
        return jnp.where(i < nc0, nc0 - 1 - i, (nch - 1) - (i - nc0)) if reverse else i
    blk = lambda n: pl.BlockSpec((B, ML_T, n), lambda i: (0, order(i), 0))
    nst = B * ML_HEADS
    return pl.pallas_call(
        functools.partial(_mlstm_kernel, reverse=reverse),
        out_shape=jax.ShapeDtypeStruct((B, S, W), F32), grid=(nch,),
        in_specs=[blk(W), blk(W), blk(W), blk(LANES)], out_specs=blk(W),
        scratch_shapes=[pltpu.VMEM((nst, ML_HD, ML_HD), F32), pltpu.VMEM((nst, 1, ML_HD), F32),
                        pltpu.VMEM((nst, 1, LANES), F32)],
        compiler_params=_cparams(("arbitrary",)), name="mlstm_bwd" if reverse else "mlstm_fwd")(q, k, v, g)


def _merge_kernel(x_ref, mod_ref, yd_ref, yg_ref, u_ref, sf_ref, sb_ref, hf_ref, hb_ref, mo_ref,
                  s5d_ref, wglu_ref, bglu_ref, mlg_ref, wgate_ref, bgate_ref, wbr_ref, wout_ref,
                  lng_ref, lnb_ref, ones_ref, o_ref):
    x = x_ref[0]
    mod = mod_ref[0, 0]
    xm = (_ln(x) * (1.0 + mod[1:2, :]) + mod[0:1, :]).astype(BF16)
    ys = u_ref[...] * s5d_ref[...] + sf_ref[...] + sb_ref[...]
    z = _dot(_gelu(ys).astype(BF16), wglu_ref[...]) + bglu_ref[...]
    ys = z[:, :512] * _sigmoid(z[:, 512:])
    hm = hf_ref[0] + hb_ref[0]
    ss = _dot((hm * hm).astype(BF16), ones_ref[...])
    ym = hm * lax.rsqrt(ss * (1.0 / ML_HD) + LN_EPS) * mlg_ref[...] * _sigmoid(mo_ref[0])
    branches = (yd_ref[0], ys.astype(BF16), ym.astype(BF16), yg_ref[0])
    merged = None
    for j, yb in enumerate(branches):
        gate = _sigmoid(_dot(xm, wgate_ref[:, j * D_MODEL:(j + 1) * D_MODEL]) + bgate_ref[:, j * D_MODEL:(j + 1) * D_MODEL])
        term = gate * _dot(yb, wbr_ref[j])
        merged = term if merged is None else merged + term
    mix = _dot(merged.astype(BF16), wout_ref[...])
    o_ref[...] = _ln(ALPHA * x + mod[2:3, :] * mix) * lng_ref[...] + lnb_ref[...]


def merge(xa, mod, yd, yg, u, sf, sb, hf, hb, mo, s5d, wglu, bglu, mlg, wgate, bgate, wbr, wout, lng, lnb, ones_ml, nct,
          latent_only):
    B, S, D = xa.shape
    first = nct if latent_only else 0
    nti = S // TM - first
    tok = lambda n: pl.BlockSpec((1, TM, n), lambda b, i: (b, i + first, 0))
    consts = (s5d, wglu, bglu, mlg, wgate, bgate, wbr, wout, lng, lnb, ones_ml)
    return pl.pallas_call(
        _merge_kernel, out_shape=jax.ShapeDtypeStruct((B * nti * TM, D), F32), grid=(B, nti),
        in_specs=[tok(D), pl.BlockSpec((1, 1, 6, D), lambda b, i: (b, jnp.where(i + first >= nct, 1, 0), 0, 0))]
        + [tok(512)] * 2 + [pl.BlockSpec((TM, 512), lambda b, i: (i + first, b))] * 3 + [tok(512)] * 3
        + [_const_spec(a.shape) for a in consts],
        out_specs=pl.BlockSpec((TM, D), lambda b, i: (b * nti + i, 0)),
        compiler_params=_cparams(("parallel", "parallel")), name="merge")(xa, mod, yd, yg, u, sf, sb, hf, hb, mo, *consts)


def _top16(*problems):
    iotas = {prob[0].shape: lax.broadcasted_iota(I32, prob[0].shape, 0).astype(F32) for prob in problems}

    def one(kk, s, val_ref, idx_ref, payload):
        R, ri = s.shape[0], iotas[s.shape]
        m = jnp.max(s, axis=0, keepdims=True)
        ix = jnp.min(jnp.where(s == m, ri, float(R)), axis=0, keepdims=True)
        hit = ri == ix
        val_ref[pl.ds(kk, 1), :] = m
        idx_ref[pl.ds(kk, 1), :] = ix if payload is None else jnp.max(jnp.where(hit, payload, -1.0), axis=0, keepdims=True)
        return jnp.where(hit, -jnp.inf, s)

    def body(kk, ss):
        return tuple(one(kk, s, *prob[1:]) for s, prob in zip(ss, problems))

    lax.fori_loop(0, PEER_TOPK, body, tuple(prob[0] for prob in problems))


_PEER_CAND_COUNTS = tuple(PEER_TOPK // (p + 1) for p in range(8))
_PEER_CAND_ROWS = tuple((sum(_PEER_CAND_COUNTS[:p]), n) for p, n in enumerate(_PEER_CAND_COUNTS))
_PEER_NCAND = -(-(sum(_PEER_CAND_COUNTS) + 8) // 8) * 8


def _peer_route_kernel(x_ref, mod0_ref, mod1_ref, wq_ref, sk_ref, t_ref, ids_ref, gate_ref,
                       v1_sc, i1_sc, v2_sc, i2_sc, cand_sc, cidx_sc, top_sc, tid_sc, ids_sc, gates_sc):
    h = pl.program_id(1)

    @pl.when(h == 0)
    def _():
        for half, mod_ref in enumerate((mod0_ref, mod1_ref)):
            rows = slice(half * TM, (half + 1) * TM)
            mod = mod_ref[0, 0]
            t_ref[rows, :] = (_ln(x_ref[rows, :]) * (1.0 + mod[4:5, :]) + mod[3:4, :]).astype(BF16)

    q = _dot(t_ref[...], wq_ref[...]).astype(BF16)
    half = PEER_DQ // 2
    s1, s2 = _dot_nt(sk_ref[0], q[:, :half]), _dot_nt(sk_ref[1], q[:, half:])
    _top16((s1, v1_sc, i1_sc, None))
    _top16((s2, v2_sc, i2_sc, None))
    nk = float(PEER_NKEYS)
    for p, (lo, n) in enumerate(_PEER_CAND_ROWS):
        cand_sc[lo:lo + n, :] = v1_sc[p:p + 1, :] + v2_sc[0:n, :]
        cidx_sc[lo:lo + n, :] = i1_sc[p:p + 1, :] * nk + i2_sc[0:n, :]
    lo = _PEER_CAND_ROWS[-1][0] + _PEER_CAND_ROWS[-1][1]
    cand_sc[lo:lo + 8, :] = v1_sc[8:16, :] + v2_sc[0:1, :]
    cidx_sc[lo:lo + 8, :] = i1_sc[8:16, :] * nk + i2_sc[0:1, :]
    pad = _PEER_NCAND - (lo + 8)
    cand_sc[lo + 8:, :] = jnp.full((pad, cand_sc.shape[1]), -jnp.inf, F32)
    cidx_sc[lo + 8:, :] = jnp.zeros((pad, cand_sc.shape[1]), F32)
    _top16((cand_sc[...], top_sc, tid_sc, cidx_sc[...]))
    top = top_sc[...]
    e = jnp.exp(top - top[0:1, :])
    entries = pl.ds(pl.multiple_of(h * PEER_TOPK, PEER_TOPK), PEER_TOPK)
    gates_sc[entries, :] = e / jnp.sum(e, axis=0, keepdims=True)
    ids_sc[entries, :] = tid_sc[...]

    @pl.when(h == PEER_HEADS - 1)
    def _():
        gate_ref[...] = gates_sc[...].T
        ids_ref[...] = ids_sc[...].T.astype(I32)


def _half_tile_mod_spec(half, nt_per_sample, nct):
    def index(i, j):
        r = 2 * i + half
        return (r // nt_per_sample, jnp.where(r % nt_per_sample >= nct, 1, 0), 0, 0)
    return pl.BlockSpec((1, 1, 6, D_MODEL), index)


def peer_route(h1, mod, wq, subkeys, nt_per_sample, nct):
    T, D = h1.shape
    assert T % TM_E == 0 and TM_E == 2 * TM
    f = lambda n: pltpu.VMEM((n, TM_E), F32)
    entries = PEER_HEADS * PEER_TOPK
    return pl.pallas_call(
        _peer_route_kernel,
        out_shape=[jax.ShapeDtypeStruct((T, D), BF16),
                   jax.ShapeDtypeStruct((T, entries), I32),
                   jax.ShapeDtypeStruct((T, entries), F32)],
        grid=(T // TM_E, PEER_HEADS),
        in_specs=[pl.BlockSpec((TM_E, D), lambda i, h: (i, 0)),
                  _half_tile_mod_spec(0, nt_per_sample, nct), _half_tile_mod_spec(1, nt_per_sample, nct),
                  pl.BlockSpec((D, PEER_DQ), lambda i, h: (0, h)),
                  pl.BlockSpec((2, PEER_NKEYS, PEER_DQ // 2), lambda i, h: (0, 0, 0))],
        out_specs=[pl.BlockSpec((TM_E, D), lambda i, h: (i, 0)),
                   pl.BlockSpec((TM_E, entries), lambda i, h: (i, 0)),
                   pl.BlockSpec((TM_E, entries), lambda i, h: (i, 0))],
        scratch_shapes=[f(16), f(16), f(16), f(16), f(_PEER_NCAND), f(_PEER_NCAND), f(16), f(16),
                        f(entries), f(entries)],
        compiler_params=_cparams(("parallel", "arbitrary")), name="peer_route")(h1, mod, mod, wq, subkeys)


def _peer_expert_kernel(t_ref, ids_ref, gate_ref, u_ref, v_ref, x_ref, mod0_ref, mod1_ref, lng_ref, lnb_ref, o_ref,
                        a_sc, wgt_sc, *, nchunk):
    j = pl.program_id(1)
    NK, CA = PEER_NKEYS, PEER_CA
    w_sc = a_sc

    @pl.when(j < nchunk)
    def _():
        a = _dot(t_ref[...], u_ref[0]).astype(a_sc.dtype)
        a3 = jnp.stack([a[:, al * NK:(al + 1) * NK] for al in range(CA)], axis=0)
        a_sc[:, pl.ds(pl.multiple_of(j * CA, CA), CA), :] = pltpu.einshape("atb->tab", a3)

    @pl.when(j == nchunk - 1)
    def _():
        io_k = lax.broadcasted_iota(I32, (NK, NK), 0)
        io_a = lax.broadcasted_iota(I32, (NK, 2 * NK), 0)

        def masks(two):
            idp = ids_ref[two, :]
            ids = jnp.concatenate([idp[0:1], idp[1:2]], axis=1)
            blocks = [jnp.where(io_k == (idp[tok:tok + 1] & (NK - 1)), 1.0, 0.0).astype(BF16) for tok in range(2)]
            zero = jnp.zeros((NK, NK), BF16)
            onehot = jnp.concatenate([jnp.concatenate([blocks[0], zero], axis=1),
                                      jnp.concatenate([zero, blocks[1]], axis=1)], axis=0)
            return onehot, io_a == (ids >> 7)

        def pick(p, carry):
            two = pl.ds(pl.multiple_of(p * 2, 2), 2)
            onehot, r1 = masks(two)
            a2 = a_sc[two]
            at = jnp.concatenate([a2[0], a2[1]], axis=1).astype(BF16)
            picked = _dot(at, onehot)
            act = jnp.sum(jnp.where(r1, picked, 0.0), axis=0, keepdims=True)
            gp = gate_ref[two, :]
            w = jnp.concatenate([gp[0:1], gp[1:2]], axis=1) * _gelu(act)
            wgt_sc[two, :] = jnp.concatenate([w[:, :NK], w[:, NK:]], axis=0)
            return carry

        def scatter(p, carry):
            two = pl.ds(pl.multiple_of(p * 2, 2), 2)
            onehot, r1 = masks(two)
            wp = wgt_sc[two, :]
            w = jnp.concatenate([wp[0:1], wp[1:2]], axis=1)
            wt = _dot_nt(jnp.where(r1, w, 0.0).astype(BF16), onehot)
            w_sc[two] = jnp.stack([wt[:, :NK], wt[:, NK:]], axis=0).astype(w_sc.dtype)
            return carry

        lax.fori_loop(0, TM_E // 2, pick, 0, unroll=16)
        lax.fori_loop(0, TM_E // 2, scatter, 0, unroll=32)
        o_ref[...] = jnp.zeros_like(o_ref)

    @pl.when(j >= nchunk)
    def _():
        w3 = pltpu.einshape("tab->atb", w_sc[:, pl.ds(pl.multiple_of((j - nchunk) * CA, CA), CA), :])
        w = jnp.concatenate([w3[al] for al in range(CA)], axis=1).astype(BF16)
        o_ref[...] += _dot(w, v_ref[...])

    @pl.when(j == 2 * nchunk - 1)
    def _():
        for half, mod_ref in enumerate((mod0_ref, mod1_ref)):
            rows = slice(half * TM, (half + 1) * TM)
            y = ALPHA * x_ref[rows, :] + mod_ref[0, 0][5:6, :] * o_ref[rows, :]
            o_ref[rows, :] = _ln(y) * lng_ref[...] + lnb_ref[...]


def peer_experts(t, ids, gates, emb_u, emb_v, h1, mod, lng, lnb, nt_per_sample, nct):
    T, D = h1.shape
    E = emb_v.shape[0]
    ce = PEER_CA * PEER_NKEYS
    nchunk = E // ce
    emb_ut = emb_u.reshape(nchunk, ce, D).transpose(0, 2, 1)
    assert T % TM_E == 0 and TM_E == 2 * TM
    tok = lambda n, **kw: pl.BlockSpec((TM_E, n), lambda i, j: (i, 0), **kw)
    mod_spec = lambda half: _half_tile_mod_spec(half, nt_per_sample, nct)
    return pl.pallas_call(
        functools.partial(_peer_expert_kernel, nchunk=nchunk),
        out_shape=jax.ShapeDtypeStruct((T, D), F32), grid=(T // TM_E, 2 * nchunk),
        in_specs=[tok(D), tok(LANES), tok(LANES),
                  pl.BlockSpec((1, D, ce), lambda i, j: (jnp.minimum(j, nchunk - 1), 0, 0)),
                  pl.BlockSpec((ce, D), lambda i, j: (jnp.maximum(j - nchunk, 0), 0)),
                  tok(D), mod_spec(0), mod_spec(1),
                  pl.BlockSpec((1, D), lambda i, j: (0, 0)), pl.BlockSpec((1, D), lambda i, j: (0, 0))],
        out_specs=tok(D),
        scratch_shapes=[pltpu.VMEM((TM_E, PEER_NKEYS, PEER_NKEYS), BF16), pltpu.VMEM((TM_E, LANES), F32)],
        compiler_params=_cparams(("parallel", "arbitrary")), name="peer_experts")(t, ids, gates, emb_ut, emb_v, h1, mod, mod, lng, lnb)


_ROPE_IDX = np.arange(HD)
_ROPE_PERM = np.where(_ROPE_IDX % 32 < 16, _ROPE_IDX + 16, _ROPE_IDX - 16)
_ROPE_SIGN = np.where(_ROPE_IDX % 32 < 16, -1.0, 1.0).astype(np.float32)


def _rope_partner(w):
    n = w.shape[1] // HD
    perm = np.concatenate([h * HD + _ROPE_PERM for h in range(n)])
    return w[:, perm] * jnp.asarray(np.tile(_ROPE_SIGN, n))


def _rope_tables(L, nctx, gq, gk):
    rows = L // GRID_W
    row = jnp.repeat(jnp.arange(rows, dtype=F32), GRID_W)
    col = jnp.tile(jnp.arange(GRID_W, dtype=F32), rows)
    nf = HD // 4
    inv = ROPE_THETA ** (-jnp.arange(nf, dtype=F32) / nf)
    ar, ac = row[:, None] * inv, col[:, None] * inv
    cos = jnp.concatenate([jnp.cos(ar), jnp.cos(ar), jnp.cos(ac), jnp.cos(ac)], axis=1)
    sin = jnp.concatenate([jnp.sin(ar), jnp.sin(ar), jnp.sin(ac), jnp.sin(ac)], axis=1)
    cos = jnp.concatenate([jnp.ones((nctx, HD), F32), cos], axis=0)
    sin = jnp.concatenate([jnp.zeros((nctx, HD), F32), sin], axis=0)
    scale = HD ** -0.5
    one = jnp.ones((HD,), F32)
    tabs = []
    for g, sc in ((one, scale), (one, 1.0), (gq, scale), (gk, 1.0)):
        tabs += [cos * (g * sc), sin * (g[_ROPE_PERM] * sc)]
    return jnp.tile(jnp.stack(tabs), (1, 1, 2))


def _s5_tables(a_re, a_im, log_dt, b_re, b_im, c_re, c_im, nb):
    dt = jnp.exp(log_dt)[..., None]
    mag = jnp.exp(a_re * dt)
    lr, li = mag * jnp.cos(a_im * dt), mag * jnp.sin(a_im * dt)
    den = a_re * a_re + a_im * a_im
    cr = ((lr - 1) * a_re + li * a_im) / den
    ci = (li * a_re - (lr - 1) * a_im) / den
    br = cr[..., None] * b_re - ci[..., None] * b_im
    bi = cr[..., None] * b_im + ci[..., None] * b_re
    G2, N, C = S5_GROUPS // 2, S5_STATE, S5_GROUP
    eye = jnp.eye(G2, dtype=F32)

    def blockdiag_in(m):
        m = m.reshape(2, 2, G2, N, C)
        return jnp.einsum('dhgnc,gk->dhgckn', m, eye).reshape(2, 2, G2 * C, G2 * N)

    def blockdiag_out(m):
        m = m.reshape(2, 2, G2, C, N)
        return jnp.einsum('dhgcn,gk->dhgnkc', m, eye).reshape(2, 2, G2 * N, G2 * C)

    bblk = jnp.concatenate([blockdiag_in(br), blockdiag_in(bi)], axis=-1).astype(BF16)
    cblk = jnp.concatenate([blockdiag_out(c_re), blockdiag_out(-c_im)], axis=-2).astype(BF16)
    lam = jnp.stack([lr, li], axis=1).reshape(2, 2, 2, G2 * N)
    lam = jnp.tile(lam[:, :, None], (1, 1, nb, 1, 1)).reshape(2, 2, 2 * nb, G2 * N)
    return bblk, cblk, lam


def _blockdiag_ones(group, n=LANES):
    i = np.arange(n) // group
    return jnp.asarray((i[:, None] == i[None, :]).astype(np.float32), dtype=BF16)


def kernel(x, c, ctx, c_ctx, ada_w, ada_b, w_in, b_gate, diff_lam, diff_norm_g, gqa_qnorm_g, gqa_knorm_g,
           s5_a_re, s5_a_im, s5_log_dt, s5_b_re, s5_b_im, s5_c_re, s5_c_im, s5_d, s5_w_glu, s5_b_glu,
           ml_conv_w, ml_conv_b, ml_gate_b, ml_norm_g, w_branch, w_out, ln_mix_g, ln_mix_b, ln_ffn_g, ln_ffn_b,
           peer_wq, peer_subkeys, peer_u, peer_v):
    B, L, D = x.shape
    nctx = ctx.shape[1]
    S = nctx + L
    assert D == D_MODEL and nctx % TM == 0 and L % TM == 0 and L % GRID_W == 0
    nt, nct = S // TM, nctx // TM
    depth = ada_w.shape[0]

    h = jnp.concatenate([ctx, x], axis=1)
    R = -(-(B + 1) // 8) * 8
    cond = jnp.zeros((R, D), F32).at[:B].set(c).at[B].set(c_ctx)
    ones64, ones128 = _blockdiag_ones(HD), _blockdiag_ones(ML_HD, ML_HEADS * ML_HD)
    o = IN_OFFS

    for l in range(depth):
        last = l == depth - 1
        lam_init = 0.8 - 0.6 * math.exp(-0.3 * l)
        m = ada_modulation(cond, ada_w[l].astype(BF16), ada_b[l][None, :])
        mod = jnp.stack([jnp.broadcast_to(m[B], (B, 6 * D)), m[:B]], axis=1).reshape(B, 2, 6, D)

        w = w_in[l]
        seg = lambda i: w[:, o[i]:o[i + 1]]
        dup = lambda t: jnp.concatenate([t[:, :HD], t[:, :HD], t[:, HD:], t[:, HD:]], axis=1)
        def with_ones_slots(t, width):
            heads = t.reshape(D, -1, width)
            return jnp.concatenate([heads, jnp.zeros_like(heads)], axis=2).reshape(D, -1)
        w_a = jnp.concatenate([seg(0), _rope_partner(seg(0)), seg(1), _rope_partner(seg(1)), seg(2),
                               seg(9), _rope_partner(seg(9)), dup(seg(10)), dup(_rope_partner(seg(10))),
                               with_ones_slots(seg(11), HD)], axis=1).astype(BF16)
        w_b = jnp.concatenate([seg(3), seg(4), seg(5), seg(6), seg(7), seg(8),
                               jnp.zeros((D, LANES - 16), F32)], axis=1).astype(BF16)
        gate_b = jnp.concatenate([ml_gate_b[l], jnp.zeros((LANES - 16,), F32)])[None, :]
        tab = _rope_tables(L, nctx, gqa_qnorm_g[l], gqa_knorm_g[l])

        dq, dk, dv, gq, gk, gv = proj_attn(h, mod, w_a, tab, ones64, nct)
        u, mqk, mv, mo, mg = proj_seq(h, mod, w_b, gate_b, nct)

        yd = diff_attention(dq, dk, dv, diff_lam[l], diff_norm_g[l][None, :], nctx, lam_init)
        yg = gqa_attention(gq, gk, gv, nctx)

        bblk, cblk, lam = _s5_tables(s5_a_re[l], s5_a_im[l], s5_log_dt[l], s5_b_re[l], s5_b_im[l],
                                     s5_c_re[l], s5_c_im[l], B)
        sf, sb = s5_scan(u, bblk, cblk, lam, nctx)

        mq, mk = ml_prep(mqk, ml_conv_w[l], ml_conv_b[l][None, :], nctx)
        hf = mlstm_scan(mq, mk, mv, mg, nctx, False)
        hb = mlstm_scan(mq, mk, mv, mg, nctx, True)

        h1 = merge(h, mod, yd, yg, u, sf, sb, hf, hb, mo,
                   s5_d[l][None, :], s5_w_glu[l].astype(BF16), s5_b_glu[l][None, :], ml_norm_g[l][None, :],
                   seg(12).astype(BF16), b_gate[l][None, :], w_branch[l].astype(BF16), w_out[l].astype(BF16),
                   ln_mix_g[l][None, :], ln_mix_b[l][None, :], ones128, nct, last)

        tiles, ctx_tiles = (nt - nct, 0) if last else (nt, nct)
        t, ids, gates = peer_route(h1, mod, peer_wq[l].astype(BF16), peer_subkeys[l].astype(BF16), tiles, ctx_tiles)
        h = peer_experts(t, ids, gates, peer_u[l].astype(BF16), peer_v[l].astype(BF16),
                         h1, mod, ln_ffn_g[l][None, :], ln_ffn_b[l][None, :], tiles, ctx_tiles)
        h = h.reshape(B, -1, D)
    return h
```

```python
import functools
import math

import numpy as np
import jax
import jax.numpy as jnp
from jax import lax
from jax.experimental import pallas as pl
from jax.experimental.pallas import tpu as pltpu

F32 = jnp.float32
BF16 = jnp.bfloat16
I32 = jnp.int32

D_MODEL = 1024
DEPTH = 2
GRID_W = 64
ROPE_THETA = 10000.0
LN_EPS = 1e-6
HD = 64
DIFF_HEADS = 4
DIFF_HPS = 2
GQA_KV = 2
GQA_GPS = 2
S5_GROUP = 16
S5_GROUPS = 32
S5_STATE = 64
ML_HEADS = 4
ML_HD = 128
N_BRANCH = 4
PEER_HEADS = 8
PEER_NKEYS = 128
PEER_TOPK = 16
PEER_DQ = 256
ALPHA = (2 * DEPTH) ** 0.25

LANES = 128
TM = 256
TM_E = 2 * TM
TQ = 256
TK = 2048
S5_TC = 128
ML_T = 256
ML_LOCKSTEP = 4
PEER_CA = 16
VMEM_LIMIT = 56 * 1024 * 1024

IN_SPLITS = (512, 512, 512, 512, 512, 512, 512, 512, 16, 512, 128, 128, N_BRANCH * D_MODEL)
IN_OFFS = tuple(int(v) for v in np.cumsum((0,) + IN_SPLITS))


def _cparams(sem):
    return pltpu.CompilerParams(dimension_semantics=sem, vmem_limit_bytes=VMEM_LIMIT)


def _const_spec(shape):
    nd = len(shape)
    return pl.BlockSpec(shape, lambda *_: (0,) * nd, pipeline_mode=pl.Buffered(1))


def _ln(x):
    xc = x - jnp.mean(x, axis=-1, keepdims=True)
    return xc * lax.rsqrt(jnp.mean(xc * xc, axis=-1, keepdims=True) + LN_EPS)


def _sigmoid(x):
    return 1.0 / (1.0 + jnp.exp(-x))


def _gelu(x):
    return 0.5 * x * (1.0 + lax.erf(x * (2.0 ** -0.5)))


def _dot(a, b):
    return jnp.dot(a, b, preferred_element_type=F32)


def _dot_nt(a, b):
    return lax.dot_general(a, b, (((1,), (1,)), ((), ())), preferred_element_type=F32)


def _dot_tn(a, b):
    return lax.dot_general(a, b, (((0,), (0,)), ((), ())), preferred_element_type=F32)


def _ada_kernel(c_ref, w_ref, b_ref, o_ref):
    c = c_ref[...]
    o_ref[...] = _dot((c * _sigmoid(c)).astype(BF16), w_ref[...]) + b_ref[...]


def ada_modulation(cond, w, b):
    R, D = cond.shape
    N = w.shape[1]
    tn = 1536
    return pl.pallas_call(
        _ada_kernel, out_shape=jax.ShapeDtypeStruct((R, N), F32), grid=(N // tn,),
        in_specs=[pl.BlockSpec((R, D), lambda j: (0, 0)), pl.BlockSpec((D, tn), lambda j: (0, j)),
                  pl.BlockSpec((1, tn), lambda j: (0, j))],
        out_specs=pl.BlockSpec((R, tn), lambda j: (0, j)),
        compiler_params=_cparams(("arbitrary",)), name="ada")(cond, w, b)


def _proj_attn_kernel(x_ref, mod_ref, w_ref, tab_ref, ones_ref,
                      dq_ref, dk_ref, dv_ref, gq_ref, gk_ref, gv_ref):
    mod = mod_ref[0, 0]
    xm = (_ln(x_ref[0]) * (1.0 + mod[1:2, :]) + mod[0:1, :]).astype(BF16)

    def mm(lo, n):
        return _dot(xm, w_ref[:, lo:lo + n])

    def rope_store(ref, lo, n, ci, norm):
        t, tp = mm(lo, n), mm(lo + n, n)
        c, s = tab_ref[ci], tab_ref[ci + 1]
        for j in range(n // LANES):
            sl = slice(j * LANES, (j + 1) * LANES)
            tb = t[:, sl]
            y = tb * c + tp[:, sl] * s
            if norm:
                ss = _dot((tb * tb).astype(BF16), ones_ref[...])
                y = y * lax.rsqrt(ss * (1.0 / HD) + LN_EPS)
            ref[0, :, sl] = y.astype(ref.dtype)

    def value_store(ref, lo, n, width):
        lane = lax.broadcasted_iota(I32, (1, n), 1)
        ref[0] = (mm(lo, n) + jnp.where(lane % (2 * width) >= width, 1.0, 0.0)).astype(ref.dtype)

    rope_store(dq_ref, 0, 512, 0, False)
    rope_store(dk_ref, 1024, 512, 2, False)
    dv_ref[0] = mm(2048, 512).astype(dv_ref.dtype)
    rope_store(gq_ref, 2560, 512, 4, True)
    rope_store(gk_ref, 3584, 256, 6, True)
    value_store(gv_ref, 4096, 256, HD)


def proj_attn(xa, mod, w_a, tab, ones_bd, nct):
    B, S, D = xa.shape
    tok = lambda n: pl.BlockSpec((1, TM, n), lambda b, i: (b, i, 0))
    widths = (512, 512, 512, 512, 256, 256)
    outs = [jax.ShapeDtypeStruct((B, S, n), BF16) for n in widths]
    return pl.pallas_call(
        _proj_attn_kernel, out_shape=outs, grid=(B, S // TM),
        in_specs=[tok(D),
                  pl.BlockSpec((1, 1, 6, D), lambda b, i: (b, jnp.where(i >= nct, 1, 0), 0, 0)),
                  _const_spec(w_a.shape),
                  pl.BlockSpec((8, TM, LANES), lambda b, i: (0, i, 0)),
                  _const_spec(ones_bd.shape)],
        out_specs=[tok(n) for n in widths],
        compiler_params=_cparams(("parallel", "parallel")), name="proj_attn")(xa, mod, w_a, tab, ones_bd)


def _proj_seq_kernel(x_ref, mod_ref, w_ref, gb_ref, u_ref, qk_ref, v_ref, o_ref, g_ref):
    mod = mod_ref[0, 0]
    xm = (_ln(x_ref[0]) * (1.0 + mod[1:2, :]) + mod[0:1, :]).astype(BF16)
    u_ref[...] = _dot(xm, w_ref[:, 0:512])
    qk_ref[0] = _dot(xm, w_ref[:, 512:1536])
    v_ref[0] = _dot(xm, w_ref[:, 1536:2048])
    o_ref[0] = _dot(xm, w_ref[:, 2048:2560])
    g_ref[0] = _dot(xm, w_ref[:, 2560:2688]) + gb_ref[...]


def proj_seq(xa, mod, w_b, gate_b, nct):
    B, S, D = xa.shape
    tok = lambda n: pl.BlockSpec((1, TM, n), lambda b, i: (b, i, 0))
    widths = (1024, 512, 512, LANES)
    return pl.pallas_call(
        _proj_seq_kernel,
        out_shape=[jax.ShapeDtypeStruct((S, B * 512), F32)] + [jax.ShapeDtypeStruct((B, S, n), F32) for n in widths],
        grid=(B, S // TM),
        in_specs=[tok(D),
                  pl.BlockSpec((1, 1, 6, D), lambda b, i: (b, jnp.where(i >= nct, 1, 0), 0, 0)),
                  _const_spec(w_b.shape), _const_spec(gate_b.shape)],
        out_specs=[pl.BlockSpec((TM, 512), lambda b, i: (i, b))] + [tok(n) for n in widths],
        compiler_params=_cparams(("parallel", "parallel")), name="proj_seq")(xa, mod, w_b, gate_b)


def _flash(qqs, k_ref, v_ref, nctx, n_lat, tk, dv):
    P = len(qqs)
    R, W = qqs[0].shape[0], v_ref.shape[2] // P
    mxu_sums = dv < W

    def finish(carry, s, v):
        m, l, acc = carry
        m_new = jnp.maximum(m, jnp.max(s, axis=1, keepdims=True))
        alpha = jnp.exp(m - m_new)
        if mxu_sums:
            p = jnp.exp((s - m_new).astype(BF16))
        else:
            p = jnp.exp(s - m_new)
            l = alpha * l + jnp.sum(p, axis=1, keepdims=True)
        return m_new, l, alpha * acc + _dot(p.astype(BF16), v)

    carries = [(jnp.full((R, 1), -jnp.inf, F32), jnp.zeros((R, 1), F32), jnp.zeros((R, W), F32))] * P
    chunks = [pl.ds(0, nctx)] + [pl.ds(nctx + c * tk, tk) for c in range(n_lat)]
    for rows in chunks:
        scores = [_dot_nt(qqs[p], k_ref[0, rows, p * LANES:(p + 1) * LANES]) for p in range(P)]
        carries = [finish(carries[p], scores[p], v_ref[0, rows, p * W:(p + 1) * W]) for p in range(P)]
    return [acc[:, :dv] / (acc[:, dv:dv + 1] if mxu_sums else l) for _, l, acc in carries]


def _ctx_or_all(i, nctx, n_lat, attend):
    @pl.when(i < nctx // TQ)
    def _():
        attend(0)

    @pl.when(i >= nctx // TQ)
    def _():
        attend(n_lat)


def _diff_attn_kernel(q_ref, k_ref, v_ref, lam_ref, g_ref, o_ref, *, nctx, n_lat, tk, lam_init):
    i = pl.program_id(2)
    lane = lax.broadcasted_iota(I32, (TQ, LANES), 1)
    zero = jnp.zeros((TQ, LANES), q_ref.dtype)
    qqs = []
    for h in range(DIFF_HPS):
        q = q_ref[0, :, h * LANES:(h + 1) * LANES]
        qqs.append(jnp.concatenate([jnp.where(lane < HD, q, zero), jnp.where(lane >= HD, q, zero)], axis=0))
    lv = lam_ref[...]
    lam = (jnp.exp(jnp.sum(lv[0:1] * lv[1:2], axis=1, keepdims=True))
           - jnp.exp(jnp.sum(lv[2:3] * lv[3:4], axis=1, keepdims=True)) + lam_init)

    def attend(n):
        for h, o in enumerate(_flash(qqs, k_ref, v_ref, nctx, n, tk, 2 * HD)):
            d = o[:TQ] - lam * o[TQ:]
            y = d * lax.rsqrt(jnp.mean(d * d, axis=-1, keepdims=True) + LN_EPS)
            o_ref[0, :, h * LANES:(h + 1) * LANES] = (y * g_ref[...] * (1.0 - lam_init)).astype(o_ref.dtype)

    _ctx_or_all(i, nctx, n_lat, attend)


def diff_attention(q, k, v, lam_vec, norm_g, nctx, lam_init):
    B, S, _ = q.shape
    tk = math.gcd(S - nctx, TK)
    kern = functools.partial(_diff_attn_kernel, nctx=nctx, n_lat=(S - nctx) // tk, tk=tk, lam_init=lam_init)
    return pl.pallas_call(
        kern, out_shape=jax.ShapeDtypeStruct((B, S, 512), BF16), grid=(B, DIFF_HEADS // DIFF_HPS, S // TQ),
        in_specs=[pl.BlockSpec((1, TQ, DIFF_HPS * LANES), lambda b, h, i: (b, i, h)),
                  pl.BlockSpec((1, S, DIFF_HPS * LANES), lambda b, h, i: (b, 0, h)),
                  pl.BlockSpec((1, S, DIFF_HPS * LANES), lambda b, h, i: (b, 0, h)),
                  pl.BlockSpec((4, HD), lambda b, h, i: (0, 0)),
                  pl.BlockSpec((1, LANES), lambda b, h, i: (0, 0))],
        out_specs=pl.BlockSpec((1, TQ, DIFF_HPS * LANES), lambda b, h, i: (b, i, h)),
        compiler_params=_cparams(("parallel", "parallel", "parallel")), name="diff_attn")(q, k, v, lam_vec, norm_g)


def _gqa_attn_kernel(q_ref, k_ref, v_ref, o_ref, *, nctx, n_lat, tk):
    i = pl.program_id(2)
    lane = lax.broadcasted_iota(I32, (TQ, LANES), 1)
    zero = jnp.zeros((TQ, LANES), q_ref.dtype)
    qqs = []
    for g in range(GQA_GPS):
        parts = []
        for j in range(2 * g, 2 * g + 2):
            blk = q_ref[0, :, j * LANES:(j + 1) * LANES]
            parts += [jnp.where(lane < HD, blk, zero), jnp.where(lane >= HD, blk, zero)]
        qqs.append(jnp.concatenate(parts, axis=0))

    def attend(n):
        for g, o in enumerate(_flash(qqs, k_ref, v_ref, nctx, n, tk, HD)):
            heads = jnp.concatenate([o[h * TQ:(h + 1) * TQ] for h in range(4)], axis=1)
            o_ref[0, :, g * 2 * LANES:(g + 1) * 2 * LANES] = heads.astype(o_ref.dtype)

    _ctx_or_all(i, nctx, n_lat, attend)


def gqa_attention(q, k, v, nctx):
    B, S, _ = q.shape
    tk = math.gcd(S - nctx, TK)
    kern = functools.partial(_gqa_attn_kernel, nctx=nctx, n_lat=(S - nctx) // tk, tk=tk)
    return pl.pallas_call(
        kern, out_shape=jax.ShapeDtypeStruct((B, S, 512), BF16), grid=(B, GQA_KV // GQA_GPS, S // TQ),
        in_specs=[pl.BlockSpec((1, TQ, GQA_GPS * 2 * LANES), lambda b, g, i: (b, i, g)),
                  pl.BlockSpec((1, S, GQA_GPS * LANES), lambda b, g, i: (b, 0, g)),
                  pl.BlockSpec((1, S, GQA_GPS * LANES), lambda b, g, i: (b, 0, g))],
        out_specs=pl.BlockSpec((1, TQ, GQA_GPS * 2 * LANES), lambda b, g, i: (b, i, g)),
        compiler_params=_cparams(("parallel", "parallel", "parallel")), name="gqa_attn")(q, k, v)


def _s5_kernel(uf_ref, ub_ref, bblk_ref, cblk_ref, lam_ref, yf_ref, yb_ref, buf_f, buf_b, st_ref):
    H = S5_GROUPS * S5_STATE // 2
    R = buf_f.shape[0]
    RB = 256
    TB = RB // 8

    @pl.when(pl.program_id(0) == 0)
    def _():
        st_ref[...] = jnp.zeros_like(st_ref)

    half0 = (lax.broadcasted_iota(I32, (RB, 1), 0) & 1) == 0

    for d, (u_ref, buf) in enumerate(((uf_ref, buf_f), (ub_ref, buf_b))):
        for rb in range(R // RB):
            rows = slice(rb * RB, (rb + 1) * RB)
            ut = u_ref[rb * TB:(rb + 1) * TB, :]
            u3 = jnp.stack([ut[:, s * 256:(s + 1) * 256] for s in range(8)], axis=0)
            uh = pltpu.einshape("stc->tsc", u3).reshape(RB, 256).astype(BF16)
            for cols in (slice(0, H), slice(H, 2 * H)):
                buf[rows, cols] = jnp.where(half0, _dot(uh, bblk_ref[d, 0, :, cols]), _dot(uh, bblk_ref[d, 1, :, cols]))

    lfr, lfi, lbr, lbi = lam_ref[0, 0], lam_ref[0, 1], lam_ref[1, 0], lam_ref[1, 1]

    def step(t, carry):
        fr, fi, br, bi = carry
        rf = pl.ds(pl.multiple_of(t * 8, 8), 8)
        x = buf_f[rf, :]
        nfr = lfr * fr - lfi * fi + x[:, :H]
        nfi = lfr * fi + lfi * fr + x[:, H:]
        buf_f[rf, :] = jnp.concatenate([nfr, nfi], axis=1)
        rb = pl.ds(pl.multiple_of((S5_TC - 1 - t) * 8, 8), 8)
        z = buf_b[rb, :]
        nbr = lbr * br - lbi * bi + z[:, :H]
        nbi = lbr * bi + lbi * br + z[:, H:]
        buf_b[rb, :] = jnp.concatenate([nbr, nbi], axis=1)
        return nfr, nfi, nbr, nbi

    fin = lax.fori_loop(0, S5_TC, step, (st_ref[0], st_ref[1], st_ref[2], st_ref[3]))
    for j in range(4):
        st_ref[j] = fin[j]

    for d, (y_ref, buf) in enumerate(((yf_ref, buf_f), (yb_ref, buf_b))):
        for rb in range(R // RB):
            rows = slice(rb * RB, (rb + 1) * RB)
            h = buf[rows, :].astype(BF16)
            y = jnp.where(half0, _dot(h, cblk_ref[d, 0]), _dot(h, cblk_ref[d, 1]))
            y3 = pltpu.einshape("tsc->stc", y.reshape(TB, 8, 256))
            y_ref[rb * TB:(rb + 1) * TB, :] = jnp.concatenate([y3[s] for s in range(8)], axis=1)


def s5_scan(u, bblk, cblk, lam, nctx):
    S, W = u.shape
    assert W == 8 * 256, "the scan packs (sample, half) pairs into the eight sublanes of a vreg"
    nch, nc0 = S // S5_TC, nctx // S5_TC

    def bwd(i):
        return jnp.where(i < nc0, nc0 - 1 - i, (nch - 1) - (i - nc0))
    H2 = S5_GROUPS * S5_STATE
    R = S5_TC * 8
    blk = lambda f: pl.BlockSpec((S5_TC, W), f)
    return pl.pallas_call(
        _s5_kernel, out_shape=[jax.ShapeDtypeStruct(u.shape, F32)] * 2, grid=(nch,),
        in_specs=[blk(lambda i: (i, 0)), blk(lambda i: (bwd(i), 0)),
                  _const_spec(bblk.shape), _const_spec(cblk.shape), _const_spec(lam.shape)],
        out_specs=[blk(lambda i: (i, 0)), blk(lambda i: (bwd(i), 0))],
        scratch_shapes=[pltpu.VMEM((R, H2), F32), pltpu.VMEM((R, H2), F32), pltpu.VMEM((4, 8, H2 // 2), F32)],
        compiler_params=_cparams(("arbitrary",)), name="s5_scan")(u, u, bblk, cblk, lam)


def _ml_prep_kernel(x_ref, prev_ref, next_ref, w_ref, b_ref, q_ref, k_ref, *, seg_starts, seg_ends):
    i = pl.program_id(1)
    x = x_ref[0]
    row = lax.broadcasted_iota(I32, x.shape, 0)
    first = functools.reduce(jnp.logical_or, [i == s for s in seg_starts])
    last = functools.reduce(jnp.logical_or, [i == s for s in seg_ends])
    pr = jnp.where(first, 0.0, prev_ref[0, 7:8, :])
    nx = jnp.where(last, 0.0, next_ref[0, 0:1, :])
    xp = jnp.where(row == 0, pr, pltpu.roll(x, 1, 0))
    xn = jnp.where(row == TM - 1, nx, pltpu.roll(x, TM - 1, 0))
    w = w_ref[...]
    y = b_ref[...] + xp * w[0:1, :] + x * w[1:2, :] + xn * w[2:3, :]
    y = y * _sigmoid(y)
    q_ref[0] = y[:, :512].astype(q_ref.dtype)
    k_ref[0] = (y[:, 512:] * (ML_HD ** -0.5)).astype(k_ref.dtype)


def ml_prep(qk, conv_w, conv_b, nctx):
    B, S, W = qk.shape
    nt, nct, r8 = S // TM, nctx // TM, TM // 8
    kern = functools.partial(_ml_prep_kernel, seg_starts=(0, nct), seg_ends=(nct - 1, nt - 1))
    return pl.pallas_call(
        kern, out_shape=[jax.ShapeDtypeStruct((B, S, 512), BF16)] * 2, grid=(B, nt),
        in_specs=[pl.BlockSpec((1, TM, W), lambda b, i: (b, i, 0)),
                  pl.BlockSpec((1, 8, W), lambda b, i: (b, jnp.maximum(i * r8 - 1, 0), 0)),
                  pl.BlockSpec((1, 8, W), lambda b, i: (b, jnp.minimum((i + 1) * r8, S // 8 - 1), 0)),
                  pl.BlockSpec((3, W), lambda b, i: (0, 0)), pl.BlockSpec((1, W), lambda b, i: (0, 0))],
        out_specs=[pl.BlockSpec((1, TM, 512), lambda b, i: (b, i, 0))] * 2,
        compiler_params=_cparams(("parallel", "parallel")), name="ml_prep")(qk, qk, qk, conv_w, conv_b)


def _log_sigmoid(x):
    return jnp.minimum(x, 0.0) - jnp.log1p(jnp.exp(-jnp.abs(x)))


def _mlstm_kernel(q_ref, k_ref, v_ref, g_ref, h_ref, c_sc, n_sc, m_sc, *, reverse):
    T = ML_T
    gi, gf = (8, 12) if reverse else (0, 4)

    @pl.when(pl.program_id(0) == 0)
    def _():
        c_sc[...] = jnp.zeros_like(c_sc)
        n_sc[...] = jnp.zeros_like(n_sc)
        m_sc[...] = jnp.zeros_like(m_sc)

    r = lax.broadcasted_iota(I32, (T, T), 0)
    c = lax.broadcasted_iota(I32, (T, T), 1)
    mask = (c >= r) if reverse else (c <= r)
    tri = jnp.where(mask, 1.0, 0.0)
    hp = lax.Precision.HIGHEST
    end = 0 if reverse else T - 1
    def gates(b):
        g = g_ref[b]
        gt = g.T
        bcol_all = jnp.dot(tri, _log_sigmoid(g), preferred_element_type=F32, precision=hp)
        brow_all = lax.dot_general(_log_sigmoid(gt[0:16]), tri, (((1,), (1,)), ((), ())),
                                   preferred_element_type=F32, precision=hp)
        return g, gt, bcol_all, brow_all

    def chain(b, hh, g, gt, bcol_all, brow_all):
        st = b * ML_HEADS + hh
        sl = slice(hh * ML_HD, (hh + 1) * ML_HD)
        q, k, v = q_ref[b, :, sl], k_ref[b, :, sl], v_ref[b, :, sl]
        bcol, brow = bcol_all[:, gf + hh:gf + hh + 1], brow_all[gf + hh:gf + hh + 1, :]
        icol, irow = g[:, gi + hh:gi + hh + 1], gt[gi + hh:gi + hh + 1, :]
        m_old = m_sc[st][:, 0:1]
        n_old = n_sc[st]
        c_old = c_sc[st]
        logw = jnp.where(mask, bcol - brow + irow, -jnp.inf)
        m_inter = bcol + m_old
        m_t = jnp.maximum(m_inter, jnp.max(logw, axis=1, keepdims=True))
        yield
        s = _dot_nt(q, k) * jnp.exp(logw - m_t)
        inter = jnp.exp(m_inter - m_t)
        yield
        num = _dot(s.astype(BF16), v.astype(BF16)) + inter * _dot_nt(q, c_old.astype(BF16))
        den = jnp.sum(s, axis=1, keepdims=True) + inter * jnp.sum(q.astype(F32) * n_old, axis=1, keepdims=True)
        h_ref[b, :, sl] = num / jnp.maximum(jnp.abs(den), jnp.exp(-m_t))
        yield
        b_end = bcol[end:end + 1, :]
        g_row, g_col = b_end - brow + irow, b_end - bcol + icol
        m_new = jnp.maximum(b_end + m_old, jnp.max(g_row, axis=1, keepdims=True))
        decay = jnp.exp(b_end + m_old - m_new)
        wk = jnp.exp(g_col - m_new)
        c_sc[st] = decay * c_old + _dot_tn((v * wk).astype(BF16), k)
        n_sc[st] = decay * n_old + jnp.sum(k.astype(F32) * wk, axis=0, keepdims=True)
        m_sc[st] = jnp.broadcast_to(m_new, (1, LANES))

    nb = q_ref.shape[0]
    for b0 in range(0, nb, ML_LOCKSTEP):
        chains = []
        for b in range(b0, min(b0 + ML_LOCKSTEP, nb)):
            shared = gates(b)
            chains += [chain(b, hh, *shared) for hh in range(ML_HEADS)]
        for _ in range(4):
            for ch in chains:
                next(ch, None)


def mlstm_scan(q, k, v, g, nctx, reverse):
    B, S, W = q.shape
    nch, nc0 = S // ML_T, nctx // ML_T

    def order(i):
        return jnp.where(i < nc0, nc0 - 1 - i, (nch - 1) - (i - nc0)) if reverse else i
    blk = lambda n: pl.BlockSpec((B, ML_T, n), lambda i: (0, order(i), 0))
    nst = B * ML_HEADS
    return pl.pallas_call(
        functools.partial(_mlstm_kernel, reverse=reverse),
        out_shape=jax.ShapeDtypeStruct((B, S, W), F32), grid=(nch,),
        in_specs=[blk(W), blk(W), blk(W), blk(LANES)], out_specs=blk(W),
        scratch_shapes=[pltpu.VMEM((nst, ML_HD, ML_HD), F32), pltpu.VMEM((nst, 1, ML_HD), F32),
                        pltpu.VMEM((nst, 1, LANES), F32)],
        compiler_params=_cparams(("arbitrary",)), name="mlstm_bwd" if reverse else "mlstm_fwd")(q, k, v, g)


def _merge_kernel(x_ref, mod_ref, yd_ref, yg_ref, u_ref, sf_ref, sb_ref, hf_ref, hb_ref, mo_ref,
                  s5d_ref, wglu_ref, bglu_ref, mlg_ref, wgate_ref, bgate_ref, wbr_ref, wout_ref,
                  lng_ref, lnb_ref, ones_ref, o_ref):
    x = x_ref[0]
    mod = mod_ref[0, 0]
    xm = (_ln(x) * (1.0 + mod[1:2, :]) + mod[0:1, :]).astype(BF16)
    ys = u_ref[...] * s5d_ref[...] + sf_ref[...] + sb_ref[...]
    z = _dot(_gelu(ys).astype(BF16), wglu_ref[...]) + bglu_ref[...]
    ys = z[:, :512] * _sigmoid(z[:, 512:])
    hm = hf_ref[0] + hb_ref[0]
    ss = _dot((hm * hm).astype(BF16), ones_ref[...])
    ym = hm * lax.rsqrt(ss * (1.0 / ML_HD) + LN_EPS) * mlg_ref[...] * _sigmoid(mo_ref[0])
    branches = (yd_ref[0], ys.astype(BF16), ym.astype(BF16), yg_ref[0])
    merged = None
    for j, yb in enumerate(branches):
        gate = _sigmoid(_dot(xm, wgate_ref[:, j * D_MODEL:(j + 1) * D_MODEL]) + bgate_ref[:, j * D_MODEL:(j + 1) * D_MODEL])
        term = gate * _dot(yb, wbr_ref[j])
        merged = term if merged is None else merged + term
    mix = _dot(merged.astype(BF16), wout_ref[...])
    o_ref[...] = _ln(ALPHA * x + mod[2:3, :] * mix) * lng_ref[...] + lnb_ref[...]


def merge(xa, mod, yd, yg, u, sf, sb, hf, hb, mo, s5d, wglu, bglu, mlg, wgate, bgate, wbr, wout, lng, lnb, ones_ml, nct,
          latent_only):
    B, S, D = xa.shape
    first = nct if latent_only else 0
    nti = S // TM - first
    tok = lambda n: pl.BlockSpec((1, TM, n), lambda b, i: (b, i + first, 0))
    consts = (s5d, wglu, bglu, mlg, wgate, bgate, wbr, wout, lng, lnb, ones_ml)
    return pl.pallas_call(
        _merge_kernel, out_shape=jax.ShapeDtypeStruct((B * nti * TM, D), F32), grid=(B, nti),
        in_specs=[tok(D), pl.BlockSpec((1, 1, 6, D), lambda b, i: (b, jnp.where(i + first >= nct, 1, 0), 0, 0))]
        + [tok(512)] * 2 + [pl.BlockSpec((TM, 512), lambda b, i: (i + first, b))] * 3 + [tok(512)] * 3
        + [_const_spec(a.shape) for a in consts],
        out_specs=pl.BlockSpec((TM, D), lambda b, i: (b * nti + i, 0)),
        compiler_params=_cparams(("parallel", "parallel")), name="merge")(xa, mod, yd, yg, u, sf, sb, hf, hb, mo, *consts)


def _top16(*problems):
    iotas = {prob[0].shape: lax.broadcasted_iota(I32, prob[0].shape, 0).astype(F32) for prob in problems}

    def one(kk, s, val_ref, idx_ref, payload):
        R, ri = s.shape[0], iotas[s.shape]
        m = jnp.max(s, axis=0, keepdims=True)
        ix = jnp.min(jnp.where(s == m, ri, float(R)), axis=0, keepdims=True)
        hit = ri == ix
        val_ref[pl.ds(kk, 1), :] = m
        idx_ref[pl.ds(kk, 1), :] = ix if payload is None else jnp.max(jnp.where(hit, payload, -1.0), axis=0, keepdims=True)
        return jnp.where(hit, -jnp.inf, s)

    def body(kk, ss):
        return tuple(one(kk, s, *prob[1:]) for s, prob in zip(ss, problems))

    lax.fori_loop(0, PEER_TOPK, body, tuple(prob[0] for prob in problems))


_PEER_CAND_COUNTS = tuple(PEER_TOPK // (p + 1) for p in range(8))
_PEER_CAND_ROWS = tuple((sum(_PEER_CAND_COUNTS[:p]), n) for p, n in enumerate(_PEER_CAND_COUNTS))
_PEER_NCAND = -(-(sum(_PEER_CAND_COUNTS) + 8) // 8) * 8


def _peer_route_kernel(x_ref, mod0_ref, mod1_ref, wq_ref, sk_ref, t_ref, ids_ref, gate_ref,
                       v1_sc, i1_sc, v2_sc, i2_sc, cand_sc, cidx_sc, top_sc, tid_sc, ids_sc, gates_sc):
    h = pl.program_id(1)

    @pl.when(h == 0)
    def _():
        for half, mod_ref in enumerate((mod0_ref, mod1_ref)):
            rows = slice(half * TM, (half + 1) * TM)
            mod = mod_ref[0, 0]
            t_ref[rows, :] = (_ln(x_ref[rows, :]) * (1.0 + mod[4:5, :]) + mod[3:4, :]).astype(BF16)

    q = _dot(t_ref[...], wq_ref[...]).astype(BF16)
    half = PEER_DQ // 2
    s1, s2 = _dot_nt(sk_ref[0], q[:, :half]), _dot_nt(sk_ref[1], q[:, half:])
    _top16((s1, v1_sc, i1_sc, None))
    _top16((s2, v2_sc, i2_sc, None))
    nk = float(PEER_NKEYS)
    for p, (lo, n) in enumerate(_PEER_CAND_ROWS):
        cand_sc[lo:lo + n, :] = v1_sc[p:p + 1, :] + v2_sc[0:n, :]
        cidx_sc[lo:lo + n, :] = i1_sc[p:p + 1, :] * nk + i2_sc[0:n, :]
    lo = _PEER_CAND_ROWS[-1][0] + _PEER_CAND_ROWS[-1][1]
    cand_sc[lo:lo + 8, :] = v1_sc[8:16, :] + v2_sc[0:1, :]
    cidx_sc[lo:lo + 8, :] = i1_sc[8:16, :] * nk + i2_sc[0:1, :]
    pad = _PEER_NCAND - (lo + 8)
    cand_sc[lo + 8:, :] = jnp.full((pad, cand_sc.shape[1]), -jnp.inf, F32)
    cidx_sc[lo + 8:, :] = jnp.zeros((pad, cand_sc.shape[1]), F32)
    _top16((cand_sc[...], top_sc, tid_sc, cidx_sc[...]))
    top = top_sc[...]
    e = jnp.exp(top - top[0:1, :])
    entries = pl.ds(pl.multiple_of(h * PEER_TOPK, PEER_TOPK), PEER_TOPK)
    gates_sc[entries, :] = e / jnp.sum(e, axis=0, keepdims=True)
    ids_sc[entries, :] = tid_sc[...]

    @pl.when(h == PEER_HEADS - 1)
    def _():
        gate_ref[...] = gates_sc[...].T
        ids_ref[...] = ids_sc[...].T.astype(I32)


def _half_tile_mod_spec(half, nt_per_sample, nct):
    def index(i, j):
        r = 2 * i + half
        return (r // nt_per_sample, jnp.where(r % nt_per_sample >= nct, 1, 0), 0, 0)
    return pl.BlockSpec((1, 1, 6, D_MODEL), index)


def peer_route(h1, mod, wq, subkeys, nt_per_sample, nct):
    T, D = h1.shape
    assert T % TM_E == 0 and TM_E == 2 * TM
    f = lambda n: pltpu.VMEM((n, TM_E), F32)
    entries = PEER_HEADS * PEER_TOPK
    return pl.pallas_call(
        _peer_route_kernel,
        out_shape=[jax.ShapeDtypeStruct((T, D), BF16),
                   jax.ShapeDtypeStruct((T, entries), I32),
                   jax.ShapeDtypeStruct((T, entries), F32)],
        grid=(T // TM_E, PEER_HEADS),
        in_specs=[pl.BlockSpec((TM_E, D), lambda i, h: (i, 0)),
                  _half_tile_mod_spec(0, nt_per_sample, nct), _half_tile_mod_spec(1, nt_per_sample, nct),
                  pl.BlockSpec((D, PEER_DQ), lambda i, h: (0, h)),
                  pl.BlockSpec((2, PEER_NKEYS, PEER_DQ // 2), lambda i, h: (0, 0, 0))],
        out_specs=[pl.BlockSpec((TM_E, D), lambda i, h: (i, 0)),
                   pl.BlockSpec((TM_E, entries), lambda i, h: (i, 0)),
                   pl.BlockSpec((TM_E, entries), lambda i, h: (i, 0))],
        scratch_shapes=[f(16), f(16), f(16), f(16), f(_PEER_NCAND), f(_PEER_NCAND), f(16), f(16),
                        f(entries), f(entries)],
        compiler_params=_cparams(("parallel", "arbitrary")), name="peer_route")(h1, mod, mod, wq, subkeys)


def _peer_expert_kernel(t_ref, ids_ref, gate_ref, u_ref, v_ref, x_ref, mod0_ref, mod1_ref, lng_ref, lnb_ref, o_ref,
                        a_sc, wgt_sc, *, nchunk):
    j = pl.program_id(1)
    NK, CA = PEER_NKEYS, PEER_CA
    w_sc = a_sc

    @pl.when(j < nchunk)
    def _():
        a = _dot(t_ref[...], u_ref[0]).astype(a_sc.dtype)
        a3 = jnp.stack([a[:, al * NK:(al + 1) * NK] for al in range(CA)], axis=0)
        a_sc[:, pl.ds(pl.multiple_of(j * CA, CA), CA), :] = pltpu.einshape("atb->tab", a3)

    @pl.when(j == nchunk - 1)
    def _():
        io_k = lax.broadcasted_iota(I32, (NK, NK), 0)
        io_a = lax.broadcasted_iota(I32, (NK, 2 * NK), 0)

        def masks(two):
            idp = ids_ref[two, :]
            ids = jnp.concatenate([idp[0:1], idp[1:2]], axis=1)
            blocks = [jnp.where(io_k == (idp[tok:tok + 1] & (NK - 1)), 1.0, 0.0).astype(BF16) for tok in range(2)]
            zero = jnp.zeros((NK, NK), BF16)
            onehot = jnp.concatenate([jnp.concatenate([blocks[0], zero], axis=1),
                                      jnp.concatenate([zero, blocks[1]], axis=1)], axis=0)
            return onehot, io_a == (ids >> 7)

        def pick(p, carry):
            two = pl.ds(pl.multiple_of(p * 2, 2), 2)
            onehot, r1 = masks(two)
            a2 = a_sc[two]
            at = jnp.concatenate([a2[0], a2[1]], axis=1).astype(BF16)
            picked = _dot(at, onehot)
            act = jnp.sum(jnp.where(r1, picked, 0.0), axis=0, keepdims=True)
            gp = gate_ref[two, :]
            w = jnp.concatenate([gp[0:1], gp[1:2]], axis=1) * _gelu(act)
            wgt_sc[two, :] = jnp.concatenate([w[:, :NK], w[:, NK:]], axis=0)
            return carry

        def scatter(p, carry):
            two = pl.ds(pl.multiple_of(p * 2, 2), 2)
            onehot, r1 = masks(two)
            wp = wgt_sc[two, :]
            w = jnp.concatenate([wp[0:1], wp[1:2]], axis=1)
            wt = _dot_nt(jnp.where(r1, w, 0.0).astype(BF16), onehot)
            w_sc[two] = jnp.stack([wt[:, :NK], wt[:, NK:]], axis=0).astype(w_sc.dtype)
            return carry

        lax.fori_loop(0, TM_E // 2, pick, 0, unroll=16)
        lax.fori_loop(0, TM_E // 2, scatter, 0, unroll=64)
        o_ref[...] = jnp.zeros_like(o_ref)

    @pl.when(j >= nchunk)
    def _():
        w3 = pltpu.einshape("tab->atb", w_sc[:, pl.ds(pl.multiple_of((j - nchunk) * CA, CA), CA), :])
        w = jnp.concatenate([w3[al] for al in range(CA)], axis=1).astype(BF16)
        o_ref[...] += _dot(w, v_ref[...])

    @pl.when(j == 2 * nchunk - 1)
    def _():
        for half, mod_ref in enumerate((mod0_ref, mod1_ref)):
            rows = slice(half * TM, (half + 1) * TM)
            y = ALPHA * x_ref[rows, :] + mod_ref[0, 0][5:6, :] * o_ref[rows, :]
            o_ref[rows, :] = _ln(y) * lng_ref[...] + lnb_ref[...]


def peer_experts(t, ids, gates, emb_u, emb_v, h1, mod, lng, lnb, nt_per_sample, nct):
    T, D = h1.shape
    E = emb_v.shape[0]
    ce = PEER_CA * PEER_NKEYS
    nchunk = E // ce
    emb_ut = emb_u.reshape(nchunk, ce, D).transpose(0, 2, 1)
    assert T % TM_E == 0 and TM_E == 2 * TM
    tok = lambda n, **kw: pl.BlockSpec((TM_E, n), lambda i, j: (i, 0), **kw)
    mod_spec = lambda half: _half_tile_mod_spec(half, nt_per_sample, nct)
    return pl.pallas_call(
        functools.partial(_peer_expert_kernel, nchunk=nchunk),
        out_shape=jax.ShapeDtypeStruct((T, D), F32), grid=(T // TM_E, 2 * nchunk),
        in_specs=[tok(D), tok(LANES), tok(LANES),
                  pl.BlockSpec((1, D, ce), lambda i, j: (jnp.minimum(j, nchunk - 1), 0, 0)),
                  pl.BlockSpec((ce, D), lambda i, j: (jnp.maximum(j - nchunk, 0), 0)),
                  tok(D), mod_spec(0), mod_spec(1),
                  pl.BlockSpec((1, D), lambda i, j: (0, 0)), pl.BlockSpec((1, D), lambda i, j: (0, 0))],
        out_specs=tok(D),
        scratch_shapes=[pltpu.VMEM((TM_E, PEER_NKEYS, PEER_NKEYS), BF16), pltpu.VMEM((TM_E, LANES), F32)],
        compiler_params=_cparams(("parallel", "arbitrary")), name="peer_experts")(t, ids, gates, emb_ut, emb_v, h1, mod, mod, lng, lnb)


_ROPE_IDX = np.arange(HD)
_ROPE_PERM = np.where(_ROPE_IDX % 32 < 16, _ROPE_IDX + 16, _ROPE_IDX - 16)
_ROPE_SIGN = np.where(_ROPE_IDX % 32 < 16, -1.0, 1.0).astype(np.float32)


def _rope_partner(w):
    n = w.shape[1] // HD
    perm = np.concatenate([h * HD + _ROPE_PERM for h in range(n)])
    return w[:, perm] * jnp.asarray(np.tile(_ROPE_SIGN, n))


def _rope_tables(L, nctx, gq, gk):
    rows = L // GRID_W
    row = jnp.repeat(jnp.arange(rows, dtype=F32), GRID_W)
    col = jnp.tile(jnp.arange(GRID_W, dtype=F32), rows)
    nf = HD // 4
    inv = ROPE_THETA ** (-jnp.arange(nf, dtype=F32) / nf)
    ar, ac = row[:, None] * inv, col[:, None] * inv
    cos = jnp.concatenate([jnp.cos(ar), jnp.cos(ar), jnp.cos(ac), jnp.cos(ac)], axis=1)
    sin = jnp.concatenate([jnp.sin(ar), jnp.sin(ar), jnp.sin(ac), jnp.sin(ac)], axis=1)
    cos = jnp.concatenate([jnp.ones((nctx, HD), F32), cos], axis=0)
    sin = jnp.concatenate([jnp.zeros((nctx, HD), F32), sin], axis=0)
    scale = HD ** -0.5
    one = jnp.ones((HD,), F32)
    tabs = []
    for g, sc in ((one, scale), (one, 1.0), (gq, scale), (gk, 1.0)):
        tabs += [cos * (g * sc), sin * (g[_ROPE_PERM] * sc)]
    return jnp.tile(jnp.stack(tabs), (1, 1, 2))


def _s5_tables(a_re, a_im, log_dt, b_re, b_im, c_re, c_im, nb):
    dt = jnp.exp(log_dt)[..., None]
    mag = jnp.exp(a_re * dt)
    lr, li = mag * jnp.cos(a_im * dt), mag * jnp.sin(a_im * dt)
    den = a_re * a_re + a_im * a_im
    cr = ((lr - 1) * a_re + li * a_im) / den
    ci = (li * a_re - (lr - 1) * a_im) / den
    br = cr[..., None] * b_re - ci[..., None] * b_im
    bi = cr[..., None] * b_im + ci[..., None] * b_re
    G2, N, C = S5_GROUPS // 2, S5_STATE, S5_GROUP
    eye = jnp.eye(G2, dtype=F32)

    def blockdiag_in(m):
        m = m.reshape(2, 2, G2, N, C)
        return jnp.einsum('dhgnc,gk->dhgckn', m, eye).reshape(2, 2, G2 * C, G2 * N)

    def blockdiag_out(m):
        m = m.reshape(2, 2, G2, C, N)
        return jnp.einsum('dhgcn,gk->dhgnkc', m, eye).reshape(2, 2, G2 * N, G2 * C)

    bblk = jnp.concatenate([blockdiag_in(br), blockdiag_in(bi)], axis=-1).astype(BF16)
    cblk = jnp.concatenate([blockdiag_out(c_re), blockdiag_out(-c_im)], axis=-2).astype(BF16)
    lam = jnp.stack([lr, li], axis=1).reshape(2, 2, 2, G2 * N)
    lam = jnp.tile(lam[:, :, None], (1, 1, nb, 1, 1)).reshape(2, 2, 2 * nb, G2 * N)
    return bblk, cblk, lam


def _blockdiag_ones(group, n=LANES):
    i = np.arange(n) // group
    return jnp.asarray((i[:, None] == i[None, :]).astype(np.float32), dtype=BF16)


def kernel(x, c, ctx, c_ctx, ada_w, ada_b, w_in, b_gate, diff_lam, diff_norm_g, gqa_qnorm_g, gqa_knorm_g,
           s5_a_re, s5_a_im, s5_log_dt, s5_b_re, s5_b_im, s5_c_re, s5_c_im, s5_d, s5_w_glu, s5_b_glu,
           ml_conv_w, ml_conv_b, ml_gate_b, ml_norm_g, w_branch, w_out, ln_mix_g, ln_mix_b, ln_ffn_g, ln_ffn_b,
           peer_wq, peer_subkeys, peer_u, peer_v):
    B, L, D = x.shape
    nctx = ctx.shape[1]
    S = nctx + L
    assert D == D_MODEL and nctx % TM == 0 and L % TM == 0 and L % GRID_W == 0
    nt, nct = S // TM, nctx // TM
    depth = ada_w.shape[0]

    h = jnp.concatenate([ctx, x], axis=1)
    R = -(-(B + 1) // 8) * 8
    cond = jnp.zeros((R, D), F32).at[:B].set(c).at[B].set(c_ctx)
    ones64, ones128 = _blockdiag_ones(HD), _blockdiag_ones(ML_HD, ML_HEADS * ML_HD)
    o = IN_OFFS

    for l in range(depth):
        last = l == depth - 1
        lam_init = 0.8 - 0.6 * math.exp(-0.3 * l)
        m = ada_modulation(cond, ada_w[l].astype(BF16), ada_b[l][None, :])
        mod = jnp.stack([jnp.broadcast_to(m[B], (B, 6 * D)), m[:B]], axis=1).reshape(B, 2, 6, D)

        w = w_in[l]
        seg = lambda i: w[:, o[i]:o[i + 1]]
        dup = lambda t: jnp.concatenate([t[:, :HD], t[:, :HD], t[:, HD:], t[:, HD:]], axis=1)
        def with_ones_slots(t, width):
            heads = t.reshape(D, -1, width)
            return jnp.concatenate([heads, jnp.zeros_like(heads)], axis=2).reshape(D, -1)
        w_a = jnp.concatenate([seg(0), _rope_partner(seg(0)), seg(1), _rope_partner(seg(1)), seg(2),
                               seg(9), _rope_partner(seg(9)), dup(seg(10)), dup(_rope_partner(seg(10))),
                               with_ones_slots(seg(11), HD)], axis=1).astype(BF16)
        w_b = jnp.concatenate([seg(3), seg(4), seg(5), seg(6), seg(7), seg(8),
                               jnp.zeros((D, LANES - 16), F32)], axis=1).astype(BF16)
        gate_b = jnp.concatenate([ml_gate_b[l], jnp.zeros((LANES - 16,), F32)])[None, :]
        tab = _rope_tables(L, nctx, gqa_qnorm_g[l], gqa_knorm_g[l])

        dq, dk, dv, gq, gk, gv = proj_attn(h, mod, w_a, tab, ones64, nct)
        u, mqk, mv, mo, mg = proj_seq(h, mod, w_b, gate_b, nct)

        yd = diff_attention(dq, dk, dv, diff_lam[l], diff_norm_g[l][None, :], nctx, lam_init)
        yg = gqa_attention(gq, gk, gv, nctx)

        bblk, cblk, lam = _s5_tables(s5_a_re[l], s5_a_im[l], s5_log_dt[l], s5_b_re[l], s5_b_im[l],
                                     s5_c_re[l], s5_c_im[l], B)
        sf, sb = s5_scan(u, bblk, cblk, lam, nctx)

        mq, mk = ml_prep(mqk, ml_conv_w[l], ml_conv_b[l][None, :], nctx)
        hf = mlstm_scan(mq, mk, mv, mg, nctx, False)
        hb = mlstm_scan(mq, mk, mv, mg, nctx, True)

        h1 = merge(h, mod, yd, yg, u, sf, sb, hf, hb, mo,
                   s5_d[l][None, :], s5_w_glu[l].astype(BF16), s5_b_glu[l][None, :], ml_norm_g[l][None, :],
                   seg(12).astype(BF16), b_gate[l][None, :], w_branch[l].astype(BF16), w_out[l].astype(BF16),
                   ln_mix_g[l][None, :], ln_mix_b[l][None, :], ones128, nct, last)

        tiles, ctx_tiles = (nt - nct, 0) if last else (nt, nct)
        t, ids, gates = peer_route(h1, mod, peer_wq[l].astype(BF16), peer_subkeys[l].astype(BF16), tiles, ctx_tiles)
        h = peer_experts(t, ids, gates, peer_u[l].astype(BF16), peer_v[l].astype(BF16),
                         h1, mod, ln_ffn_g[l][None, :], ln_ffn_b[l][None, :], tiles, ctx_tiles)
        h = h.reshape(B, -1, D)
    return h
```
